```python
import math
import jax, jax.numpy as jnp
from jax import lax
import numpy as np

D_MODEL = 1024
BATCH = 4
SEQ = 4096
DEPTH = 2

ATT_HEADS = 8
ATT_HEAD_DIM = 64
ATT_WIDTH = ATT_HEADS * ATT_HEAD_DIM
DILATED_PATTERNS = ((128, 1), (512, 4), (2048, 16))
ATT_BLOCK = 128
ROPE_THETA = 500000.0
ROPE_DIM = ATT_HEAD_DIM // 4

MLSTM_HEADS = 4
MLSTM_HEAD_DIM = 128
MLSTM_WIDTH = MLSTM_HEADS * MLSTM_HEAD_DIM
MLSTM_CHUNK = 64
CONV_WIDTH = 4
FORGET_BIAS = 3.0

D_FF = 4 * D_MODEL
NORM_EPS = 1e-6

COL_AQ = 0
COL_AK = COL_AQ + ATT_WIDTH
COL_AV = COL_AK + ATT_WIDTH
COL_MQ = COL_AV + ATT_WIDTH
COL_MK = COL_MQ + MLSTM_WIDTH
COL_MV = COL_MK + MLSTM_WIDTH
COL_MO = COL_MV + MLSTM_WIDTH
COL_MI = COL_MO + MLSTM_WIDTH
COL_MF = COL_MI + MLSTM_HEADS
COL_GA = COL_MF + MLSTM_HEADS
COL_GB = COL_GA + D_MODEL
D_IN = COL_GB + D_MODEL

kernel_name = "hybrid_dilated_attn_mlstm_gated_block"


def rms_norm(x, w):
    xf = x.astype(jnp.float32)
    y = xf * lax.rsqrt(jnp.mean(xf * xf, axis=-1, keepdims=True) + NORM_EPS)
    return (y * w.astype(jnp.float32)).astype(x.dtype)


def rope_tables(positions):
    inv_freq = ROPE_THETA ** (-jnp.arange(0, ROPE_DIM, 2, dtype=jnp.float32) / ROPE_DIM)
    ang = positions.astype(jnp.float32)[..., None] * inv_freq
    return jnp.cos(ang)[:, :, None, :], jnp.sin(ang)[:, :, None, :]


def apply_partial_rope(t, cos, sin):
    half = ROPE_DIM // 2
    t1 = t[..., :half]
    t2 = t[..., half:ROPE_DIM]
    return jnp.concatenate([t1 * cos - t2 * sin, t2 * cos + t1 * sin, t[..., ROPE_DIM:]], axis=-1)


def dilated_window_attention(q, k, v, window, dilation):
    B, S, H, Dh = q.shape
    span = window // dilation
    L = S // dilation
    nb = -(-L // ATT_BLOCK)
    Lp = nb * ATT_BLOCK
    BD = B * dilation

    def to_sub(t):
        return t.reshape(B, L, dilation, H, Dh).transpose(0, 2, 1, 3, 4).reshape(BD, L, H, Dh)

    qs, ks, vs = to_sub(q), to_sub(k), to_sub(v)
    qb = jnp.pad(qs, ((0, 0), (0, Lp - L), (0, 0), (0, 0))).reshape(BD, nb, ATT_BLOCK, H, Dh)

    def key_blocks(t):
        tp = jnp.pad(t, ((0, 0), (ATT_BLOCK, Lp - L), (0, 0), (0, 0)))
        tp = tp.reshape(BD, nb + 1, ATT_BLOCK, H, Dh)
        return jnp.concatenate([tp[:, :-1], tp[:, 1:]], axis=2)

    kb, vb = key_blocks(ks), key_blocks(vs)
    blk = jnp.arange(nb)[:, None, None]
    qa = jnp.arange(ATT_BLOCK)[None, :, None]
    kc = jnp.arange(2 * ATT_BLOCK)[None, None, :]
    dist = qa + ATT_BLOCK - kc
    key_idx = blk * ATT_BLOCK - ATT_BLOCK + kc
    mask = (dist >= 0) & (dist <= span) & (key_idx >= 0)

    s = jnp.einsum('bnqhd,bnkhd->bnhqk', qb, kb) * (Dh ** -0.5)
    s = jnp.where(mask[None, :, None], s, -jnp.inf)
    m = jnp.max(s, axis=-1, keepdims=True)
    p = jnp.exp(s - m)
    l = jnp.sum(p, axis=-1)
    o = jnp.einsum('bnhqk,bnkhd->bnqhd', p, vb) / jnp.transpose(l, (0, 1, 3, 2))[..., None]
    lse = jnp.transpose(m[..., 0] + jnp.log(l), (0, 1, 3, 2))

    o = o.reshape(BD, Lp, H, Dh)[:, :L].reshape(B, dilation, L, H, Dh)
    o = o.transpose(0, 2, 1, 3, 4).reshape(B, S, H, Dh)
    lse = lse.reshape(BD, Lp, H)[:, :L].reshape(B, dilation, L, H)
    lse = lse.transpose(0, 2, 1, 3).reshape(B, S, H)
    return o, lse


def dilated_mixture_attention(q, k, v):
    outs, lses = [], []
    for window, dilation in DILATED_PATTERNS:
        o, lse = dilated_window_attention(q, k, v, window, dilation)
        outs.append(o)
        lses.append(lse)
    wts = jax.nn.softmax(jnp.stack(lses), axis=0)
    return jnp.einsum('gbsh,gbshd->bshd', wts, jnp.stack(outs))


def causal_short_conv(u, w, b):
    S = u.shape[1]
    up = jnp.pad(u, ((0, 0), (CONV_WIDTH - 1, 0), (0, 0)))
    out = b
    for j in range(CONV_WIDTH):
        out = out + w[j] * up[:, j:j + S]
    return out


def mlstm_chunkwise(q, k, v, ig, fg):
    B, S, H, D = q.shape
    L = MLSTM_CHUNK
    NC = S // L

    def chunks(t):
        return t.reshape(B, NC, L, H, D).transpose(0, 3, 1, 2, 4)

    def gchunks(t):
        return t.reshape(B, NC, L, H).transpose(0, 3, 1, 2)

    q = chunks(q) * (D ** -0.5)
    k = chunks(k)
    v = chunks(v)
    ig = gchunks(ig)
    lf = gchunks(jax.nn.log_sigmoid(fg))
    bcum = jnp.cumsum(lf, axis=-1)
    g = bcum[..., -1]

    a = g[..., None] - bcum + ig
    m_loc = jnp.max(a, axis=-1)
    wk = jnp.exp(a - m_loc[..., None])
    kv_c = jnp.einsum('bhcl,bhclk,bhclv->bhckv', wk, k, v)
    n_c = jnp.einsum('bhcl,bhclk->bhck', wk, k)

    def step(carry, xs):
        C, n, m = carry
        g_c, ml, kv, nn = xs
        m_new = jnp.maximum(g_c + m, ml)
        decay = jnp.exp(g_c + m - m_new)
        inj = jnp.exp(ml - m_new)
        C_new = decay[..., None, None] * C + inj[..., None, None] * kv
        n_new = decay[..., None] * n + inj[..., None] * nn
        return (C_new, n_new, m_new), (C, n, m)

    init = (jnp.zeros((B, H, D, D), jnp.float32), jnp.zeros((B, H, D), jnp.float32),
            jnp.zeros((B, H), jnp.float32))
    xs = (g.transpose(2, 0, 1), m_loc.transpose(2, 0, 1),
          kv_c.transpose(2, 0, 1, 3, 4), n_c.transpose(2, 0, 1, 3))
    _, (C_prev, n_prev, m_prev) = lax.scan(step, init, xs)
    C_prev = C_prev.transpose(1, 2, 0, 3, 4)
    n_prev = n_prev.transpose(1, 2, 0, 3)
    m_prev = m_prev.transpose(1, 2, 0)

    causal = jnp.tril(jnp.ones((L, L), dtype=bool))
    dmat = bcum[..., :, None] - bcum[..., None, :] + ig[..., None, :]
    dmat = jnp.where(causal, dmat, -jnp.inf)
    m_inter = bcum + m_prev[..., None]
    m_t = jnp.maximum(jnp.max(dmat, axis=-1), m_inter)
    p = jnp.exp(dmat - m_t[..., None]) * jnp.einsum('bhcld,bhcsd->bhcls', q, k)
    w_inter = jnp.exp(m_inter - m_t)
    num = (w_inter[..., None] * jnp.einsum('bhcld,bhcdv->bhclv', q, C_prev)
           + jnp.einsum('bhcls,bhcsv->bhclv', p, v))
    den = w_inter * jnp.einsum('bhcld,bhcd->bhcl', q, n_prev) + jnp.sum(p, axis=-1)
    h = num / jnp.maximum(jnp.abs(den), jnp.exp(-m_t))[..., None]
    return h.transpose(0, 2, 3, 1, 4).reshape(B, S, H, D)


def hybrid_layer(x, cos, sin, norm_mix_w, w_in, b_in, conv_w, conv_b, mlstm_norm_w,
                 w_proj_att, w_proj_mlstm, w_out, norm_mlp_w, w_ff1, w_ff2):
    B, S, _ = x.shape
    f32 = jnp.float32
    h = rms_norm(x, norm_mix_w)
    z = h @ w_in + b_in

    aq = z[..., COL_AQ:COL_AK].astype(f32).reshape(B, S, ATT_HEADS, ATT_HEAD_DIM)
    ak = z[..., COL_AK:COL_AV].astype(f32).reshape(B, S, ATT_HEADS, ATT_HEAD_DIM)
    av = z[..., COL_AV:COL_MQ].astype(f32).reshape(B, S, ATT_HEADS, ATT_HEAD_DIM)
    aq = apply_partial_rope(aq, cos, sin)
    ak = apply_partial_rope(ak, cos, sin)
    ya = dilated_mixture_attention(aq, ak, av).reshape(B, S, ATT_WIDTH).astype(x.dtype)

    qk = jax.nn.silu(causal_short_conv(z[..., COL_MQ:COL_MV].astype(f32),
                                       conv_w.astype(f32), conv_b.astype(f32)))
    mq = qk[..., :MLSTM_WIDTH].reshape(B, S, MLSTM_HEADS, MLSTM_HEAD_DIM)
    mk = qk[..., MLSTM_WIDTH:].reshape(B, S, MLSTM_HEADS, MLSTM_HEAD_DIM)
    mv = z[..., COL_MV:COL_MO].astype(f32).reshape(B, S, MLSTM_HEADS, MLSTM_HEAD_DIM)
    mi = z[..., COL_MI:COL_MF].astype(f32)
    mf = z[..., COL_MF:COL_GA].astype(f32)
    hm = mlstm_chunkwise(mq, mk, mv, mi, mf)
    hm = hm * lax.rsqrt(jnp.mean(hm * hm, axis=-1, keepdims=True) + NORM_EPS)
    hm = hm.reshape(B, S, MLSTM_WIDTH) * mlstm_norm_w.astype(f32)
    yb = (jax.nn.sigmoid(z[..., COL_MO:COL_MI].astype(f32)) * hm).astype(x.dtype)

    ga = jax.nn.sigmoid(z[..., COL_GA:COL_GB])
    gb = jax.nn.sigmoid(z[..., COL_GB:D_IN])
    mixed = ga * (ya @ w_proj_att) + gb * (yb @ w_proj_mlstm)
    x = x + mixed @ w_out

    h2 = rms_norm(x, norm_mlp_w)
    x = x + jnp.square(jax.nn.relu(h2 @ w_ff1)) @ w_ff2
    return x


def setup_inputs(seed: int = 0) -> dict:
    key = jax.random.key(seed)
    ks = jax.random.split(key, 16)
    nrm = jax.random.normal
    x = nrm(ks[0], (BATCH, SEQ, D_MODEL), jnp.float32)
    offsets = jax.random.randint(ks[1], (BATCH, 1), 0, 1024, dtype=jnp.int32)
    positions = offsets + jnp.arange(SEQ, dtype=jnp.int32)[None, :]
    norm_mix_w = 1.0 + 0.05 * nrm(ks[2], (DEPTH, D_MODEL), jnp.float32)
    w_in = nrm(ks[3], (DEPTH, D_MODEL, D_IN), jnp.float32) * D_MODEL ** -0.5
    b_in = 0.02 * nrm(ks[4], (DEPTH, D_IN), jnp.float32)
    b_in = b_in.at[:, COL_MF:COL_GA].add(FORGET_BIAS)
    conv_w = nrm(ks[5], (DEPTH, CONV_WIDTH, 2 * MLSTM_WIDTH), jnp.float32) * CONV_WIDTH ** -0.5
    conv_b = 0.02 * nrm(ks[6], (DEPTH, 2 * MLSTM_WIDTH), jnp.float32)
    mlstm_norm_w = 1.0 + 0.05 * nrm(ks[7], (DEPTH, MLSTM_WIDTH), jnp.float32)
    w_proj_att = nrm(ks[8], (DEPTH, ATT_WIDTH, D_MODEL), jnp.float32) * ATT_WIDTH ** -0.5
    w_proj_mlstm = nrm(ks[9], (DEPTH, MLSTM_WIDTH, D_MODEL), jnp.float32) * MLSTM_WIDTH ** -0.5
    w_out = nrm(ks[10], (DEPTH, D_MODEL, D_MODEL), jnp.float32) * D_MODEL ** -0.5
    norm_mlp_w = 1.0 + 0.05 * nrm(ks[11], (DEPTH, D_MODEL), jnp.float32)
    w_ff1 = nrm(ks[12], (DEPTH, D_MODEL, D_FF), jnp.float32) * D_MODEL ** -0.5
    w_ff2 = nrm(ks[13], (DEPTH, D_FF, D_MODEL), jnp.float32) * (0.5 * D_FF ** -0.5)
    final_norm_w = 1.0 + 0.05 * nrm(ks[14], (D_MODEL,), jnp.float32)
    return {"x": x, "positions": positions, "norm_mix_w": norm_mix_w, "w_in": w_in,
            "b_in": b_in, "conv_w": conv_w, "conv_b": conv_b, "mlstm_norm_w": mlstm_norm_w,
            "w_proj_att": w_proj_att, "w_proj_mlstm": w_proj_mlstm, "w_out": w_out,
            "norm_mlp_w": norm_mlp_w, "w_ff1": w_ff1, "w_ff2": w_ff2,
            "final_norm_w": final_norm_w}


def reference(x, positions, norm_mix_w, w_in, b_in, conv_w, conv_b, mlstm_norm_w,
              w_proj_att, w_proj_mlstm, w_out, norm_mlp_w, w_ff1, w_ff2, final_norm_w):
    cos, sin = rope_tables(positions)
    for l in range(DEPTH):
        x = hybrid_layer(x, cos, sin, norm_mix_w[l], w_in[l], b_in[l], conv_w[l], conv_b[l],
                         mlstm_norm_w[l], w_proj_att[l], w_proj_mlstm[l], w_out[l],
                         norm_mlp_w[l], w_ff1[l], w_ff2[l])
    return rms_norm(x, final_norm_w)
```

```python
import functools
import math

import jax
import jax.numpy as jnp
from jax import lax
from jax.experimental import pallas as pl
from jax.experimental.pallas import tpu as pltpu

F32 = jnp.float32
BF16 = jnp.bfloat16

D_MODEL = 1024
ATT_HEADS = 8
ATT_HEAD_DIM = 64
ATT_WIDTH = ATT_HEADS * ATT_HEAD_DIM
ATT_SPAN = 128
DILATIONS = (1, 4, 16)
ROPE_THETA = 500000.0
ROPE_DIM = ATT_HEAD_DIM // 4
MLSTM_HEADS = 4
MLSTM_HEAD_DIM = 128
MLSTM_WIDTH = MLSTM_HEADS * MLSTM_HEAD_DIM
CONV_WIDTH = 4
D_FF = 4 * D_MODEL
NORM_EPS = 1e-6

COL_AQ = 0
COL_AK = COL_AQ + ATT_WIDTH
COL_AV = COL_AK + ATT_WIDTH
COL_MQ = COL_AV + ATT_WIDTH
COL_MK = COL_MQ + MLSTM_WIDTH
COL_MV = COL_MK + MLSTM_WIDTH
COL_MO = COL_MV + MLSTM_WIDTH
COL_MI = COL_MO + MLSTM_WIDTH
COL_MF = COL_MI + MLSTM_HEADS
COL_GA = COL_MF + MLSTM_HEADS
COL_GB = COL_GA + D_MODEL
D_IN = COL_GB + D_MODEL

LANES = 128
P_AQ, P_AK, P_AV = 0, 512, 1024
P_MQ, P_MK, P_MV, P_MO = 1536, 2048, 2560, 3072
P_GA, P_GB = 3584, 4608
P_GT = 5632
P_END = P_GT + LANES

TM_IN = 512
TM_POST = 256
ATT_BLK = 2048
MCHUNK = 128
VMEM_LIMIT = 56 * 1024 * 1024
NEG = -1e30


def _const_spec(shape):
    nd = len(shape)
    return pl.BlockSpec(shape, lambda *_: (0,) * nd, pipeline_mode=pl.Buffered(1))


def _rope_table_kernel(pos_ref, invf_ref, cos_ref, sin_ref):
    pos = pos_ref[...]
    for j in range(ROPE_DIM // 2):
        ang = pos * invf_ref[j]
        cos_ref[j] = jnp.cos(ang)
        sin_ref[j] = jnp.sin(ang)


def _rope_tables(positions):
    T = positions.size
    half = ROPE_DIM // 2
    rows = T // LANES
    pos = positions.astype(F32).reshape(rows, LANES)
    inv_freq = ROPE_THETA ** (-jnp.arange(0, ROPE_DIM, 2, dtype=F32) / ROPE_DIM)
    invf = jnp.broadcast_to(inv_freq[:, None, None], (half, 1, LANES))
    cos8, sin8 = pl.pallas_call(
        _rope_table_kernel,
        out_shape=(jax.ShapeDtypeStruct((half, rows, LANES), F32),) * 2,
        name="rope_tables",
    )(pos, invf)
    ct = cos8.reshape(half, T).T
    st = sin8.reshape(half, T).T
    pad1 = jnp.ones((T, ATT_HEAD_DIM - ROPE_DIM), F32)
    pad0 = jnp.zeros((T, ATT_HEAD_DIM - ROPE_DIM), F32)
    c64 = jnp.concatenate([ct, ct, pad1], axis=1)
    s64 = jnp.concatenate([-st, st, pad0], axis=1)
    return jnp.tile(c64, (1, LANES // ATT_HEAD_DIM)), jnp.tile(s64, (1, LANES // ATT_HEAD_DIM))


def _inproj_kernel(x_ref, nw_ref, w_ref, b_ref, c_ref, s_ref,
                   aq_ref, ak_ref, av_ref, mq_ref, mk_ref, mv_ref, mo_ref, ga_ref, gb_ref, gt_ref):
    x = x_ref[...]
    var = jnp.mean(x * x, axis=-1, keepdims=True)
    h = (x * lax.rsqrt(var + NORM_EPS) * nw_ref[...]).astype(BF16)

    def proj(lo, width):
        return jnp.dot(h, w_ref[:, lo:lo + width], preferred_element_type=F32) + b_ref[:, lo:lo + width]

    cos = c_ref[...]
    sin = s_ref[...]
    lane = lax.broadcasted_iota(jnp.int32, cos.shape, 1)
    first_half = (lane % ATT_HEAD_DIM) < (ROPE_DIM // 2)

    def rope_store(dst_ref, lo):
        zz = proj(lo, ATT_WIDTH)
        for j in range(ATT_WIDTH // LANES):
            z = zz[:, j * LANES:(j + 1) * LANES]
            partner = jnp.where(first_half,
                                pltpu.roll(z, LANES - ROPE_DIM // 2, axis=1),
                                pltpu.roll(z, ROPE_DIM // 2, axis=1))
            dst_ref[:, j * LANES:(j + 1) * LANES] = z * cos + partner * sin

    rope_store(aq_ref, P_AQ)
    rope_store(ak_ref, P_AK)
    av_ref[...] = proj(P_AV, ATT_WIDTH)
    mq_ref[...] = proj(P_MQ, MLSTM_WIDTH)
    mk_ref[...] = proj(P_MK, MLSTM_WIDTH)
    mv_ref[...] = proj(P_MV, MLSTM_WIDTH).astype(BF16)
    mo_ref[...] = jax.nn.sigmoid(proj(P_MO, MLSTM_WIDTH))
    ga_ref[...] = jax.nn.sigmoid(proj(P_GA, D_MODEL))
    gb_ref[...] = jax.nn.sigmoid(proj(P_GB, D_MODEL))
    gt_ref[...] = proj(P_GT, LANES)


def _inproj(x2d, norm_w, w_all, b_all, rope_c, rope_s):
    T = x2d.shape[0]
    tile = lambda w: pl.BlockSpec((TM_IN, w), lambda i: (i, 0))
    out_shapes = (
        jax.ShapeDtypeStruct((T, ATT_WIDTH), F32),
        jax.ShapeDtypeStruct((T, ATT_WIDTH), F32),
        jax.ShapeDtypeStruct((T, ATT_WIDTH), F32),
        jax.ShapeDtypeStruct((T, MLSTM_WIDTH), F32),
        jax.ShapeDtypeStruct((T, MLSTM_WIDTH), F32),
        jax.ShapeDtypeStruct((T, MLSTM_WIDTH), BF16),
        jax.ShapeDtypeStruct((T, MLSTM_WIDTH), F32),
        jax.ShapeDtypeStruct((T, D_MODEL), F32),
        jax.ShapeDtypeStruct((T, D_MODEL), F32),
        jax.ShapeDtypeStruct((T, LANES), F32),
    )
    out_specs = tuple(tile(s.shape[1]) for s in out_shapes)
    return pl.pallas_call(
        _inproj_kernel,
        grid=(T // TM_IN,),
        in_specs=[tile(D_MODEL), _const_spec((1, D_MODEL)), _const_spec((D_MODEL, P_END)),
                  _const_spec((1, P_END)), tile(LANES), tile(LANES)],
        out_specs=out_specs,
        out_shape=out_shapes,
        compiler_params=pltpu.CompilerParams(dimension_semantics=("parallel",),
                                             vmem_limit_bytes=VMEM_LIMIT),
        name="inproj",
    )(x2d, norm_w, w_all, b_all, rope_c, rope_s)


def _attn_kernel(q_ref, kc_ref, kp_ref, vc_ref, vp_ref, o_ref,
                 kd1, vd1, qd4, kd4, vd4, qd16, kd16, vd16, acc_s, m_s, l_s, bias_s):
    j = pl.program_id(1)
    blk = ATT_SPAN
    nsub = {d: ATT_BLK // d // blk for d in DILATIONS}

    row = lax.broadcasted_iota(jnp.int32, (2 * blk, 2 * blk), 0) % blk
    col = lax.broadcasted_iota(jnp.int32, (2 * blk, 2 * blk), 1)
    band = (col >= row) & (col <= row + ATT_SPAN)
    bias_s[0] = jnp.where(band, 0.0, NEG)
    bias_s[1] = jnp.where(band & (col >= blk), 0.0, NEG)

    kd1[0:blk] = kp_ref[0, ATT_BLK - blk:ATT_BLK, :].astype(BF16)
    vd1[0:blk] = vp_ref[0, ATT_BLK - blk:ATT_BLK, :].astype(BF16)
    kd1[blk:blk + ATT_BLK] = kc_ref[0].astype(BF16)
    vd1[blk:blk + ATT_BLK] = vc_ref[0].astype(BF16)
    for d, qd, kd, vd in ((4, qd4, kd4, vd4), (16, qd16, kd16, vd16)):
        n = ATT_BLK // d
        for r in range(d):
            base = r * (n + blk)
            qd[r * n:(r + 1) * n] = q_ref[0, pl.ds(r, n, stride=d), :].astype(BF16)
            kd[base:base + blk] = kp_ref[0, pl.ds(ATT_BLK - blk * d + r, blk, stride=d), :].astype(BF16)
            vd[base:base + blk] = vp_ref[0, pl.ds(ATT_BLK - blk * d + r, blk, stride=d), :].astype(BF16)
            kd[base + blk:base + blk + n] = kc_ref[0, pl.ds(r, n, stride=d), :].astype(BF16)
            vd[base + blk:base + blk + n] = vc_ref[0, pl.ds(r, n, stride=d), :].astype(BF16)

    head_a = lax.broadcasted_iota(jnp.int32, (blk, LANES), 1) < ATT_HEAD_DIM

    def unit(q2, k2, v2, bias):
        zero = jnp.zeros_like(q2)
        qs = jnp.concatenate([jnp.where(head_a, q2, zero), jnp.where(head_a, zero, q2)], axis=0)
        s = lax.dot_general(qs, k2, (((1,), (1,)), ((), ())), preferred_element_type=F32) + bias
        m = jnp.max(s, axis=-1, keepdims=True)
        p = jnp.exp(s - m)
        l = jnp.sum(p, axis=-1, keepdims=True)
        pv = jnp.dot(p.astype(BF16), v2, preferred_element_type=F32)
        acc = jnp.where(head_a, pv[:blk], pv[blk:])
        mm = jnp.where(head_a, m[:blk], m[blk:])
        ll = jnp.where(head_a, l[:blk], l[blk:])
        return acc, mm, ll

    first_blk = (j == 0).astype(jnp.int32)

    def body1(u, carry):
        r0 = pl.multiple_of(u * blk, blk)
        q2 = q_ref[0, pl.ds(r0, blk), :].astype(BF16)
        bias = bias_s[first_blk * (u == 0).astype(jnp.int32)]
        acc, mm, ll = unit(q2, kd1[pl.ds(r0, 2 * blk), :], vd1[pl.ds(r0, 2 * blk), :], bias)
        acc_s[0, pl.ds(r0, blk), :] = acc
        m_s[0, pl.ds(r0, blk), :] = mm
        l_s[0, pl.ds(r0, blk), :] = ll
        return carry

    lax.fori_loop(0, nsub[1], body1, 0, unroll=2)

    def make_body(g, d, qd, kd, vd):
        n = ATT_BLK // d

        def body(u, carry):
            r = u // nsub[d]
            sb = u % nsub[d]
            q0 = pl.multiple_of(u * blk, blk)
            k0 = pl.multiple_of(r * (n + blk) + sb * blk, blk)
            bias = bias_s[first_blk * (sb == 0).astype(jnp.int32)]
            acc, mm, ll = unit(qd[pl.ds(q0, blk), :], kd[pl.ds(k0, 2 * blk), :], vd[pl.ds(k0, 2 * blk), :], bias)
            t0 = sb * (blk * d) + r
            acc_s[g, pl.ds(t0, blk, stride=d), :] = acc
            m_s[g, pl.ds(t0, blk, stride=d), :] = mm
            l_s[g, pl.ds(t0, blk, stride=d), :] = ll
            return carry

        return body

    lax.fori_loop(0, ATT_BLK // blk, make_body(1, 4, qd4, kd4, vd4), 0, unroll=2)
    lax.fori_loop(0, ATT_BLK // blk, make_body(2, 16, qd16, kd16, vd16), 0, unroll=2)

    def combine(u, carry):
        r0 = pl.multiple_of(u * blk, blk)
        sl = pl.ds(r0, blk)
        m0, m1, m2 = m_s[0, sl, :], m_s[1, sl, :], m_s[2, sl, :]
        mx = jnp.maximum(jnp.maximum(m0, m1), m2)
        w0, w1, w2 = jnp.exp(m0 - mx), jnp.exp(m1 - mx), jnp.exp(m2 - mx)
        num = w0 * acc_s[0, sl, :] + w1 * acc_s[1, sl, :] + w2 * acc_s[2, sl, :]
        den = w0 * l_s[0, sl, :] + w1 * l_s[1, sl, :] + w2 * l_s[2, sl, :]
        o_ref[0, sl, :] = (num / den).astype(o_ref.dtype)
        return carry

    lax.fori_loop(0, ATT_BLK // blk, combine, 0, unroll=2)


def _attention(aq, ak, av):
    B, S, _ = aq.shape
    blk = ATT_SPAN
    cur = pl.BlockSpec((1, ATT_BLK, LANES), lambda b, j, hp: (b, j, hp))
    prev = pl.BlockSpec((1, ATT_BLK, LANES), lambda b, j, hp: (b, jnp.maximum(j - 1, 0), hp))
    kv_rows = {d: d * (ATT_BLK // d + blk) for d in DILATIONS}
    scratch = [
        pltpu.VMEM((kv_rows[1], LANES), BF16), pltpu.VMEM((kv_rows[1], LANES), BF16),
        pltpu.VMEM((ATT_BLK, LANES), BF16),
        pltpu.VMEM((kv_rows[4], LANES), BF16), pltpu.VMEM((kv_rows[4], LANES), BF16),
        pltpu.VMEM((ATT_BLK, LANES), BF16),
        pltpu.VMEM((kv_rows[16], LANES), BF16), pltpu.VMEM((kv_rows[16], LANES), BF16),
        pltpu.VMEM((3, ATT_BLK, LANES), F32), pltpu.VMEM((3, ATT_BLK, LANES), F32),
        pltpu.VMEM((3, ATT_BLK, LANES), F32),
        pltpu.VMEM((2, 2 * blk, 2 * blk), F32),
    ]
    return pl.pallas_call(
        _attn_kernel,
        grid=(B, S // ATT_BLK, ATT_WIDTH // LANES),
        in_specs=[cur, cur, prev, cur, prev],
        out_specs=cur,
        out_shape=jax.ShapeDtypeStruct((B, S, ATT_WIDTH), BF16),
        scratch_shapes=scratch,
        compiler_params=pltpu.CompilerParams(dimension_semantics=("parallel", "parallel", "parallel"),
                                             vmem_limit_bytes=VMEM_LIMIT),
        name="dilated_attention",
    )(aq, ak, ak, av, av)


def _log_sigmoid(x):
    return jnp.minimum(x, 0.0) - jnp.log(1.0 + jnp.exp(-jnp.abs(x)))


def _mlstm_kernel(mq_ref, mk_ref, mv_ref, mo_ref, g_ref, cw_ref, cb_ref, nw_ref, y_ref,
                  uq, uk, c_state, n_state, m_state):
    c = pl.program_id(1)
    L = MCHUNK
    tail = 8

    @pl.when(c == 0)
    def _():
        uq[L:L + tail, :] = jnp.zeros((tail, MLSTM_WIDTH), F32)
        uk[L:L + tail, :] = jnp.zeros((tail, MLSTM_WIDTH), F32)
        c_state[...] = jnp.zeros_like(c_state)
        n_state[...] = jnp.zeros_like(n_state)
        m_state[...] = jnp.zeros_like(m_state)

    def conv_silu(u_ref, src_ref, col0):
        u_ref[0:tail, :] = u_ref[L:L + tail, :]
        u_ref[tail:tail + L, :] = src_ref[...]
        out = cb_ref[:, col0:col0 + MLSTM_WIDTH]
        for t in range(CONV_WIDTH):
            lo = tail - (CONV_WIDTH - 1) + t
            out = out + cw_ref[t:t + 1, col0:col0 + MLSTM_WIDTH] * u_ref[lo:lo + L, :]
        return out * jax.nn.sigmoid(out)

    q_all = conv_silu(uq, mq_ref, 0) * (MLSTM_HEAD_DIM ** -0.5)
    k_all = conv_silu(uk, mk_ref, MLSTM_WIDTH)

    gates = g_ref[...]
    logf = _log_sigmoid(gates)
    ri = lax.broadcasted_iota(jnp.int32, (L, L), 0)
    ci = lax.broadcasted_iota(jnp.int32, (L, L), 1)
    causal = ci <= ri
    bcum = jnp.dot(causal.astype(F32), logf, precision=lax.Precision.HIGHEST,
                   preferred_element_type=F32)
    gates_t = gates.T
    bcum_t = bcum.T

    for h in range(MLSTM_HEADS):
        hs = slice(h * MLSTM_HEAD_DIM, (h + 1) * MLSTM_HEAD_DIM)
        fh = MLSTM_HEADS + h
        q = q_all[:, hs]
        k = k_all[:, hs]
        v = mv_ref[:, hs]
        qb = q.astype(BF16)
        b_col = bcum[:, fh:fh + 1]
        i_col = gates[:, h:h + 1]
        b_row = bcum_t[fh:fh + 1, :]
        i_row = gates_t[h:h + 1, :]
        g_tot = bcum[L - 1:L, fh:fh + 1]
        m_prev = m_state[h:h + 1, 0:1]
        n_prev = n_state[h:h + 1, :]

        dmat = jnp.where(causal, b_col + (i_row - b_row), NEG)
        m_inter = b_col + m_prev
        m_t = jnp.maximum(jnp.max(dmat, axis=-1, keepdims=True), m_inter)
        s = lax.dot_general(qb, k.astype(BF16), (((1,), (1,)), ((), ())), preferred_element_type=F32)
        p = jnp.exp(dmat - m_t) * s
        w_inter = jnp.exp(m_inter - m_t)
        q_c = jnp.dot(qb, c_state[h].astype(BF16), preferred_element_type=F32)
        num = w_inter * q_c + jnp.dot(p.astype(BF16), v, preferred_element_type=F32)
        den = w_inter * jnp.sum(q * n_prev, axis=-1, keepdims=True) + jnp.sum(p, axis=-1, keepdims=True)
        hh = num / jnp.maximum(jnp.abs(den), jnp.exp(-m_t))

        a_row = g_tot + (i_row - b_row)
        m_new = jnp.maximum(g_tot + m_prev, jnp.max(a_row, axis=-1, keepdims=True))
        kw = k * jnp.exp(g_tot - b_col + i_col - m_new)
        decay = jnp.exp(g_tot + m_prev - m_new)
        kv = lax.dot_general(kw.astype(BF16), v, (((0,), (0,)), ((), ())), preferred_element_type=F32)
        c_state[h] = decay * c_state[h] + kv
        n_state[h:h + 1, :] = decay * n_prev + jnp.sum(kw, axis=0, keepdims=True)
        m_state[h:h + 1, :] = jnp.broadcast_to(m_new, (1, LANES))

        hn = hh * lax.rsqrt(jnp.mean(hh * hh, axis=-1, keepdims=True) + NORM_EPS) * nw_ref[:, hs]
        y_ref[:, hs] = (mo_ref[:, hs] * hn).astype(y_ref.dtype)


def _mlstm(mq, mk, mv, mo_s, gates, conv_w, conv_b, norm_w, batch):
    T = mq.shape[0]
    nc = T // batch // MCHUNK
    tile = lambda w: pl.BlockSpec((MCHUNK, w), lambda b, c: (b * nc + c, 0))
    const = lambda shape: pl.BlockSpec(shape, lambda b, c: (0, 0))
    return pl.pallas_call(
        _mlstm_kernel,
        grid=(batch, nc),
        in_specs=[tile(MLSTM_WIDTH), tile(MLSTM_WIDTH), tile(MLSTM_WIDTH), tile(MLSTM_WIDTH), tile(LANES),
                  const((CONV_WIDTH, 2 * MLSTM_WIDTH)), const((1, 2 * MLSTM_WIDTH)), const((1, MLSTM_WIDTH))],
        out_specs=tile(MLSTM_WIDTH),
        out_shape=jax.ShapeDtypeStruct((T, MLSTM_WIDTH), BF16),
        scratch_shapes=[
            pltpu.VMEM((MCHUNK + 8, MLSTM_WIDTH), F32), pltpu.VMEM((MCHUNK + 8, MLSTM_WIDTH), F32),
            pltpu.VMEM((MLSTM_HEADS, MLSTM_HEAD_DIM, MLSTM_HEAD_DIM), F32),
            pltpu.VMEM((8, LANES), F32), pltpu.VMEM((8, LANES), F32),
        ],
        compiler_params=pltpu.CompilerParams(dimension_semantics=("parallel", "arbitrary"),
                                             vmem_limit_bytes=VMEM_LIMIT),
        name="mlstm",
    )(mq, mk, mv, mo_s, gates, conv_w, conv_b, norm_w)


def _post_kernel(x_ref, ya_ref, yb_ref, ga_ref, gb_ref, wpa_ref, wpm_ref, wo_ref, nw_ref, w1_ref, w2_ref,
                 fw_ref, o_ref, *, final_norm):
    pa = jnp.dot(ya_ref[...], wpa_ref[...], preferred_element_type=F32)
    pb = jnp.dot(yb_ref[...], wpm_ref[...], preferred_element_type=F32)
    mixed = (ga_ref[...] * pa + gb_ref[...] * pb).astype(BF16)
    x1 = x_ref[...] + jnp.dot(mixed, wo_ref[...], preferred_element_type=F32)
    var = jnp.mean(x1 * x1, axis=-1, keepdims=True)
    h2 = (x1 * lax.rsqrt(var + NORM_EPS) * nw_ref[...]).astype(BF16)
    acc = x1
    ff_chunk = D_MODEL
    for c in range(D_FF // ff_chunk):
        cs = slice(c * ff_chunk, (c + 1) * ff_chunk)
        u = jnp.maximum(jnp.dot(h2, w1_ref[:, cs], preferred_element_type=F32), 0.0)
        acc = acc + jnp.dot((u * u).astype(BF16), w2_ref[cs, :], preferred_element_type=F32)
    if final_norm:
        var = jnp.mean(acc * acc, axis=-1, keepdims=True)
        acc = acc * lax.rsqrt(var + NORM_EPS) * fw_ref[...]
    o_ref[...] = acc


def _post(x2d, ya, yb, ga_s, gb_s, wpa, wpm, wo, norm_w, w1, w2, final_w, final_norm):
    T = x2d.shape[0]
    tile = lambda w: pl.BlockSpec((TM_POST, w), lambda i: (i, 0))
    return pl.pallas_call(
        functools.partial(_post_kernel, final_norm=final_norm),
        grid=(T // TM_POST,),
        in_specs=[tile(D_MODEL), tile(ATT_WIDTH), tile(MLSTM_WIDTH), tile(D_MODEL), tile(D_MODEL),
                  _const_spec((ATT_WIDTH, D_MODEL)), _const_spec((MLSTM_WIDTH, D_MODEL)),
                  _const_spec((D_MODEL, D_MODEL)), _const_spec((1, D_MODEL)),
                  _const_spec((D_MODEL, D_FF)), _const_spec((D_FF, D_MODEL)), _const_spec((1, D_MODEL))],
        out_specs=tile(D_MODEL),
        out_shape=jax.ShapeDtypeStruct((T, D_MODEL), F32),
        compiler_params=pltpu.CompilerParams(dimension_semantics=("parallel",),
                                             vmem_limit_bytes=VMEM_LIMIT),
        name="post",
    )(x2d, ya, yb, ga_s, gb_s, wpa, wpm, wo, norm_w, w1, w2, final_w)


def _pack_in_proj(w_in, b_in):
    scale = ATT_HEAD_DIM ** -0.5
    pad_w = jnp.zeros((D_MODEL, LANES - 2 * MLSTM_HEADS), w_in.dtype)
    pad_b = jnp.zeros((LANES - 2 * MLSTM_HEADS,), b_in.dtype)
    w_all = jnp.concatenate([w_in[:, COL_AQ:COL_AK] * scale, w_in[:, COL_AK:COL_MI],
                             w_in[:, COL_GA:D_IN], w_in[:, COL_MI:COL_GA], pad_w], axis=1)
    b_all = jnp.concatenate([b_in[COL_AQ:COL_AK] * scale, b_in[COL_AK:COL_MI],
                             b_in[COL_GA:D_IN], b_in[COL_MI:COL_GA], pad_b])
    return w_all.astype(BF16), b_all.reshape(1, P_END).astype(F32)


def kernel(x, positions, norm_mix_w, w_in, b_in, conv_w, conv_b, mlstm_norm_w, w_proj_att, w_proj_mlstm,
           w_out, norm_mlp_w, w_ff1, w_ff2, final_norm_w):
    B, S, D = x.shape
    T = B * S
    depth = w_in.shape[0]
    assert D == D_MODEL and S % ATT_BLK == 0 and T % TM_IN == 0 and T % TM_POST == 0
    assert math.isclose(ATT_HEAD_DIM ** -0.5, 0.125)
    rope_c, rope_s = _rope_tables(positions)
    x2d = x.reshape(T, D)
    final_w = final_norm_w.reshape(1, D).astype(F32)
    for l in range(depth):
        w_all, b_all = _pack_in_proj(w_in[l], b_in[l])
        aq, ak, av, mq, mk, mv, mo_s, ga_s, gb_s, gates = _inproj(
            x2d, norm_mix_w[l].reshape(1, D), w_all, b_all, rope_c, rope_s)
        ya = _attention(aq.reshape(B, S, ATT_WIDTH), ak.reshape(B, S, ATT_WIDTH), av.reshape(B, S, ATT_WIDTH))
        yb = _mlstm(mq, mk, mv, mo_s, gates, conv_w[l], conv_b[l].reshape(1, -1),
                    mlstm_norm_w[l].reshape(1, -1), B)
        x2d = _post(x2d, ya.reshape(T, ATT_WIDTH), yb, ga_s, gb_s,
                    w_proj_att[l].astype(BF16), w_proj_mlstm[l].astype(BF16), w_out[l].astype(BF16),
                    norm_mlp_w[l].reshape(1, D), w_ff1[l].astype(BF16), w_ff2[l].astype(BF16),
                    final_w, final_norm=(l == depth - 1))
    return x2d.reshape(B, S, D)
```

```python
import functools
import math

import jax
import jax.numpy as jnp
from jax import lax
from jax.experimental import pallas as pl
from jax.experimental.pallas import tpu as pltpu

F32 = jnp.float32
BF16 = jnp.bfloat16

D_MODEL = 1024
ATT_HEADS = 8
ATT_HEAD_DIM = 64
ATT_WIDTH = ATT_HEADS * ATT_HEAD_DIM
ATT_SPAN = 128
DILATIONS = (1, 4, 16)
ROPE_THETA = 500000.0
ROPE_DIM = ATT_HEAD_DIM // 4
MLSTM_HEADS = 4
MLSTM_HEAD_DIM = 128
MLSTM_WIDTH = MLSTM_HEADS * MLSTM_HEAD_DIM
CONV_WIDTH = 4
D_FF = 4 * D_MODEL
NORM_EPS = 1e-6

COL_AQ = 0
COL_AK = COL_AQ + ATT_WIDTH
COL_AV = COL_AK + ATT_WIDTH
COL_MQ = COL_AV + ATT_WIDTH
COL_MK = COL_MQ + MLSTM_WIDTH
COL_MV = COL_MK + MLSTM_WIDTH
COL_MO = COL_MV + MLSTM_WIDTH
COL_MI = COL_MO + MLSTM_WIDTH
COL_MF = COL_MI + MLSTM_HEADS
COL_GA = COL_MF + MLSTM_HEADS
COL_GB = COL_GA + D_MODEL
D_IN = COL_GB + D_MODEL

LANES = 128
P_AQ, P_AK, P_AV = 0, 512, 1024
P_MQ, P_MK, P_MV, P_MO = 1536, 2048, 2560, 3072
P_GA, P_GB = 3584, 4608
P_GT = 5632
P_END = P_GT + LANES

TM_IN = 512
TM_POST = 256
ATT_BLK = 2048
MCHUNK = 128
TM_MLSTM = 512
VMEM_LIMIT = 56 * 1024 * 1024
NEG = -1e30


def _const_spec(shape):
    nd = len(shape)
    return pl.BlockSpec(shape, lambda *_: (0,) * nd, pipeline_mode=pl.Buffered(1))


def _rope_table_kernel(pos_ref, invf_ref, cos_ref, sin_ref):
    pos = pos_ref[...]
    for j in range(ROPE_DIM // 2):
        ang = pos * invf_ref[j]
        cos_ref[j] = jnp.cos(ang)
        sin_ref[j] = jnp.sin(ang)


def _rope_tables(positions):
    T = positions.size
    half = ROPE_DIM // 2
    rows = T // LANES
    pos = positions.astype(F32).reshape(rows, LANES)
    inv_freq = ROPE_THETA ** (-jnp.arange(0, ROPE_DIM, 2, dtype=F32) / ROPE_DIM)
    invf = jnp.broadcast_to(inv_freq[:, None, None], (half, 1, LANES))
    cos8, sin8 = pl.pallas_call(
        _rope_table_kernel,
        out_shape=(jax.ShapeDtypeStruct((half, rows, LANES), F32),) * 2,
        name="rope_tables",
    )(pos, invf)
    ct = cos8.reshape(half, T).T
    st = sin8.reshape(half, T).T
    pad1 = jnp.ones((T, ATT_HEAD_DIM - ROPE_DIM), F32)
    pad0 = jnp.zeros((T, ATT_HEAD_DIM - ROPE_DIM), F32)
    c64 = jnp.concatenate([ct, ct, pad1], axis=1)
    s64 = jnp.concatenate([-st, st, pad0], axis=1)
    return jnp.tile(c64, (1, LANES // ATT_HEAD_DIM)), jnp.tile(s64, (1, LANES // ATT_HEAD_DIM))


CONV_TAIL = 8
GATE_ROWS = 8


def _inproj_kernel(x_ref, nw_ref, w_ref, b_ref, c_ref, s_ref, cw_ref, cb_ref,
                   aq_ref, ak_ref, av_ref, mq_ref, mk_ref, mv_ref, mo_ref, ga_ref, gb_ref, gt_ref, gtt_ref,
                   uq, uk, *, tiles_per_seq):
    @pl.when(pl.program_id(0) % tiles_per_seq == 0)
    def _():
        uq[TM_IN:TM_IN + CONV_TAIL, :] = jnp.zeros((CONV_TAIL, MLSTM_WIDTH), F32)
        uk[TM_IN:TM_IN + CONV_TAIL, :] = jnp.zeros((CONV_TAIL, MLSTM_WIDTH), F32)

    x = x_ref[...]
    var = jnp.mean(x * x, axis=-1, keepdims=True)
    h = (x * lax.rsqrt(var + NORM_EPS) * nw_ref[...]).astype(BF16)

    def proj(lo, width):
        return jnp.dot(h, w_ref[:, lo:lo + width], preferred_element_type=F32) + b_ref[:, lo:lo + width]

    cos = c_ref[...]
    sin = s_ref[...]
    lane = lax.broadcasted_iota(jnp.int32, cos.shape, 1)
    first_half = (lane % ATT_HEAD_DIM) < (ROPE_DIM // 2)

    def rope_store(dst_ref, lo):
        zz = proj(lo, ATT_WIDTH)
        for j in range(ATT_WIDTH // LANES):
            z = zz[:, j * LANES:(j + 1) * LANES]
            partner = jnp.where(first_half,
                                pltpu.roll(z, LANES - ROPE_DIM // 2, axis=1),
                                pltpu.roll(z, ROPE_DIM // 2, axis=1))
            dst_ref[:, j * LANES:(j + 1) * LANES] = z * cos + partner * sin

    def conv_silu(u_ref, lo, col0):
        u_ref[0:CONV_TAIL, :] = u_ref[TM_IN:TM_IN + CONV_TAIL, :]
        u_ref[CONV_TAIL:CONV_TAIL + TM_IN, :] = proj(lo, MLSTM_WIDTH)
        out = cb_ref[:, col0:col0 + MLSTM_WIDTH]
        for t in range(CONV_WIDTH):
            r0 = CONV_TAIL - (CONV_WIDTH - 1) + t
            out = out + cw_ref[t:t + 1, col0:col0 + MLSTM_WIDTH] * u_ref[r0:r0 + TM_IN, :]
        return out * jax.nn.sigmoid(out)

    rope_store(aq_ref, P_AQ)
    rope_store(ak_ref, P_AK)
    av_ref[...] = proj(P_AV, ATT_WIDTH)
    mq_ref[...] = (conv_silu(uq, P_MQ, 0) * (MLSTM_HEAD_DIM ** -0.5)).astype(BF16)
    mk_ref[...] = conv_silu(uk, P_MK, MLSTM_WIDTH).astype(BF16)
    mv_ref[...] = proj(P_MV, MLSTM_WIDTH).astype(BF16)
    mo_ref[...] = jax.nn.sigmoid(proj(P_MO, MLSTM_WIDTH))
    ga_ref[...] = jax.nn.sigmoid(proj(P_GA, D_MODEL))
    gb_ref[...] = jax.nn.sigmoid(proj(P_GB, D_MODEL))

    zg = proj(P_GT, LANES)
    logf = _log_sigmoid(zg)
    ri = lax.broadcasted_iota(jnp.int32, (MCHUNK, MCHUNK), 0)
    ci = lax.broadcasted_iota(jnp.int32, (MCHUNK, MCHUNK), 1)
    tri = (ci <= ri).astype(F32)
    is_input_gate = lax.broadcasted_iota(jnp.int32, (MCHUNK, LANES), 1) < MLSTM_HEADS
    for cc in range(TM_IN // MCHUNK):
        rows = slice(cc * MCHUNK, (cc + 1) * MCHUNK)
        bcum = jnp.dot(tri, logf[rows], precision=lax.Precision.HIGHEST, preferred_element_type=F32)
        gc = jnp.where(is_input_gate, zg[rows], bcum)
        gt_ref[rows, :] = gc
        gtt_ref[cc] = gc.T[0:GATE_ROWS, :]


def _inproj(x2d, norm_w, w_all, b_all, rope_c, rope_s, conv_w, conv_b, seq_len):
    T = x2d.shape[0]
    tile = lambda w: pl.BlockSpec((TM_IN, w), lambda i: (i, 0))
    out_shapes = (
        jax.ShapeDtypeStruct((T, ATT_WIDTH), F32),
        jax.ShapeDtypeStruct((T, ATT_WIDTH), F32),
        jax.ShapeDtypeStruct((T, ATT_WIDTH), F32),
        jax.ShapeDtypeStruct((T, MLSTM_WIDTH), BF16),
        jax.ShapeDtypeStruct((T, MLSTM_WIDTH), BF16),
        jax.ShapeDtypeStruct((T, MLSTM_WIDTH), BF16),
        jax.ShapeDtypeStruct((T, MLSTM_WIDTH), F32),
        jax.ShapeDtypeStruct((T, D_MODEL), F32),
        jax.ShapeDtypeStruct((T, D_MODEL), F32),
        jax.ShapeDtypeStruct((T, LANES), F32),
        jax.ShapeDtypeStruct((T // MCHUNK, GATE_ROWS, MCHUNK), F32),
    )
    out_specs = tuple(tile(s.shape[1]) for s in out_shapes[:-1]) + (
        pl.BlockSpec((TM_IN // MCHUNK, GATE_ROWS, MCHUNK), lambda i: (i, 0, 0)),)
    assert seq_len % TM_IN == 0
    return pl.pallas_call(
        functools.partial(_inproj_kernel, tiles_per_seq=seq_len // TM_IN),
        grid=(T // TM_IN,),
        in_specs=[tile(D_MODEL), _const_spec((1, D_MODEL)), _const_spec((D_MODEL, P_END)),
                  _const_spec((1, P_END)), tile(LANES), tile(LANES),
                  _const_spec((CONV_WIDTH, 2 * MLSTM_WIDTH)), _const_spec((1, 2 * MLSTM_WIDTH))],
        out_specs=out_specs,
        out_shape=out_shapes,
        scratch_shapes=[pltpu.VMEM((TM_IN + CONV_TAIL, MLSTM_WIDTH), F32),
                        pltpu.VMEM((TM_IN + CONV_TAIL, MLSTM_WIDTH), F32)],
        compiler_params=pltpu.CompilerParams(dimension_semantics=("arbitrary",),
                                             vmem_limit_bytes=VMEM_LIMIT),
        name="inproj",
    )(x2d, norm_w, w_all, b_all, rope_c, rope_s, conv_w, conv_b)


ATT_UNROLL = 16


def _attn_kernel(q_ref, k_ref, v_ref, o_ref,
                 kd1, vd1, qd4, kd4, vd4, qd16, kd16, vd16, tmp, acc_s, m_s, l_s, bias_s):
    j = pl.program_id(2)
    blk = ATT_SPAN
    nsub = {d: ATT_BLK // d // blk for d in DILATIONS}
    kv_bufs = ((1, kd1, vd1), (4, kd4, vd4), (16, kd16, vd16))

    row = lax.broadcasted_iota(jnp.int32, (2 * blk, 2 * blk), 0) % blk
    col = lax.broadcasted_iota(jnp.int32, (2 * blk, 2 * blk), 1)
    band = (col >= row) & (col <= row + ATT_SPAN)
    bias_s[0] = jnp.where(band, 0.0, NEG)
    bias_s[1] = jnp.where(band & (col >= blk), 0.0, NEG)

    @pl.when(j == 0)
    def _():
        for d, kd, vd in kv_bufs:
            n = ATT_BLK // d
            for r in range(d):
                base = r * (n + blk)
                kd[base:base + blk] = jnp.zeros((blk, LANES), BF16)
                vd[base:base + blk] = jnp.zeros((blk, LANES), BF16)

    @pl.when(j != 0)
    def _():
        for d, kd, vd in kv_bufs:
            n = ATT_BLK // d
            for r in range(d):
                base = r * (n + blk)
                kd[base:base + blk] = kd[base + n:base + n + blk]
                vd[base:base + blk] = vd[base + n:base + n + blk]

    def deinterleave(src_ref, dst1, dst4, dst16, is_kv):
        pad = blk if is_kv else 0
        if dst1 is not None:
            dst1[blk:blk + ATT_BLK] = src_ref[0].astype(BF16)
        n4 = ATT_BLK // 4
        for r4 in range(4):
            t4 = src_ref[0, pl.ds(r4, n4, stride=4), :]
            tmp[r4] = t4
            o4 = r4 * (n4 + pad) + pad
            dst4[o4:o4 + n4] = t4.astype(BF16)
        n16 = ATT_BLK // 16
        for r4 in range(4):
            for rr in range(4):
                o16 = (4 * rr + r4) * (n16 + pad) + pad
                dst16[o16:o16 + n16] = tmp[r4, pl.ds(rr, n16, stride=4), :].astype(BF16)

    deinterleave(q_ref, None, qd4, qd16, False)
    deinterleave(k_ref, kd1, kd4, kd16, True)
    deinterleave(v_ref, vd1, vd4, vd16, True)

    head_a = lax.broadcasted_iota(jnp.int32, (blk, LANES), 1) < ATT_HEAD_DIM

    def unit(q2, k2, v2, bias):
        zero = jnp.zeros_like(q2)
        qs = jnp.concatenate([jnp.where(head_a, q2, zero), jnp.where(head_a, zero, q2)], axis=0)
        s = lax.dot_general(qs, k2, (((1,), (1,)), ((), ())), preferred_element_type=F32) + bias
        m = jnp.max(s, axis=-1, keepdims=True)
        p = jnp.exp(s - m).astype(BF16)
        v_aug = jnp.concatenate([v2, jnp.ones_like(v2)], axis=1)
        pv = jnp.dot(p, v_aug, preferred_element_type=F32)
        acc = jnp.where(head_a, pv[:blk, :LANES], pv[blk:, :LANES])
        ll = jnp.where(head_a, pv[:blk, LANES:], pv[blk:, LANES:])
        mm = jnp.where(head_a, m[:blk], m[blk:])
        return acc, mm, ll

    first_blk = (j == 0).astype(jnp.int32)

    def body1(u, carry):
        r0 = pl.multiple_of(u * blk, blk)
        q2 = q_ref[0, pl.ds(r0, blk), :].astype(BF16)
        bias = bias_s[first_blk * (u == 0).astype(jnp.int32)]
        acc, mm, ll = unit(q2, kd1[pl.ds(r0, 2 * blk), :], vd1[pl.ds(r0, 2 * blk), :], bias)
        acc_s[0, pl.ds(r0, blk), :] = acc
        m_s[0, pl.ds(r0, blk), :] = mm
        l_s[0, pl.ds(r0, blk), :] = ll
        return carry

    lax.fori_loop(0, nsub[1], body1, 0, unroll=ATT_UNROLL)

    def make_body(g, d, qd, kd, vd):
        n = ATT_BLK // d

        def body(u, carry):
            r = u // nsub[d]
            sb = u % nsub[d]
            q0 = pl.multiple_of(u * blk, blk)
            k0 = pl.multiple_of(r * (n + blk) + sb * blk, blk)
            bias = bias_s[first_blk * (sb == 0).astype(jnp.int32)]
            acc, mm, ll = unit(qd[pl.ds(q0, blk), :], kd[pl.ds(k0, 2 * blk), :], vd[pl.ds(k0, 2 * blk), :], bias)
            t0 = sb * (blk * d) + r
            acc_s[g, pl.ds(t0, blk, stride=d), :] = acc
            m_s[g, pl.ds(t0, blk, stride=d), :] = mm
            l_s[g, pl.ds(t0, blk, stride=d), :] = ll
            return carry

        return body

    lax.fori_loop(0, ATT_BLK // blk, make_body(1, 4, qd4, kd4, vd4), 0, unroll=ATT_UNROLL)
    lax.fori_loop(0, ATT_BLK // blk, make_body(2, 16, qd16, kd16, vd16), 0, unroll=ATT_UNROLL)

    def combine(u, carry):
        r0 = pl.multiple_of(u * blk, blk)
        sl = pl.ds(r0, blk)
        m0, m1, m2 = m_s[0, sl, :], m_s[1, sl, :], m_s[2, sl, :]
        mx = jnp.maximum(jnp.maximum(m0, m1), m2)
        w0, w1, w2 = jnp.exp(m0 - mx), jnp.exp(m1 - mx), jnp.exp(m2 - mx)
        num = w0 * acc_s[0, sl, :] + w1 * acc_s[1, sl, :] + w2 * acc_s[2, sl, :]
        den = w0 * l_s[0, sl, :] + w1 * l_s[1, sl, :] + w2 * l_s[2, sl, :]
        o_ref[0, sl, :] = (num / den).astype(o_ref.dtype)
        return carry

    lax.fori_loop(0, ATT_BLK // blk, combine, 0, unroll=2)


def _attention(aq, ak, av):
    B, S, _ = aq.shape
    blk = ATT_SPAN
    cur = pl.BlockSpec((1, ATT_BLK, LANES), lambda b, hp, j: (b, j, hp))
    kv_rows = {d: d * (ATT_BLK // d + blk) for d in DILATIONS}
    scratch = [
        pltpu.VMEM((kv_rows[1], LANES), BF16), pltpu.VMEM((kv_rows[1], LANES), BF16),
        pltpu.VMEM((ATT_BLK, LANES), BF16),
        pltpu.VMEM((kv_rows[4], LANES), BF16), pltpu.VMEM((kv_rows[4], LANES), BF16),
        pltpu.VMEM((ATT_BLK, LANES), BF16),
        pltpu.VMEM((kv_rows[16], LANES), BF16), pltpu.VMEM((kv_rows[16], LANES), BF16),
        pltpu.VMEM((4, ATT_BLK // 4, LANES), F32),
        pltpu.VMEM((3, ATT_BLK, LANES), F32), pltpu.VMEM((3, ATT_BLK, LANES), F32),
        pltpu.VMEM((3, ATT_BLK, LANES), F32),
        pltpu.VMEM((2, 2 * blk, 2 * blk), F32),
    ]
    return pl.pallas_call(
        _attn_kernel,
        grid=(B, ATT_WIDTH // LANES, S // ATT_BLK),
        in_specs=[cur, cur, cur],
        out_specs=cur,
        out_shape=jax.ShapeDtypeStruct((B, S, ATT_WIDTH), BF16),
        scratch_shapes=scratch,
        compiler_params=pltpu.CompilerParams(dimension_semantics=("parallel", "parallel", "arbitrary"),
                                             vmem_limit_bytes=VMEM_LIMIT),
        name="dilated_attention",
    )(aq, ak, av)


def _log_sigmoid(x):
    return jnp.minimum(x, 0.0) - jnp.log(1.0 + jnp.exp(-jnp.abs(x)))


def _mlstm_kernel(q_ref, k_ref, v_ref, mo_ref, g_ref, gt_ref, nw_ref, y_ref, c_state, m_state):
    L = MCHUNK
    D = MLSTM_HEAD_DIM

    @pl.when(pl.program_id(1) == 0)
    def _():
        c_state[...] = jnp.zeros_like(c_state)
        m_state[...] = jnp.zeros_like(m_state)

    ri = lax.broadcasted_iota(jnp.int32, (L, L), 0)
    ci = lax.broadcasted_iota(jnp.int32, (L, L), 1)
    causal = ci <= ri
    ones = jnp.ones((L, D), BF16)

    units = [(cc, h) for cc in range(TM_MLSTM // L) for h in range(MLSTM_HEADS)]
    rows = lambda cc: slice(cc * L, (cc + 1) * L)
    cols = lambda h: slice(h * D, (h + 1) * D)
    fcol = lambda h: slice(MLSTM_HEADS + h, MLSTM_HEADS + h + 1)

    s_all = {u: lax.dot_general(q_ref[rows(u[0]), cols(u[1])], k_ref[rows(u[0]), cols(u[1])],
                                (((1,), (1,)), ((), ())), preferred_element_type=F32) for u in units}
    r_row, g_tot, pm, v_aug, intra, m_loc, kv = {}, {}, {}, {}, {}, {}, {}
    for u in units:
        cc, h = u
        r_row[u] = gt_ref[cc, h:h + 1, :] - gt_ref[cc, MLSTM_HEADS + h:MLSTM_HEADS + h + 1, :]
        g_tot[u] = g_ref[cc * L + L - 1:(cc + 1) * L, fcol(h)]
        pm[u] = jnp.max(jnp.where(causal, r_row[u], NEG), axis=-1, keepdims=True)
        p = (jnp.exp(jnp.where(causal, r_row[u] - pm[u], NEG)) * s_all[u]).astype(BF16)
        v_aug[u] = jnp.concatenate([v_ref[rows(cc), cols(h)], ones], axis=1)
        intra[u] = jnp.dot(p, v_aug[u], preferred_element_type=F32)
    for u in units:
        cc, h = u
        m_loc[u] = jnp.max(g_tot[u] + r_row[u], axis=-1, keepdims=True)
        kw_t = (k_ref[rows(cc), cols(h)].astype(F32).T
                * jnp.exp(g_tot[u] + r_row[u] - m_loc[u])).astype(BF16)
        kv[u] = jnp.dot(kw_t, v_aug[u], preferred_element_type=F32)

    c_aug = [c_state[h] for h in range(MLSTM_HEADS)]
    m_prev = [m_state[h:h + 1, 0:1] for h in range(MLSTM_HEADS)]
    for u in units:
        cc, h = u
        inter = jnp.dot(q_ref[rows(cc), cols(h)], c_aug[h].astype(BF16), preferred_element_type=F32)
        mm = jnp.maximum(pm[u], m_prev[h])
        nd = jnp.exp(pm[u] - mm) * intra[u] + jnp.exp(m_prev[h] - mm) * inter
        b_col = g_ref[rows(cc), fcol(h)]
        hh = nd[:, :D] / jnp.maximum(jnp.abs(nd[:, D:]), jnp.exp(-(b_col + mm)))
        hn = hh * lax.rsqrt(jnp.mean(hh * hh, axis=-1, keepdims=True) + NORM_EPS) * nw_ref[:, cols(h)]
        y_ref[rows(cc), cols(h)] = (mo_ref[rows(cc), cols(h)] * hn).astype(y_ref.dtype)

        m_new = jnp.maximum(g_tot[u] + m_prev[h], m_loc[u])
        c_aug[h] = jnp.exp(g_tot[u] + m_prev[h] - m_new) * c_aug[h] + jnp.exp(m_loc[u] - m_new) * kv[u]
        m_prev[h] = m_new
    for h in range(MLSTM_HEADS):
        c_state[h] = c_aug[h]
        m_state[h:h + 1, :] = jnp.broadcast_to(m_prev[h], (1, LANES))


def _mlstm(mq, mk, mv, mo_s, gates, gates_t, norm_w, batch):
    T = mq.shape[0]
    nt = T // batch // TM_MLSTM
    nch = TM_MLSTM // MCHUNK
    tile = lambda w: pl.BlockSpec((TM_MLSTM, w), lambda b, c: (b * nt + c, 0))
    return pl.pallas_call(
        _mlstm_kernel,
        grid=(batch, nt),
        in_specs=[tile(MLSTM_WIDTH), tile(MLSTM_WIDTH), tile(MLSTM_WIDTH), tile(MLSTM_WIDTH), tile(LANES),
                  pl.BlockSpec((nch, GATE_ROWS, MCHUNK), lambda b, c: (b * nt + c, 0, 0)),
                  pl.BlockSpec((1, MLSTM_WIDTH), lambda b, c: (0, 0))],
        out_specs=tile(MLSTM_WIDTH),
        out_shape=jax.ShapeDtypeStruct((T, MLSTM_WIDTH), BF16),
        scratch_shapes=[
            pltpu.VMEM((MLSTM_HEADS, MLSTM_HEAD_DIM, 2 * MLSTM_HEAD_DIM), F32),
            pltpu.VMEM((8, LANES), F32),
        ],
        compiler_params=pltpu.CompilerParams(dimension_semantics=("parallel", "arbitrary"),
                                             vmem_limit_bytes=VMEM_LIMIT),
        name="mlstm",
    )(mq, mk, mv, mo_s, gates, gates_t, norm_w)


def _post_kernel(x_ref, ya_ref, yb_ref, ga_ref, gb_ref, wpa_ref, wpm_ref, wo_ref, nw_ref, w1_ref, w2_ref,
                 fw_ref, o_ref, *, final_norm):
    pa = jnp.dot(ya_ref[...], wpa_ref[...], preferred_element_type=F32)
    pb = jnp.dot(yb_ref[...], wpm_ref[...], preferred_element_type=F32)
    mixed = (ga_ref[...] * pa + gb_ref[...] * pb).astype(BF16)
    x1 = x_ref[...] + jnp.dot(mixed, wo_ref[...], preferred_element_type=F32)
    var = jnp.mean(x1 * x1, axis=-1, keepdims=True)
    h2 = (x1 * lax.rsqrt(var + NORM_EPS) * nw_ref[...]).astype(BF16)
    acc = x1
    ff_chunk = D_MODEL
    for c in range(D_FF // ff_chunk):
        cs = slice(c * ff_chunk, (c + 1) * ff_chunk)
        u = jnp.maximum(jnp.dot(h2, w1_ref[:, cs], preferred_element_type=F32), 0.0)
        acc = acc + jnp.dot((u * u).astype(BF16), w2_ref[cs, :], preferred_element_type=F32)
    if final_norm:
        var = jnp.mean(acc * acc, axis=-1, keepdims=True)
        acc = acc * lax.rsqrt(var + NORM_EPS) * fw_ref[...]
    o_ref[...] = acc


def _post(x2d, ya, yb, ga_s, gb_s, wpa, wpm, wo, norm_w, w1, w2, final_w, final_norm):
    T = x2d.shape[0]
    tile = lambda w: pl.BlockSpec((TM_POST, w), lambda i: (i, 0))
    return pl.pallas_call(
        functools.partial(_post_kernel, final_norm=final_norm),
        grid=(T // TM_POST,),
        in_specs=[tile(D_MODEL), tile(ATT_WIDTH), tile(MLSTM_WIDTH), tile(D_MODEL), tile(D_MODEL),
                  _const_spec((ATT_WIDTH, D_MODEL)), _const_spec((MLSTM_WIDTH, D_MODEL)),
                  _const_spec((D_MODEL, D_MODEL)), _const_spec((1, D_MODEL)),
                  _const_spec((D_MODEL, D_FF)), _const_spec((D_FF, D_MODEL)), _const_spec((1, D_MODEL))],
        out_specs=tile(D_MODEL),
        out_shape=jax.ShapeDtypeStruct((T, D_MODEL), F32),
        compiler_params=pltpu.CompilerParams(dimension_semantics=("parallel",),
                                             vmem_limit_bytes=VMEM_LIMIT),
        name="post",
    )(x2d, ya, yb, ga_s, gb_s, wpa, wpm, wo, norm_w, w1, w2, final_w)


def _pack_in_proj(w_in, b_in):
    scale = ATT_HEAD_DIM ** -0.5
    pad_w = jnp.zeros((D_MODEL, LANES - 2 * MLSTM_HEADS), w_in.dtype)
    pad_b = jnp.zeros((LANES - 2 * MLSTM_HEADS,), b_in.dtype)
    w_all = jnp.concatenate([w_in[:, COL_AQ:COL_AK] * scale, w_in[:, COL_AK:COL_MI],
                             w_in[:, COL_GA:D_IN], w_in[:, COL_MI:COL_GA], pad_w], axis=1)
    b_all = jnp.concatenate([b_in[COL_AQ:COL_AK] * scale, b_in[COL_AK:COL_MI],
                             b_in[COL_GA:D_IN], b_in[COL_MI:COL_GA], pad_b])
    return w_all.astype(BF16), b_all.reshape(1, P_END).astype(F32)


def kernel(x, positions, norm_mix_w, w_in, b_in, conv_w, conv_b, mlstm_norm_w, w_proj_att, w_proj_mlstm,
           w_out, norm_mlp_w, w_ff1, w_ff2, final_norm_w):
    B, S, D = x.shape
    T = B * S
    depth = w_in.shape[0]
    assert D == D_MODEL and S % ATT_BLK == 0 and T % TM_IN == 0 and T % TM_POST == 0
    assert math.isclose(ATT_HEAD_DIM ** -0.5, 0.125)
    rope_c, rope_s = _rope_tables(positions)
    x2d = x.reshape(T, D)
    final_w = final_norm_w.reshape(1, D).astype(F32)
    for l in range(depth):
        w_all, b_all = _pack_in_proj(w_in[l], b_in[l])
        aq, ak, av, mq, mk, mv, mo_s, ga_s, gb_s, gates, gates_t = _inproj(
            x2d, norm_mix_w[l].reshape(1, D), w_all, b_all, rope_c, rope_s,
            conv_w[l], conv_b[l].reshape(1, -1), S)
        ya = _attention(aq.reshape(B, S, ATT_WIDTH), ak.reshape(B, S, ATT_WIDTH), av.reshape(B, S, ATT_WIDTH))
        yb = _mlstm(mq, mk, mv, mo_s, gates, gates_t, mlstm_norm_w[l].reshape(1, -1), B)
        x2d = _post(x2d, ya.reshape(T, ATT_WIDTH), yb, ga_s, gb_s,
                    w_proj_att[l].astype(BF16), w_proj_mlstm[l].astype(BF16), w_out[l].astype(BF16),
                    norm_mlp_w[l].reshape(1, D), w_ff1[l].astype(BF16), w_ff2[l].astype(BF16),
                    final_w, final_norm=(l == depth - 1))
    return x2d.reshape(B, S, D)
```

```python
import functools
import math

import jax
import jax.numpy as jnp
from jax import lax
from jax.experimental import pallas as pl
from jax.experimental.pallas import tpu as pltpu

F32 = jnp.float32
BF16 = jnp.bfloat16

D_MODEL = 1024
ATT_HEADS = 8
ATT_HEAD_DIM = 64
ATT_WIDTH = ATT_HEADS * ATT_HEAD_DIM
ATT_SPAN = 128
DILATIONS = (1, 4, 16)
ROPE_THETA = 500000.0
ROPE_DIM = ATT_HEAD_DIM // 4
MLSTM_HEADS = 4
MLSTM_HEAD_DIM = 128
MLSTM_WIDTH = MLSTM_HEADS * MLSTM_HEAD_DIM
CONV_WIDTH = 4
D_FF = 4 * D_MODEL
NORM_EPS = 1e-6

COL_AQ = 0
COL_AK = COL_AQ + ATT_WIDTH
COL_AV = COL_AK + ATT_WIDTH
COL_MQ = COL_AV + ATT_WIDTH
COL_MK = COL_MQ + MLSTM_WIDTH
COL_MV = COL_MK + MLSTM_WIDTH
COL_MO = COL_MV + MLSTM_WIDTH
COL_MI = COL_MO + MLSTM_WIDTH
COL_MF = COL_MI + MLSTM_HEADS
COL_GA = COL_MF + MLSTM_HEADS
COL_GB = COL_GA + D_MODEL
D_IN = COL_GB + D_MODEL

LANES = 128
P_AQ, P_AK, P_AV = 0, 512, 1024
P_MQ, P_MK, P_MV, P_MO = 1536, 2048, 2560, 3072
P_GA, P_GB = 3584, 4608
P_GT = 5632
P_END = P_GT + LANES

TM_IN = 512
TM_POST = 512
ATT_BLK = 2048
MCHUNK = 128
TM_MLSTM = 512
VMEM_LIMIT = 56 * 1024 * 1024
NEG = -1e30


def _const_spec(shape):
    nd = len(shape)
    return pl.BlockSpec(shape, lambda *_: (0,) * nd, pipeline_mode=pl.Buffered(1))


ROPE_HALF = ROPE_DIM // 2
ROPE_PACK = LANES // ROPE_HALF
ROPE_ROWS = 128


def _rope_table_kernel(pos_ref, invf_ref, c_ref, s_ref):
    ang = pos_ref[...] * invf_ref[...]
    cosx = jnp.cos(ang)
    sinx = jnp.sin(ang)
    src = lax.broadcasted_iota(jnp.int32, (LANES, LANES), 0)
    dst = lax.broadcasted_iota(jnp.int32, (LANES, LANES), 1)
    in_head = dst % ATT_HEAD_DIM
    rotary = in_head < ROPE_DIM
    sign = jnp.where(in_head < ROPE_HALF, -1.0, 1.0)
    one_elsewhere = jnp.where(lax.broadcasted_iota(jnp.int32, (1, LANES), 1) % ATT_HEAD_DIM < ROPE_DIM, 0.0, 1.0)
    for r in range(ROPE_PACK):
        pick = rotary & (src == r * ROPE_HALF + dst % ROPE_HALF)
        c_sel = jnp.where(pick, 1.0, 0.0)
        s_sel = jnp.where(pick, sign, 0.0)
        c_ref[pl.ds(r, ROPE_ROWS, stride=ROPE_PACK), :] = jnp.dot(
            cosx, c_sel, precision=lax.Precision.HIGHEST, preferred_element_type=F32) + one_elsewhere
        s_ref[pl.ds(r, ROPE_ROWS, stride=ROPE_PACK), :] = jnp.dot(
            sinx, s_sel, precision=lax.Precision.HIGHEST, preferred_element_type=F32)


def _rope_tables(positions):
    T = positions.size
    assert T % (ROPE_PACK * ROPE_ROWS) == 0
    pos = jnp.repeat(positions.astype(F32).reshape(T // ROPE_PACK, ROPE_PACK), ROPE_HALF, axis=1)
    inv_freq = ROPE_THETA ** (-jnp.arange(0, ROPE_DIM, 2, dtype=F32) / ROPE_DIM)
    invf = jnp.tile(inv_freq, ROPE_PACK).reshape(1, LANES)
    table = jax.ShapeDtypeStruct((T, LANES), F32)
    out_spec = pl.BlockSpec((ROPE_PACK * ROPE_ROWS, LANES), lambda i: (i, 0))
    return pl.pallas_call(
        _rope_table_kernel,
        grid=(T // (ROPE_PACK * ROPE_ROWS),),
        in_specs=[pl.BlockSpec((ROPE_ROWS, LANES), lambda i: (i, 0)), pl.BlockSpec((1, LANES), lambda i: (0, 0))],
        out_specs=(out_spec, out_spec),
        out_shape=(table, table),
        compiler_params=pltpu.CompilerParams(dimension_semantics=("parallel",)),
        name="rope_tables",
    )(pos, invf)


CONV_TAIL = 8
GATE_ROWS = 8


def _log_sigmoid(x):
    return jnp.minimum(x, 0.0) - jnp.log(1.0 + jnp.exp(-jnp.abs(x)))


def _inproj_kernel(x_ref, nw_ref, w_ref, b_ref, c_ref, s_ref, cw_ref, cb_ref,
                   aq_ref, ak_ref, av_ref, mq_ref, mk_ref, mv_ref, mo_ref, ga_ref, gb_ref, gt_ref, gtt_ref,
                   uq, uk, *, tiles_per_seq):
    @pl.when(pl.program_id(0) % tiles_per_seq == 0)
    def _():
        uq[TM_IN:TM_IN + CONV_TAIL, :] = jnp.zeros((CONV_TAIL, MLSTM_WIDTH), F32)
        uk[TM_IN:TM_IN + CONV_TAIL, :] = jnp.zeros((CONV_TAIL, MLSTM_WIDTH), F32)

    x = x_ref[...]
    var = jnp.mean(x * x, axis=-1, keepdims=True)
    h = (x * lax.rsqrt(var + NORM_EPS) * nw_ref[...]).astype(BF16)

    def proj(lo, width):
        return jnp.dot(h, w_ref[:, lo:lo + width], preferred_element_type=F32) + b_ref[:, lo:lo + width]

    cos = c_ref[...]
    sin = s_ref[...]
    lane = lax.broadcasted_iota(jnp.int32, cos.shape, 1)
    first_half = (lane % ATT_HEAD_DIM) < (ROPE_DIM // 2)

    def rope_store(dst_ref, lo):
        zz = proj(lo, ATT_WIDTH)
        for j in range(ATT_WIDTH // LANES):
            z = zz[:, j * LANES:(j + 1) * LANES]
            partner = jnp.where(first_half,
                                pltpu.roll(z, LANES - ROPE_DIM // 2, axis=1),
                                pltpu.roll(z, ROPE_DIM // 2, axis=1))
            dst_ref[:, j * LANES:(j + 1) * LANES] = z * cos + partner * sin

    def conv_silu_store(dst_ref, u_ref, lo, col0, scale):
        u_ref[0:CONV_TAIL, :] = u_ref[TM_IN:TM_IN + CONV_TAIL, :]
        u_ref[CONV_TAIL:CONV_TAIL + TM_IN, :] = proj(lo, MLSTM_WIDTH)
        for j in range(MLSTM_WIDTH // LANES):
            ls = slice(j * LANES, (j + 1) * LANES)
            ws = slice(col0 + j * LANES, col0 + (j + 1) * LANES)
            out = cb_ref[:, ws]
            for t in range(CONV_WIDTH):
                r0 = CONV_TAIL - (CONV_WIDTH - 1) + t
                out = out + cw_ref[t:t + 1, ws] * u_ref[r0:r0 + TM_IN, ls]
            out = out * jax.nn.sigmoid(out)
            dst_ref[:, ls] = (out if scale is None else out * scale).astype(dst_ref.dtype)

    rope_store(aq_ref, P_AQ)
    rope_store(ak_ref, P_AK)
    av_ref[...] = proj(P_AV, ATT_WIDTH)
    conv_silu_store(mq_ref, uq, P_MQ, 0, MLSTM_HEAD_DIM ** -0.5)
    conv_silu_store(mk_ref, uk, P_MK, MLSTM_WIDTH, None)
    mv_ref[...] = proj(P_MV, MLSTM_WIDTH).astype(mv_ref.dtype)
    mo_ref[...] = proj(P_MO, MLSTM_WIDTH).astype(mo_ref.dtype)
    ga_ref[...] = proj(P_GA, D_MODEL).astype(ga_ref.dtype)
    gb_ref[...] = proj(P_GB, D_MODEL).astype(gb_ref.dtype)

    zg = proj(P_GT, LANES)
    logf = _log_sigmoid(zg)
    ri = lax.broadcasted_iota(jnp.int32, (MCHUNK, MCHUNK), 0)
    ci = lax.broadcasted_iota(jnp.int32, (MCHUNK, MCHUNK), 1)
    tri = (ci <= ri).astype(F32)
    is_input_gate = lax.broadcasted_iota(jnp.int32, (MCHUNK, LANES), 1) < MLSTM_HEADS
    for cc in range(TM_IN // MCHUNK):
        rows = slice(cc * MCHUNK, (cc + 1) * MCHUNK)
        bcum = jnp.dot(tri, logf[rows], precision=lax.Precision.HIGHEST, preferred_element_type=F32)
        gc = jnp.where(is_input_gate, zg[rows], bcum)
        gt_ref[rows, :] = gc
        gtt_ref[cc] = gc.T[0:GATE_ROWS, :]


def _inproj(x2d, norm_w, w_all, b_all, rope_c, rope_s, conv_w, conv_b, seq_len):
    T = x2d.shape[0]
    tile = lambda w: pl.BlockSpec((TM_IN, w), lambda i: (i, 0))
    out_shapes = (
        jax.ShapeDtypeStruct((T, ATT_WIDTH), F32),
        jax.ShapeDtypeStruct((T, ATT_WIDTH), F32),
        jax.ShapeDtypeStruct((T, ATT_WIDTH), F32),
        jax.ShapeDtypeStruct((T, MLSTM_WIDTH), BF16),
        jax.ShapeDtypeStruct((T, MLSTM_WIDTH), BF16),
        jax.ShapeDtypeStruct((T, MLSTM_WIDTH), BF16),
        jax.ShapeDtypeStruct((T, MLSTM_WIDTH), BF16),
        jax.ShapeDtypeStruct((T, D_MODEL), BF16),
        jax.ShapeDtypeStruct((T, D_MODEL), BF16),
        jax.ShapeDtypeStruct((T, LANES), F32),
        jax.ShapeDtypeStruct((T // MCHUNK, GATE_ROWS, MCHUNK), F32),
    )
    out_specs = tuple(tile(s.shape[1]) for s in out_shapes[:-1]) + (
        pl.BlockSpec((TM_IN // MCHUNK, GATE_ROWS, MCHUNK), lambda i: (i, 0, 0)),)
    assert seq_len % TM_IN == 0
    return pl.pallas_call(
        functools.partial(_inproj_kernel, tiles_per_seq=seq_len // TM_IN),
        grid=(T // TM_IN,),
        in_specs=[tile(D_MODEL), _const_spec((1, D_MODEL)), _const_spec((D_MODEL, P_END)),
                  _const_spec((1, P_END)), tile(LANES), tile(LANES),
                  _const_spec((CONV_WIDTH, 2 * MLSTM_WIDTH)), _const_spec((1, 2 * MLSTM_WIDTH))],
        out_specs=out_specs,
        out_shape=out_shapes,
        scratch_shapes=[pltpu.VMEM((TM_IN + CONV_TAIL, MLSTM_WIDTH), F32),
                        pltpu.VMEM((TM_IN + CONV_TAIL, MLSTM_WIDTH), F32)],
        compiler_params=pltpu.CompilerParams(dimension_semantics=("arbitrary",),
                                             vmem_limit_bytes=VMEM_LIMIT),
        name="inproj",
    )(x2d, norm_w, w_all, b_all, rope_c, rope_s, conv_w, conv_b)


ATT_UNROLL = 16


def _attn_kernel(q_ref, k_ref, v_ref, o_ref,
                 kd1, vd1, qd4, kd4, vd4, qd16, kd16, vd16, tmp, acc_s, m_s, l_s, bias_s):
    j = pl.program_id(2)
    blk = ATT_SPAN
    nsub = {d: ATT_BLK // d // blk for d in DILATIONS}
    kv_bufs = ((1, kd1, vd1), (4, kd4, vd4), (16, kd16, vd16))

    row = lax.broadcasted_iota(jnp.int32, (2 * blk, 2 * blk), 0) % blk
    col = lax.broadcasted_iota(jnp.int32, (2 * blk, 2 * blk), 1)
    band = (col >= row) & (col <= row + ATT_SPAN)
    bias_s[0] = jnp.where(band, 0.0, NEG)
    bias_s[1] = jnp.where(band & (col >= blk), 0.0, NEG)

    @pl.when(j == 0)
    def _():
        for d, kd, vd in kv_bufs:
            n = ATT_BLK // d
            for r in range(d):
                base = r * (n + blk)
                kd[base:base + blk] = jnp.zeros((blk, LANES), BF16)
                vd[base:base + blk] = jnp.zeros((blk, LANES), BF16)

    @pl.when(j != 0)
    def _():
        for d, kd, vd in kv_bufs:
            n = ATT_BLK // d
            for r in range(d):
                base = r * (n + blk)
                kd[base:base + blk] = kd[base + n:base + n + blk]
                vd[base:base + blk] = vd[base + n:base + n + blk]

    def deinterleave(src_ref, dst1, dst4, dst16, is_kv):
        pad = blk if is_kv else 0
        if dst1 is not None:
            dst1[blk:blk + ATT_BLK] = src_ref[0].astype(BF16)
        n4 = ATT_BLK // 4
        for r4 in range(4):
            t4 = src_ref[0, pl.ds(r4, n4, stride=4), :]
            tmp[r4] = t4
            o4 = r4 * (n4 + pad) + pad
            dst4[o4:o4 + n4] = t4.astype(BF16)
        n16 = ATT_BLK // 16
        for r4 in range(4):
            for rr in range(4):
                o16 = (4 * rr + r4) * (n16 + pad) + pad
                dst16[o16:o16 + n16] = tmp[r4, pl.ds(rr, n16, stride=4), :].astype(BF16)

    deinterleave(q_ref, None, qd4, qd16, False)
    deinterleave(k_ref, kd1, kd4, kd16, True)
    deinterleave(v_ref, vd1, vd4, vd16, True)

    head_a = lax.broadcasted_iota(jnp.int32, (blk, LANES), 1) < ATT_HEAD_DIM

    def unit(q2, k2, v2, bias):
        zero = jnp.zeros_like(q2)
        qs = jnp.concatenate([jnp.where(head_a, q2, zero), jnp.where(head_a, zero, q2)], axis=0)
        s = lax.dot_general(qs, k2, (((1,), (1,)), ((), ())), preferred_element_type=F32) + bias
        m = jnp.max(s, axis=-1, keepdims=True)
        p = jnp.exp(s - m).astype(BF16)
        v_aug = jnp.concatenate([v2, jnp.ones_like(v2)], axis=1)
        pv = jnp.dot(p, v_aug, preferred_element_type=F32)
        acc = jnp.where(head_a, pv[:blk, :LANES], pv[blk:, :LANES])
        ll = jnp.where(head_a, pv[:blk, LANES:], pv[blk:, LANES:])
        mm = jnp.where(head_a, m[:blk], m[blk:])
        return acc, mm, ll

    first_blk = jnp.where(j == 0, 1, 0)

    def body1(u, carry):
        r0 = pl.multiple_of(u * blk, blk)
        q2 = q_ref[0, pl.ds(r0, blk), :].astype(BF16)
        bias = bias_s[jnp.where(u == 0, first_blk, 0)]
        acc, mm, ll = unit(q2, kd1[pl.ds(r0, 2 * blk), :], vd1[pl.ds(r0, 2 * blk), :], bias)
        acc_s[0, pl.ds(r0, blk), :] = acc
        m_s[0, pl.ds(r0, blk), :] = mm
        l_s[0, pl.ds(r0, blk), :] = ll
        return carry

    lax.fori_loop(0, nsub[1], body1, 0, unroll=ATT_UNROLL)

    def make_body(g, d, qd, kd, vd):
        n = ATT_BLK // d

        def body(u, carry):
            r = u // nsub[d]
            sb = u % nsub[d]
            q0 = pl.multiple_of(u * blk, blk)
            k0 = pl.multiple_of(r * (n + blk) + sb * blk, blk)
            bias = bias_s[jnp.where(sb == 0, first_blk, 0)]
            acc, mm, ll = unit(qd[pl.ds(q0, blk), :], kd[pl.ds(k0, 2 * blk), :], vd[pl.ds(k0, 2 * blk), :], bias)
            t0 = sb * (blk * d) + r
            acc_s[g, pl.ds(t0, blk, stride=d), :] = acc
            m_s[g, pl.ds(t0, blk, stride=d), :] = mm
            l_s[g, pl.ds(t0, blk, stride=d), :] = ll
            return carry

        return body

    lax.fori_loop(0, ATT_BLK // blk, make_body(1, 4, qd4, kd4, vd4), 0, unroll=ATT_UNROLL)
    lax.fori_loop(0, ATT_BLK // blk, make_body(2, 16, qd16, kd16, vd16), 0, unroll=ATT_UNROLL)

    def combine(u, carry):
        r0 = pl.multiple_of(u * blk, blk)
        sl = pl.ds(r0, blk)
        m0, m1, m2 = m_s[0, sl, :], m_s[1, sl, :], m_s[2, sl, :]
        mx = jnp.maximum(jnp.maximum(m0, m1), m2)
        w0, w1, w2 = jnp.exp(m0 - mx), jnp.exp(m1 - mx), jnp.exp(m2 - mx)
        num = w0 * acc_s[0, sl, :] + w1 * acc_s[1, sl, :] + w2 * acc_s[2, sl, :]
        den = w0 * l_s[0, sl, :] + w1 * l_s[1, sl, :] + w2 * l_s[2, sl, :]
        o_ref[0, sl, :] = (num / den).astype(o_ref.dtype)
        return carry

    lax.fori_loop(0, ATT_BLK // blk, combine, 0, unroll=2)


def _attention(aq, ak, av):
    B, S, _ = aq.shape
    blk = ATT_SPAN
    cur = pl.BlockSpec((1, ATT_BLK, LANES), lambda b, hp, j: (b, j, hp))
    kv_rows = {d: d * (ATT_BLK // d + blk) for d in DILATIONS}
    scratch = [
        pltpu.VMEM((kv_rows[1], LANES), BF16), pltpu.VMEM((kv_rows[1], LANES), BF16),
        pltpu.VMEM((ATT_BLK, LANES), BF16),
        pltpu.VMEM((kv_rows[4], LANES), BF16), pltpu.VMEM((kv_rows[4], LANES), BF16),
        pltpu.VMEM((ATT_BLK, LANES), BF16),
        pltpu.VMEM((kv_rows[16], LANES), BF16), pltpu.VMEM((kv_rows[16], LANES), BF16),
        pltpu.VMEM((4, ATT_BLK // 4, LANES), F32),
        pltpu.VMEM((3, ATT_BLK, LANES), F32), pltpu.VMEM((3, ATT_BLK, LANES), F32),
        pltpu.VMEM((3, ATT_BLK, LANES), F32),
        pltpu.VMEM((2, 2 * blk, 2 * blk), F32),
    ]
    return pl.pallas_call(
        _attn_kernel,
        grid=(B, ATT_WIDTH // LANES, S // ATT_BLK),
        in_specs=[cur, cur, cur],
        out_specs=cur,
        out_shape=jax.ShapeDtypeStruct((B, S, ATT_WIDTH), BF16),
        scratch_shapes=scratch,
        compiler_params=pltpu.CompilerParams(dimension_semantics=("parallel", "parallel", "arbitrary"),
                                             vmem_limit_bytes=VMEM_LIMIT),
        name="dilated_attention",
    )(aq, ak, av)


def _mlstm_kernel(q_ref, k_ref, v_ref, mo_ref, g_ref, gt_ref, nw_ref, y_ref, c_state, m_state):
    L = MCHUNK
    D = MLSTM_HEAD_DIM

    @pl.when(pl.program_id(1) == 0)
    def _():
        c_state[...] = jnp.zeros_like(c_state)
        m_state[...] = jnp.zeros_like(m_state)

    ri = lax.broadcasted_iota(jnp.int32, (L, L), 0)
    ci = lax.broadcasted_iota(jnp.int32, (L, L), 1)
    causal = ci <= ri
    ones = jnp.ones((L, D), BF16)

    units = [(cc, h) for cc in range(TM_MLSTM // L) for h in range(MLSTM_HEADS)]
    rows = lambda cc: slice(cc * L, (cc + 1) * L)
    cols = lambda h: slice(h * D, (h + 1) * D)
    fcol = lambda h: slice(MLSTM_HEADS + h, MLSTM_HEADS + h + 1)

    s_all = {u: lax.dot_general(q_ref[rows(u[0]), cols(u[1])], k_ref[rows(u[0]), cols(u[1])],
                                (((1,), (1,)), ((), ())), preferred_element_type=F32) for u in units}
    r_row, g_tot, pm, v_aug, intra, m_loc, kv = {}, {}, {}, {}, {}, {}, {}
    for u in units:
        cc, h = u
        r_row[u] = gt_ref[cc, h:h + 1, :] - gt_ref[cc, MLSTM_HEADS + h:MLSTM_HEADS + h + 1, :]
        g_tot[u] = g_ref[cc * L + L - 1:(cc + 1) * L, fcol(h)]
        pm[u] = jnp.max(jnp.where(causal, r_row[u], NEG), axis=-1, keepdims=True)
        p = (jnp.exp(jnp.where(causal, r_row[u] - pm[u], NEG)) * s_all[u]).astype(BF16)
        v_aug[u] = jnp.concatenate([v_ref[rows(cc), cols(h)], ones], axis=1)
        intra[u] = jnp.dot(p, v_aug[u], preferred_element_type=F32)
    for u in units:
        cc, h = u
        m_loc[u] = jnp.max(g_tot[u] + r_row[u], axis=-1, keepdims=True)
        kw_t = (k_ref[rows(cc), cols(h)].astype(F32).T
                * jnp.exp(g_tot[u] + r_row[u] - m_loc[u])).astype(BF16)
        kv[u] = jnp.dot(kw_t, v_aug[u], preferred_element_type=F32)

    c_aug = [c_state[h] for h in range(MLSTM_HEADS)]
    m_prev = [m_state[h:h + 1, 0:1] for h in range(MLSTM_HEADS)]
    for u in units:
        cc, h = u
        inter = jnp.dot(q_ref[rows(cc), cols(h)], c_aug[h].astype(BF16), preferred_element_type=F32)
        mm = jnp.maximum(pm[u], m_prev[h])
        nd = jnp.exp(pm[u] - mm) * intra[u] + jnp.exp(m_prev[h] - mm) * inter
        b_col = g_ref[rows(cc), fcol(h)]
        hh = nd[:, :D] / jnp.maximum(jnp.abs(nd[:, D:]), jnp.exp(-(b_col + mm)))
        hn = hh * lax.rsqrt(jnp.mean(hh * hh, axis=-1, keepdims=True) + NORM_EPS) * nw_ref[:, cols(h)]
        o_gate = jax.nn.sigmoid(mo_ref[rows(cc), cols(h)].astype(F32))
        y_ref[rows(cc), cols(h)] = (o_gate * hn).astype(y_ref.dtype)

        m_new = jnp.maximum(g_tot[u] + m_prev[h], m_loc[u])
        c_aug[h] = jnp.exp(g_tot[u] + m_prev[h] - m_new) * c_aug[h] + jnp.exp(m_loc[u] - m_new) * kv[u]
        m_prev[h] = m_new
    for h in range(MLSTM_HEADS):
        c_state[h] = c_aug[h]
        m_state[h:h + 1, :] = jnp.broadcast_to(m_prev[h], (1, LANES))


def _mlstm(mq, mk, mv, mo, gates, gates_t, norm_w, batch):
    T = mq.shape[0]
    nt = T // batch // TM_MLSTM
    nch = TM_MLSTM // MCHUNK
    tile = lambda w: pl.BlockSpec((TM_MLSTM, w), lambda b, c: (b * nt + c, 0))
    return pl.pallas_call(
        _mlstm_kernel,
        grid=(batch, nt),
        in_specs=[tile(MLSTM_WIDTH), tile(MLSTM_WIDTH), tile(MLSTM_WIDTH), tile(MLSTM_WIDTH), tile(LANES),
                  pl.BlockSpec((nch, GATE_ROWS, MCHUNK), lambda b, c: (b * nt + c, 0, 0)),
                  pl.BlockSpec((1, MLSTM_WIDTH), lambda b, c: (0, 0))],
        out_specs=tile(MLSTM_WIDTH),
        out_shape=jax.ShapeDtypeStruct((T, MLSTM_WIDTH), BF16),
        scratch_shapes=[
            pltpu.VMEM((MLSTM_HEADS, MLSTM_HEAD_DIM, 2 * MLSTM_HEAD_DIM), F32),
            pltpu.VMEM((8, LANES), F32),
        ],
        compiler_params=pltpu.CompilerParams(dimension_semantics=("parallel", "arbitrary"),
                                             vmem_limit_bytes=VMEM_LIMIT),
        name="mlstm",
    )(mq, mk, mv, mo, gates, gates_t, norm_w)


def _post_kernel(x_ref, ya_ref, yb_ref, ga_ref, gb_ref, wpa_ref, wpm_ref, wo_ref, nw_ref, w1_ref, w2_ref,
                 fw_ref, o_ref, *, final_norm):
    pa = jnp.dot(ya_ref[...], wpa_ref[...], preferred_element_type=F32)
    pb = jnp.dot(yb_ref[...], wpm_ref[...], preferred_element_type=F32)
    mixed = (jax.nn.sigmoid(ga_ref[...].astype(F32)) * pa
             + jax.nn.sigmoid(gb_ref[...].astype(F32)) * pb).astype(BF16)
    x1 = x_ref[...] + jnp.dot(mixed, wo_ref[...], preferred_element_type=F32)
    var = jnp.mean(x1 * x1, axis=-1, keepdims=True)
    h2 = (x1 * lax.rsqrt(var + NORM_EPS) * nw_ref[...]).astype(BF16)
    acc = x1
    ff_chunk = D_MODEL
    for c in range(D_FF // ff_chunk):
        cs = slice(c * ff_chunk, (c + 1) * ff_chunk)
        u = jnp.maximum(jnp.dot(h2, w1_ref[:, cs], preferred_element_type=F32), 0.0)
        acc = acc + jnp.dot((u * u).astype(BF16), w2_ref[cs, :], preferred_element_type=F32)
    if final_norm:
        var = jnp.mean(acc * acc, axis=-1, keepdims=True)
        acc = acc * lax.rsqrt(var + NORM_EPS) * fw_ref[...]
    o_ref[...] = acc


def _post(x2d, ya, yb, ga, gb, wpa, wpm, wo, norm_w, w1, w2, final_w, final_norm):
    T = x2d.shape[0]
    tile = lambda w: pl.BlockSpec((TM_POST, w), lambda i: (i, 0))
    return pl.pallas_call(
        functools.partial(_post_kernel, final_norm=final_norm),
        grid=(T // TM_POST,),
        in_specs=[tile(D_MODEL), tile(ATT_WIDTH), tile(MLSTM_WIDTH), tile(D_MODEL), tile(D_MODEL),
                  _const_spec((ATT_WIDTH, D_MODEL)), _const_spec((MLSTM_WIDTH, D_MODEL)),
                  _const_spec((D_MODEL, D_MODEL)), _const_spec((1, D_MODEL)),
                  _const_spec((D_MODEL, D_FF)), _const_spec((D_FF, D_MODEL)), _const_spec((1, D_MODEL))],
        out_specs=tile(D_MODEL),
        out_shape=jax.ShapeDtypeStruct((T, D_MODEL), F32),
        compiler_params=pltpu.CompilerParams(dimension_semantics=("parallel",),
                                             vmem_limit_bytes=VMEM_LIMIT),
        name="post",
    )(x2d, ya, yb, ga, gb, wpa, wpm, wo, norm_w, w1, w2, final_w)


def _pack_in_proj(w_in, b_in):
    scale = ATT_HEAD_DIM ** -0.5
    pad_w = jnp.zeros((D_MODEL, LANES - 2 * MLSTM_HEADS), w_in.dtype)
    pad_b = jnp.zeros((LANES - 2 * MLSTM_HEADS,), b_in.dtype)
    w_all = jnp.concatenate([w_in[:, COL_AQ:COL_AK] * scale, w_in[:, COL_AK:COL_MI],
                             w_in[:, COL_GA:D_IN], w_in[:, COL_MI:COL_GA], pad_w], axis=1)
    b_all = jnp.concatenate([b_in[COL_AQ:COL_AK] * scale, b_in[COL_AK:COL_MI],
                             b_in[COL_GA:D_IN], b_in[COL_MI:COL_GA], pad_b])
    return w_all.astype(BF16), b_all.reshape(1, P_END).astype(F32)


def kernel(x, positions, norm_mix_w, w_in, b_in, conv_w, conv_b, mlstm_norm_w, w_proj_att, w_proj_mlstm,
           w_out, norm_mlp_w, w_ff1, w_ff2, final_norm_w):
    B, S, D = x.shape
    T = B * S
    depth = w_in.shape[0]
    assert D == D_MODEL and S % ATT_BLK == 0 and T % TM_IN == 0 and T % TM_POST == 0
    assert math.isclose(ATT_HEAD_DIM ** -0.5, 0.125)
    rope_c, rope_s = _rope_tables(positions)
    x2d = x.reshape(T, D)
    final_w = final_norm_w.reshape(1, D).astype(F32)
    for l in range(depth):
        w_all, b_all = _pack_in_proj(w_in[l], b_in[l])
        aq, ak, av, mq, mk, mv, mo, ga, gb, gates, gates_t = _inproj(
            x2d, norm_mix_w[l].reshape(1, D), w_all, b_all, rope_c, rope_s,
            conv_w[l], conv_b[l].reshape(1, -1), S)
        ya = _attention(aq.reshape(B, S, ATT_WIDTH), ak.reshape(B, S, ATT_WIDTH), av.reshape(B, S, ATT_WIDTH))
        yb = _mlstm(mq, mk, mv, mo, gates, gates_t, mlstm_norm_w[l].reshape(1, -1), B)
        x2d = _post(x2d, ya.reshape(T, ATT_WIDTH), yb, ga, gb,
                    w_proj_att[l].astype(BF16), w_proj_mlstm[l].astype(BF16), w_out[l].astype(BF16),
                    norm_mlp_w[l].reshape(1, D), w_ff1[l].astype(BF16), w_ff2[l].astype(BF16),
                    final_w, final_norm=(l == depth - 1))
    return x2d.reshape(B, S, D)
```

```python
import functools
import math

import jax
import jax.numpy as jnp
from jax import lax
from jax.experimental import pallas as pl
from jax.experimental.pallas import tpu as pltpu

F32 = jnp.float32
BF16 = jnp.bfloat16

D_MODEL = 1024
ATT_HEADS = 8
ATT_HEAD_DIM = 64
ATT_WIDTH = ATT_HEADS * ATT_HEAD_DIM
ATT_SPAN = 128
DILATIONS = (1, 4, 16)
ROPE_THETA = 500000.0
ROPE_DIM = ATT_HEAD_DIM // 4
MLSTM_HEADS = 4
MLSTM_HEAD_DIM = 128
MLSTM_WIDTH = MLSTM_HEADS * MLSTM_HEAD_DIM
CONV_WIDTH = 4
D_FF = 4 * D_MODEL
NORM_EPS = 1e-6

COL_AQ = 0
COL_AK = COL_AQ + ATT_WIDTH
COL_AV = COL_AK + ATT_WIDTH
COL_MQ = COL_AV + ATT_WIDTH
COL_MK = COL_MQ + MLSTM_WIDTH
COL_MV = COL_MK + MLSTM_WIDTH
COL_MO = COL_MV + MLSTM_WIDTH
COL_MI = COL_MO + MLSTM_WIDTH
COL_MF = COL_MI + MLSTM_HEADS
COL_GA = COL_MF + MLSTM_HEADS
COL_GB = COL_GA + D_MODEL
D_IN = COL_GB + D_MODEL

LANES = 128
assert COL_MI % LANES == 0
W_MAIN = D_IN // LANES * LANES
W_PAD = W_MAIN + LANES
GATE_SHIFT = 2 * MLSTM_HEADS

TM_IN = 512
TM_POST = 512
ATT_BLK = 2048
MCHUNK = 128
TM_MLSTM = 512
VMEM_LIMIT = 56 * 1024 * 1024
NEG = -1e30


def _const_spec(shape):
    nd = len(shape)
    return pl.BlockSpec(shape, lambda *_: (0,) * nd, pipeline_mode=pl.Buffered(1))


ROPE_HALF = ROPE_DIM // 2
ROPE_PACK = LANES // ROPE_HALF
ROPE_ROWS = 128


def _rope_table_kernel(pos_ref, invf_ref, c_ref, s_ref):
    ang = pos_ref[...] * invf_ref[...]
    cosx = jnp.cos(ang)
    sinx = jnp.sin(ang)
    src = lax.broadcasted_iota(jnp.int32, (LANES, LANES), 0)
    dst = lax.broadcasted_iota(jnp.int32, (LANES, LANES), 1)
    in_head = dst % ATT_HEAD_DIM
    rotary = in_head < ROPE_DIM
    sign = jnp.where(in_head < ROPE_HALF, -1.0, 1.0)
    one_elsewhere = jnp.where(lax.broadcasted_iota(jnp.int32, (1, LANES), 1) % ATT_HEAD_DIM < ROPE_DIM, 0.0, 1.0)
    for r in range(ROPE_PACK):
        pick = rotary & (src == r * ROPE_HALF + dst % ROPE_HALF)
        c_sel = jnp.where(pick, 1.0, 0.0)
        s_sel = jnp.where(pick, sign, 0.0)
        c_ref[pl.ds(r, ROPE_ROWS, stride=ROPE_PACK), :] = jnp.dot(
            cosx, c_sel, precision=lax.Precision.HIGHEST, preferred_element_type=F32) + one_elsewhere
        s_ref[pl.ds(r, ROPE_ROWS, stride=ROPE_PACK), :] = jnp.dot(
            sinx, s_sel, precision=lax.Precision.HIGHEST, preferred_element_type=F32)


def _rope_tables(positions):
    T = positions.size
    assert T % (ROPE_PACK * ROPE_ROWS) == 0
    pos = jnp.repeat(positions.astype(F32).reshape(T // ROPE_PACK, ROPE_PACK), ROPE_HALF, axis=1)
    inv_freq = ROPE_THETA ** (-jnp.arange(0, ROPE_DIM, 2, dtype=F32) / ROPE_DIM)
    invf = jnp.tile(inv_freq, ROPE_PACK).reshape(1, LANES)
    table = jax.ShapeDtypeStruct((T, LANES), F32)
    out_spec = pl.BlockSpec((ROPE_PACK * ROPE_ROWS, LANES), lambda i: (i, 0))
    return pl.pallas_call(
        _rope_table_kernel,
        grid=(T // (ROPE_PACK * ROPE_ROWS),),
        in_specs=[pl.BlockSpec((ROPE_ROWS, LANES), lambda i: (i, 0)), pl.BlockSpec((1, LANES), lambda i: (0, 0))],
        out_specs=(out_spec, out_spec),
        out_shape=(table, table),
        compiler_params=pltpu.CompilerParams(dimension_semantics=("parallel",)),
        name="rope_tables",
    )(pos, invf)


CONV_TAIL = 8
GATE_ROWS = 8


def _log_sigmoid(x):
    return jnp.minimum(x, 0.0) - jnp.log(1.0 + jnp.exp(-jnp.abs(x)))


def _inproj_kernel(x_ref, nw_ref, w_ref, wt_ref, b_ref, c_ref, s_ref, cw_ref, cb_ref,
                   aq_ref, ak_ref, av_ref, mq_ref, mk_ref, mv_ref, mo_ref, ga_ref, gb_ref, gt_ref, gtt_ref,
                   uq, uk, *, tiles_per_seq):
    @pl.when(pl.program_id(0) % tiles_per_seq == 0)
    def _():
        uq[TM_IN:TM_IN + CONV_TAIL, :] = jnp.zeros((CONV_TAIL, MLSTM_WIDTH), F32)
        uk[TM_IN:TM_IN + CONV_TAIL, :] = jnp.zeros((CONV_TAIL, MLSTM_WIDTH), F32)

    x = x_ref[...]
    var = jnp.mean(x * x, axis=-1, keepdims=True)
    h = (x * lax.rsqrt(var + NORM_EPS) * nw_ref[...]).astype(BF16)

    def proj(lo, width):
        return jnp.dot(h, w_ref[:, lo:lo + width], preferred_element_type=F32) + b_ref[:, lo:lo + width]

    cos = c_ref[...]
    sin = s_ref[...]
    lane = lax.broadcasted_iota(jnp.int32, cos.shape, 1)
    first_half = (lane % ATT_HEAD_DIM) < (ROPE_DIM // 2)

    def rope_store(dst_ref, lo):
        zz = proj(lo, ATT_WIDTH)
        for j in range(ATT_WIDTH // LANES):
            z = zz[:, j * LANES:(j + 1) * LANES]
            partner = jnp.where(first_half,
                                pltpu.roll(z, LANES - ROPE_DIM // 2, axis=1),
                                pltpu.roll(z, ROPE_DIM // 2, axis=1))
            dst_ref[:, j * LANES:(j + 1) * LANES] = z * cos + partner * sin

    def conv_silu_store(dst_ref, u_ref, lo, col0, scale):
        u_ref[0:CONV_TAIL, :] = u_ref[TM_IN:TM_IN + CONV_TAIL, :]
        u_ref[CONV_TAIL:CONV_TAIL + TM_IN, :] = proj(lo, MLSTM_WIDTH)
        for j in range(MLSTM_WIDTH // LANES):
            ls = slice(j * LANES, (j + 1) * LANES)
            ws = slice(col0 + j * LANES, col0 + (j + 1) * LANES)
            out = cb_ref[:, ws]
            for t in range(CONV_WIDTH):
                r0 = CONV_TAIL - (CONV_WIDTH - 1) + t
                out = out + cw_ref[t:t + 1, ws] * u_ref[r0:r0 + TM_IN, ls]
            out = out * jax.nn.sigmoid(out)
            dst_ref[:, ls] = (out if scale is None else out * scale).astype(dst_ref.dtype)

    rope_store(aq_ref, COL_AQ)
    rope_store(ak_ref, COL_AK)
    av_ref[...] = proj(COL_AV, ATT_WIDTH)
    conv_silu_store(mq_ref, uq, COL_MQ, 0, MLSTM_HEAD_DIM ** -0.5)
    conv_silu_store(mk_ref, uk, COL_MK, MLSTM_WIDTH, None)
    mv_ref[...] = proj(COL_MV, MLSTM_WIDTH).astype(mv_ref.dtype)
    mo_ref[...] = proj(COL_MO, MLSTM_WIDTH).astype(mo_ref.dtype)

    z_main = proj(COL_MI, W_MAIN - COL_MI)
    z_tail = jnp.dot(h, wt_ref[...], preferred_element_type=F32) + b_ref[:, W_MAIN:W_PAD]
    tiles = [z_main[:, j * LANES:(j + 1) * LANES] for j in range((W_MAIN - COL_MI) // LANES)] + [z_tail]
    rolled = [pltpu.roll(t, LANES - GATE_SHIFT, axis=1) for t in tiles]
    low_lanes = lane < LANES - GATE_SHIFT
    for j in range(D_MODEL // LANES):
        ga_ref[:, j * LANES:(j + 1) * LANES] = jnp.where(low_lanes, rolled[j], rolled[j + 1]).astype(ga_ref.dtype)
        k = j + D_MODEL // LANES
        gb_ref[:, j * LANES:(j + 1) * LANES] = jnp.where(low_lanes, rolled[k], rolled[k + 1]).astype(gb_ref.dtype)

    zg = jnp.where(lane < GATE_SHIFT, tiles[0], 0.0)
    logf = _log_sigmoid(zg)
    ri = lax.broadcasted_iota(jnp.int32, (MCHUNK, MCHUNK), 0)
    ci = lax.broadcasted_iota(jnp.int32, (MCHUNK, MCHUNK), 1)
    tri = (ci <= ri).astype(F32)
    is_input_gate = lax.broadcasted_iota(jnp.int32, (MCHUNK, LANES), 1) < MLSTM_HEADS
    for cc in range(TM_IN // MCHUNK):
        rows = slice(cc * MCHUNK, (cc + 1) * MCHUNK)
        bcum = jnp.dot(tri, logf[rows], precision=lax.Precision.HIGHEST, preferred_element_type=F32)
        gc = jnp.where(is_input_gate, zg[rows], bcum)
        gt_ref[rows, :] = gc
        gtt_ref[cc] = gc.T[0:GATE_ROWS, :]


def _layer_spec(shape, layer):
    nd = len(shape)
    return pl.BlockSpec((None,) + tuple(shape), lambda *_: (layer,) + (0,) * nd, pipeline_mode=pl.Buffered(1))


def _inproj(x2d, norm_w, w_main, w_tail, b_all, rope_c, rope_s, conv_w, conv_b, seq_len, layer):
    T = x2d.shape[0]
    tile = lambda w: pl.BlockSpec((TM_IN, w), lambda i: (i, 0))
    out_shapes = (
        jax.ShapeDtypeStruct((T, ATT_WIDTH), F32),
        jax.ShapeDtypeStruct((T, ATT_WIDTH), F32),
        jax.ShapeDtypeStruct((T, ATT_WIDTH), F32),
        jax.ShapeDtypeStruct((T, MLSTM_WIDTH), BF16),
        jax.ShapeDtypeStruct((T, MLSTM_WIDTH), BF16),
        jax.ShapeDtypeStruct((T, MLSTM_WIDTH), BF16),
        jax.ShapeDtypeStruct((T, MLSTM_WIDTH), BF16),
        jax.ShapeDtypeStruct((T, D_MODEL), BF16),
        jax.ShapeDtypeStruct((T, D_MODEL), BF16),
        jax.ShapeDtypeStruct((T, LANES), F32),
        jax.ShapeDtypeStruct((T // MCHUNK, GATE_ROWS, MCHUNK), F32),
    )
    out_specs = tuple(tile(s.shape[1]) for s in out_shapes[:-1]) + (
        pl.BlockSpec((TM_IN // MCHUNK, GATE_ROWS, MCHUNK), lambda i: (i, 0, 0)),)
    assert seq_len % TM_IN == 0
    return pl.pallas_call(
        functools.partial(_inproj_kernel, tiles_per_seq=seq_len // TM_IN),
        grid=(T // TM_IN,),
        in_specs=[tile(D_MODEL), _layer_spec((1, D_MODEL), layer), _layer_spec((D_MODEL, W_MAIN), layer),
                  _layer_spec((D_MODEL, LANES), layer), _layer_spec((1, W_PAD), layer), tile(LANES), tile(LANES),
                  _layer_spec((CONV_WIDTH, 2 * MLSTM_WIDTH), layer), _layer_spec((1, 2 * MLSTM_WIDTH), layer)],
        out_specs=out_specs,
        out_shape=out_shapes,
        scratch_shapes=[pltpu.VMEM((TM_IN + CONV_TAIL, MLSTM_WIDTH), F32),
                        pltpu.VMEM((TM_IN + CONV_TAIL, MLSTM_WIDTH), F32)],
        compiler_params=pltpu.CompilerParams(dimension_semantics=("arbitrary",),
                                             vmem_limit_bytes=VMEM_LIMIT),
        name="inproj",
    )(x2d, norm_w, w_main, w_tail, b_all, rope_c, rope_s, conv_w, conv_b)


ATT_UNROLL = 16


def _attn_kernel(q_ref, k_ref, v_ref, o_ref,
                 kd1, vd1, qd4, kd4, vd4, qd16, kd16, vd16, tmp, acc_s, m_s, l_s, bias_s):
    j = pl.program_id(2)
    blk = ATT_SPAN
    nsub = {d: ATT_BLK // d // blk for d in DILATIONS}
    kv_bufs = ((1, kd1, vd1), (4, kd4, vd4), (16, kd16, vd16))

    row = lax.broadcasted_iota(jnp.int32, (2 * blk, 2 * blk), 0) % blk
    col = lax.broadcasted_iota(jnp.int32, (2 * blk, 2 * blk), 1)
    band = (col >= row) & (col <= row + ATT_SPAN)
    bias_s[0] = jnp.where(band, 0.0, NEG)
    bias_s[1] = jnp.where(band & (col >= blk), 0.0, NEG)

    @pl.when(j == 0)
    def _():
        for d, kd, vd in kv_bufs:
            n = ATT_BLK // d
            for r in range(d):
                base = r * (n + blk)
                kd[base:base + blk] = jnp.zeros((blk, LANES), BF16)
                vd[base:base + blk] = jnp.zeros((blk, LANES), BF16)

    @pl.when(j != 0)
    def _():
        for d, kd, vd in kv_bufs:
            n = ATT_BLK // d
            for r in range(d):
                base = r * (n + blk)
                kd[base:base + blk] = kd[base + n:base + n + blk]
                vd[base:base + blk] = vd[base + n:base + n + blk]

    def deinterleave(src_ref, dst1, dst4, dst16, is_kv):
        pad = blk if is_kv else 0
        if dst1 is not None:
            dst1[blk:blk + ATT_BLK] = src_ref[0].astype(BF16)
        n4 = ATT_BLK // 4
        for r4 in range(4):
            t4 = src_ref[0, pl.ds(r4, n4, stride=4), :]
            tmp[r4] = t4
            o4 = r4 * (n4 + pad) + pad
            dst4[o4:o4 + n4] = t4.astype(BF16)
        n16 = ATT_BLK // 16
        for r4 in range(4):
            for rr in range(4):
                o16 = (4 * rr + r4) * (n16 + pad) + pad
                dst16[o16:o16 + n16] = tmp[r4, pl.ds(rr, n16, stride=4), :].astype(BF16)

    deinterleave(q_ref, None, qd4, qd16, False)
    deinterleave(k_ref, kd1, kd4, kd16, True)
    deinterleave(v_ref, vd1, vd4, vd16, True)

    head_a = lax.broadcasted_iota(jnp.int32, (blk, LANES), 1) < ATT_HEAD_DIM

    def unit(q2, k2, v2, bias):
        zero = jnp.zeros_like(q2)
        qs = jnp.concatenate([jnp.where(head_a, q2, zero), jnp.where(head_a, zero, q2)], axis=0)
        s = lax.dot_general(qs, k2, (((1,), (1,)), ((), ())), preferred_element_type=F32) + bias
        m = jnp.max(s, axis=-1, keepdims=True)
        p = jnp.exp(s - m).astype(BF16)
        v_aug = jnp.concatenate([v2, jnp.ones_like(v2)], axis=1)
        pv = jnp.dot(p, v_aug, preferred_element_type=F32)
        acc = jnp.where(head_a, pv[:blk, :LANES], pv[blk:, :LANES])
        ll = jnp.where(head_a, pv[:blk, LANES:], pv[blk:, LANES:])
        mm = jnp.where(head_a, m[:blk], m[blk:])
        return acc, mm, ll

    first_blk = jnp.where(j == 0, 1, 0)

    def body1(u, carry):
        r0 = pl.multiple_of(u * blk, blk)
        q2 = q_ref[0, pl.ds(r0, blk), :].astype(BF16)
        bias = bias_s[jnp.where(u == 0, first_blk, 0)]
        acc, mm, ll = unit(q2, kd1[pl.ds(r0, 2 * blk), :], vd1[pl.ds(r0, 2 * blk), :], bias)
        acc_s[0, pl.ds(r0, blk), :] = acc
        m_s[0, pl.ds(r0, blk), :] = mm
        l_s[0, pl.ds(r0, blk), :] = ll
        return carry

    lax.fori_loop(0, nsub[1], body1, 0, unroll=ATT_UNROLL)

    def make_body(g, d, qd, kd, vd):
        n = ATT_BLK // d

        def body(u, carry):
            r = u // nsub[d]
            sb = u % nsub[d]
            q0 = pl.multiple_of(u * blk, blk)
            k0 = pl.multiple_of(r * (n + blk) + sb * blk, blk)
            bias = bias_s[jnp.where(sb == 0, first_blk, 0)]
            acc, mm, ll = unit(qd[pl.ds(q0, blk), :], kd[pl.ds(k0, 2 * blk), :], vd[pl.ds(k0, 2 * blk), :], bias)
            t0 = sb * (blk * d) + r
            acc_s[g, pl.ds(t0, blk, stride=d), :] = acc
            m_s[g, pl.ds(t0, blk, stride=d), :] = mm
            l_s[g, pl.ds(t0, blk, stride=d), :] = ll
            return carry

        return body

    lax.fori_loop(0, ATT_BLK // blk, make_body(1, 4, qd4, kd4, vd4), 0, unroll=ATT_UNROLL)
    lax.fori_loop(0, ATT_BLK // blk, make_body(2, 16, qd16, kd16, vd16), 0, unroll=ATT_UNROLL)

    def combine(u, carry):
        r0 = pl.multiple_of(u * blk, blk)
        sl = pl.ds(r0, blk)
        m0, m1, m2 = m_s[0, sl, :], m_s[1, sl, :], m_s[2, sl, :]
        mx = jnp.maximum(jnp.maximum(m0, m1), m2)
        w0, w1, w2 = jnp.exp(m0 - mx), jnp.exp(m1 - mx), jnp.exp(m2 - mx)
        num = w0 * acc_s[0, sl, :] + w1 * acc_s[1, sl, :] + w2 * acc_s[2, sl, :]
        den = w0 * l_s[0, sl, :] + w1 * l_s[1, sl, :] + w2 * l_s[2, sl, :]
        o_ref[0, sl, :] = (num / den).astype(o_ref.dtype)
        return carry

    lax.fori_loop(0, ATT_BLK // blk, combine, 0, unroll=2)


def _attention(aq, ak, av):
    B, S, _ = aq.shape
    blk = ATT_SPAN
    cur = pl.BlockSpec((1, ATT_BLK, LANES), lambda b, hp, j: (b, j, hp))
    kv_rows = {d: d * (ATT_BLK // d + blk) for d in DILATIONS}
    scratch = [
        pltpu.VMEM((kv_rows[1], LANES), BF16), pltpu.VMEM((kv_rows[1], LANES), BF16),
        pltpu.VMEM((ATT_BLK, LANES), BF16),
        pltpu.VMEM((kv_rows[4], LANES), BF16), pltpu.VMEM((kv_rows[4], LANES), BF16),
        pltpu.VMEM((ATT_BLK, LANES), BF16),
        pltpu.VMEM((kv_rows[16], LANES), BF16), pltpu.VMEM((kv_rows[16], LANES), BF16),
        pltpu.VMEM((4, ATT_BLK // 4, LANES), F32),
        pltpu.VMEM((3, ATT_BLK, LANES), F32), pltpu.VMEM((3, ATT_BLK, LANES), F32),
        pltpu.VMEM((3, ATT_BLK, LANES), F32),
        pltpu.VMEM((2, 2 * blk, 2 * blk), F32),
    ]
    return pl.pallas_call(
        _attn_kernel,
        grid=(B, ATT_WIDTH // LANES, S // ATT_BLK),
        in_specs=[cur, cur, cur],
        out_specs=cur,
        out_shape=jax.ShapeDtypeStruct((B, S, ATT_WIDTH), BF16),
        scratch_shapes=scratch,
        compiler_params=pltpu.CompilerParams(dimension_semantics=("parallel", "parallel", "arbitrary"),
                                             vmem_limit_bytes=VMEM_LIMIT),
        name="dilated_attention",
    )(aq, ak, av)


def _mlstm_kernel(q_ref, k_ref, v_ref, mo_ref, g_ref, gt_ref, nw_ref, y_ref, c_state, m_state):
    L = MCHUNK
    D = MLSTM_HEAD_DIM

    @pl.when(pl.program_id(1) == 0)
    def _():
        c_state[...] = jnp.zeros_like(c_state)
        m_state[...] = jnp.zeros_like(m_state)

    ri = lax.broadcasted_iota(jnp.int32, (L, L), 0)
    ci = lax.broadcasted_iota(jnp.int32, (L, L), 1)
    causal = ci <= ri
    ones = jnp.ones((L, D), BF16)

    units = [(cc, h) for cc in range(TM_MLSTM // L) for h in range(MLSTM_HEADS)]
    rows = lambda cc: slice(cc * L, (cc + 1) * L)
    cols = lambda h: slice(h * D, (h + 1) * D)
    fcol = lambda h: slice(MLSTM_HEADS + h, MLSTM_HEADS + h + 1)

    s_all = {u: lax.dot_general(q_ref[rows(u[0]), cols(u[1])], k_ref[rows(u[0]), cols(u[1])],
                                (((1,), (1,)), ((), ())), preferred_element_type=F32) for u in units}
    r_row, g_tot, pm, v_aug, intra, m_loc, kv = {}, {}, {}, {}, {}, {}, {}
    for u in units:
        cc, h = u
        r_row[u] = gt_ref[cc, h:h + 1, :] - gt_ref[cc, MLSTM_HEADS + h:MLSTM_HEADS + h + 1, :]
        g_tot[u] = g_ref[cc * L + L - 1:(cc + 1) * L, fcol(h)]
        pm[u] = jnp.max(jnp.where(causal, r_row[u], NEG), axis=-1, keepdims=True)
        p = (jnp.exp(jnp.where(causal, r_row[u] - pm[u], NEG)) * s_all[u]).astype(BF16)
        v_aug[u] = jnp.concatenate([v_ref[rows(cc), cols(h)], ones], axis=1)
        intra[u] = jnp.dot(p, v_aug[u], preferred_element_type=F32)
    for u in units:
        cc, h = u
        m_loc[u] = jnp.max(g_tot[u] + r_row[u], axis=-1, keepdims=True)
        kw_t = (k_ref[rows(cc), cols(h)].astype(F32).T
                * jnp.exp(g_tot[u] + r_row[u] - m_loc[u])).astype(BF16)
        kv[u] = jnp.dot(kw_t, v_aug[u], preferred_element_type=F32)

    c_aug = [c_state[h] for h in range(MLSTM_HEADS)]
    m_prev = [m_state[h:h + 1, 0:1] for h in range(MLSTM_HEADS)]
    for u in units:
        cc, h = u
        inter = jnp.dot(q_ref[rows(cc), cols(h)], c_aug[h].astype(BF16), preferred_element_type=F32)
        mm = jnp.maximum(pm[u], m_prev[h])
        nd = jnp.exp(pm[u] - mm) * intra[u] + jnp.exp(m_prev[h] - mm) * inter
        b_col = g_ref[rows(cc), fcol(h)]
        hh = nd[:, :D] / jnp.maximum(jnp.abs(nd[:, D:]), jnp.exp(-(b_col + mm)))
        hn = hh * lax.rsqrt(jnp.mean(hh * hh, axis=-1, keepdims=True) + NORM_EPS) * nw_ref[:, cols(h)]
        o_gate = jax.nn.sigmoid(mo_ref[rows(cc), cols(h)].astype(F32))
        y_ref[rows(cc), cols(h)] = (o_gate * hn).astype(y_ref.dtype)

        m_new = jnp.maximum(g_tot[u] + m_prev[h], m_loc[u])
        c_aug[h] = jnp.exp(g_tot[u] + m_prev[h] - m_new) * c_aug[h] + jnp.exp(m_loc[u] - m_new) * kv[u]
        m_prev[h] = m_new
    for h in range(MLSTM_HEADS):
        c_state[h] = c_aug[h]
        m_state[h:h + 1, :] = jnp.broadcast_to(m_prev[h], (1, LANES))


def _mlstm(mq, mk, mv, mo, gates, gates_t, norm_w, batch):
    T = mq.shape[0]
    nt = T // batch // TM_MLSTM
    nch = TM_MLSTM // MCHUNK
    tile = lambda w: pl.BlockSpec((TM_MLSTM, w), lambda b, c: (b * nt + c, 0))
    return pl.pallas_call(
        _mlstm_kernel,
        grid=(batch, nt),
        in_specs=[tile(MLSTM_WIDTH), tile(MLSTM_WIDTH), tile(MLSTM_WIDTH), tile(MLSTM_WIDTH), tile(LANES),
                  pl.BlockSpec((nch, GATE_ROWS, MCHUNK), lambda b, c: (b * nt + c, 0, 0)),
                  pl.BlockSpec((1, MLSTM_WIDTH), lambda b, c: (0, 0))],
        out_specs=tile(MLSTM_WIDTH),
        out_shape=jax.ShapeDtypeStruct((T, MLSTM_WIDTH), BF16),
        scratch_shapes=[
            pltpu.VMEM((MLSTM_HEADS, MLSTM_HEAD_DIM, 2 * MLSTM_HEAD_DIM), F32),
            pltpu.VMEM((8, LANES), F32),
        ],
        compiler_params=pltpu.CompilerParams(dimension_semantics=("parallel", "arbitrary"),
                                             vmem_limit_bytes=VMEM_LIMIT),
        name="mlstm",
    )(mq, mk, mv, mo, gates, gates_t, norm_w)


def _post_kernel(x_ref, ya_ref, yb_ref, ga_ref, gb_ref, wpa_ref, wpm_ref, wo_ref, nw_ref, w1_ref, w2_ref,
                 fw_ref, o_ref, *, final_norm):
    pa = jnp.dot(ya_ref[...], wpa_ref[...], preferred_element_type=F32)
    pb = jnp.dot(yb_ref[...], wpm_ref[...], preferred_element_type=F32)
    mixed = (jax.nn.sigmoid(ga_ref[...].astype(F32)) * pa
             + jax.nn.sigmoid(gb_ref[...].astype(F32)) * pb).astype(BF16)
    x1 = x_ref[...] + jnp.dot(mixed, wo_ref[...], preferred_element_type=F32)
    var = jnp.mean(x1 * x1, axis=-1, keepdims=True)
    h2 = (x1 * lax.rsqrt(var + NORM_EPS) * nw_ref[...]).astype(BF16)
    acc = x1
    ff_chunk = D_MODEL
    for c in range(D_FF // ff_chunk):
        cs = slice(c * ff_chunk, (c + 1) * ff_chunk)
        u = jnp.maximum(jnp.dot(h2, w1_ref[:, cs], preferred_element_type=F32), 0.0)
        acc = acc + jnp.dot((u * u).astype(BF16), w2_ref[cs, :], preferred_element_type=F32)
    if final_norm:
        var = jnp.mean(acc * acc, axis=-1, keepdims=True)
        acc = acc * lax.rsqrt(var + NORM_EPS) * fw_ref[...]
    o_ref[...] = acc


def _post(x2d, ya, yb, ga, gb, wpa, wpm, wo, norm_w, w1, w2, final_w, layer, final_norm):
    T = x2d.shape[0]
    tile = lambda w: pl.BlockSpec((TM_POST, w), lambda i: (i, 0))
    return pl.pallas_call(
        functools.partial(_post_kernel, final_norm=final_norm),
        grid=(T // TM_POST,),
        in_specs=[tile(D_MODEL), tile(ATT_WIDTH), tile(MLSTM_WIDTH), tile(D_MODEL), tile(D_MODEL),
                  _layer_spec((ATT_WIDTH, D_MODEL), layer), _layer_spec((MLSTM_WIDTH, D_MODEL), layer),
                  _layer_spec((D_MODEL, D_MODEL), layer), _layer_spec((1, D_MODEL), layer),
                  _layer_spec((D_MODEL, D_FF), layer), _layer_spec((D_FF, D_MODEL), layer),
                  _const_spec((1, D_MODEL))],
        out_specs=tile(D_MODEL),
        out_shape=jax.ShapeDtypeStruct((T, D_MODEL), F32),
        compiler_params=pltpu.CompilerParams(dimension_semantics=("parallel",),
                                             vmem_limit_bytes=VMEM_LIMIT),
        name="post",
    )(x2d, ya, yb, ga, gb, wpa, wpm, wo, norm_w, w1, w2, final_w)


PREP_STEPS = 16


def _weight_prep_kernel(*refs):
    n = len(refs) // 2
    w_in_ref, out_in_ref = refs[0], refs[n]
    scale = ATT_HEAD_DIM ** -0.5
    out_in_ref[:, :, COL_AQ:COL_AK] = (w_in_ref[:, :, COL_AQ:COL_AK] * scale).astype(BF16)
    out_in_ref[:, :, COL_AK:W_MAIN] = w_in_ref[:, :, COL_AK:W_MAIN].astype(BF16)
    for src, dst in zip(refs[1:n], refs[n + 1:]):
        dst[...] = src[...].astype(BF16)


def _weight_prep(w_in, others):
    def slab(shape, cols):
        assert shape[1] % (PREP_STEPS * 16) == 0
        return pl.BlockSpec((shape[0], shape[1] // PREP_STEPS, cols), lambda i: (0, i, 0))

    arrays = [w_in] + list(others)
    out_cols = [W_MAIN] + [a.shape[2] for a in others]
    return pl.pallas_call(
        _weight_prep_kernel,
        grid=(PREP_STEPS,),
        in_specs=[slab(a.shape, c) for a, c in zip(arrays, out_cols)],
        out_specs=[slab(a.shape, c) for a, c in zip(arrays, out_cols)],
        out_shape=[jax.ShapeDtypeStruct((a.shape[0], a.shape[1], c), BF16) for a, c in zip(arrays, out_cols)],
        compiler_params=pltpu.CompilerParams(dimension_semantics=("parallel",), vmem_limit_bytes=VMEM_LIMIT),
        name="weight_prep",
    )(*arrays)


def kernel(x, positions, norm_mix_w, w_in, b_in, conv_w, conv_b, mlstm_norm_w, w_proj_att, w_proj_mlstm,
           w_out, norm_mlp_w, w_ff1, w_ff2, final_norm_w):
    B, S, D = x.shape
    T = B * S
    depth = w_in.shape[0]
    assert D == D_MODEL and S % ATT_BLK == 0 and T % TM_IN == 0 and T % TM_POST == 0
    assert math.isclose(ATT_HEAD_DIM ** -0.5, 0.125)
    rope_c, rope_s = _rope_tables(positions)
    w_main, wpa, wpm, wo, w1, w2 = _weight_prep(w_in, (w_proj_att, w_proj_mlstm, w_out, w_ff1, w_ff2))
    w_tail = jnp.pad(w_in[:, :, W_MAIN:], ((0, 0), (0, 0), (0, W_PAD - D_IN))).astype(BF16)
    q_scale = jnp.where(jnp.arange(W_PAD) < COL_AK, ATT_HEAD_DIM ** -0.5, 1.0).astype(F32)
    b_all = (jnp.pad(b_in, ((0, 0), (0, W_PAD - D_IN))) * q_scale).reshape(depth, 1, W_PAD)
    x2d = x.reshape(T, D)
    final_w = final_norm_w.reshape(1, D).astype(F32)
    for l in range(depth):
        aq, ak, av, mq, mk, mv, mo, ga, gb, gates, gates_t = _inproj(
            x2d, norm_mix_w.reshape(depth, 1, D), w_main, w_tail, b_all, rope_c, rope_s,
            conv_w, conv_b.reshape(depth, 1, -1), S, l)
        ya = _attention(aq.reshape(B, S, ATT_WIDTH), ak.reshape(B, S, ATT_WIDTH), av.reshape(B, S, ATT_WIDTH))
        yb = _mlstm(mq, mk, mv, mo, gates, gates_t, mlstm_norm_w[l].reshape(1, -1), B)
        x2d = _post(x2d, ya.reshape(T, ATT_WIDTH), yb, ga, gb, wpa, wpm, wo,
                    norm_mlp_w.reshape(depth, 1, D), w1, w2, final_w, l, final_norm=(l == depth - 1))
    return x2d.reshape(B, S, D)
```

```python
import functools
import math

import jax
import jax.numpy as jnp
from jax import lax
from jax.experimental import pallas as pl
from jax.experimental.pallas import tpu as pltpu

F32 = jnp.float32
BF16 = jnp.bfloat16

D_MODEL = 1024
ATT_HEADS = 8
ATT_HEAD_DIM = 64
ATT_WIDTH = ATT_HEADS * ATT_HEAD_DIM
ATT_SPAN = 128
DILATIONS = (1, 4, 16)
ROPE_THETA = 500000.0
ROPE_DIM = ATT_HEAD_DIM // 4
MLSTM_HEADS = 4
MLSTM_HEAD_DIM = 128
MLSTM_WIDTH = MLSTM_HEADS * MLSTM_HEAD_DIM
CONV_WIDTH = 4
D_FF = 4 * D_MODEL
NORM_EPS = 1e-6

COL_AQ = 0
COL_AK = COL_AQ + ATT_WIDTH
COL_AV = COL_AK + ATT_WIDTH
COL_MQ = COL_AV + ATT_WIDTH
COL_MK = COL_MQ + MLSTM_WIDTH
COL_MV = COL_MK + MLSTM_WIDTH
COL_MO = COL_MV + MLSTM_WIDTH
COL_MI = COL_MO + MLSTM_WIDTH
COL_MF = COL_MI + MLSTM_HEADS
COL_GA = COL_MF + MLSTM_HEADS
COL_GB = COL_GA + D_MODEL
D_IN = COL_GB + D_MODEL

LANES = 128
assert COL_MI % LANES == 0
W_MAIN = D_IN // LANES * LANES
W_PAD = W_MAIN + LANES
GATE_SHIFT = 2 * MLSTM_HEADS

TM_IN = 512
TM_POST = 512
ATT_BLK = 2048
MCHUNK = 128
TM_MLSTM = 512
VMEM_LIMIT = 56 * 1024 * 1024
NEG = -1e30


def _const_spec(shape):
    nd = len(shape)
    return pl.BlockSpec(shape, lambda *_: (0,) * nd, pipeline_mode=pl.Buffered(1))


ROPE_HALF = ROPE_DIM // 2
ROPE_PACK = LANES // ROPE_HALF
ROPE_ROWS = 128


def _rope_table_kernel(pos_ref, invf_ref, c_ref, s_ref):
    ang = pos_ref[...] * invf_ref[...]
    cosx = jnp.cos(ang)
    sinx = jnp.sin(ang)
    src = lax.broadcasted_iota(jnp.int32, (LANES, LANES), 0)
    dst = lax.broadcasted_iota(jnp.int32, (LANES, LANES), 1)
    in_head = dst % ATT_HEAD_DIM
    rotary = in_head < ROPE_DIM
    sign = jnp.where(in_head < ROPE_HALF, -1.0, 1.0)
    one_elsewhere = jnp.where(lax.broadcasted_iota(jnp.int32, (1, LANES), 1) % ATT_HEAD_DIM < ROPE_DIM, 0.0, 1.0)
    for r in range(ROPE_PACK):
        pick = rotary & (src == r * ROPE_HALF + dst % ROPE_HALF)
        c_sel = jnp.where(pick, 1.0, 0.0)
        s_sel = jnp.where(pick, sign, 0.0)
        c_ref[pl.ds(r, ROPE_ROWS, stride=ROPE_PACK), :] = jnp.dot(
            cosx, c_sel, precision=lax.Precision.HIGHEST, preferred_element_type=F32) + one_elsewhere
        s_ref[pl.ds(r, ROPE_ROWS, stride=ROPE_PACK), :] = jnp.dot(
            sinx, s_sel, precision=lax.Precision.HIGHEST, preferred_element_type=F32)


def _rope_tables(positions):
    T = positions.size
    assert T % (ROPE_PACK * ROPE_ROWS) == 0
    pos = jnp.repeat(positions.astype(F32).reshape(T // ROPE_PACK, ROPE_PACK), ROPE_HALF, axis=1)
    inv_freq = ROPE_THETA ** (-jnp.arange(0, ROPE_DIM, 2, dtype=F32) / ROPE_DIM)
    invf = jnp.tile(inv_freq, ROPE_PACK).reshape(1, LANES)
    table = jax.ShapeDtypeStruct((T, LANES), F32)
    out_spec = pl.BlockSpec((ROPE_PACK * ROPE_ROWS, LANES), lambda i: (i, 0))
    return pl.pallas_call(
        _rope_table_kernel,
        grid=(T // (ROPE_PACK * ROPE_ROWS),),
        in_specs=[pl.BlockSpec((ROPE_ROWS, LANES), lambda i: (i, 0)), pl.BlockSpec((1, LANES), lambda i: (0, 0))],
        out_specs=(out_spec, out_spec),
        out_shape=(table, table),
        compiler_params=pltpu.CompilerParams(dimension_semantics=("parallel",)),
        name="rope_tables",
    )(pos, invf)


CONV_TAIL = 8
GATE_ROWS = 8


def _log_sigmoid(x):
    return jnp.minimum(x, 0.0) - jnp.log(1.0 + jnp.exp(-jnp.abs(x)))


def _inproj_kernel(x_ref, nw_ref, w_ref, b_ref, c_ref, s_ref, cw_ref, cb_ref,
                   aq_ref, ak_ref, av_ref, mq_ref, mk_ref, mv_ref, mo_ref, ga_ref, gb_ref, gt_ref, gtt_ref,
                   uq, uk, *, tiles_per_seq):
    @pl.when(pl.program_id(0) % tiles_per_seq == 0)
    def _():
        uq[TM_IN:TM_IN + CONV_TAIL, :] = jnp.zeros((CONV_TAIL, MLSTM_WIDTH), F32)
        uk[TM_IN:TM_IN + CONV_TAIL, :] = jnp.zeros((CONV_TAIL, MLSTM_WIDTH), F32)

    x = x_ref[...]
    var = jnp.mean(x * x, axis=-1, keepdims=True)
    h = (x * lax.rsqrt(var + NORM_EPS) * nw_ref[...]).astype(BF16)

    def proj(lo, width):
        return lax.dot_general(h, w_ref[lo:lo + width, :], (((1,), (1,)), ((), ())),
                               preferred_element_type=F32) + b_ref[:, lo:lo + width]

    cos = c_ref[...]
    sin = s_ref[...]
    lane = lax.broadcasted_iota(jnp.int32, cos.shape, 1)
    first_half = (lane % ATT_HEAD_DIM) < (ROPE_DIM // 2)

    def rope_store(dst_ref, lo):
        zz = proj(lo, ATT_WIDTH)
        for j in range(ATT_WIDTH // LANES):
            z = zz[:, j * LANES:(j + 1) * LANES]
            partner = jnp.where(first_half,
                                pltpu.roll(z, LANES - ROPE_DIM // 2, axis=1),
                                pltpu.roll(z, ROPE_DIM // 2, axis=1))
            dst_ref[:, j * LANES:(j + 1) * LANES] = z * cos + partner * sin

    def conv_silu_store(dst_ref, u_ref, lo, col0, scale):
        u_ref[0:CONV_TAIL, :] = u_ref[TM_IN:TM_IN + CONV_TAIL, :]
        u_ref[CONV_TAIL:CONV_TAIL + TM_IN, :] = proj(lo, MLSTM_WIDTH)
        for j in range(MLSTM_WIDTH // LANES):
            ls = slice(j * LANES, (j + 1) * LANES)
            ws = slice(col0 + j * LANES, col0 + (j + 1) * LANES)
            out = cb_ref[:, ws]
            for t in range(CONV_WIDTH):
                r0 = CONV_TAIL - (CONV_WIDTH - 1) + t
                out = out + cw_ref[t:t + 1, ws] * u_ref[r0:r0 + TM_IN, ls]
            out = out * jax.nn.sigmoid(out)
            dst_ref[:, ls] = (out if scale is None else out * scale).astype(dst_ref.dtype)

    rope_store(aq_ref, COL_AQ)
    rope_store(ak_ref, COL_AK)
    av_ref[...] = proj(COL_AV, ATT_WIDTH)
    conv_silu_store(mq_ref, uq, COL_MQ, 0, MLSTM_HEAD_DIM ** -0.5)
    conv_silu_store(mk_ref, uk, COL_MK, MLSTM_WIDTH, None)
    mv_ref[...] = proj(COL_MV, MLSTM_WIDTH).astype(mv_ref.dtype)
    mo_ref[...] = proj(COL_MO, MLSTM_WIDTH).astype(mo_ref.dtype)

    z_tail = proj(COL_MI, W_PAD - COL_MI)
    tiles = [z_tail[:, j * LANES:(j + 1) * LANES] for j in range((W_PAD - COL_MI) // LANES)]
    rolled = [pltpu.roll(t, LANES - GATE_SHIFT, axis=1) for t in tiles]
    low_lanes = lane < LANES - GATE_SHIFT
    for j in range(D_MODEL // LANES):
        ga_ref[:, j * LANES:(j + 1) * LANES] = jnp.where(low_lanes, rolled[j], rolled[j + 1]).astype(ga_ref.dtype)
        k = j + D_MODEL // LANES
        gb_ref[:, j * LANES:(j + 1) * LANES] = jnp.where(low_lanes, rolled[k], rolled[k + 1]).astype(gb_ref.dtype)

    zg = jnp.where(lane < GATE_SHIFT, tiles[0], 0.0)
    logf = _log_sigmoid(zg)
    ri = lax.broadcasted_iota(jnp.int32, (MCHUNK, MCHUNK), 0)
    ci = lax.broadcasted_iota(jnp.int32, (MCHUNK, MCHUNK), 1)
    tri = (ci <= ri).astype(F32)
    is_input_gate = lax.broadcasted_iota(jnp.int32, (MCHUNK, LANES), 1) < MLSTM_HEADS
    for cc in range(TM_IN // MCHUNK):
        rows = slice(cc * MCHUNK, (cc + 1) * MCHUNK)
        bcum = jnp.dot(tri, logf[rows], precision=lax.Precision.HIGHEST, preferred_element_type=F32)
        gc = jnp.where(is_input_gate, zg[rows], bcum)
        gt_ref[rows, :] = gc
        gtt_ref[cc] = gc.T[0:GATE_ROWS, :]


def _layer_spec(shape, layer):
    nd = len(shape)
    return pl.BlockSpec((None,) + tuple(shape), lambda *_: (layer,) + (0,) * nd, pipeline_mode=pl.Buffered(1))


def _inproj(x2d, norm_w, w_t, b_all, rope_c, rope_s, conv_w, conv_b, seq_len, layer):
    T = x2d.shape[0]
    tile = lambda w: pl.BlockSpec((TM_IN, w), lambda i: (i, 0))
    out_shapes = (
        jax.ShapeDtypeStruct((T, ATT_WIDTH), F32),
        jax.ShapeDtypeStruct((T, ATT_WIDTH), F32),
        jax.ShapeDtypeStruct((T, ATT_WIDTH), F32),
        jax.ShapeDtypeStruct((T, MLSTM_WIDTH), BF16),
        jax.ShapeDtypeStruct((T, MLSTM_WIDTH), BF16),
        jax.ShapeDtypeStruct((T, MLSTM_WIDTH), BF16),
        jax.ShapeDtypeStruct((T, MLSTM_WIDTH), BF16),
        jax.ShapeDtypeStruct((T, D_MODEL), BF16),
        jax.ShapeDtypeStruct((T, D_MODEL), BF16),
        jax.ShapeDtypeStruct((T, LANES), F32),
        jax.ShapeDtypeStruct((T // MCHUNK, GATE_ROWS, MCHUNK), F32),
    )
    out_specs = tuple(tile(s.shape[1]) for s in out_shapes[:-1]) + (
        pl.BlockSpec((TM_IN // MCHUNK, GATE_ROWS, MCHUNK), lambda i: (i, 0, 0)),)
    assert seq_len % TM_IN == 0
    return pl.pallas_call(
        functools.partial(_inproj_kernel, tiles_per_seq=seq_len // TM_IN),
        grid=(T // TM_IN,),
        in_specs=[tile(D_MODEL), _layer_spec((1, D_MODEL), layer), _layer_spec((W_PAD, D_MODEL), layer),
                  _layer_spec((1, W_PAD), layer), tile(LANES), tile(LANES),
                  _layer_spec((CONV_WIDTH, 2 * MLSTM_WIDTH), layer), _layer_spec((1, 2 * MLSTM_WIDTH), layer)],
        out_specs=out_specs,
        out_shape=out_shapes,
        scratch_shapes=[pltpu.VMEM((TM_IN + CONV_TAIL, MLSTM_WIDTH), F32),
                        pltpu.VMEM((TM_IN + CONV_TAIL, MLSTM_WIDTH), F32)],
        compiler_params=pltpu.CompilerParams(dimension_semantics=("arbitrary",),
                                             vmem_limit_bytes=VMEM_LIMIT),
        name="inproj",
    )(x2d, norm_w, w_t, b_all, rope_c, rope_s, conv_w, conv_b)


ATT_UNROLL = 16


def _attn_kernel(q_ref, k_ref, v_ref, o_ref,
                 kd1, vd1, qd4, kd4, vd4, qd16, kd16, vd16, tmp, acc_s, m_s, l_s, bias_s):
    j = pl.program_id(2)
    blk = ATT_SPAN
    nsub = {d: ATT_BLK // d // blk for d in DILATIONS}
    kv_bufs = ((1, kd1, vd1), (4, kd4, vd4), (16, kd16, vd16))

    row = lax.broadcasted_iota(jnp.int32, (2 * blk, 2 * blk), 0) % blk
    col = lax.broadcasted_iota(jnp.int32, (2 * blk, 2 * blk), 1)
    band = (col >= row) & (col <= row + ATT_SPAN)
    bias_s[0] = jnp.where(band, 0.0, NEG)
    bias_s[1] = jnp.where(band & (col >= blk), 0.0, NEG)

    @pl.when(j == 0)
    def _():
        for d, kd, vd in kv_bufs:
            n = ATT_BLK // d
            for r in range(d):
                base = r * (n + blk)
                kd[base:base + blk] = jnp.zeros((blk, LANES), BF16)
                vd[base:base + blk] = jnp.zeros((blk, LANES), BF16)

    @pl.when(j != 0)
    def _():
        for d, kd, vd in kv_bufs:
            n = ATT_BLK // d
            for r in range(d):
                base = r * (n + blk)
                kd[base:base + blk] = kd[base + n:base + n + blk]
                vd[base:base + blk] = vd[base + n:base + n + blk]

    def deinterleave(src_ref, dst1, dst4, dst16, is_kv):
        pad = blk if is_kv else 0
        if dst1 is not None:
            dst1[blk:blk + ATT_BLK] = src_ref[0].astype(BF16)
        n4 = ATT_BLK // 4
        for r4 in range(4):
            t4 = src_ref[0, pl.ds(r4, n4, stride=4), :]
            tmp[r4] = t4
            o4 = r4 * (n4 + pad) + pad
            dst4[o4:o4 + n4] = t4.astype(BF16)
        n16 = ATT_BLK // 16
        for r4 in range(4):
            for rr in range(4):
                o16 = (4 * rr + r4) * (n16 + pad) + pad
                dst16[o16:o16 + n16] = tmp[r4, pl.ds(rr, n16, stride=4), :].astype(BF16)

    deinterleave(q_ref, None, qd4, qd16, False)
    deinterleave(k_ref, kd1, kd4, kd16, True)
    deinterleave(v_ref, vd1, vd4, vd16, True)

    head_a = lax.broadcasted_iota(jnp.int32, (blk, LANES), 1) < ATT_HEAD_DIM

    def unit(q2, k2, v2, bias):
        zero = jnp.zeros_like(q2)
        qs = jnp.concatenate([jnp.where(head_a, q2, zero), jnp.where(head_a, zero, q2)], axis=0)
        s = lax.dot_general(qs, k2, (((1,), (1,)), ((), ())), preferred_element_type=F32) + bias
        m = jnp.max(s, axis=-1, keepdims=True)
        p = jnp.exp(s - m).astype(BF16)
        v_aug = jnp.concatenate([v2, jnp.ones_like(v2)], axis=1)
        pv = jnp.dot(p, v_aug, preferred_element_type=F32)
        acc = jnp.where(head_a, pv[:blk, :LANES], pv[blk:, :LANES])
        ll = jnp.where(head_a, pv[:blk, LANES:], pv[blk:, LANES:])
        mm = jnp.where(head_a, m[:blk], m[blk:])
        return acc, mm, ll

    first_blk = jnp.where(j == 0, 1, 0)

    def body1(u, carry):
        r0 = pl.multiple_of(u * blk, blk)
        q2 = q_ref[0, pl.ds(r0, blk), :].astype(BF16)
        bias = bias_s[jnp.where(u == 0, first_blk, 0)]
        acc, mm, ll = unit(q2, kd1[pl.ds(r0, 2 * blk), :], vd1[pl.ds(r0, 2 * blk), :], bias)
        acc_s[0, pl.ds(r0, blk), :] = acc
        m_s[0, pl.ds(r0, blk), :] = mm
        l_s[0, pl.ds(r0, blk), :] = ll
        return carry

    lax.fori_loop(0, nsub[1], body1, 0, unroll=ATT_UNROLL)

    def make_body(g, d, qd, kd, vd):
        n = ATT_BLK // d

        def body(u, carry):
            r = u // nsub[d]
            sb = u % nsub[d]
            q0 = pl.multiple_of(u * blk, blk)
            k0 = pl.multiple_of(r * (n + blk) + sb * blk, blk)
            bias = bias_s[jnp.where(sb == 0, first_blk, 0)]
            acc, mm, ll = unit(qd[pl.ds(q0, blk), :], kd[pl.ds(k0, 2 * blk), :], vd[pl.ds(k0, 2 * blk), :], bias)
            t0 = sb * (blk * d) + r
            acc_s[g, pl.ds(t0, blk, stride=d), :] = acc
            m_s[g, pl.ds(t0, blk, stride=d), :] = mm
            l_s[g, pl.ds(t0, blk, stride=d), :] = ll
            return carry

        return body

    lax.fori_loop(0, ATT_BLK // blk, make_body(1, 4, qd4, kd4, vd4), 0, unroll=ATT_UNROLL)
    lax.fori_loop(0, ATT_BLK // blk, make_body(2, 16, qd16, kd16, vd16), 0, unroll=ATT_UNROLL)

    def combine(u, carry):
        r0 = pl.multiple_of(u * blk, blk)
        sl = pl.ds(r0, blk)
        m0, m1, m2 = m_s[0, sl, :], m_s[1, sl, :], m_s[2, sl, :]
        mx = jnp.maximum(jnp.maximum(m0, m1), m2)
        w0, w1, w2 = jnp.exp(m0 - mx), jnp.exp(m1 - mx), jnp.exp(m2 - mx)
        num = w0 * acc_s[0, sl, :] + w1 * acc_s[1, sl, :] + w2 * acc_s[2, sl, :]
        den = w0 * l_s[0, sl, :] + w1 * l_s[1, sl, :] + w2 * l_s[2, sl, :]
        o_ref[0, sl, :] = (num / den).astype(o_ref.dtype)
        return carry

    lax.fori_loop(0, ATT_BLK // blk, combine, 0, unroll=2)


def _attention(aq, ak, av):
    B, S, _ = aq.shape
    blk = ATT_SPAN
    cur = pl.BlockSpec((1, ATT_BLK, LANES), lambda b, hp, j: (b, j, hp))
    kv_rows = {d: d * (ATT_BLK // d + blk) for d in DILATIONS}
    scratch = [
        pltpu.VMEM((kv_rows[1], LANES), BF16), pltpu.VMEM((kv_rows[1], LANES), BF16),
        pltpu.VMEM((ATT_BLK, LANES), BF16),
        pltpu.VMEM((kv_rows[4], LANES), BF16), pltpu.VMEM((kv_rows[4], LANES), BF16),
        pltpu.VMEM((ATT_BLK, LANES), BF16),
        pltpu.VMEM((kv_rows[16], LANES), BF16), pltpu.VMEM((kv_rows[16], LANES), BF16),
        pltpu.VMEM((4, ATT_BLK // 4, LANES), F32),
        pltpu.VMEM((3, ATT_BLK, LANES), F32), pltpu.VMEM((3, ATT_BLK, LANES), F32),
        pltpu.VMEM((3, ATT_BLK, LANES), F32),
        pltpu.VMEM((2, 2 * blk, 2 * blk), F32),
    ]
    return pl.pallas_call(
        _attn_kernel,
        grid=(B, ATT_WIDTH // LANES, S // ATT_BLK),
        in_specs=[cur, cur, cur],
        out_specs=cur,
        out_shape=jax.ShapeDtypeStruct((B, S, ATT_WIDTH), BF16),
        scratch_shapes=scratch,
        compiler_params=pltpu.CompilerParams(dimension_semantics=("parallel", "parallel", "arbitrary"),
                                             vmem_limit_bytes=VMEM_LIMIT),
        name="dilated_attention",
    )(aq, ak, av)


def _mlstm_kernel(q_ref, k_ref, v_ref, mo_ref, g_ref, gt_ref, nw_ref, y_ref, c_state, m_state):
    L = MCHUNK
    D = MLSTM_HEAD_DIM

    @pl.when(pl.program_id(1) == 0)
    def _():
        c_state[...] = jnp.zeros_like(c_state)
        m_state[...] = jnp.zeros_like(m_state)

    ri = lax.broadcasted_iota(jnp.int32, (L, L), 0)
    ci = lax.broadcasted_iota(jnp.int32, (L, L), 1)
    causal = ci <= ri
    ones = jnp.ones((L, D), BF16)

    units = [(cc, h) for cc in range(TM_MLSTM // L) for h in range(MLSTM_HEADS)]
    rows = lambda cc: slice(cc * L, (cc + 1) * L)
    cols = lambda h: slice(h * D, (h + 1) * D)
    fcol = lambda h: slice(MLSTM_HEADS + h, MLSTM_HEADS + h + 1)

    s_all = {u: lax.dot_general(q_ref[rows(u[0]), cols(u[1])], k_ref[rows(u[0]), cols(u[1])],
                                (((1,), (1,)), ((), ())), preferred_element_type=F32) for u in units}
    r_row, g_tot, pm, v_aug, intra, m_loc, kv = {}, {}, {}, {}, {}, {}, {}
    for u in units:
        cc, h = u
        r_row[u] = gt_ref[cc, h:h + 1, :] - gt_ref[cc, MLSTM_HEADS + h:MLSTM_HEADS + h + 1, :]
        g_tot[u] = g_ref[cc * L + L - 1:(cc + 1) * L, fcol(h)]
        pm[u] = jnp.max(jnp.where(causal, r_row[u], NEG), axis=-1, keepdims=True)
        p = (jnp.exp(jnp.where(causal, r_row[u] - pm[u], NEG)) * s_all[u]).astype(BF16)
        v_aug[u] = jnp.concatenate([v_ref[rows(cc), cols(h)], ones], axis=1)
        intra[u] = jnp.dot(p, v_aug[u], preferred_element_type=F32)
    for u in units:
        cc, h = u
        m_loc[u] = jnp.max(g_tot[u] + r_row[u], axis=-1, keepdims=True)
        kw_t = (k_ref[rows(cc), cols(h)].astype(F32).T
                * jnp.exp(g_tot[u] + r_row[u] - m_loc[u])).astype(BF16)
        kv[u] = jnp.dot(kw_t, v_aug[u], preferred_element_type=F32)

    c_aug = [c_state[h] for h in range(MLSTM_HEADS)]
    m_prev = [m_state[h:h + 1, 0:1] for h in range(MLSTM_HEADS)]
    for u in units:
        cc, h = u
        inter = jnp.dot(q_ref[rows(cc), cols(h)], c_aug[h].astype(BF16), preferred_element_type=F32)
        mm = jnp.maximum(pm[u], m_prev[h])
        nd = jnp.exp(pm[u] - mm) * intra[u] + jnp.exp(m_prev[h] - mm) * inter
        b_col = g_ref[rows(cc), fcol(h)]
        hh = nd[:, :D] / jnp.maximum(jnp.abs(nd[:, D:]), jnp.exp(-(b_col + mm)))
        hn = hh * lax.rsqrt(jnp.mean(hh * hh, axis=-1, keepdims=True) + NORM_EPS) * nw_ref[:, cols(h)]
        o_gate = jax.nn.sigmoid(mo_ref[rows(cc), cols(h)].astype(F32))
        y_ref[rows(cc), cols(h)] = (o_gate * hn).astype(y_ref.dtype)

        m_new = jnp.maximum(g_tot[u] + m_prev[h], m_loc[u])
        c_aug[h] = jnp.exp(g_tot[u] + m_prev[h] - m_new) * c_aug[h] + jnp.exp(m_loc[u] - m_new) * kv[u]
        m_prev[h] = m_new
    for h in range(MLSTM_HEADS):
        c_state[h] = c_aug[h]
        m_state[h:h + 1, :] = jnp.broadcast_to(m_prev[h], (1, LANES))


def _mlstm(mq, mk, mv, mo, gates, gates_t, norm_w, batch):
    T = mq.shape[0]
    nt = T // batch // TM_MLSTM
    nch = TM_MLSTM // MCHUNK
    tile = lambda w: pl.BlockSpec((TM_MLSTM, w), lambda b, c: (b * nt + c, 0))
    return pl.pallas_call(
        _mlstm_kernel,
        grid=(batch, nt),
        in_specs=[tile(MLSTM_WIDTH), tile(MLSTM_WIDTH), tile(MLSTM_WIDTH), tile(MLSTM_WIDTH), tile(LANES),
                  pl.BlockSpec((nch, GATE_ROWS, MCHUNK), lambda b, c: (b * nt + c, 0, 0)),
                  pl.BlockSpec((1, MLSTM_WIDTH), lambda b, c: (0, 0))],
        out_specs=tile(MLSTM_WIDTH),
        out_shape=jax.ShapeDtypeStruct((T, MLSTM_WIDTH), BF16),
        scratch_shapes=[
            pltpu.VMEM((MLSTM_HEADS, MLSTM_HEAD_DIM, 2 * MLSTM_HEAD_DIM), F32),
            pltpu.VMEM((8, LANES), F32),
        ],
        compiler_params=pltpu.CompilerParams(dimension_semantics=("parallel", "arbitrary"),
                                             vmem_limit_bytes=VMEM_LIMIT),
        name="mlstm",
    )(mq, mk, mv, mo, gates, gates_t, norm_w)


def _post_kernel(x_ref, ya_ref, yb_ref, ga_ref, gb_ref, wpa_ref, wpm_ref, wo_ref, nw_ref, w1_ref, w2_ref,
                 fw_ref, o_ref, *, final_norm):
    pa = jnp.dot(ya_ref[...], wpa_ref[...], preferred_element_type=F32)
    pb = jnp.dot(yb_ref[...], wpm_ref[...], preferred_element_type=F32)
    mixed = (jax.nn.sigmoid(ga_ref[...].astype(F32)) * pa
             + jax.nn.sigmoid(gb_ref[...].astype(F32)) * pb).astype(BF16)
    x1 = x_ref[...] + jnp.dot(mixed, wo_ref[...], preferred_element_type=F32)
    var = jnp.mean(x1 * x1, axis=-1, keepdims=True)
    h2 = (x1 * lax.rsqrt(var + NORM_EPS) * nw_ref[...]).astype(BF16)
    acc = x1
    ff_chunk = D_MODEL
    for c in range(D_FF // ff_chunk):
        cs = slice(c * ff_chunk, (c + 1) * ff_chunk)
        u = jnp.maximum(jnp.dot(h2, w1_ref[:, cs], preferred_element_type=F32), 0.0)
        acc = acc + jnp.dot((u * u).astype(BF16), w2_ref[cs, :], preferred_element_type=F32)
    if final_norm:
        var = jnp.mean(acc * acc, axis=-1, keepdims=True)
        acc = acc * lax.rsqrt(var + NORM_EPS) * fw_ref[...]
    o_ref[...] = acc


def _post(x2d, ya, yb, ga, gb, wpa, wpm, wo, norm_w, w1, w2, final_w, layer, final_norm):
    T = x2d.shape[0]
    tile = lambda w: pl.BlockSpec((TM_POST, w), lambda i: (i, 0))
    return pl.pallas_call(
        functools.partial(_post_kernel, final_norm=final_norm),
        grid=(T // TM_POST,),
        in_specs=[tile(D_MODEL), tile(ATT_WIDTH), tile(MLSTM_WIDTH), tile(D_MODEL), tile(D_MODEL),
                  _layer_spec((ATT_WIDTH, D_MODEL), layer), _layer_spec((MLSTM_WIDTH, D_MODEL), layer),
                  _layer_spec((D_MODEL, D_MODEL), layer), _layer_spec((1, D_MODEL), layer),
                  _layer_spec((D_MODEL, D_FF), layer), _layer_spec((D_FF, D_MODEL), layer),
                  _const_spec((1, D_MODEL))],
        out_specs=tile(D_MODEL),
        out_shape=jax.ShapeDtypeStruct((T, D_MODEL), F32),
        compiler_params=pltpu.CompilerParams(dimension_semantics=("parallel",),
                                             vmem_limit_bytes=VMEM_LIMIT),
        name="post",
    )(x2d, ya, yb, ga, gb, wpa, wpm, wo, norm_w, w1, w2, final_w)


PREP_STEPS = 16


PREP_IN_ROWS = 384


def _cast_in_proj_kernel(w_ref, o_ref):
    row = pl.program_id(0) * PREP_IN_ROWS + lax.broadcasted_iota(jnp.int32, o_ref.shape, 1)
    w = w_ref[...]
    w = jnp.where(row < COL_AK, w * (ATT_HEAD_DIM ** -0.5), w)
    o_ref[...] = jnp.where(row < D_IN, w, 0.0).astype(BF16)


def _cast_in_proj(w_t):
    layers, _, feat = w_t.shape
    assert W_PAD % PREP_IN_ROWS == 0
    spec = pl.BlockSpec((layers, PREP_IN_ROWS, feat), lambda i: (0, i, 0))
    return pl.pallas_call(
        _cast_in_proj_kernel,
        grid=(W_PAD // PREP_IN_ROWS,),
        in_specs=[spec],
        out_specs=spec,
        out_shape=jax.ShapeDtypeStruct((layers, W_PAD, feat), BF16),
        compiler_params=pltpu.CompilerParams(dimension_semantics=("parallel",), vmem_limit_bytes=VMEM_LIMIT),
        name="cast_in_proj",
    )(w_t)


def _cast_weights_kernel(*refs):
    n = len(refs) // 2
    for src, dst in zip(refs[:n], refs[n:]):
        dst[...] = src[...].astype(BF16)


def _cast_weights(weights):
    def slab(shape):
        assert shape[1] % (PREP_STEPS * 16) == 0
        return pl.BlockSpec((shape[0], shape[1] // PREP_STEPS, shape[2]), lambda i: (0, i, 0))

    return pl.pallas_call(
        _cast_weights_kernel,
        grid=(PREP_STEPS,),
        in_specs=[slab(a.shape) for a in weights],
        out_specs=[slab(a.shape) for a in weights],
        out_shape=[jax.ShapeDtypeStruct(a.shape, BF16) for a in weights],
        compiler_params=pltpu.CompilerParams(dimension_semantics=("parallel",), vmem_limit_bytes=VMEM_LIMIT),
        name="cast_weights",
    )(*weights)


def kernel(x, positions, norm_mix_w, w_in, b_in, conv_w, conv_b, mlstm_norm_w, w_proj_att, w_proj_mlstm,
           w_out, norm_mlp_w, w_ff1, w_ff2, final_norm_w):
    B, S, D = x.shape
    T = B * S
    depth = w_in.shape[0]
    assert D == D_MODEL and S % ATT_BLK == 0 and T % TM_IN == 0 and T % TM_POST == 0
    assert math.isclose(ATT_HEAD_DIM ** -0.5, 0.125)
    rope_c, rope_s = _rope_tables(positions)
    w_t = _cast_in_proj(jnp.swapaxes(w_in, 1, 2))
    wpa, wpm, wo, w1, w2 = _cast_weights((w_proj_att, w_proj_mlstm, w_out, w_ff1, w_ff2))
    q_scale = jnp.where(jnp.arange(W_PAD) < COL_AK, ATT_HEAD_DIM ** -0.5, 1.0).astype(F32)
    b_all = (jnp.pad(b_in, ((0, 0), (0, W_PAD - D_IN))) * q_scale).reshape(depth, 1, W_PAD)
    x2d = x.reshape(T, D)
    final_w = final_norm_w.reshape(1, D).astype(F32)
    for l in range(depth):
        aq, ak, av, mq, mk, mv, mo, ga, gb, gates, gates_t = _inproj(
            x2d, norm_mix_w.reshape(depth, 1, D), w_t, b_all, rope_c, rope_s,
            conv_w, conv_b.reshape(depth, 1, -1), S, l)
        ya = _attention(aq.reshape(B, S, ATT_WIDTH), ak.reshape(B, S, ATT_WIDTH), av.reshape(B, S, ATT_WIDTH))
        yb = _mlstm(mq, mk, mv, mo, gates, gates_t, mlstm_norm_w[l].reshape(1, -1), B)
        x2d = _post(x2d, ya.reshape(T, ATT_WIDTH), yb, ga, gb, wpa, wpm, wo,
                    norm_mlp_w.reshape(depth, 1, D), w1, w2, final_w, l, final_norm=(l == depth - 1))
    return x2d.reshape(B, S, D)
```

```python
import functools
import math

import jax
import jax.numpy as jnp
from jax import lax
from jax.experimental import pallas as pl
from jax.experimental.pallas import tpu as pltpu

F32 = jnp.float32
BF16 = jnp.bfloat16

D_MODEL = 1024
ATT_HEADS = 8
ATT_HEAD_DIM = 64
ATT_WIDTH = ATT_HEADS * ATT_HEAD_DIM
ATT_SPAN = 128
DILATIONS = (1, 4, 16)
ROPE_THETA = 500000.0
ROPE_DIM = ATT_HEAD_DIM // 4
MLSTM_HEADS = 4
MLSTM_HEAD_DIM = 128
MLSTM_WIDTH = MLSTM_HEADS * MLSTM_HEAD_DIM
CONV_WIDTH = 4
D_FF = 4 * D_MODEL
NORM_EPS = 1e-6

COL_AQ = 0
COL_AK = COL_AQ + ATT_WIDTH
COL_AV = COL_AK + ATT_WIDTH
COL_MQ = COL_AV + ATT_WIDTH
COL_MK = COL_MQ + MLSTM_WIDTH
COL_MV = COL_MK + MLSTM_WIDTH
COL_MO = COL_MV + MLSTM_WIDTH
COL_MI = COL_MO + MLSTM_WIDTH
COL_MF = COL_MI + MLSTM_HEADS
COL_GA = COL_MF + MLSTM_HEADS
COL_GB = COL_GA + D_MODEL
D_IN = COL_GB + D_MODEL

LANES = 128
assert COL_MI % LANES == 0
W_MAIN = D_IN // LANES * LANES
W_PAD = W_MAIN + LANES
GATE_SHIFT = 2 * MLSTM_HEADS

TM_IN = 512
TM_POST = 512
ATT_BLK = 2048
MCHUNK = 128
TM_MLSTM = 512
VMEM_LIMIT = 56 * 1024 * 1024
NEG = -1e30


def _const_spec(shape):
    nd = len(shape)
    return pl.BlockSpec(shape, lambda *_: (0,) * nd, pipeline_mode=pl.Buffered(1))


ROPE_HALF = ROPE_DIM // 2
ROPE_PACK = LANES // ROPE_HALF
ROPE_ROWS = 128


def _rope_table_kernel(pos_ref, invf_ref, c_ref, s_ref):
    ang = pos_ref[...] * invf_ref[...]
    cosx = jnp.cos(ang)
    sinx = jnp.sin(ang)
    src = lax.broadcasted_iota(jnp.int32, (LANES, LANES), 0)
    dst = lax.broadcasted_iota(jnp.int32, (LANES, LANES), 1)
    in_head = dst % ATT_HEAD_DIM
    rotary = in_head < ROPE_DIM
    sign = jnp.where(in_head < ROPE_HALF, -1.0, 1.0)
    one_elsewhere = jnp.where(lax.broadcasted_iota(jnp.int32, (1, LANES), 1) % ATT_HEAD_DIM < ROPE_DIM, 0.0, 1.0)
    for r in range(ROPE_PACK):
        pick = rotary & (src == r * ROPE_HALF + dst % ROPE_HALF)
        c_sel = jnp.where(pick, 1.0, 0.0)
        s_sel = jnp.where(pick, sign, 0.0)
        c_ref[pl.ds(r, ROPE_ROWS, stride=ROPE_PACK), :] = jnp.dot(
            cosx, c_sel, precision=lax.Precision.HIGHEST, preferred_element_type=F32) + one_elsewhere
        s_ref[pl.ds(r, ROPE_ROWS, stride=ROPE_PACK), :] = jnp.dot(
            sinx, s_sel, precision=lax.Precision.HIGHEST, preferred_element_type=F32)


def _rope_tables(positions):
    T = positions.size
    assert T % (ROPE_PACK * ROPE_ROWS) == 0
    pos = jnp.repeat(positions.astype(F32).reshape(T // ROPE_PACK, ROPE_PACK), ROPE_HALF, axis=1)
    inv_freq = ROPE_THETA ** (-jnp.arange(0, ROPE_DIM, 2, dtype=F32) / ROPE_DIM)
    invf = jnp.tile(inv_freq, ROPE_PACK).reshape(1, LANES)
    table = jax.ShapeDtypeStruct((T, LANES), F32)
    out_spec = pl.BlockSpec((ROPE_PACK * ROPE_ROWS, LANES), lambda i: (i, 0))
    return pl.pallas_call(
        _rope_table_kernel,
        grid=(T // (ROPE_PACK * ROPE_ROWS),),
        in_specs=[pl.BlockSpec((ROPE_ROWS, LANES), lambda i: (i, 0)), pl.BlockSpec((1, LANES), lambda i: (0, 0))],
        out_specs=(out_spec, out_spec),
        out_shape=(table, table),
        compiler_params=pltpu.CompilerParams(dimension_semantics=("parallel",)),
        name="rope_tables",
    )(pos, invf)


CONV_TAIL = 8
GATE_ROWS = 8


def _log_sigmoid(x):
    return jnp.minimum(x, 0.0) - jnp.log(1.0 + jnp.exp(-jnp.abs(x)))


def _inproj_kernel(x_ref, nw_ref, w_ref, b_ref, c_ref, s_ref, cw_ref, cb_ref,
                   aq_ref, ak_ref, av_ref, mq_ref, mk_ref, mv_ref, mo_ref, ga_ref, gb_ref, gt_ref, gtt_ref,
                   uq, uk, *, tiles_per_seq):
    @pl.when(pl.program_id(0) % tiles_per_seq == 0)
    def _():
        uq[TM_IN:TM_IN + CONV_TAIL, :] = jnp.zeros((CONV_TAIL, MLSTM_WIDTH), F32)
        uk[TM_IN:TM_IN + CONV_TAIL, :] = jnp.zeros((CONV_TAIL, MLSTM_WIDTH), F32)

    x = x_ref[...]
    var = jnp.mean(x * x, axis=-1, keepdims=True)
    h = (x * lax.rsqrt(var + NORM_EPS) * nw_ref[...]).astype(BF16)

    def proj(lo, width):
        return lax.dot_general(h, w_ref[lo:lo + width, :], (((1,), (1,)), ((), ())),
                               preferred_element_type=F32) + b_ref[:, lo:lo + width]

    cos = c_ref[...]
    sin = s_ref[...]
    lane = lax.broadcasted_iota(jnp.int32, cos.shape, 1)
    first_half = (lane % ATT_HEAD_DIM) < (ROPE_DIM // 2)

    def rope_store(dst_ref, lo):
        zz = proj(lo, ATT_WIDTH)
        for j in range(ATT_WIDTH // LANES):
            z = zz[:, j * LANES:(j + 1) * LANES]
            partner = jnp.where(first_half,
                                pltpu.roll(z, LANES - ROPE_DIM // 2, axis=1),
                                pltpu.roll(z, ROPE_DIM // 2, axis=1))
            dst_ref[:, j * LANES:(j + 1) * LANES] = z * cos + partner * sin

    def conv_silu_store(dst_ref, u_ref, lo, col0, scale):
        u_ref[0:CONV_TAIL, :] = u_ref[TM_IN:TM_IN + CONV_TAIL, :]
        u_ref[CONV_TAIL:CONV_TAIL + TM_IN, :] = proj(lo, MLSTM_WIDTH)
        for j in range(MLSTM_WIDTH // LANES):
            ls = slice(j * LANES, (j + 1) * LANES)
            ws = slice(col0 + j * LANES, col0 + (j + 1) * LANES)
            out = cb_ref[:, ws]
            for t in range(CONV_WIDTH):
                r0 = CONV_TAIL - (CONV_WIDTH - 1) + t
                out = out + cw_ref[t:t + 1, ws] * u_ref[r0:r0 + TM_IN, ls]
            out = out * jax.nn.sigmoid(out)
            dst_ref[:, ls] = (out if scale is None else out * scale).astype(dst_ref.dtype)

    rope_store(aq_ref, COL_AQ)
    rope_store(ak_ref, COL_AK)
    av_ref[...] = proj(COL_AV, ATT_WIDTH)
    conv_silu_store(mq_ref, uq, COL_MQ, 0, MLSTM_HEAD_DIM ** -0.5)
    conv_silu_store(mk_ref, uk, COL_MK, MLSTM_WIDTH, None)
    z_mv = proj(COL_MV, MLSTM_WIDTH)
    for cc in range(TM_IN // MCHUNK):
        for hd in range(MLSTM_HEADS):
            blk = z_mv[cc * MCHUNK:(cc + 1) * MCHUNK, hd * MLSTM_HEAD_DIM:(hd + 1) * MLSTM_HEAD_DIM]
            mv_ref[cc, hd * MLSTM_HEAD_DIM:(hd + 1) * MLSTM_HEAD_DIM, :] = blk.T.astype(mv_ref.dtype)
    mo_ref[...] = proj(COL_MO, MLSTM_WIDTH).astype(mo_ref.dtype)

    z_tail = proj(COL_MI, W_PAD - COL_MI)
    tiles = [z_tail[:, j * LANES:(j + 1) * LANES] for j in range((W_PAD - COL_MI) // LANES)]
    rolled = [pltpu.roll(t, LANES - GATE_SHIFT, axis=1) for t in tiles]
    low_lanes = lane < LANES - GATE_SHIFT
    for j in range(D_MODEL // LANES):
        ga_ref[:, j * LANES:(j + 1) * LANES] = jnp.where(low_lanes, rolled[j], rolled[j + 1]).astype(ga_ref.dtype)
        k = j + D_MODEL // LANES
        gb_ref[:, j * LANES:(j + 1) * LANES] = jnp.where(low_lanes, rolled[k], rolled[k + 1]).astype(gb_ref.dtype)

    zg = jnp.where(lane < GATE_SHIFT, tiles[0], 0.0)
    logf = _log_sigmoid(zg)
    ri = lax.broadcasted_iota(jnp.int32, (MCHUNK, MCHUNK), 0)
    ci = lax.broadcasted_iota(jnp.int32, (MCHUNK, MCHUNK), 1)
    tri = (ci <= ri).astype(F32)
    is_input_gate = lax.broadcasted_iota(jnp.int32, (MCHUNK, LANES), 1) < MLSTM_HEADS
    for cc in range(TM_IN // MCHUNK):
        rows = slice(cc * MCHUNK, (cc + 1) * MCHUNK)
        bcum = jnp.dot(tri, logf[rows], precision=lax.Precision.HIGHEST, preferred_element_type=F32)
        gc = jnp.where(is_input_gate, zg[rows], bcum)
        gt_ref[rows, :] = gc
        gtt_ref[cc] = gc.T[0:GATE_ROWS, :]


def _layer_spec(shape, layer):
    nd = len(shape)
    return pl.BlockSpec((None,) + tuple(shape), lambda *_: (layer,) + (0,) * nd, pipeline_mode=pl.Buffered(1))


def _inproj(x2d, norm_w, w_t, b_all, rope_c, rope_s, conv_w, conv_b, seq_len, layer):
    T = x2d.shape[0]
    tile = lambda w: pl.BlockSpec((TM_IN, w), lambda i: (i, 0))
    out_shapes = (
        jax.ShapeDtypeStruct((T, ATT_WIDTH), F32),
        jax.ShapeDtypeStruct((T, ATT_WIDTH), F32),
        jax.ShapeDtypeStruct((T, ATT_WIDTH), F32),
        jax.ShapeDtypeStruct((T, MLSTM_WIDTH), BF16),
        jax.ShapeDtypeStruct((T, MLSTM_WIDTH), BF16),
        jax.ShapeDtypeStruct((T // MCHUNK, MLSTM_WIDTH, MCHUNK), BF16),
        jax.ShapeDtypeStruct((T, MLSTM_WIDTH), BF16),
        jax.ShapeDtypeStruct((T, D_MODEL), BF16),
        jax.ShapeDtypeStruct((T, D_MODEL), BF16),
        jax.ShapeDtypeStruct((T, LANES), F32),
        jax.ShapeDtypeStruct((T // MCHUNK, GATE_ROWS, MCHUNK), F32),
    )
    per_chunk = lambda rows: pl.BlockSpec((TM_IN // MCHUNK, rows, MCHUNK), lambda i: (i, 0, 0))
    out_specs = tuple(tile(s.shape[1]) if len(s.shape) == 2 else per_chunk(s.shape[1]) for s in out_shapes)
    assert seq_len % TM_IN == 0
    return pl.pallas_call(
        functools.partial(_inproj_kernel, tiles_per_seq=seq_len // TM_IN),
        grid=(T // TM_IN,),
        in_specs=[tile(D_MODEL), _layer_spec((1, D_MODEL), layer), _layer_spec((W_PAD, D_MODEL), layer),
                  _layer_spec((1, W_PAD), layer), tile(LANES), tile(LANES),
                  _layer_spec((CONV_WIDTH, 2 * MLSTM_WIDTH), layer), _layer_spec((1, 2 * MLSTM_WIDTH), layer)],
        out_specs=out_specs,
        out_shape=out_shapes,
        scratch_shapes=[pltpu.VMEM((TM_IN + CONV_TAIL, MLSTM_WIDTH), F32),
                        pltpu.VMEM((TM_IN + CONV_TAIL, MLSTM_WIDTH), F32)],
        compiler_params=pltpu.CompilerParams(dimension_semantics=("arbitrary",),
                                             vmem_limit_bytes=VMEM_LIMIT),
        name="inproj",
    )(x2d, norm_w, w_t, b_all, rope_c, rope_s, conv_w, conv_b)


ATT_UNROLL = 16


def _attn_kernel(q_ref, k_ref, v_ref, o_ref,
                 kd1, vd1, qd4, kd4, vd4, qd16, kd16, vd16, tmp, acc_s, m_s, l_s, bias_s):
    j = pl.program_id(2)
    blk = ATT_SPAN
    nsub = {d: ATT_BLK // d // blk for d in DILATIONS}
    kv_bufs = ((1, kd1, vd1), (4, kd4, vd4), (16, kd16, vd16))

    row = lax.broadcasted_iota(jnp.int32, (2 * blk, 2 * blk), 0) % blk
    col = lax.broadcasted_iota(jnp.int32, (2 * blk, 2 * blk), 1)
    band = (col >= row) & (col <= row + ATT_SPAN)
    bias_s[0] = jnp.where(band, 0.0, NEG)
    bias_s[1] = jnp.where(band & (col >= blk), 0.0, NEG)

    @pl.when(j == 0)
    def _():
        for d, kd, vd in kv_bufs:
            n = ATT_BLK // d
            for r in range(d):
                base = r * (n + blk)
                kd[base:base + blk] = jnp.zeros((blk, LANES), BF16)
                vd[base:base + blk] = jnp.zeros((blk, LANES), BF16)

    @pl.when(j != 0)
    def _():
        for d, kd, vd in kv_bufs:
            n = ATT_BLK // d
            for r in range(d):
                base = r * (n + blk)
                kd[base:base + blk] = kd[base + n:base + n + blk]
                vd[base:base + blk] = vd[base + n:base + n + blk]

    def deinterleave(src_ref, dst1, dst4, dst16, is_kv):
        pad = blk if is_kv else 0
        if dst1 is not None:
            dst1[blk:blk + ATT_BLK] = src_ref[0].astype(BF16)
        n4 = ATT_BLK // 4
        for r4 in range(4):
            t4 = src_ref[0, pl.ds(r4, n4, stride=4), :]
            tmp[r4] = t4
            o4 = r4 * (n4 + pad) + pad
            dst4[o4:o4 + n4] = t4.astype(BF16)
        n16 = ATT_BLK // 16
        for r4 in range(4):
            for rr in range(4):
                o16 = (4 * rr + r4) * (n16 + pad) + pad
                dst16[o16:o16 + n16] = tmp[r4, pl.ds(rr, n16, stride=4), :].astype(BF16)

    deinterleave(q_ref, None, qd4, qd16, False)
    deinterleave(k_ref, kd1, kd4, kd16, True)
    deinterleave(v_ref, vd1, vd4, vd16, True)

    head_a = lax.broadcasted_iota(jnp.int32, (blk, LANES), 1) < ATT_HEAD_DIM

    def unit(q2, k2, v2, bias):
        zero = jnp.zeros_like(q2)
        qs = jnp.concatenate([jnp.where(head_a, q2, zero), jnp.where(head_a, zero, q2)], axis=0)
        s = lax.dot_general(qs, k2, (((1,), (1,)), ((), ())), preferred_element_type=F32) + bias
        m = jnp.max(s, axis=-1, keepdims=True)
        p = jnp.exp(s - m).astype(BF16)
        v_aug = jnp.concatenate([v2, jnp.ones_like(v2)], axis=1)
        pv = jnp.dot(p, v_aug, preferred_element_type=F32)
        acc = jnp.where(head_a, pv[:blk, :LANES], pv[blk:, :LANES])
        ll = jnp.where(head_a, pv[:blk, LANES:], pv[blk:, LANES:])
        mm = jnp.where(head_a, m[:blk], m[blk:])
        return acc, mm, ll

    first_blk = jnp.where(j == 0, 1, 0)

    def body1(u, carry):
        r0 = pl.multiple_of(u * blk, blk)
        q2 = q_ref[0, pl.ds(r0, blk), :].astype(BF16)
        bias = bias_s[jnp.where(u == 0, first_blk, 0)]
        acc, mm, ll = unit(q2, kd1[pl.ds(r0, 2 * blk), :], vd1[pl.ds(r0, 2 * blk), :], bias)
        acc_s[0, pl.ds(r0, blk), :] = acc
        m_s[0, pl.ds(r0, blk), :] = mm
        l_s[0, pl.ds(r0, blk), :] = ll
        return carry

    lax.fori_loop(0, nsub[1], body1, 0, unroll=ATT_UNROLL)

    def make_body(g, d, qd, kd, vd):
        n = ATT_BLK // d

        def body(u, carry):
            r = u // nsub[d]
            sb = u % nsub[d]
            q0 = pl.multiple_of(u * blk, blk)
            k0 = pl.multiple_of(r * (n + blk) + sb * blk, blk)
            bias = bias_s[jnp.where(sb == 0, first_blk, 0)]
            acc, mm, ll = unit(qd[pl.ds(q0, blk), :], kd[pl.ds(k0, 2 * blk), :], vd[pl.ds(k0, 2 * blk), :], bias)
            t0 = sb * (blk * d) + r
            acc_s[g, pl.ds(t0, blk, stride=d), :] = acc
            m_s[g, pl.ds(t0, blk, stride=d), :] = mm
            l_s[g, pl.ds(t0, blk, stride=d), :] = ll
            return carry

        return body

    lax.fori_loop(0, ATT_BLK // blk, make_body(1, 4, qd4, kd4, vd4), 0, unroll=ATT_UNROLL)
    lax.fori_loop(0, ATT_BLK // blk, make_body(2, 16, qd16, kd16, vd16), 0, unroll=ATT_UNROLL)

    def combine(u, carry):
        r0 = pl.multiple_of(u * blk, blk)
        sl = pl.ds(r0, blk)
        m0, m1, m2 = m_s[0, sl, :], m_s[1, sl, :], m_s[2, sl, :]
        mx = jnp.maximum(jnp.maximum(m0, m1), m2)
        w0, w1, w2 = jnp.exp(m0 - mx), jnp.exp(m1 - mx), jnp.exp(m2 - mx)
        num = w0 * acc_s[0, sl, :] + w1 * acc_s[1, sl, :] + w2 * acc_s[2, sl, :]
        den = w0 * l_s[0, sl, :] + w1 * l_s[1, sl, :] + w2 * l_s[2, sl, :]
        o_ref[0, sl, :] = (num / den).astype(o_ref.dtype)
        return carry

    lax.fori_loop(0, ATT_BLK // blk, combine, 0, unroll=2)


def _attention(aq, ak, av):
    B, S, _ = aq.shape
    blk = ATT_SPAN
    cur = pl.BlockSpec((1, ATT_BLK, LANES), lambda b, hp, j: (b, j, hp))
    kv_rows = {d: d * (ATT_BLK // d + blk) for d in DILATIONS}
    scratch = [
        pltpu.VMEM((kv_rows[1], LANES), BF16), pltpu.VMEM((kv_rows[1], LANES), BF16),
        pltpu.VMEM((ATT_BLK, LANES), BF16),
        pltpu.VMEM((kv_rows[4], LANES), BF16), pltpu.VMEM((kv_rows[4], LANES), BF16),
        pltpu.VMEM((ATT_BLK, LANES), BF16),
        pltpu.VMEM((kv_rows[16], LANES), BF16), pltpu.VMEM((kv_rows[16], LANES), BF16),
        pltpu.VMEM((4, ATT_BLK // 4, LANES), F32),
        pltpu.VMEM((3, ATT_BLK, LANES), F32), pltpu.VMEM((3, ATT_BLK, LANES), F32),
        pltpu.VMEM((3, ATT_BLK, LANES), F32),
        pltpu.VMEM((2, 2 * blk, 2 * blk), F32),
    ]
    return pl.pallas_call(
        _attn_kernel,
        grid=(B, ATT_WIDTH // LANES, S // ATT_BLK),
        in_specs=[cur, cur, cur],
        out_specs=cur,
        out_shape=jax.ShapeDtypeStruct((B, S, ATT_WIDTH), BF16),
        scratch_shapes=scratch,
        compiler_params=pltpu.CompilerParams(dimension_semantics=("parallel", "parallel", "arbitrary"),
                                             vmem_limit_bytes=VMEM_LIMIT),
        name="dilated_attention",
    )(aq, ak, av)


NORM_ROWS = 16


def _mlstm_kernel(q_ref, k_ref, vt_ref, mo_ref, g_ref, gt_ref, nw_ref, y_ref, c_state, m_state):
    L = MCHUNK
    D = MLSTM_HEAD_DIM
    nt_dims = (((1,), (1,)), ((), ()))

    @pl.when(pl.program_id(1) == 0)
    def _():
        c_state[...] = jnp.zeros_like(c_state)
        m_state[...] = jnp.zeros_like(m_state)

    key = lax.broadcasted_iota(jnp.int32, (L, L), 0)
    qry = lax.broadcasted_iota(jnp.int32, (L, L), 1)
    visible = key <= qry
    ones_rows = jnp.ones((NORM_ROWS, L), BF16)

    units = [(cc, h) for cc in range(TM_MLSTM // L) for h in range(MLSTM_HEADS)]
    rows = lambda cc: slice(cc * L, (cc + 1) * L)
    cols = lambda h: slice(h * D, (h + 1) * D)

    s_t = {u: lax.dot_general(k_ref[rows(u[0]), cols(u[1])], q_ref[rows(u[0]), cols(u[1])], nt_dims,
                              preferred_element_type=F32) for u in units}
    b_row, g_tot, pm, vt_aug, intra, m_loc, kv = {}, {}, {}, {}, {}, {}, {}
    for u in units:
        cc, h = u
        fh = MLSTM_HEADS + h
        r_col = g_ref[rows(cc), h:h + 1] - g_ref[rows(cc), fh:fh + 1]
        b_row[u] = gt_ref[cc, fh:fh + 1, :]
        g_tot[u] = b_row[u][:, L - 1:L]
        r_vis = jnp.where(visible, r_col, NEG)
        pm[u] = jnp.max(r_vis, axis=0, keepdims=True)
        p_t = (jnp.exp(r_vis - pm[u]) * s_t[u]).astype(BF16)
        vt_aug[u] = jnp.concatenate([vt_ref[cc, cols(h), :], ones_rows], axis=0)
        intra[u] = jnp.dot(vt_aug[u], p_t, preferred_element_type=F32)
    for u in units:
        cc, h = u
        r_row = gt_ref[cc, h:h + 1, :] - b_row[u]
        m_loc[u] = jnp.max(g_tot[u] + r_row, axis=-1, keepdims=True)
        vw = (vt_aug[u].astype(F32) * jnp.exp(g_tot[u] + r_row - m_loc[u])).astype(BF16)
        kv[u] = jnp.dot(vw, k_ref[rows(cc), cols(h)], preferred_element_type=F32)

    c_aug = [c_state[h] for h in range(MLSTM_HEADS)]
    m_prev = [m_state[h:h + 1, 0:1] for h in range(MLSTM_HEADS)]
    for u in units:
        cc, h = u
        inter = lax.dot_general(c_aug[h].astype(BF16), q_ref[rows(cc), cols(h)], nt_dims,
                                preferred_element_type=F32)
        mm = jnp.maximum(pm[u], m_prev[h])
        nd = jnp.exp(pm[u] - mm) * intra[u] + jnp.exp(m_prev[h] - mm) * inter
        inv = 1.0 / jnp.maximum(jnp.abs(nd[D:D + 1, :]), jnp.exp(-(b_row[u] + mm)))
        hh = nd[0:D, :] * inv
        hn = (hh * lax.rsqrt(jnp.mean(hh * hh, axis=0, keepdims=True) + NORM_EPS)).T
        o_gate = jax.nn.sigmoid(mo_ref[rows(cc), cols(h)].astype(F32))
        y_ref[rows(cc), cols(h)] = (o_gate * hn * nw_ref[:, cols(h)]).astype(y_ref.dtype)

        m_new = jnp.maximum(g_tot[u] + m_prev[h], m_loc[u])
        c_aug[h] = jnp.exp(g_tot[u] + m_prev[h] - m_new) * c_aug[h] + jnp.exp(m_loc[u] - m_new) * kv[u]
        m_prev[h] = m_new
    for h in range(MLSTM_HEADS):
        c_state[h] = c_aug[h]
        m_state[h:h + 1, :] = jnp.broadcast_to(m_prev[h], (1, LANES))


def _mlstm(mq, mk, mvt, mo, gates, gates_t, norm_w, batch):
    T = mq.shape[0]
    nt = T // batch // TM_MLSTM
    nch = TM_MLSTM // MCHUNK
    tile = lambda w: pl.BlockSpec((TM_MLSTM, w), lambda b, c: (b * nt + c, 0))
    per_chunk = lambda r: pl.BlockSpec((nch, r, MCHUNK), lambda b, c: (b * nt + c, 0, 0))
    return pl.pallas_call(
        _mlstm_kernel,
        grid=(batch, nt),
        in_specs=[tile(MLSTM_WIDTH), tile(MLSTM_WIDTH), per_chunk(MLSTM_WIDTH), tile(MLSTM_WIDTH), tile(LANES),
                  per_chunk(GATE_ROWS), pl.BlockSpec((1, MLSTM_WIDTH), lambda b, c: (0, 0))],
        out_specs=tile(MLSTM_WIDTH),
        out_shape=jax.ShapeDtypeStruct((T, MLSTM_WIDTH), BF16),
        scratch_shapes=[
            pltpu.VMEM((MLSTM_HEADS, MLSTM_HEAD_DIM + NORM_ROWS, MLSTM_HEAD_DIM), F32),
            pltpu.VMEM((8, LANES), F32),
        ],
        compiler_params=pltpu.CompilerParams(dimension_semantics=("parallel", "arbitrary"),
                                             vmem_limit_bytes=VMEM_LIMIT),
        name="mlstm",
    )(mq, mk, mvt, mo, gates, gates_t, norm_w)


def _post_kernel(x_ref, ya_ref, yb_ref, ga_ref, gb_ref, wpa_ref, wpm_ref, wo_ref, nw_ref, w1_ref, w2_ref,
                 fw_ref, o_ref, *, final_norm):
    pa = jnp.dot(ya_ref[...], wpa_ref[...], preferred_element_type=F32)
    pb = jnp.dot(yb_ref[...], wpm_ref[...], preferred_element_type=F32)
    mixed = (jax.nn.sigmoid(ga_ref[...].astype(F32)) * pa
             + jax.nn.sigmoid(gb_ref[...].astype(F32)) * pb).astype(BF16)
    x1 = x_ref[...] + jnp.dot(mixed, wo_ref[...], preferred_element_type=F32)
    var = jnp.mean(x1 * x1, axis=-1, keepdims=True)
    h2 = (x1 * lax.rsqrt(var + NORM_EPS) * nw_ref[...]).astype(BF16)
    acc = x1
    ff_chunk = D_MODEL
    for c in range(D_FF // ff_chunk):
        cs = slice(c * ff_chunk, (c + 1) * ff_chunk)
        u = jnp.maximum(jnp.dot(h2, w1_ref[:, cs], preferred_element_type=F32), 0.0)
        acc = acc + jnp.dot((u * u).astype(BF16), w2_ref[cs, :], preferred_element_type=F32)
    if final_norm:
        var = jnp.mean(acc * acc, axis=-1, keepdims=True)
        acc = acc * lax.rsqrt(var + NORM_EPS) * fw_ref[...]
    o_ref[...] = acc


def _post(x2d, ya, yb, ga, gb, wpa, wpm, wo, norm_w, w1, w2, final_w, layer, final_norm):
    T = x2d.shape[0]
    tile = lambda w: pl.BlockSpec((TM_POST, w), lambda i: (i, 0))
    return pl.pallas_call(
        functools.partial(_post_kernel, final_norm=final_norm),
        grid=(T // TM_POST,),
        in_specs=[tile(D_MODEL), tile(ATT_WIDTH), tile(MLSTM_WIDTH), tile(D_MODEL), tile(D_MODEL),
                  _layer_spec((ATT_WIDTH, D_MODEL), layer), _layer_spec((MLSTM_WIDTH, D_MODEL), layer),
                  _layer_spec((D_MODEL, D_MODEL), layer), _layer_spec((1, D_MODEL), layer),
                  _layer_spec((D_MODEL, D_FF), layer), _layer_spec((D_FF, D_MODEL), layer),
                  _const_spec((1, D_MODEL))],
        out_specs=tile(D_MODEL),
        out_shape=jax.ShapeDtypeStruct((T, D_MODEL), F32),
        compiler_params=pltpu.CompilerParams(dimension_semantics=("parallel",),
                                             vmem_limit_bytes=VMEM_LIMIT),
        name="post",
    )(x2d, ya, yb, ga, gb, wpa, wpm, wo, norm_w, w1, w2, final_w)


PREP_STEPS = 16


PREP_IN_ROWS = 384


def _cast_in_proj_kernel(w_ref, o_ref):
    row = pl.program_id(0) * PREP_IN_ROWS + lax.broadcasted_iota(jnp.int32, o_ref.shape, 1)
    w = w_ref[...]
    w = jnp.where(row < COL_AK, w * (ATT_HEAD_DIM ** -0.5), w)
    o_ref[...] = jnp.where(row < D_IN, w, 0.0).astype(BF16)


def _cast_in_proj(w_t):
    layers, _, feat = w_t.shape
    assert W_PAD % PREP_IN_ROWS == 0
    spec = pl.BlockSpec((layers, PREP_IN_ROWS, feat), lambda i: (0, i, 0))
    return pl.pallas_call(
        _cast_in_proj_kernel,
        grid=(W_PAD // PREP_IN_ROWS,),
        in_specs=[spec],
        out_specs=spec,
        out_shape=jax.ShapeDtypeStruct((layers, W_PAD, feat), BF16),
        compiler_params=pltpu.CompilerParams(dimension_semantics=("parallel",), vmem_limit_bytes=VMEM_LIMIT),
        name="cast_in_proj",
    )(w_t)


def _cast_weights_kernel(*refs):
    n = len(refs) // 2
    for src, dst in zip(refs[:n], refs[n:]):
        dst[...] = src[...].astype(BF16)


def _cast_weights(weights):
    def slab(shape):
        assert shape[1] % (PREP_STEPS * 16) == 0
        return pl.BlockSpec((shape[0], shape[1] // PREP_STEPS, shape[2]), lambda i: (0, i, 0))

    return pl.pallas_call(
        _cast_weights_kernel,
        grid=(PREP_STEPS,),
        in_specs=[slab(a.shape) for a in weights],
        out_specs=[slab(a.shape) for a in weights],
        out_shape=[jax.ShapeDtypeStruct(a.shape, BF16) for a in weights],
        compiler_params=pltpu.CompilerParams(dimension_semantics=("parallel",), vmem_limit_bytes=VMEM_LIMIT),
        name="cast_weights",
    )(*weights)


def kernel(x, positions, norm_mix_w, w_in, b_in, conv_w, conv_b, mlstm_norm_w, w_proj_att, w_proj_mlstm,
           w_out, norm_mlp_w, w_ff1, w_ff2, final_norm_w):
    B, S, D = x.shape
    T = B * S
    depth = w_in.shape[0]
    assert D == D_MODEL and S % ATT_BLK == 0 and T % TM_IN == 0 and T % TM_POST == 0
    assert math.isclose(ATT_HEAD_DIM ** -0.5, 0.125)
    rope_c, rope_s = _rope_tables(positions)
    w_t = _cast_in_proj(jnp.swapaxes(w_in, 1, 2))
    wpa, wpm, wo, w1, w2 = _cast_weights((w_proj_att, w_proj_mlstm, w_out, w_ff1, w_ff2))
    q_scale = jnp.where(jnp.arange(W_PAD) < COL_AK, ATT_HEAD_DIM ** -0.5, 1.0).astype(F32)
    b_all = (jnp.pad(b_in, ((0, 0), (0, W_PAD - D_IN))) * q_scale).reshape(depth, 1, W_PAD)
    x2d = x.reshape(T, D)
    final_w = final_norm_w.reshape(1, D).astype(F32)
    for l in range(depth):
        aq, ak, av, mq, mk, mv, mo, ga, gb, gates, gates_t = _inproj(
            x2d, norm_mix_w.reshape(depth, 1, D), w_t, b_all, rope_c, rope_s,
            conv_w, conv_b.reshape(depth, 1, -1), S, l)
        ya = _attention(aq.reshape(B, S, ATT_WIDTH), ak.reshape(B, S, ATT_WIDTH), av.reshape(B, S, ATT_WIDTH))
        yb = _mlstm(mq, mk, mv, mo, gates, gates_t, mlstm_norm_w[l].reshape(1, -1), B)
        x2d = _post(x2d, ya.reshape(T, ATT_WIDTH), yb, ga, gb, wpa, wpm, wo,
                    norm_mlp_w.reshape(depth, 1, D), w1, w2, final_w, l, final_norm=(l == depth - 1))
    return x2d.reshape(B, S, D)
```

```python
import functools
import math

import jax
import jax.numpy as jnp
from jax import lax
from jax.experimental import pallas as pl
from jax.experimental.pallas import tpu as pltpu

F32 = jnp.float32
BF16 = jnp.bfloat16

D_MODEL = 1024
ATT_HEADS = 8
ATT_HEAD_DIM = 64
ATT_WIDTH = ATT_HEADS * ATT_HEAD_DIM
ATT_SPAN = 128
DILATIONS = (1, 4, 16)
ROPE_THETA = 500000.0
ROPE_DIM = ATT_HEAD_DIM // 4
MLSTM_HEADS = 4
MLSTM_HEAD_DIM = 128
MLSTM_WIDTH = MLSTM_HEADS * MLSTM_HEAD_DIM
CONV_WIDTH = 4
D_FF = 4 * D_MODEL
NORM_EPS = 1e-6

COL_AQ = 0
COL_AK = COL_AQ + ATT_WIDTH
COL_AV = COL_AK + ATT_WIDTH
COL_MQ = COL_AV + ATT_WIDTH
COL_MK = COL_MQ + MLSTM_WIDTH
COL_MV = COL_MK + MLSTM_WIDTH
COL_MO = COL_MV + MLSTM_WIDTH
COL_MI = COL_MO + MLSTM_WIDTH
COL_MF = COL_MI + MLSTM_HEADS
COL_GA = COL_MF + MLSTM_HEADS
COL_GB = COL_GA + D_MODEL
D_IN = COL_GB + D_MODEL

LANES = 128
assert COL_MI % LANES == 0
W_MAIN = D_IN // LANES * LANES
W_PAD = W_MAIN + LANES
GATE_SHIFT = 2 * MLSTM_HEADS

TM_IN = 512
TM_POST = 512
ATT_BLK = 2048
MCHUNK = 128
TM_MLSTM = 512
VMEM_LIMIT = 56 * 1024 * 1024
NEG = -1e30


def _const_spec(shape):
    nd = len(shape)
    return pl.BlockSpec(shape, lambda *_: (0,) * nd, pipeline_mode=pl.Buffered(1))


ROPE_HALF = ROPE_DIM // 2
ROPE_PACK = LANES // ROPE_HALF
ROPE_ROWS = 128


def _rope_table_kernel(pos_ref, invf_ref, c_ref, s_ref):
    ang = pos_ref[...] * invf_ref[...]
    cosx = jnp.cos(ang)
    sinx = jnp.sin(ang)
    src = lax.broadcasted_iota(jnp.int32, (LANES, LANES), 0)
    dst = lax.broadcasted_iota(jnp.int32, (LANES, LANES), 1)
    in_head = dst % ATT_HEAD_DIM
    rotary = in_head < ROPE_DIM
    sign = jnp.where(in_head < ROPE_HALF, -1.0, 1.0)
    one_elsewhere = jnp.where(lax.broadcasted_iota(jnp.int32, (1, LANES), 1) % ATT_HEAD_DIM < ROPE_DIM, 0.0, 1.0)
    for r in range(ROPE_PACK):
        pick = rotary & (src == r * ROPE_HALF + dst % ROPE_HALF)
        c_sel = jnp.where(pick, 1.0, 0.0)
        s_sel = jnp.where(pick, sign, 0.0)
        c_ref[pl.ds(r, ROPE_ROWS, stride=ROPE_PACK), :] = jnp.dot(
            cosx, c_sel, precision=lax.Precision.HIGHEST, preferred_element_type=F32) + one_elsewhere
        s_ref[pl.ds(r, ROPE_ROWS, stride=ROPE_PACK), :] = jnp.dot(
            sinx, s_sel, precision=lax.Precision.HIGHEST, preferred_element_type=F32)


def _rope_tables(positions):
    T = positions.size
    assert T % (ROPE_PACK * ROPE_ROWS) == 0
    pos = jnp.repeat(positions.astype(F32).reshape(T // ROPE_PACK, ROPE_PACK), ROPE_HALF, axis=1)
    inv_freq = ROPE_THETA ** (-jnp.arange(0, ROPE_DIM, 2, dtype=F32) / ROPE_DIM)
    invf = jnp.tile(inv_freq, ROPE_PACK).reshape(1, LANES)
    table = jax.ShapeDtypeStruct((T, LANES), F32)
    out_spec = pl.BlockSpec((ROPE_PACK * ROPE_ROWS, LANES), lambda i: (i, 0))
    return pl.pallas_call(
        _rope_table_kernel,
        grid=(T // (ROPE_PACK * ROPE_ROWS),),
        in_specs=[pl.BlockSpec((ROPE_ROWS, LANES), lambda i: (i, 0)), pl.BlockSpec((1, LANES), lambda i: (0, 0))],
        out_specs=(out_spec, out_spec),
        out_shape=(table, table),
        compiler_params=pltpu.CompilerParams(dimension_semantics=("parallel",)),
        name="rope_tables",
    )(pos, invf)


CONV_TAIL = 8
GATE_ROWS = 8


def _log_sigmoid(x):
    return jnp.minimum(x, 0.0) - jnp.log(1.0 + jnp.exp(-jnp.abs(x)))


def _inproj_kernel(x_ref, nw_ref, w_ref, b_ref, c_ref, s_ref, cw_ref, cb_ref,
                   aq_ref, ak_ref, av_ref, mq_ref, mk_ref, mv_ref, mo_ref, ga_ref, gb_ref, gt_ref, gtt_ref,
                   u_s, res_s, *, tiles_per_seq):
    @pl.when(pl.program_id(0) % tiles_per_seq == 0)
    def _():
        u_s[:, TM_IN:TM_IN + CONV_TAIL, :] = jnp.zeros((u_s.shape[0], CONV_TAIL, LANES), F32)

    x = x_ref[...]
    var = jnp.mean(x * x, axis=-1, keepdims=True)
    h = (x * lax.rsqrt(var + NORM_EPS) * nw_ref[...]).astype(BF16)

    def proj(lo, width):
        return lax.dot_general(h, w_ref[lo:lo + width, :], (((1,), (1,)), ((), ())),
                               preferred_element_type=F32) + b_ref[:, lo:lo + width]

    cos = c_ref[...]
    sin = s_ref[...]
    lane = lax.broadcasted_iota(jnp.int32, cos.shape, 1)
    first_half = (lane % ATT_HEAD_DIM) < (ROPE_DIM // 2)

    def rope_store(dst_ref, lo):
        zz = proj(lo, ATT_WIDTH)
        for j in range(ATT_WIDTH // LANES):
            z = zz[:, j * LANES:(j + 1) * LANES]
            partner = jnp.where(first_half,
                                pltpu.roll(z, LANES - ROPE_DIM // 2, axis=1),
                                pltpu.roll(z, ROPE_DIM // 2, axis=1))
            dst_ref[:, j * LANES:(j + 1) * LANES] = z * cos + partner * sin

    def conv_silu_store(dst_ref, slab0, lo, col0, scale):
        z = proj(lo, MLSTM_WIDTH)
        half = TM_IN // 2
        for j in range(MLSTM_WIDTH // LANES):
            sl = slab0 + j
            ws = slice(col0 + j * LANES, col0 + (j + 1) * LANES)
            u_s[sl, 0:CONV_TAIL, :] = u_s[sl, TM_IN:TM_IN + CONV_TAIL, :]
            u_s[sl, CONV_TAIL:CONV_TAIL + TM_IN, :] = z[:, j * LANES:(j + 1) * LANES]
            for parity in range(2):
                out = cb_ref[:, ws]
                for t in range(CONV_WIDTH):
                    r0 = CONV_TAIL - (CONV_WIDTH - 1) + t + parity
                    out = out + cw_ref[t:t + 1, ws] * u_s[sl, pl.ds(r0, half, stride=2), :]
                out = out * jax.nn.sigmoid(out)
                res_s[sl, pl.ds(parity, half, stride=2), :] = out if scale is None else out * scale
            dst_ref[:, j * LANES:(j + 1) * LANES] = res_s[sl].astype(dst_ref.dtype)

    rope_store(aq_ref, COL_AQ)
    rope_store(ak_ref, COL_AK)
    av_ref[...] = proj(COL_AV, ATT_WIDTH)
    conv_silu_store(mq_ref, 0, COL_MQ, 0, MLSTM_HEAD_DIM ** -0.5)
    conv_silu_store(mk_ref, MLSTM_WIDTH // LANES, COL_MK, MLSTM_WIDTH, None)
    z_mv = proj(COL_MV, MLSTM_WIDTH)
    for cc in range(TM_IN // MCHUNK):
        for hd in range(MLSTM_HEADS):
            blk = z_mv[cc * MCHUNK:(cc + 1) * MCHUNK, hd * MLSTM_HEAD_DIM:(hd + 1) * MLSTM_HEAD_DIM]
            mv_ref[cc, hd * MLSTM_HEAD_DIM:(hd + 1) * MLSTM_HEAD_DIM, :] = blk.T.astype(mv_ref.dtype)
    mo_ref[...] = proj(COL_MO, MLSTM_WIDTH).astype(mo_ref.dtype)

    z_tail = proj(COL_MI, W_PAD - COL_MI)
    tiles = [z_tail[:, j * LANES:(j + 1) * LANES] for j in range((W_PAD - COL_MI) // LANES)]
    rolled = [pltpu.roll(t, LANES - GATE_SHIFT, axis=1) for t in tiles]
    low_lanes = lane < LANES - GATE_SHIFT
    for j in range(D_MODEL // LANES):
        ga_ref[:, j * LANES:(j + 1) * LANES] = jnp.where(low_lanes, rolled[j], rolled[j + 1]).astype(ga_ref.dtype)
        k = j + D_MODEL // LANES
        gb_ref[:, j * LANES:(j + 1) * LANES] = jnp.where(low_lanes, rolled[k], rolled[k + 1]).astype(gb_ref.dtype)

    zg = jnp.where(lane < GATE_SHIFT, tiles[0], 0.0)
    logf = _log_sigmoid(zg)
    ri = lax.broadcasted_iota(jnp.int32, (MCHUNK, MCHUNK), 0)
    ci = lax.broadcasted_iota(jnp.int32, (MCHUNK, MCHUNK), 1)
    tri = (ci <= ri).astype(F32)
    is_input_gate = lax.broadcasted_iota(jnp.int32, (MCHUNK, LANES), 1) < MLSTM_HEADS
    for cc in range(TM_IN // MCHUNK):
        rows = slice(cc * MCHUNK, (cc + 1) * MCHUNK)
        bcum = jnp.dot(tri, logf[rows], precision=lax.Precision.HIGHEST, preferred_element_type=F32)
        gc = jnp.where(is_input_gate, zg[rows], bcum)
        gt_ref[rows, :] = gc
        gtt_ref[cc] = gc.T[0:GATE_ROWS, :]


def _layer_spec(shape, layer):
    nd = len(shape)
    return pl.BlockSpec((None,) + tuple(shape), lambda *_: (layer,) + (0,) * nd, pipeline_mode=pl.Buffered(1))


def _inproj(x2d, norm_w, w_t, b_all, rope_c, rope_s, conv_w, conv_b, seq_len, layer):
    T = x2d.shape[0]
    tile = lambda w: pl.BlockSpec((TM_IN, w), lambda i: (i, 0))
    out_shapes = (
        jax.ShapeDtypeStruct((T, ATT_WIDTH), F32),
        jax.ShapeDtypeStruct((T, ATT_WIDTH), F32),
        jax.ShapeDtypeStruct((T, ATT_WIDTH), F32),
        jax.ShapeDtypeStruct((T, MLSTM_WIDTH), BF16),
        jax.ShapeDtypeStruct((T, MLSTM_WIDTH), BF16),
        jax.ShapeDtypeStruct((T // MCHUNK, MLSTM_WIDTH, MCHUNK), BF16),
        jax.ShapeDtypeStruct((T, MLSTM_WIDTH), BF16),
        jax.ShapeDtypeStruct((T, D_MODEL), BF16),
        jax.ShapeDtypeStruct((T, D_MODEL), BF16),
        jax.ShapeDtypeStruct((T, LANES), F32),
        jax.ShapeDtypeStruct((T // MCHUNK, GATE_ROWS, MCHUNK), F32),
    )
    per_chunk = lambda rows: pl.BlockSpec((TM_IN // MCHUNK, rows, MCHUNK), lambda i: (i, 0, 0))
    out_specs = tuple(tile(s.shape[1]) if len(s.shape) == 2 else per_chunk(s.shape[1]) for s in out_shapes)
    assert seq_len % TM_IN == 0
    return pl.pallas_call(
        functools.partial(_inproj_kernel, tiles_per_seq=seq_len // TM_IN),
        grid=(T // TM_IN,),
        in_specs=[tile(D_MODEL), _layer_spec((1, D_MODEL), layer), _layer_spec((W_PAD, D_MODEL), layer),
                  _layer_spec((1, W_PAD), layer), tile(LANES), tile(LANES),
                  _layer_spec((CONV_WIDTH, 2 * MLSTM_WIDTH), layer), _layer_spec((1, 2 * MLSTM_WIDTH), layer)],
        out_specs=out_specs,
        out_shape=out_shapes,
        scratch_shapes=[pltpu.VMEM((2 * MLSTM_WIDTH // LANES, TM_IN + CONV_TAIL, LANES), F32),
                        pltpu.VMEM((2 * MLSTM_WIDTH // LANES, TM_IN, LANES), F32)],
        compiler_params=pltpu.CompilerParams(dimension_semantics=("arbitrary",),
                                             vmem_limit_bytes=VMEM_LIMIT),
        name="inproj",
    )(x2d, norm_w, w_t, b_all, rope_c, rope_s, conv_w, conv_b)


ATT_UNROLL = 16


def _attn_kernel(q_ref, k_ref, v_ref, o_ref,
                 kd1, vd1, qd4, kd4, vd4, qd16, kd16, vd16, tmp, acc_s, m_s, l_s, bias_s):
    j = pl.program_id(2)
    blk = ATT_SPAN
    nsub = {d: ATT_BLK // d // blk for d in DILATIONS}
    kv_bufs = ((1, kd1, vd1), (4, kd4, vd4), (16, kd16, vd16))

    row = lax.broadcasted_iota(jnp.int32, (2 * blk, 2 * blk), 0) % blk
    col = lax.broadcasted_iota(jnp.int32, (2 * blk, 2 * blk), 1)
    band = (col >= row) & (col <= row + ATT_SPAN)
    bias_s[0] = jnp.where(band, 0.0, NEG)
    bias_s[1] = jnp.where(band & (col >= blk), 0.0, NEG)

    @pl.when(j == 0)
    def _():
        for d, kd, vd in kv_bufs:
            n = ATT_BLK // d
            for r in range(d):
                base = r * (n + blk)
                kd[base:base + blk] = jnp.zeros((blk, LANES), BF16)
                vd[base:base + blk] = jnp.zeros((blk, LANES), BF16)

    @pl.when(j != 0)
    def _():
        for d, kd, vd in kv_bufs:
            n = ATT_BLK // d
            for r in range(d):
                base = r * (n + blk)
                kd[base:base + blk] = kd[base + n:base + n + blk]
                vd[base:base + blk] = vd[base + n:base + n + blk]

    def deinterleave(src_ref, dst1, dst4, dst16, is_kv):
        pad = blk if is_kv else 0
        if dst1 is not None:
            dst1[blk:blk + ATT_BLK] = src_ref[0].astype(BF16)
        n4 = ATT_BLK // 4
        for r4 in range(4):
            t4 = src_ref[0, pl.ds(r4, n4, stride=4), :]
            tmp[r4] = t4
            o4 = r4 * (n4 + pad) + pad
            dst4[o4:o4 + n4] = t4.astype(BF16)
        n16 = ATT_BLK // 16
        for r4 in range(4):
            for rr in range(4):
                o16 = (4 * rr + r4) * (n16 + pad) + pad
                dst16[o16:o16 + n16] = tmp[r4, pl.ds(rr, n16, stride=4), :].astype(BF16)

    deinterleave(q_ref, None, qd4, qd16, False)
    deinterleave(k_ref, kd1, kd4, kd16, True)
    deinterleave(v_ref, vd1, vd4, vd16, True)

    head_a = lax.broadcasted_iota(jnp.int32, (blk, LANES), 1) < ATT_HEAD_DIM

    def unit(q2, k2, v2, bias):
        zero = jnp.zeros_like(q2)
        qs = jnp.concatenate([jnp.where(head_a, q2, zero), jnp.where(head_a, zero, q2)], axis=0)
        s = lax.dot_general(qs, k2, (((1,), (1,)), ((), ())), preferred_element_type=F32) + bias
        m = jnp.max(s, axis=-1, keepdims=True)
        p = jnp.exp(s - m).astype(BF16)
        v_aug = jnp.concatenate([v2, jnp.ones_like(v2)], axis=1)
        pv = jnp.dot(p, v_aug, preferred_element_type=F32)
        acc = jnp.where(head_a, pv[:blk, :LANES], pv[blk:, :LANES])
        ll = jnp.where(head_a, pv[:blk, LANES:], pv[blk:, LANES:])
        mm = jnp.where(head_a, m[:blk], m[blk:])
        return acc, mm, ll

    first_blk = jnp.where(j == 0, 1, 0)

    def body1(u, carry):
        r0 = pl.multiple_of(u * blk, blk)
        q2 = q_ref[0, pl.ds(r0, blk), :].astype(BF16)
        bias = bias_s[jnp.where(u == 0, first_blk, 0)]
        acc, mm, ll = unit(q2, kd1[pl.ds(r0, 2 * blk), :], vd1[pl.ds(r0, 2 * blk), :], bias)
        acc_s[0, pl.ds(r0, blk), :] = acc
        m_s[0, pl.ds(r0, blk), :] = mm
        l_s[0, pl.ds(r0, blk), :] = ll
        return carry

    lax.fori_loop(0, nsub[1], body1, 0, unroll=ATT_UNROLL)

    def make_body(g, d, qd, kd, vd):
        n = ATT_BLK // d

        def body(u, carry):
            r = u // nsub[d]
            sb = u % nsub[d]
            q0 = pl.multiple_of(u * blk, blk)
            k0 = pl.multiple_of(r * (n + blk) + sb * blk, blk)
            bias = bias_s[jnp.where(sb == 0, first_blk, 0)]
            acc, mm, ll = unit(qd[pl.ds(q0, blk), :], kd[pl.ds(k0, 2 * blk), :], vd[pl.ds(k0, 2 * blk), :], bias)
            t0 = sb * (blk * d) + r
            acc_s[g, pl.ds(t0, blk, stride=d), :] = acc
            m_s[g, pl.ds(t0, blk, stride=d), :] = mm
            l_s[g, pl.ds(t0, blk, stride=d), :] = ll
            return carry

        return body

    lax.fori_loop(0, ATT_BLK // blk, make_body(1, 4, qd4, kd4, vd4), 0, unroll=ATT_UNROLL)
    lax.fori_loop(0, ATT_BLK // blk, make_body(2, 16, qd16, kd16, vd16), 0, unroll=ATT_UNROLL)

    def combine(u, carry):
        r0 = pl.multiple_of(u * blk, blk)
        sl = pl.ds(r0, blk)
        m0, m1, m2 = m_s[0, sl, :], m_s[1, sl, :], m_s[2, sl, :]
        mx = jnp.maximum(jnp.maximum(m0, m1), m2)
        w0, w1, w2 = jnp.exp(m0 - mx), jnp.exp(m1 - mx), jnp.exp(m2 - mx)
        num = w0 * acc_s[0, sl, :] + w1 * acc_s[1, sl, :] + w2 * acc_s[2, sl, :]
        den = w0 * l_s[0, sl, :] + w1 * l_s[1, sl, :] + w2 * l_s[2, sl, :]
        o_ref[0, sl, :] = (num / den).astype(o_ref.dtype)
        return carry

    lax.fori_loop(0, ATT_BLK // blk, combine, 0, unroll=2)


def _attention(aq, ak, av):
    B, S, _ = aq.shape
    blk = ATT_SPAN
    cur = pl.BlockSpec((1, ATT_BLK, LANES), lambda b, hp, j: (b, j, hp))
    kv_rows = {d: d * (ATT_BLK // d + blk) for d in DILATIONS}
    scratch = [
        pltpu.VMEM((kv_rows[1], LANES), BF16), pltpu.VMEM((kv_rows[1], LANES), BF16),
        pltpu.VMEM((ATT_BLK, LANES), BF16),
        pltpu.VMEM((kv_rows[4], LANES), BF16), pltpu.VMEM((kv_rows[4], LANES), BF16),
        pltpu.VMEM((ATT_BLK, LANES), BF16),
        pltpu.VMEM((kv_rows[16], LANES), BF16), pltpu.VMEM((kv_rows[16], LANES), BF16),
        pltpu.VMEM((4, ATT_BLK // 4, LANES), F32),
        pltpu.VMEM((3, ATT_BLK, LANES), F32), pltpu.VMEM((3, ATT_BLK, LANES), F32),
        pltpu.VMEM((3, ATT_BLK, LANES), F32),
        pltpu.VMEM((2, 2 * blk, 2 * blk), F32),
    ]
    return pl.pallas_call(
        _attn_kernel,
        grid=(B, ATT_WIDTH // LANES, S // ATT_BLK),
        in_specs=[cur, cur, cur],
        out_specs=cur,
        out_shape=jax.ShapeDtypeStruct((B, S, ATT_WIDTH), BF16),
        scratch_shapes=scratch,
        compiler_params=pltpu.CompilerParams(dimension_semantics=("parallel", "parallel", "arbitrary"),
                                             vmem_limit_bytes=VMEM_LIMIT),
        name="dilated_attention",
    )(aq, ak, av)


NORM_ROWS = 16


def _mlstm_kernel(q_ref, k_ref, vt_ref, mo_ref, g_ref, gt_ref, nw_ref, y_ref, c_state, m_state):
    L = MCHUNK
    D = MLSTM_HEAD_DIM
    nt_dims = (((1,), (1,)), ((), ()))

    @pl.when(pl.program_id(1) == 0)
    def _():
        c_state[...] = jnp.zeros_like(c_state)
        m_state[...] = jnp.zeros_like(m_state)

    key = lax.broadcasted_iota(jnp.int32, (L, L), 0)
    qry = lax.broadcasted_iota(jnp.int32, (L, L), 1)
    visible = key <= qry
    ones_rows = jnp.ones((NORM_ROWS, L), BF16)

    units = [(cc, h) for cc in range(TM_MLSTM // L) for h in range(MLSTM_HEADS)]
    rows = lambda cc: slice(cc * L, (cc + 1) * L)
    cols = lambda h: slice(h * D, (h + 1) * D)

    s_t = {u: lax.dot_general(k_ref[rows(u[0]), cols(u[1])], q_ref[rows(u[0]), cols(u[1])], nt_dims,
                              preferred_element_type=F32) for u in units}
    b_row, g_tot, pm, vt_aug, intra, m_loc, kv = {}, {}, {}, {}, {}, {}, {}
    for u in units:
        cc, h = u
        fh = MLSTM_HEADS + h
        r_col = g_ref[rows(cc), h:h + 1] - g_ref[rows(cc), fh:fh + 1]
        b_row[u] = gt_ref[cc, fh:fh + 1, :]
        g_tot[u] = b_row[u][:, L - 1:L]
        r_vis = jnp.where(visible, r_col, NEG)
        pm[u] = jnp.max(r_vis, axis=0, keepdims=True)
        p_t = (jnp.exp(r_vis - pm[u]) * s_t[u]).astype(BF16)
        vt_aug[u] = jnp.concatenate([vt_ref[cc, cols(h), :], ones_rows], axis=0)
        intra[u] = jnp.dot(vt_aug[u], p_t, preferred_element_type=F32)
    for u in units:
        cc, h = u
        r_row = gt_ref[cc, h:h + 1, :] - b_row[u]
        m_loc[u] = jnp.max(g_tot[u] + r_row, axis=-1, keepdims=True)
        vw = (vt_aug[u].astype(F32) * jnp.exp(g_tot[u] + r_row - m_loc[u])).astype(BF16)
        kv[u] = jnp.dot(vw, k_ref[rows(cc), cols(h)], preferred_element_type=F32)

    c_aug = [c_state[h] for h in range(MLSTM_HEADS)]
    m_prev = [m_state[h:h + 1, 0:1] for h in range(MLSTM_HEADS)]
    for u in units:
        cc, h = u
        inter = lax.dot_general(c_aug[h].astype(BF16), q_ref[rows(cc), cols(h)], nt_dims,
                                preferred_element_type=F32)
        mm = jnp.maximum(pm[u], m_prev[h])
        nd = jnp.exp(pm[u] - mm) * intra[u] + jnp.exp(m_prev[h] - mm) * inter
        inv = 1.0 / jnp.maximum(jnp.abs(nd[D:D + 1, :]), jnp.exp(-(b_row[u] + mm)))
        hh = nd[0:D, :] * inv
        hn = (hh * lax.rsqrt(jnp.mean(hh * hh, axis=0, keepdims=True) + NORM_EPS)).T
        o_gate = jax.nn.sigmoid(mo_ref[rows(cc), cols(h)].astype(F32))
        y_ref[rows(cc), cols(h)] = (o_gate * hn * nw_ref[:, cols(h)]).astype(y_ref.dtype)

        m_new = jnp.maximum(g_tot[u] + m_prev[h], m_loc[u])
        c_aug[h] = jnp.exp(g_tot[u] + m_prev[h] - m_new) * c_aug[h] + jnp.exp(m_loc[u] - m_new) * kv[u]
        m_prev[h] = m_new
    for h in range(MLSTM_HEADS):
        c_state[h] = c_aug[h]
        m_state[h:h + 1, :] = jnp.broadcast_to(m_prev[h], (1, LANES))


def _mlstm(mq, mk, mvt, mo, gates, gates_t, norm_w, batch):
    T = mq.shape[0]
    nt = T // batch // TM_MLSTM
    nch = TM_MLSTM // MCHUNK
    tile = lambda w: pl.BlockSpec((TM_MLSTM, w), lambda b, c: (b * nt + c, 0))
    per_chunk = lambda r: pl.BlockSpec((nch, r, MCHUNK), lambda b, c: (b * nt + c, 0, 0))
    return pl.pallas_call(
        _mlstm_kernel,
        grid=(batch, nt),
        in_specs=[tile(MLSTM_WIDTH), tile(MLSTM_WIDTH), per_chunk(MLSTM_WIDTH), tile(MLSTM_WIDTH), tile(LANES),
                  per_chunk(GATE_ROWS), pl.BlockSpec((1, MLSTM_WIDTH), lambda b, c: (0, 0))],
        out_specs=tile(MLSTM_WIDTH),
        out_shape=jax.ShapeDtypeStruct((T, MLSTM_WIDTH), BF16),
        scratch_shapes=[
            pltpu.VMEM((MLSTM_HEADS, MLSTM_HEAD_DIM + NORM_ROWS, MLSTM_HEAD_DIM), F32),
            pltpu.VMEM((8, LANES), F32),
        ],
        compiler_params=pltpu.CompilerParams(dimension_semantics=("parallel", "arbitrary"),
                                             vmem_limit_bytes=VMEM_LIMIT),
        name="mlstm",
    )(mq, mk, mvt, mo, gates, gates_t, norm_w)


def _post_kernel(x_ref, ya_ref, yb_ref, ga_ref, gb_ref, wpa_ref, wpm_ref, wo_ref, nw_ref, w1_ref, w2_ref,
                 fw_ref, o_ref, *, final_norm):
    pa = jnp.dot(ya_ref[...], wpa_ref[...], preferred_element_type=F32)
    pb = jnp.dot(yb_ref[...], wpm_ref[...], preferred_element_type=F32)
    mixed = (jax.nn.sigmoid(ga_ref[...].astype(F32)) * pa
             + jax.nn.sigmoid(gb_ref[...].astype(F32)) * pb).astype(BF16)
    x1 = x_ref[...] + jnp.dot(mixed, wo_ref[...], preferred_element_type=F32)
    var = jnp.mean(x1 * x1, axis=-1, keepdims=True)
    h2 = (x1 * lax.rsqrt(var + NORM_EPS) * nw_ref[...]).astype(BF16)
    acc = x1
    ff_chunk = D_MODEL
    for c in range(D_FF // ff_chunk):
        cs = slice(c * ff_chunk, (c + 1) * ff_chunk)
        u = jnp.maximum(jnp.dot(h2, w1_ref[:, cs], preferred_element_type=F32), 0.0)
        acc = acc + jnp.dot((u * u).astype(BF16), w2_ref[cs, :], preferred_element_type=F32)
    if final_norm:
        var = jnp.mean(acc * acc, axis=-1, keepdims=True)
        acc = acc * lax.rsqrt(var + NORM_EPS) * fw_ref[...]
    o_ref[...] = acc


def _post(x2d, ya, yb, ga, gb, wpa, wpm, wo, norm_w, w1, w2, final_w, layer, final_norm):
    T = x2d.shape[0]
    tile = lambda w: pl.BlockSpec((TM_POST, w), lambda i: (i, 0))
    return pl.pallas_call(
        functools.partial(_post_kernel, final_norm=final_norm),
        grid=(T // TM_POST,),
        in_specs=[tile(D_MODEL), tile(ATT_WIDTH), tile(MLSTM_WIDTH), tile(D_MODEL), tile(D_MODEL),
                  _layer_spec((ATT_WIDTH, D_MODEL), layer), _layer_spec((MLSTM_WIDTH, D_MODEL), layer),
                  _layer_spec((D_MODEL, D_MODEL), layer), _layer_spec((1, D_MODEL), layer),
                  _layer_spec((D_MODEL, D_FF), layer), _layer_spec((D_FF, D_MODEL), layer),
                  _const_spec((1, D_MODEL))],
        out_specs=tile(D_MODEL),
        out_shape=jax.ShapeDtypeStruct((T, D_MODEL), F32),
        compiler_params=pltpu.CompilerParams(dimension_semantics=("parallel",),
                                             vmem_limit_bytes=VMEM_LIMIT),
        name="post",
    )(x2d, ya, yb, ga, gb, wpa, wpm, wo, norm_w, w1, w2, final_w)


PREP_STEPS = 16


PREP_IN_ROWS = 384


def _cast_in_proj_kernel(w_ref, o_ref):
    row = pl.program_id(0) * PREP_IN_ROWS + lax.broadcasted_iota(jnp.int32, o_ref.shape, 1)
    w = w_ref[...]
    w = jnp.where(row < COL_AK, w * (ATT_HEAD_DIM ** -0.5), w)
    o_ref[...] = jnp.where(row < D_IN, w, 0.0).astype(BF16)


def _cast_in_proj(w_t):
    layers, _, feat = w_t.shape
    assert W_PAD % PREP_IN_ROWS == 0
    spec = pl.BlockSpec((layers, PREP_IN_ROWS, feat), lambda i: (0, i, 0))
    return pl.pallas_call(
        _cast_in_proj_kernel,
        grid=(W_PAD // PREP_IN_ROWS,),
        in_specs=[spec],
        out_specs=spec,
        out_shape=jax.ShapeDtypeStruct((layers, W_PAD, feat), BF16),
        compiler_params=pltpu.CompilerParams(dimension_semantics=("parallel",), vmem_limit_bytes=VMEM_LIMIT),
        name="cast_in_proj",
    )(w_t)


def _cast_weights_kernel(*refs):
    n = len(refs) // 2
    for src, dst in zip(refs[:n], refs[n:]):
        dst[...] = src[...].astype(BF16)


def _cast_weights(weights):
    def slab(shape):
        assert shape[1] % (PREP_STEPS * 16) == 0
        return pl.BlockSpec((shape[0], shape[1] // PREP_STEPS, shape[2]), lambda i: (0, i, 0))

    return pl.pallas_call(
        _cast_weights_kernel,
        grid=(PREP_STEPS,),
        in_specs=[slab(a.shape) for a in weights],
        out_specs=[slab(a.shape) for a in weights],
        out_shape=[jax.ShapeDtypeStruct(a.shape, BF16) for a in weights],
        compiler_params=pltpu.CompilerParams(dimension_semantics=("parallel",), vmem_limit_bytes=VMEM_LIMIT),
        name="cast_weights",
    )(*weights)


def kernel(x, positions, norm_mix_w, w_in, b_in, conv_w, conv_b, mlstm_norm_w, w_proj_att, w_proj_mlstm,
           w_out, norm_mlp_w, w_ff1, w_ff2, final_norm_w):
    B, S, D = x.shape
    T = B * S
    depth = w_in.shape[0]
    assert D == D_MODEL and S % ATT_BLK == 0 and T % TM_IN == 0 and T % TM_POST == 0
    assert math.isclose(ATT_HEAD_DIM ** -0.5, 0.125)
    rope_c, rope_s = _rope_tables(positions)
    w_t = _cast_in_proj(jnp.swapaxes(w_in, 1, 2))
    wpa, wpm, wo, w1, w2 = _cast_weights((w_proj_att, w_proj_mlstm, w_out, w_ff1, w_ff2))
    q_scale = jnp.where(jnp.arange(W_PAD) < COL_AK, ATT_HEAD_DIM ** -0.5, 1.0).astype(F32)
    b_all = (jnp.pad(b_in, ((0, 0), (0, W_PAD - D_IN))) * q_scale).reshape(depth, 1, W_PAD)
    x2d = x.reshape(T, D)
    final_w = final_norm_w.reshape(1, D).astype(F32)
    for l in range(depth):
        aq, ak, av, mq, mk, mv, mo, ga, gb, gates, gates_t = _inproj(
            x2d, norm_mix_w.reshape(depth, 1, D), w_t, b_all, rope_c, rope_s,
            conv_w, conv_b.reshape(depth, 1, -1), S, l)
        ya = _attention(aq.reshape(B, S, ATT_WIDTH), ak.reshape(B, S, ATT_WIDTH), av.reshape(B, S, ATT_WIDTH))
        yb = _mlstm(mq, mk, mv, mo, gates, gates_t, mlstm_norm_w[l].reshape(1, -1), B)
        x2d = _post(x2d, ya.reshape(T, ATT_WIDTH), yb, ga, gb, wpa, wpm, wo,
                    norm_mlp_w.reshape(depth, 1, D), w1, w2, final_w, l, final_norm=(l == depth - 1))
    return x2d.reshape(B, S, D)
```

```python
import functools
import math

import jax
import jax.numpy as jnp
from jax import lax
from jax.experimental import pallas as pl
from jax.experimental.pallas import tpu as pltpu

F32 = jnp.float32
BF16 = jnp.bfloat16

D_MODEL = 1024
ATT_HEADS = 8
ATT_HEAD_DIM = 64
ATT_WIDTH = ATT_HEADS * ATT_HEAD_DIM
ATT_SPAN = 128
DILATIONS = (1, 4, 16)
ROPE_THETA = 500000.0
ROPE_DIM = ATT_HEAD_DIM // 4
MLSTM_HEADS = 4
MLSTM_HEAD_DIM = 128
MLSTM_WIDTH = MLSTM_HEADS * MLSTM_HEAD_DIM
CONV_WIDTH = 4
D_FF = 4 * D_MODEL
NORM_EPS = 1e-6

COL_AQ = 0
COL_AK = COL_AQ + ATT_WIDTH
COL_AV = COL_AK + ATT_WIDTH
COL_MQ = COL_AV + ATT_WIDTH
COL_MK = COL_MQ + MLSTM_WIDTH
COL_MV = COL_MK + MLSTM_WIDTH
COL_MO = COL_MV + MLSTM_WIDTH
COL_MI = COL_MO + MLSTM_WIDTH
COL_MF = COL_MI + MLSTM_HEADS
COL_GA = COL_MF + MLSTM_HEADS
COL_GB = COL_GA + D_MODEL
D_IN = COL_GB + D_MODEL

LANES = 128
assert COL_MI % LANES == 0
W_MAIN = D_IN // LANES * LANES
W_PAD = W_MAIN + LANES
GATE_SHIFT = 2 * MLSTM_HEADS

TM_IN = 512
TM_POST = 512
ATT_BLK = 2048
MCHUNK = 128
TM_MLSTM = 1024
VMEM_LIMIT = 56 * 1024 * 1024
NEG = -1e30


def _const_spec(shape):
    nd = len(shape)
    return pl.BlockSpec(shape, lambda *_: (0,) * nd, pipeline_mode=pl.Buffered(1))


ROPE_HALF = ROPE_DIM // 2
ROPE_PACK = LANES // ROPE_HALF
ROPE_ROWS = 128


def _rope_table_kernel(pos_ref, invf_ref, c_ref, s_ref):
    ang = pos_ref[...] * invf_ref[...]
    cosx = jnp.cos(ang)
    sinx = jnp.sin(ang)
    src = lax.broadcasted_iota(jnp.int32, (LANES, LANES), 0)
    dst = lax.broadcasted_iota(jnp.int32, (LANES, LANES), 1)
    in_head = dst % ATT_HEAD_DIM
    rotary = in_head < ROPE_DIM
    sign = jnp.where(in_head < ROPE_HALF, -1.0, 1.0)
    one_elsewhere = jnp.where(lax.broadcasted_iota(jnp.int32, (1, LANES), 1) % ATT_HEAD_DIM < ROPE_DIM, 0.0, 1.0)
    for r in range(ROPE_PACK):
        pick = rotary & (src == r * ROPE_HALF + dst % ROPE_HALF)
        c_sel = jnp.where(pick, 1.0, 0.0)
        s_sel = jnp.where(pick, sign, 0.0)
        c_ref[pl.ds(r, ROPE_ROWS, stride=ROPE_PACK), :] = jnp.dot(
            cosx, c_sel, precision=lax.Precision.HIGHEST, preferred_element_type=F32) + one_elsewhere
        s_ref[pl.ds(r, ROPE_ROWS, stride=ROPE_PACK), :] = jnp.dot(
            sinx, s_sel, precision=lax.Precision.HIGHEST, preferred_element_type=F32)


def _rope_tables(positions):
    T = positions.size
    assert T % (ROPE_PACK * ROPE_ROWS) == 0
    pos = jnp.repeat(positions.astype(F32).reshape(T // ROPE_PACK, ROPE_PACK), ROPE_HALF, axis=1)
    inv_freq = ROPE_THETA ** (-jnp.arange(0, ROPE_DIM, 2, dtype=F32) / ROPE_DIM)
    invf = jnp.tile(inv_freq, ROPE_PACK).reshape(1, LANES)
    table = jax.ShapeDtypeStruct((T, LANES), F32)
    out_spec = pl.BlockSpec((ROPE_PACK * ROPE_ROWS, LANES), lambda i: (i, 0))
    return pl.pallas_call(
        _rope_table_kernel,
        grid=(T // (ROPE_PACK * ROPE_ROWS),),
        in_specs=[pl.BlockSpec((ROPE_ROWS, LANES), lambda i: (i, 0)), pl.BlockSpec((1, LANES), lambda i: (0, 0))],
        out_specs=(out_spec, out_spec),
        out_shape=(table, table),
        compiler_params=pltpu.CompilerParams(dimension_semantics=("parallel",)),
        name="rope_tables",
    )(pos, invf)


CONV_TAIL = 8
GATE_ROWS = 8


def _log_sigmoid(x):
    return jnp.minimum(x, 0.0) - jnp.log(1.0 + jnp.exp(-jnp.abs(x)))


def _inproj_kernel(x_ref, nw_ref, w_ref, b_ref, c_ref, s_ref, cw_ref, cb_ref,
                   aq_ref, ak_ref, av_ref, mq_ref, mk_ref, mv_ref, mo_ref, ga_ref, gb_ref, gt_ref, gtt_ref,
                   u_s, res_s, *, tiles_per_seq):
    @pl.when(pl.program_id(0) % tiles_per_seq == 0)
    def _():
        u_s[:, TM_IN:TM_IN + CONV_TAIL, :] = jnp.zeros((u_s.shape[0], CONV_TAIL, LANES), F32)

    x = x_ref[...]
    var = jnp.mean(x * x, axis=-1, keepdims=True)
    h = (x * lax.rsqrt(var + NORM_EPS) * nw_ref[...]).astype(BF16)

    def proj(lo, width):
        return lax.dot_general(h, w_ref[lo:lo + width, :], (((1,), (1,)), ((), ())),
                               preferred_element_type=F32) + b_ref[:, lo:lo + width]

    cos = c_ref[...]
    sin = s_ref[...]
    lane = lax.broadcasted_iota(jnp.int32, cos.shape, 1)
    first_half = (lane % ATT_HEAD_DIM) < (ROPE_DIM // 2)

    def rope_store(dst_ref, lo):
        zz = proj(lo, ATT_WIDTH)
        for j in range(ATT_WIDTH // LANES):
            z = zz[:, j * LANES:(j + 1) * LANES]
            partner = jnp.where(first_half,
                                pltpu.roll(z, LANES - ROPE_DIM // 2, axis=1),
                                pltpu.roll(z, ROPE_DIM // 2, axis=1))
            dst_ref[:, j * LANES:(j + 1) * LANES] = z * cos + partner * sin

    def conv_silu_store(dst_ref, slab0, lo, col0, scale):
        z = proj(lo, MLSTM_WIDTH)
        half = TM_IN // 2
        for j in range(MLSTM_WIDTH // LANES):
            sl = slab0 + j
            ws = slice(col0 + j * LANES, col0 + (j + 1) * LANES)
            u_s[sl, 0:CONV_TAIL, :] = u_s[sl, TM_IN:TM_IN + CONV_TAIL, :]
            u_s[sl, CONV_TAIL:CONV_TAIL + TM_IN, :] = z[:, j * LANES:(j + 1) * LANES]
            for parity in range(2):
                out = cb_ref[:, ws]
                for t in range(CONV_WIDTH):
                    r0 = CONV_TAIL - (CONV_WIDTH - 1) + t + parity
                    out = out + cw_ref[t:t + 1, ws] * u_s[sl, pl.ds(r0, half, stride=2), :]
                out = out * jax.nn.sigmoid(out)
                res_s[sl, pl.ds(parity, half, stride=2), :] = out if scale is None else out * scale
            dst_ref[:, j * LANES:(j + 1) * LANES] = res_s[sl].astype(dst_ref.dtype)

    rope_store(aq_ref, COL_AQ)
    rope_store(ak_ref, COL_AK)
    av_ref[...] = proj(COL_AV, ATT_WIDTH)
    conv_silu_store(mq_ref, 0, COL_MQ, 0, MLSTM_HEAD_DIM ** -0.5)
    conv_silu_store(mk_ref, MLSTM_WIDTH // LANES, COL_MK, MLSTM_WIDTH, None)
    z_mv = proj(COL_MV, MLSTM_WIDTH)
    for cc in range(TM_IN // MCHUNK):
        for hd in range(MLSTM_HEADS):
            blk = z_mv[cc * MCHUNK:(cc + 1) * MCHUNK, hd * MLSTM_HEAD_DIM:(hd + 1) * MLSTM_HEAD_DIM]
            mv_ref[cc, hd * MLSTM_HEAD_DIM:(hd + 1) * MLSTM_HEAD_DIM, :] = blk.T.astype(mv_ref.dtype)
    mo_ref[...] = proj(COL_MO, MLSTM_WIDTH).astype(mo_ref.dtype)

    z_tail = proj(COL_MI, W_PAD - COL_MI)
    tiles = [z_tail[:, j * LANES:(j + 1) * LANES] for j in range((W_PAD - COL_MI) // LANES)]
    rolled = [pltpu.roll(t, LANES - GATE_SHIFT, axis=1) for t in tiles]
    low_lanes = lane < LANES - GATE_SHIFT
    for j in range(D_MODEL // LANES):
        ga_ref[:, j * LANES:(j + 1) * LANES] = jnp.where(low_lanes, rolled[j], rolled[j + 1]).astype(ga_ref.dtype)
        k = j + D_MODEL // LANES
        gb_ref[:, j * LANES:(j + 1) * LANES] = jnp.where(low_lanes, rolled[k], rolled[k + 1]).astype(gb_ref.dtype)

    zg = jnp.where(lane < GATE_SHIFT, tiles[0], 0.0)
    logf = _log_sigmoid(zg)
    ri = lax.broadcasted_iota(jnp.int32, (MCHUNK, MCHUNK), 0)
    ci = lax.broadcasted_iota(jnp.int32, (MCHUNK, MCHUNK), 1)
    tri = (ci <= ri).astype(F32)
    is_input_gate = lax.broadcasted_iota(jnp.int32, (MCHUNK, LANES), 1) < MLSTM_HEADS
    for cc in range(TM_IN // MCHUNK):
        rows = slice(cc * MCHUNK, (cc + 1) * MCHUNK)
        bcum = jnp.dot(tri, logf[rows], precision=lax.Precision.HIGHEST, preferred_element_type=F32)
        gc = jnp.where(is_input_gate, zg[rows], bcum)
        gt_ref[rows, :] = gc
        gtt_ref[cc] = gc.T[0:GATE_ROWS, :]


def _layer_spec(shape, layer):
    nd = len(shape)
    return pl.BlockSpec((None,) + tuple(shape), lambda *_: (layer,) + (0,) * nd, pipeline_mode=pl.Buffered(1))


def _inproj(x2d, norm_w, w_t, b_all, rope_c, rope_s, conv_w, conv_b, seq_len, layer):
    T = x2d.shape[0]
    tile = lambda w: pl.BlockSpec((TM_IN, w), lambda i: (i, 0))
    out_shapes = (
        jax.ShapeDtypeStruct((T, ATT_WIDTH), F32),
        jax.ShapeDtypeStruct((T, ATT_WIDTH), F32),
        jax.ShapeDtypeStruct((T, ATT_WIDTH), F32),
        jax.ShapeDtypeStruct((T, MLSTM_WIDTH), BF16),
        jax.ShapeDtypeStruct((T, MLSTM_WIDTH), BF16),
        jax.ShapeDtypeStruct((T // MCHUNK, MLSTM_WIDTH, MCHUNK), BF16),
        jax.ShapeDtypeStruct((T, MLSTM_WIDTH), BF16),
        jax.ShapeDtypeStruct((T, D_MODEL), BF16),
        jax.ShapeDtypeStruct((T, D_MODEL), BF16),
        jax.ShapeDtypeStruct((T, LANES), F32),
        jax.ShapeDtypeStruct((T // MCHUNK, GATE_ROWS, MCHUNK), F32),
    )
    per_chunk = lambda rows: pl.BlockSpec((TM_IN // MCHUNK, rows, MCHUNK), lambda i: (i, 0, 0))
    out_specs = tuple(tile(s.shape[1]) if len(s.shape) == 2 else per_chunk(s.shape[1]) for s in out_shapes)
    assert seq_len % TM_IN == 0
    return pl.pallas_call(
        functools.partial(_inproj_kernel, tiles_per_seq=seq_len // TM_IN),
        grid=(T // TM_IN,),
        in_specs=[tile(D_MODEL), _layer_spec((1, D_MODEL), layer), _layer_spec((W_PAD, D_MODEL), layer),
                  _layer_spec((1, W_PAD), layer), tile(LANES), tile(LANES),
                  _layer_spec((CONV_WIDTH, 2 * MLSTM_WIDTH), layer), _layer_spec((1, 2 * MLSTM_WIDTH), layer)],
        out_specs=out_specs,
        out_shape=out_shapes,
        scratch_shapes=[pltpu.VMEM((2 * MLSTM_WIDTH // LANES, TM_IN + CONV_TAIL, LANES), F32),
                        pltpu.VMEM((2 * MLSTM_WIDTH // LANES, TM_IN, LANES), F32)],
        compiler_params=pltpu.CompilerParams(dimension_semantics=("arbitrary",),
                                             vmem_limit_bytes=VMEM_LIMIT),
        name="inproj",
    )(x2d, norm_w, w_t, b_all, rope_c, rope_s, conv_w, conv_b)


ATT_UNROLL = 16


def _attn_kernel(q_ref, k_ref, v_ref, o_ref,
                 kd1, vd1, qd4, kd4, vd4, qd16, kd16, vd16, tmp, acc_s, m_s, l_s, bias_s):
    j = pl.program_id(2)
    blk = ATT_SPAN
    nsub = {d: ATT_BLK // d // blk for d in DILATIONS}
    kv_bufs = ((1, kd1, vd1), (4, kd4, vd4), (16, kd16, vd16))

    row = lax.broadcasted_iota(jnp.int32, (2 * blk, 2 * blk), 0) % blk
    col = lax.broadcasted_iota(jnp.int32, (2 * blk, 2 * blk), 1)
    band = (col >= row) & (col <= row + ATT_SPAN)
    bias_s[0] = jnp.where(band, 0.0, NEG)
    bias_s[1] = jnp.where(band & (col >= blk), 0.0, NEG)

    @pl.when(j == 0)
    def _():
        for d, kd, vd in kv_bufs:
            n = ATT_BLK // d
            for r in range(d):
                base = r * (n + blk)
                kd[base:base + blk] = jnp.zeros((blk, LANES), BF16)
                vd[base:base + blk] = jnp.zeros((blk, LANES), BF16)

    @pl.when(j != 0)
    def _():
        for d, kd, vd in kv_bufs:
            n = ATT_BLK // d
            for r in range(d):
                base = r * (n + blk)
                kd[base:base + blk] = kd[base + n:base + n + blk]
                vd[base:base + blk] = vd[base + n:base + n + blk]

    def deinterleave(src_ref, dst1, dst4, dst16, is_kv):
        pad = blk if is_kv else 0
        if dst1 is not None:
            dst1[blk:blk + ATT_BLK] = src_ref[0].astype(BF16)
        n4 = ATT_BLK // 4
        for r4 in range(4):
            t4 = src_ref[0, pl.ds(r4, n4, stride=4), :]
            tmp[r4] = t4
            o4 = r4 * (n4 + pad) + pad
            dst4[o4:o4 + n4] = t4.astype(BF16)
        n16 = ATT_BLK // 16
        for r4 in range(4):
            for rr in range(4):
                o16 = (4 * rr + r4) * (n16 + pad) + pad
                dst16[o16:o16 + n16] = tmp[r4, pl.ds(rr, n16, stride=4), :].astype(BF16)

    deinterleave(q_ref, None, qd4, qd16, False)
    deinterleave(k_ref, kd1, kd4, kd16, True)
    deinterleave(v_ref, vd1, vd4, vd16, True)

    head_a = lax.broadcasted_iota(jnp.int32, (blk, LANES), 1) < ATT_HEAD_DIM

    def unit(q2, k2, v2, bias):
        zero = jnp.zeros_like(q2)
        qs = jnp.concatenate([jnp.where(head_a, q2, zero), jnp.where(head_a, zero, q2)], axis=0)
        s = lax.dot_general(qs, k2, (((1,), (1,)), ((), ())), preferred_element_type=F32) + bias
        m = jnp.max(s, axis=-1, keepdims=True)
        p = jnp.exp(s - m).astype(BF16)
        v_aug = jnp.concatenate([v2, jnp.ones_like(v2)], axis=1)
        pv = jnp.dot(p, v_aug, preferred_element_type=F32)
        acc = jnp.where(head_a, pv[:blk, :LANES], pv[blk:, :LANES])
        ll = jnp.where(head_a, pv[:blk, LANES:], pv[blk:, LANES:])
        mm = jnp.where(head_a, m[:blk], m[blk:])
        return acc, mm, ll

    first_blk = jnp.where(j == 0, 1, 0)

    def body1(u, carry):
        r0 = pl.multiple_of(u * blk, blk)
        q2 = q_ref[0, pl.ds(r0, blk), :].astype(BF16)
        bias = bias_s[jnp.where(u == 0, first_blk, 0)]
        acc, mm, ll = unit(q2, kd1[pl.ds(r0, 2 * blk), :], vd1[pl.ds(r0, 2 * blk), :], bias)
        acc_s[0, pl.ds(r0, blk), :] = acc
        m_s[0, pl.ds(r0, blk), :] = mm
        l_s[0, pl.ds(r0, blk), :] = ll
        return carry

    lax.fori_loop(0, nsub[1], body1, 0, unroll=ATT_UNROLL)

    def make_body(g, d, qd, kd, vd):
        n = ATT_BLK // d

        def body(u, carry):
            r = u // nsub[d]
            sb = u % nsub[d]
            q0 = pl.multiple_of(u * blk, blk)
            k0 = pl.multiple_of(r * (n + blk) + sb * blk, blk)
            bias = bias_s[jnp.where(sb == 0, first_blk, 0)]
            acc, mm, ll = unit(qd[pl.ds(q0, blk), :], kd[pl.ds(k0, 2 * blk), :], vd[pl.ds(k0, 2 * blk), :], bias)
            t0 = sb * (blk * d) + r
            acc_s[g, pl.ds(t0, blk, stride=d), :] = acc
            m_s[g, pl.ds(t0, blk, stride=d), :] = mm
            l_s[g, pl.ds(t0, blk, stride=d), :] = ll
            return carry

        return body

    lax.fori_loop(0, ATT_BLK // blk, make_body(1, 4, qd4, kd4, vd4), 0, unroll=ATT_UNROLL)
    lax.fori_loop(0, ATT_BLK // blk, make_body(2, 16, qd16, kd16, vd16), 0, unroll=ATT_UNROLL)

    def combine(u, carry):
        r0 = pl.multiple_of(u * blk, blk)
        sl = pl.ds(r0, blk)
        m0, m1, m2 = m_s[0, sl, :], m_s[1, sl, :], m_s[2, sl, :]
        mx = jnp.maximum(jnp.maximum(m0, m1), m2)
        w0, w1, w2 = jnp.exp(m0 - mx), jnp.exp(m1 - mx), jnp.exp(m2 - mx)
        num = w0 * acc_s[0, sl, :] + w1 * acc_s[1, sl, :] + w2 * acc_s[2, sl, :]
        den = w0 * l_s[0, sl, :] + w1 * l_s[1, sl, :] + w2 * l_s[2, sl, :]
        o_ref[0, sl, :] = (num / den).astype(o_ref.dtype)
        return carry

    lax.fori_loop(0, ATT_BLK // blk, combine, 0, unroll=2)


def _attention(aq, ak, av):
    B, S, _ = aq.shape
    blk = ATT_SPAN
    cur = pl.BlockSpec((1, ATT_BLK, LANES), lambda b, hp, j: (b, j, hp))
    kv_rows = {d: d * (ATT_BLK // d + blk) for d in DILATIONS}
    scratch = [
        pltpu.VMEM((kv_rows[1], LANES), BF16), pltpu.VMEM((kv_rows[1], LANES), BF16),
        pltpu.VMEM((ATT_BLK, LANES), BF16),
        pltpu.VMEM((kv_rows[4], LANES), BF16), pltpu.VMEM((kv_rows[4], LANES), BF16),
        pltpu.VMEM((ATT_BLK, LANES), BF16),
        pltpu.VMEM((kv_rows[16], LANES), BF16), pltpu.VMEM((kv_rows[16], LANES), BF16),
        pltpu.VMEM((4, ATT_BLK // 4, LANES), F32),
        pltpu.VMEM((3, ATT_BLK, LANES), F32), pltpu.VMEM((3, ATT_BLK, LANES), F32),
        pltpu.VMEM((3, ATT_BLK, LANES), F32),
        pltpu.VMEM((2, 2 * blk, 2 * blk), F32),
    ]
    return pl.pallas_call(
        _attn_kernel,
        grid=(B, ATT_WIDTH // LANES, S // ATT_BLK),
        in_specs=[cur, cur, cur],
        out_specs=cur,
        out_shape=jax.ShapeDtypeStruct((B, S, ATT_WIDTH), BF16),
        scratch_shapes=scratch,
        compiler_params=pltpu.CompilerParams(dimension_semantics=("parallel", "parallel", "arbitrary"),
                                             vmem_limit_bytes=VMEM_LIMIT),
        name="dilated_attention",
    )(aq, ak, av)


NORM_ROWS = 16


def _mlstm_kernel(q_ref, k_ref, vt_ref, mo_ref, g_ref, gt_ref, nw_ref, y_ref, c_state, m_state):
    L = MCHUNK
    D = MLSTM_HEAD_DIM
    nt_dims = (((1,), (1,)), ((), ()))

    @pl.when(pl.program_id(1) == 0)
    def _():
        c_state[...] = jnp.zeros_like(c_state)
        m_state[...] = jnp.zeros_like(m_state)

    key = lax.broadcasted_iota(jnp.int32, (L, L), 0)
    qry = lax.broadcasted_iota(jnp.int32, (L, L), 1)
    visible = key <= qry
    ones_rows = jnp.ones((NORM_ROWS, L), BF16)

    units = [(cc, h) for cc in range(TM_MLSTM // L) for h in range(MLSTM_HEADS)]
    rows = lambda cc: slice(cc * L, (cc + 1) * L)
    cols = lambda h: slice(h * D, (h + 1) * D)

    s_t = {u: lax.dot_general(k_ref[rows(u[0]), cols(u[1])], q_ref[rows(u[0]), cols(u[1])], nt_dims,
                              preferred_element_type=F32) for u in units}
    b_row, g_tot, pm, vt_aug, intra, m_loc, kv = {}, {}, {}, {}, {}, {}, {}
    for u in units:
        cc, h = u
        fh = MLSTM_HEADS + h
        r_col = g_ref[rows(cc), h:h + 1] - g_ref[rows(cc), fh:fh + 1]
        b_row[u] = gt_ref[cc, fh:fh + 1, :]
        g_tot[u] = b_row[u][:, L - 1:L]
        r_vis = jnp.where(visible, r_col, NEG)
        pm[u] = jnp.max(r_vis, axis=0, keepdims=True)
        p_t = (jnp.exp(r_vis - pm[u]) * s_t[u]).astype(BF16)
        vt_aug[u] = jnp.concatenate([vt_ref[cc, cols(h), :], ones_rows], axis=0)
        intra[u] = jnp.dot(vt_aug[u], p_t, preferred_element_type=F32)
    for u in units:
        cc, h = u
        r_row = gt_ref[cc, h:h + 1, :] - b_row[u]
        m_loc[u] = jnp.max(g_tot[u] + r_row, axis=-1, keepdims=True)
        vw = (vt_aug[u].astype(F32) * jnp.exp(g_tot[u] + r_row - m_loc[u])).astype(BF16)
        kv[u] = jnp.dot(vw, k_ref[rows(cc), cols(h)], preferred_element_type=F32)

    c_aug = [c_state[h] for h in range(MLSTM_HEADS)]
    m_prev = [m_state[h:h + 1, 0:1] for h in range(MLSTM_HEADS)]
    for u in units:
        cc, h = u
        inter = lax.dot_general(c_aug[h].astype(BF16), q_ref[rows(cc), cols(h)], nt_dims,
                                preferred_element_type=F32)
        mm = jnp.maximum(pm[u], m_prev[h])
        nd = jnp.exp(pm[u] - mm) * intra[u] + jnp.exp(m_prev[h] - mm) * inter
        inv = 1.0 / jnp.maximum(jnp.abs(nd[D:D + 1, :]), jnp.exp(-(b_row[u] + mm)))
        hh = nd[0:D, :] * inv
        hn = (hh * lax.rsqrt(jnp.mean(hh * hh, axis=0, keepdims=True) + NORM_EPS)).T
        o_gate = jax.nn.sigmoid(mo_ref[rows(cc), cols(h)].astype(F32))
        y_ref[rows(cc), cols(h)] = (o_gate * hn * nw_ref[:, cols(h)]).astype(y_ref.dtype)

        m_new = jnp.maximum(g_tot[u] + m_prev[h], m_loc[u])
        c_aug[h] = jnp.exp(g_tot[u] + m_prev[h] - m_new) * c_aug[h] + jnp.exp(m_loc[u] - m_new) * kv[u]
        m_prev[h] = m_new
    for h in range(MLSTM_HEADS):
        c_state[h] = c_aug[h]
        m_state[h:h + 1, :] = jnp.broadcast_to(m_prev[h], (1, LANES))


def _mlstm(mq, mk, mvt, mo, gates, gates_t, norm_w, batch):
    T = mq.shape[0]
    nt = T // batch // TM_MLSTM
    nch = TM_MLSTM // MCHUNK
    tile = lambda w: pl.BlockSpec((TM_MLSTM, w), lambda b, c: (b * nt + c, 0))
    per_chunk = lambda r: pl.BlockSpec((nch, r, MCHUNK), lambda b, c: (b * nt + c, 0, 0))
    return pl.pallas_call(
        _mlstm_kernel,
        grid=(batch, nt),
        in_specs=[tile(MLSTM_WIDTH), tile(MLSTM_WIDTH), per_chunk(MLSTM_WIDTH), tile(MLSTM_WIDTH), tile(LANES),
                  per_chunk(GATE_ROWS), pl.BlockSpec((1, MLSTM_WIDTH), lambda b, c: (0, 0))],
        out_specs=tile(MLSTM_WIDTH),
        out_shape=jax.ShapeDtypeStruct((T, MLSTM_WIDTH), BF16),
        scratch_shapes=[
            pltpu.VMEM((MLSTM_HEADS, MLSTM_HEAD_DIM + NORM_ROWS, MLSTM_HEAD_DIM), F32),
            pltpu.VMEM((8, LANES), F32),
        ],
        compiler_params=pltpu.CompilerParams(dimension_semantics=("parallel", "arbitrary"),
                                             vmem_limit_bytes=VMEM_LIMIT),
        name="mlstm",
    )(mq, mk, mvt, mo, gates, gates_t, norm_w)


def _post_kernel(x_ref, ya_ref, yb_ref, ga_ref, gb_ref, wpa_ref, wpm_ref, wo_ref, nw_ref, w1_ref, w2_ref,
                 fw_ref, o_ref, *, final_norm):
    pa = jnp.dot(ya_ref[...], wpa_ref[...], preferred_element_type=F32)
    pb = jnp.dot(yb_ref[...], wpm_ref[...], preferred_element_type=F32)
    mixed = (jax.nn.sigmoid(ga_ref[...].astype(F32)) * pa
             + jax.nn.sigmoid(gb_ref[...].astype(F32)) * pb).astype(BF16)
    x1 = x_ref[...] + jnp.dot(mixed, wo_ref[...], preferred_element_type=F32)
    var = jnp.mean(x1 * x1, axis=-1, keepdims=True)
    h2 = (x1 * lax.rsqrt(var + NORM_EPS) * nw_ref[...]).astype(BF16)
    acc = x1
    ff_chunk = D_MODEL
    for c in range(D_FF // ff_chunk):
        cs = slice(c * ff_chunk, (c + 1) * ff_chunk)
        u = jnp.maximum(jnp.dot(h2, w1_ref[:, cs], preferred_element_type=F32), 0.0)
        acc = acc + jnp.dot((u * u).astype(BF16), w2_ref[cs, :], preferred_element_type=F32)
    if final_norm:
        var = jnp.mean(acc * acc, axis=-1, keepdims=True)
        acc = acc * lax.rsqrt(var + NORM_EPS) * fw_ref[...]
    o_ref[...] = acc


def _post(x2d, ya, yb, ga, gb, wpa, wpm, wo, norm_w, w1, w2, final_w, layer, final_norm):
    T = x2d.shape[0]
    tile = lambda w: pl.BlockSpec((TM_POST, w), lambda i: (i, 0))
    return pl.pallas_call(
        functools.partial(_post_kernel, final_norm=final_norm),
        grid=(T // TM_POST,),
        in_specs=[tile(D_MODEL), tile(ATT_WIDTH), tile(MLSTM_WIDTH), tile(D_MODEL), tile(D_MODEL),
                  _layer_spec((ATT_WIDTH, D_MODEL), layer), _layer_spec((MLSTM_WIDTH, D_MODEL), layer),
                  _layer_spec((D_MODEL, D_MODEL), layer), _layer_spec((1, D_MODEL), layer),
                  _layer_spec((D_MODEL, D_FF), layer), _layer_spec((D_FF, D_MODEL), layer),
                  _const_spec((1, D_MODEL))],
        out_specs=tile(D_MODEL),
        out_shape=jax.ShapeDtypeStruct((T, D_MODEL), F32),
        compiler_params=pltpu.CompilerParams(dimension_semantics=("parallel",),
                                             vmem_limit_bytes=VMEM_LIMIT),
        name="post",
    )(x2d, ya, yb, ga, gb, wpa, wpm, wo, norm_w, w1, w2, final_w)


PREP_STEPS = 8


PREP_IN_ROWS = 640


def _cast_in_proj_kernel(w_ref, o_ref):
    row = pl.program_id(0) * PREP_IN_ROWS + lax.broadcasted_iota(jnp.int32, o_ref.shape, 1)
    w = w_ref[...]
    w = jnp.where(row < COL_AK, w * (ATT_HEAD_DIM ** -0.5), w)
    o_ref[...] = jnp.where(row < D_IN, w, 0.0).astype(BF16)


def _cast_in_proj(w_t):
    layers, _, feat = w_t.shape
    assert W_PAD % PREP_IN_ROWS == 0
    spec = pl.BlockSpec((layers, PREP_IN_ROWS, feat), lambda i: (0, i, 0))
    return pl.pallas_call(
        _cast_in_proj_kernel,
        grid=(W_PAD // PREP_IN_ROWS,),
        in_specs=[spec],
        out_specs=spec,
        out_shape=jax.ShapeDtypeStruct((layers, W_PAD, feat), BF16),
        compiler_params=pltpu.CompilerParams(dimension_semantics=("parallel",), vmem_limit_bytes=VMEM_LIMIT),
        name="cast_in_proj",
    )(w_t)


def _cast_weights_kernel(*refs):
    n = len(refs) // 2
    for src, dst in zip(refs[:n], refs[n:]):
        dst[...] = src[...].astype(BF16)


def _cast_weights(weights):
    def slab(shape):
        assert shape[1] % (PREP_STEPS * 16) == 0
        return pl.BlockSpec((shape[0], shape[1] // PREP_STEPS, shape[2]), lambda i: (0, i, 0))

    return pl.pallas_call(
        _cast_weights_kernel,
        grid=(PREP_STEPS,),
        in_specs=[slab(a.shape) for a in weights],
        out_specs=[slab(a.shape) for a in weights],
        out_shape=[jax.ShapeDtypeStruct(a.shape, BF16) for a in weights],
        compiler_params=pltpu.CompilerParams(dimension_semantics=("parallel",), vmem_limit_bytes=VMEM_LIMIT),
        name="cast_weights",
    )(*weights)


def kernel(x, positions, norm_mix_w, w_in, b_in, conv_w, conv_b, mlstm_norm_w, w_proj_att, w_proj_mlstm,
           w_out, norm_mlp_w, w_ff1, w_ff2, final_norm_w):
    B, S, D = x.shape
    T = B * S
    depth = w_in.shape[0]
    assert D == D_MODEL and S % ATT_BLK == 0 and T % TM_IN == 0 and T % TM_POST == 0
    assert math.isclose(ATT_HEAD_DIM ** -0.5, 0.125)
    rope_c, rope_s = _rope_tables(positions)
    w_t = _cast_in_proj(jnp.swapaxes(w_in, 1, 2))
    wpa, wpm, wo, w1, w2 = _cast_weights((w_proj_att, w_proj_mlstm, w_out, w_ff1, w_ff2))
    q_scale = jnp.where(jnp.arange(W_PAD) < COL_AK, ATT_HEAD_DIM ** -0.5, 1.0).astype(F32)
    b_all = (jnp.pad(b_in, ((0, 0), (0, W_PAD - D_IN))) * q_scale).reshape(depth, 1, W_PAD)
    x2d = x.reshape(T, D)
    final_w = final_norm_w.reshape(1, D).astype(F32)
    for l in range(depth):
        aq, ak, av, mq, mk, mv, mo, ga, gb, gates, gates_t = _inproj(
            x2d, norm_mix_w.reshape(depth, 1, D), w_t, b_all, rope_c, rope_s,
            conv_w, conv_b.reshape(depth, 1, -1), S, l)
        ya = _attention(aq.reshape(B, S, ATT_WIDTH), ak.reshape(B, S, ATT_WIDTH), av.reshape(B, S, ATT_WIDTH))
        yb = _mlstm(mq, mk, mv, mo, gates, gates_t, mlstm_norm_w[l].reshape(1, -1), B)
        x2d = _post(x2d, ya.reshape(T, ATT_WIDTH), yb, ga, gb, wpa, wpm, wo,
                    norm_mlp_w.reshape(depth, 1, D), w1, w2, final_w, l, final_norm=(l == depth - 1))
    return x2d.reshape(B, S, D)
```

```python
import functools
import math

import jax
import jax.numpy as jnp
from jax import lax
from jax.experimental import pallas as pl
from jax.experimental.pallas import tpu as pltpu

F32 = jnp.float32
BF16 = jnp.bfloat16

D_MODEL = 1024
ATT_HEADS = 8
ATT_HEAD_DIM = 64
ATT_WIDTH = ATT_HEADS * ATT_HEAD_DIM
ATT_SPAN = 128
DILATIONS = (1, 4, 16)
ROPE_THETA = 500000.0
ROPE_DIM = ATT_HEAD_DIM // 4
MLSTM_HEADS = 4
MLSTM_HEAD_DIM = 128
MLSTM_WIDTH = MLSTM_HEADS * MLSTM_HEAD_DIM
CONV_WIDTH = 4
D_FF = 4 * D_MODEL
NORM_EPS = 1e-6

COL_AQ = 0
COL_AK = COL_AQ + ATT_WIDTH
COL_AV = COL_AK + ATT_WIDTH
COL_MQ = COL_AV + ATT_WIDTH
COL_MK = COL_MQ + MLSTM_WIDTH
COL_MV = COL_MK + MLSTM_WIDTH
COL_MO = COL_MV + MLSTM_WIDTH
COL_MI = COL_MO + MLSTM_WIDTH
COL_MF = COL_MI + MLSTM_HEADS
COL_GA = COL_MF + MLSTM_HEADS
COL_GB = COL_GA + D_MODEL
D_IN = COL_GB + D_MODEL

LANES = 128
assert COL_MI % LANES == 0
W_MAIN = D_IN // LANES * LANES
W_PAD = W_MAIN + LANES
GATE_SHIFT = 2 * MLSTM_HEADS

TM_IN = 512
TM_POST = 512
ATT_BLK = 2048
MCHUNK = 128
TM_MLSTM = 1024
VMEM_LIMIT = 56 * 1024 * 1024
NEG = -1e30


def _const_spec(shape):
    nd = len(shape)
    return pl.BlockSpec(shape, lambda *_: (0,) * nd, pipeline_mode=pl.Buffered(1))


ROPE_HALF = ROPE_DIM // 2
ROPE_PACK = LANES // ROPE_HALF
ROPE_ROWS = 128


def _rope_table_kernel(pos_ref, invf_ref, c_ref, s_ref):
    ang = pos_ref[...] * invf_ref[...]
    cosx = jnp.cos(ang)
    sinx = jnp.sin(ang)
    dst = lax.broadcasted_iota(jnp.int32, (ROPE_ROWS, LANES), 1)
    in_head = dst % ATT_HEAD_DIM
    rotary = in_head < ROPE_DIM
    sign = jnp.where(in_head < ROPE_HALF, -1.0, 1.0)
    for r in range(ROPE_PACK):
        src = r * ROPE_HALF + dst % ROPE_HALF
        c_ref[pl.ds(r, ROPE_ROWS, stride=ROPE_PACK), :] = jnp.where(
            rotary, jnp.take_along_axis(cosx, src, axis=1), 1.0)
        s_ref[pl.ds(r, ROPE_ROWS, stride=ROPE_PACK), :] = jnp.where(
            rotary, jnp.take_along_axis(sinx, src, axis=1) * sign, 0.0)


def _rope_tables(positions):
    T = positions.size
    assert T % (ROPE_PACK * ROPE_ROWS) == 0
    pos = jnp.repeat(positions.astype(F32).reshape(T // ROPE_PACK, ROPE_PACK), ROPE_HALF, axis=1)
    inv_freq = ROPE_THETA ** (-jnp.arange(0, ROPE_DIM, 2, dtype=F32) / ROPE_DIM)
    invf = jnp.tile(inv_freq, ROPE_PACK).reshape(1, LANES)
    table = jax.ShapeDtypeStruct((T, LANES), F32)
    out_spec = pl.BlockSpec((ROPE_PACK * ROPE_ROWS, LANES), lambda i: (i, 0))
    return pl.pallas_call(
        _rope_table_kernel,
        grid=(T // (ROPE_PACK * ROPE_ROWS),),
        in_specs=[pl.BlockSpec((ROPE_ROWS, LANES), lambda i: (i, 0)), pl.BlockSpec((1, LANES), lambda i: (0, 0))],
        out_specs=(out_spec, out_spec),
        out_shape=(table, table),
        compiler_params=pltpu.CompilerParams(dimension_semantics=("parallel",)),
        name="rope_tables",
    )(pos, invf)


CONV_TAIL = 8
GATE_ROWS = 8


def _log_sigmoid(x):
    return jnp.minimum(x, 0.0) - jnp.log(1.0 + jnp.exp(-jnp.abs(x)))


def _inproj_kernel(x_ref, nw_ref, w_ref, b_ref, c_ref, s_ref, cw_ref, cb_ref,
                   aq_ref, ak_ref, av_ref, mq_ref, mk_ref, mv_ref, mo_ref, ga_ref, gb_ref, gt_ref, gtt_ref,
                   u_s, res_s, *, tiles_per_seq):
    @pl.when(pl.program_id(0) % tiles_per_seq == 0)
    def _():
        u_s[:, TM_IN:TM_IN + CONV_TAIL, :] = jnp.zeros((u_s.shape[0], CONV_TAIL, LANES), F32)

    x = x_ref[...]
    var = jnp.mean(x * x, axis=-1, keepdims=True)
    h = (x * lax.rsqrt(var + NORM_EPS) * nw_ref[...]).astype(BF16)

    def proj(lo, width):
        return jnp.dot(h, w_ref[:, lo:lo + width], preferred_element_type=F32) + b_ref[:, lo:lo + width]

    cos = c_ref[...]
    sin = s_ref[...]
    lane = lax.broadcasted_iota(jnp.int32, cos.shape, 1)
    first_half = (lane % ATT_HEAD_DIM) < (ROPE_DIM // 2)

    def rope_store(dst_ref, lo):
        zz = proj(lo, ATT_WIDTH)
        for j in range(ATT_WIDTH // LANES):
            z = zz[:, j * LANES:(j + 1) * LANES]
            partner = jnp.where(first_half,
                                pltpu.roll(z, LANES - ROPE_DIM // 2, axis=1),
                                pltpu.roll(z, ROPE_DIM // 2, axis=1))
            dst_ref[:, j * LANES:(j + 1) * LANES] = z * cos + partner * sin

    def conv_silu_store(dst_ref, slab0, lo, col0, scale):
        z = proj(lo, MLSTM_WIDTH)
        half = TM_IN // 2
        for j in range(MLSTM_WIDTH // LANES):
            sl = slab0 + j
            ws = slice(col0 + j * LANES, col0 + (j + 1) * LANES)
            u_s[sl, 0:CONV_TAIL, :] = u_s[sl, TM_IN:TM_IN + CONV_TAIL, :]
            u_s[sl, CONV_TAIL:CONV_TAIL + TM_IN, :] = z[:, j * LANES:(j + 1) * LANES]
            for parity in range(2):
                out = cb_ref[:, ws]
                for t in range(CONV_WIDTH):
                    r0 = CONV_TAIL - (CONV_WIDTH - 1) + t + parity
                    out = out + cw_ref[t:t + 1, ws] * u_s[sl, pl.ds(r0, half, stride=2), :]
                out = out * jax.nn.sigmoid(out)
                res_s[sl, pl.ds(parity, half, stride=2), :] = out if scale is None else out * scale
            dst_ref[:, j * LANES:(j + 1) * LANES] = res_s[sl].astype(dst_ref.dtype)

    rope_store(aq_ref, COL_AQ)
    rope_store(ak_ref, COL_AK)
    av_ref[...] = proj(COL_AV, ATT_WIDTH)
    conv_silu_store(mq_ref, 0, COL_MQ, 0, MLSTM_HEAD_DIM ** -0.5)
    conv_silu_store(mk_ref, MLSTM_WIDTH // LANES, COL_MK, MLSTM_WIDTH, None)
    z_mv = proj(COL_MV, MLSTM_WIDTH)
    for cc in range(TM_IN // MCHUNK):
        for hd in range(MLSTM_HEADS):
            blk = z_mv[cc * MCHUNK:(cc + 1) * MCHUNK, hd * MLSTM_HEAD_DIM:(hd + 1) * MLSTM_HEAD_DIM]
            mv_ref[cc, hd * MLSTM_HEAD_DIM:(hd + 1) * MLSTM_HEAD_DIM, :] = blk.T.astype(mv_ref.dtype)
    mo_ref[...] = proj(COL_MO, MLSTM_WIDTH).astype(mo_ref.dtype)

    z_tail = proj(COL_MI, W_PAD - COL_MI)
    tiles = [z_tail[:, j * LANES:(j + 1) * LANES] for j in range((W_PAD - COL_MI) // LANES)]
    rolled = [pltpu.roll(t, LANES - GATE_SHIFT, axis=1) for t in tiles]
    low_lanes = lane < LANES - GATE_SHIFT
    for j in range(D_MODEL // LANES):
        ga_ref[:, j * LANES:(j + 1) * LANES] = jnp.where(low_lanes, rolled[j], rolled[j + 1]).astype(ga_ref.dtype)
        k = j + D_MODEL // LANES
        gb_ref[:, j * LANES:(j + 1) * LANES] = jnp.where(low_lanes, rolled[k], rolled[k + 1]).astype(gb_ref.dtype)

    zg = jnp.where(lane < GATE_SHIFT, tiles[0], 0.0)
    logf = _log_sigmoid(zg)
    ri = lax.broadcasted_iota(jnp.int32, (MCHUNK, MCHUNK), 0)
    ci = lax.broadcasted_iota(jnp.int32, (MCHUNK, MCHUNK), 1)
    tri = (ci <= ri).astype(F32)
    is_input_gate = lax.broadcasted_iota(jnp.int32, (MCHUNK, LANES), 1) < MLSTM_HEADS
    for cc in range(TM_IN // MCHUNK):
        rows = slice(cc * MCHUNK, (cc + 1) * MCHUNK)
        bcum = jnp.dot(tri, logf[rows], precision=lax.Precision.HIGHEST, preferred_element_type=F32)
        gc = jnp.where(is_input_gate, zg[rows], bcum)
        gt_ref[rows, :] = gc
        gtt_ref[cc] = gc.T[0:GATE_ROWS, :]


def _layer_spec(shape, layer):
    nd = len(shape)
    return pl.BlockSpec((None,) + tuple(shape), lambda *_: (layer,) + (0,) * nd, pipeline_mode=pl.Buffered(1))


def _inproj(x2d, norm_w, w_all, b_all, rope_c, rope_s, conv_w, conv_b, seq_len, layer):
    T = x2d.shape[0]
    tile = lambda w: pl.BlockSpec((TM_IN, w), lambda i: (i, 0))
    out_shapes = (
        jax.ShapeDtypeStruct((T, ATT_WIDTH), F32),
        jax.ShapeDtypeStruct((T, ATT_WIDTH), F32),
        jax.ShapeDtypeStruct((T, ATT_WIDTH), F32),
        jax.ShapeDtypeStruct((T, MLSTM_WIDTH), BF16),
        jax.ShapeDtypeStruct((T, MLSTM_WIDTH), BF16),
        jax.ShapeDtypeStruct((T // MCHUNK, MLSTM_WIDTH, MCHUNK), BF16),
        jax.ShapeDtypeStruct((T, MLSTM_WIDTH), BF16),
        jax.ShapeDtypeStruct((T, D_MODEL), BF16),
        jax.ShapeDtypeStruct((T, D_MODEL), BF16),
        jax.ShapeDtypeStruct((T, LANES), F32),
        jax.ShapeDtypeStruct((T // MCHUNK, GATE_ROWS, MCHUNK), F32),
    )
    per_chunk = lambda rows: pl.BlockSpec((TM_IN // MCHUNK, rows, MCHUNK), lambda i: (i, 0, 0))
    out_specs = tuple(tile(s.shape[1]) if len(s.shape) == 2 else per_chunk(s.shape[1]) for s in out_shapes)
    assert seq_len % TM_IN == 0
    return pl.pallas_call(
        functools.partial(_inproj_kernel, tiles_per_seq=seq_len // TM_IN),
        grid=(T // TM_IN,),
        in_specs=[tile(D_MODEL), _layer_spec((1, D_MODEL), layer), _layer_spec((D_MODEL, W_PAD), layer),
                  _layer_spec((1, W_PAD), layer), tile(LANES), tile(LANES),
                  _layer_spec((CONV_WIDTH, 2 * MLSTM_WIDTH), layer), _layer_spec((1, 2 * MLSTM_WIDTH), layer)],
        out_specs=out_specs,
        out_shape=out_shapes,
        scratch_shapes=[pltpu.VMEM((2 * MLSTM_WIDTH // LANES, TM_IN + CONV_TAIL, LANES), F32),
                        pltpu.VMEM((2 * MLSTM_WIDTH // LANES, TM_IN, LANES), F32)],
        compiler_params=pltpu.CompilerParams(dimension_semantics=("arbitrary",),
                                             vmem_limit_bytes=VMEM_LIMIT),
        name="inproj",
    )(x2d, norm_w, w_all, b_all, rope_c, rope_s, conv_w, conv_b)


ATT_UNROLL = 16


def _attn_kernel(q_ref, k_ref, v_ref, o_ref,
                 kd1, vd1, qd4, kd4, vd4, qd16, kd16, vd16, tmp, acc_s, m_s, l_s, bias_s):
    j = pl.program_id(2)
    blk = ATT_SPAN
    nsub = {d: ATT_BLK // d // blk for d in DILATIONS}
    kv_bufs = ((1, kd1, vd1), (4, kd4, vd4), (16, kd16, vd16))

    row = lax.broadcasted_iota(jnp.int32, (2 * blk, 2 * blk), 0) % blk
    col = lax.broadcasted_iota(jnp.int32, (2 * blk, 2 * blk), 1)
    band = (col >= row) & (col <= row + ATT_SPAN)
    bias_s[0] = jnp.where(band, 0.0, NEG)
    bias_s[1] = jnp.where(band & (col >= blk), 0.0, NEG)

    @pl.when(j == 0)
    def _():
        for d, kd, vd in kv_bufs:
            n = ATT_BLK // d
            for r in range(d):
                base = r * (n + blk)
                kd[base:base + blk] = jnp.zeros((blk, LANES), BF16)
                vd[base:base + blk] = jnp.zeros((blk, LANES), BF16)

    @pl.when(j != 0)
    def _():
        for d, kd, vd in kv_bufs:
            n = ATT_BLK // d
            for r in range(d):
                base = r * (n + blk)
                kd[base:base + blk] = kd[base + n:base + n + blk]
                vd[base:base + blk] = vd[base + n:base + n + blk]

    def deinterleave(src_ref, dst1, dst4, dst16, is_kv):
        pad = blk if is_kv else 0
        if dst1 is not None:
            dst1[blk:blk + ATT_BLK] = src_ref[0].astype(BF16)
        n4 = ATT_BLK // 4
        for r4 in range(4):
            t4 = src_ref[0, pl.ds(r4, n4, stride=4), :]
            tmp[r4] = t4
            o4 = r4 * (n4 + pad) + pad
            dst4[o4:o4 + n4] = t4.astype(BF16)
        n16 = ATT_BLK // 16
        for r4 in range(4):
            for rr in range(4):
                o16 = (4 * rr + r4) * (n16 + pad) + pad
                dst16[o16:o16 + n16] = tmp[r4, pl.ds(rr, n16, stride=4), :].astype(BF16)

    deinterleave(q_ref, None, qd4, qd16, False)
    deinterleave(k_ref, kd1, kd4, kd16, True)
    deinterleave(v_ref, vd1, vd4, vd16, True)

    head_a = lax.broadcasted_iota(jnp.int32, (blk, LANES), 1) < ATT_HEAD_DIM

    def unit(q2, k2, v2, bias):
        zero = jnp.zeros_like(q2)
        qs = jnp.concatenate([jnp.where(head_a, q2, zero), jnp.where(head_a, zero, q2)], axis=0)
        s = lax.dot_general(qs, k2, (((1,), (1,)), ((), ())), preferred_element_type=F32) + bias
        m = jnp.max(s, axis=-1, keepdims=True)
        p = jnp.exp(s - m).astype(BF16)
        v_aug = jnp.concatenate([v2, jnp.ones_like(v2)], axis=1)
        pv = jnp.dot(p, v_aug, preferred_element_type=F32)
        acc = jnp.where(head_a, pv[:blk, :LANES], pv[blk:, :LANES])
        ll = jnp.where(head_a, pv[:blk, LANES:], pv[blk:, LANES:])
        mm = jnp.where(head_a, m[:blk], m[blk:])
        return acc, mm, ll

    first_blk = jnp.where(j == 0, 1, 0)

    def body1(u, carry):
        r0 = pl.multiple_of(u * blk, blk)
        q2 = q_ref[0, pl.ds(r0, blk), :].astype(BF16)
        bias = bias_s[jnp.where(u == 0, first_blk, 0)]
        acc, mm, ll = unit(q2, kd1[pl.ds(r0, 2 * blk), :], vd1[pl.ds(r0, 2 * blk), :], bias)
        acc_s[0, pl.ds(r0, blk), :] = acc
        m_s[0, pl.ds(r0, blk), :] = mm
        l_s[0, pl.ds(r0, blk), :] = ll
        return carry

    lax.fori_loop(0, nsub[1], body1, 0, unroll=ATT_UNROLL)

    def make_body(g, d, qd, kd, vd):
        n = ATT_BLK // d

        def body(u, carry):
            r = u // nsub[d]
            sb = u % nsub[d]
            q0 = pl.multiple_of(u * blk, blk)
            k0 = pl.multiple_of(r * (n + blk) + sb * blk, blk)
            bias = bias_s[jnp.where(sb == 0, first_blk, 0)]
            acc, mm, ll = unit(qd[pl.ds(q0, blk), :], kd[pl.ds(k0, 2 * blk), :], vd[pl.ds(k0, 2 * blk), :], bias)
            t0 = sb * (blk * d) + r
            acc_s[g, pl.ds(t0, blk, stride=d), :] = acc
            m_s[g, pl.ds(t0, blk, stride=d), :] = mm
            l_s[g, pl.ds(t0, blk, stride=d), :] = ll
            return carry

        return body

    lax.fori_loop(0, ATT_BLK // blk, make_body(1, 4, qd4, kd4, vd4), 0, unroll=ATT_UNROLL)
    lax.fori_loop(0, ATT_BLK // blk, make_body(2, 16, qd16, kd16, vd16), 0, unroll=ATT_UNROLL)

    def combine(u, carry):
        r0 = pl.multiple_of(u * blk, blk)
        sl = pl.ds(r0, blk)
        m0, m1, m2 = m_s[0, sl, :], m_s[1, sl, :], m_s[2, sl, :]
        mx = jnp.maximum(jnp.maximum(m0, m1), m2)
        w0, w1, w2 = jnp.exp(m0 - mx), jnp.exp(m1 - mx), jnp.exp(m2 - mx)
        num = w0 * acc_s[0, sl, :] + w1 * acc_s[1, sl, :] + w2 * acc_s[2, sl, :]
        den = w0 * l_s[0, sl, :] + w1 * l_s[1, sl, :] + w2 * l_s[2, sl, :]
        o_ref[0, sl, :] = (num / den).astype(o_ref.dtype)
        return carry

    lax.fori_loop(0, ATT_BLK // blk, combine, 0, unroll=2)


def _attention(aq, ak, av):
    B, S, _ = aq.shape
    blk = ATT_SPAN
    cur = pl.BlockSpec((1, ATT_BLK, LANES), lambda b, hp, j: (b, j, hp))
    kv_rows = {d: d * (ATT_BLK // d + blk) for d in DILATIONS}
    scratch = [
        pltpu.VMEM((kv_rows[1], LANES), BF16), pltpu.VMEM((kv_rows[1], LANES), BF16),
        pltpu.VMEM((ATT_BLK, LANES), BF16),
        pltpu.VMEM((kv_rows[4], LANES), BF16), pltpu.VMEM((kv_rows[4], LANES), BF16),
        pltpu.VMEM((ATT_BLK, LANES), BF16),
        pltpu.VMEM((kv_rows[16], LANES), BF16), pltpu.VMEM((kv_rows[16], LANES), BF16),
        pltpu.VMEM((4, ATT_BLK // 4, LANES), F32),
        pltpu.VMEM((3, ATT_BLK, LANES), F32), pltpu.VMEM((3, ATT_BLK, LANES), F32),
        pltpu.VMEM((3, ATT_BLK, LANES), F32),
        pltpu.VMEM((2, 2 * blk, 2 * blk), F32),
    ]
    return pl.pallas_call(
        _attn_kernel,
        grid=(B, ATT_WIDTH // LANES, S // ATT_BLK),
        in_specs=[cur, cur, cur],
        out_specs=cur,
        out_shape=jax.ShapeDtypeStruct((B, S, ATT_WIDTH), BF16),
        scratch_shapes=scratch,
        compiler_params=pltpu.CompilerParams(dimension_semantics=("parallel", "parallel", "arbitrary"),
                                             vmem_limit_bytes=VMEM_LIMIT),
        name="dilated_attention",
    )(aq, ak, av)


NORM_ROWS = 16


def _mlstm_kernel(q_ref, k_ref, vt_ref, mo_ref, g_ref, gt_ref, nw_ref, y_ref, c_state, m_state):
    L = MCHUNK
    D = MLSTM_HEAD_DIM
    nt_dims = (((1,), (1,)), ((), ()))

    @pl.when(pl.program_id(1) == 0)
    def _():
        c_state[...] = jnp.zeros_like(c_state)
        m_state[...] = jnp.zeros_like(m_state)

    key = lax.broadcasted_iota(jnp.int32, (L, L), 0)
    qry = lax.broadcasted_iota(jnp.int32, (L, L), 1)
    visible = key <= qry
    ones_rows = jnp.ones((NORM_ROWS, L), BF16)

    units = [(cc, h) for cc in range(TM_MLSTM // L) for h in range(MLSTM_HEADS)]
    rows = lambda cc: slice(cc * L, (cc + 1) * L)
    cols = lambda h: slice(h * D, (h + 1) * D)

    s_t = {u: lax.dot_general(k_ref[rows(u[0]), cols(u[1])], q_ref[rows(u[0]), cols(u[1])], nt_dims,
                              preferred_element_type=F32) for u in units}
    b_row, g_tot, pm, vt_aug, intra, m_loc, kv = {}, {}, {}, {}, {}, {}, {}
    for u in units:
        cc, h = u
        fh = MLSTM_HEADS + h
        r_col = g_ref[rows(cc), h:h + 1] - g_ref[rows(cc), fh:fh + 1]
        b_row[u] = gt_ref[cc, fh:fh + 1, :]
        g_tot[u] = b_row[u][:, L - 1:L]
        r_vis = jnp.where(visible, r_col, NEG)
        pm[u] = jnp.max(r_vis, axis=0, keepdims=True)
        p_t = (jnp.exp(r_vis - pm[u]) * s_t[u]).astype(BF16)
        vt_aug[u] = jnp.concatenate([vt_ref[cc, cols(h), :], ones_rows], axis=0)
        intra[u] = jnp.dot(vt_aug[u], p_t, preferred_element_type=F32)
    for u in units:
        cc, h = u
        r_row = gt_ref[cc, h:h + 1, :] - b_row[u]
        m_loc[u] = jnp.max(g_tot[u] + r_row, axis=-1, keepdims=True)
        vw = (vt_aug[u].astype(F32) * jnp.exp(g_tot[u] + r_row - m_loc[u])).astype(BF16)
        kv[u] = jnp.dot(vw, k_ref[rows(cc), cols(h)], preferred_element_type=F32)

    c_aug = [c_state[h] for h in range(MLSTM_HEADS)]
    m_prev = [m_state[h:h + 1, 0:1] for h in range(MLSTM_HEADS)]
    for u in units:
        cc, h = u
        inter = lax.dot_general(c_aug[h].astype(BF16), q_ref[rows(cc), cols(h)], nt_dims,
                                preferred_element_type=F32)
        mm = jnp.maximum(pm[u], m_prev[h])
        nd = jnp.exp(pm[u] - mm) * intra[u] + jnp.exp(m_prev[h] - mm) * inter
        inv = 1.0 / jnp.maximum(jnp.abs(nd[D:D + 1, :]), jnp.exp(-(b_row[u] + mm)))
        hh = nd[0:D, :] * inv
        hn = (hh * lax.rsqrt(jnp.mean(hh * hh, axis=0, keepdims=True) + NORM_EPS)).T
        o_gate = jax.nn.sigmoid(mo_ref[rows(cc), cols(h)].astype(F32))
        y_ref[rows(cc), cols(h)] = (o_gate * hn * nw_ref[:, cols(h)]).astype(y_ref.dtype)

        m_new = jnp.maximum(g_tot[u] + m_prev[h], m_loc[u])
        c_aug[h] = jnp.exp(g_tot[u] + m_prev[h] - m_new) * c_aug[h] + jnp.exp(m_loc[u] - m_new) * kv[u]
        m_prev[h] = m_new
    for h in range(MLSTM_HEADS):
        c_state[h] = c_aug[h]
        m_state[h:h + 1, :] = jnp.broadcast_to(m_prev[h], (1, LANES))


def _mlstm(mq, mk, mvt, mo, gates, gates_t, norm_w, batch):
    T = mq.shape[0]
    nt = T // batch // TM_MLSTM
    nch = TM_MLSTM // MCHUNK
    tile = lambda w: pl.BlockSpec((TM_MLSTM, w), lambda b, c: (b * nt + c, 0))
    per_chunk = lambda r: pl.BlockSpec((nch, r, MCHUNK), lambda b, c: (b * nt + c, 0, 0))
    return pl.pallas_call(
        _mlstm_kernel,
        grid=(batch, nt),
        in_specs=[tile(MLSTM_WIDTH), tile(MLSTM_WIDTH), per_chunk(MLSTM_WIDTH), tile(MLSTM_WIDTH), tile(LANES),
                  per_chunk(GATE_ROWS), pl.BlockSpec((1, MLSTM_WIDTH), lambda b, c: (0, 0))],
        out_specs=tile(MLSTM_WIDTH),
        out_shape=jax.ShapeDtypeStruct((T, MLSTM_WIDTH), BF16),
        scratch_shapes=[
            pltpu.VMEM((MLSTM_HEADS, MLSTM_HEAD_DIM + NORM_ROWS, MLSTM_HEAD_DIM), F32),
            pltpu.VMEM((8, LANES), F32),
        ],
        compiler_params=pltpu.CompilerParams(dimension_semantics=("parallel", "arbitrary"),
                                             vmem_limit_bytes=VMEM_LIMIT),
        name="mlstm",
    )(mq, mk, mvt, mo, gates, gates_t, norm_w)


def _post_kernel(x_ref, ya_ref, yb_ref, ga_ref, gb_ref, wpa_ref, wpm_ref, wo_ref, nw_ref, w1_ref, w2_ref,
                 fw_ref, o_ref, *, final_norm):
    pa = jnp.dot(ya_ref[...], wpa_ref[...], preferred_element_type=F32)
    pb = jnp.dot(yb_ref[...], wpm_ref[...], preferred_element_type=F32)
    mixed = (jax.nn.sigmoid(ga_ref[...].astype(F32)) * pa
             + jax.nn.sigmoid(gb_ref[...].astype(F32)) * pb).astype(BF16)
    x1 = x_ref[...] + jnp.dot(mixed, wo_ref[...], preferred_element_type=F32)
    var = jnp.mean(x1 * x1, axis=-1, keepdims=True)
    h2 = (x1 * lax.rsqrt(var + NORM_EPS) * nw_ref[...]).astype(BF16)
    acc = x1
    ff_chunk = D_MODEL
    for c in range(D_FF // ff_chunk):
        cs = slice(c * ff_chunk, (c + 1) * ff_chunk)
        u = jnp.maximum(jnp.dot(h2, w1_ref[:, cs], preferred_element_type=F32), 0.0)
        acc = acc + jnp.dot((u * u).astype(BF16), w2_ref[cs, :], preferred_element_type=F32)
    if final_norm:
        var = jnp.mean(acc * acc, axis=-1, keepdims=True)
        acc = acc * lax.rsqrt(var + NORM_EPS) * fw_ref[...]
    o_ref[...] = acc


def _post(x2d, ya, yb, ga, gb, wpa, wpm, wo, norm_w, w1, w2, final_w, layer, final_norm):
    T = x2d.shape[0]
    tile = lambda w: pl.BlockSpec((TM_POST, w), lambda i: (i, 0))
    return pl.pallas_call(
        functools.partial(_post_kernel, final_norm=final_norm),
        grid=(T // TM_POST,),
        in_specs=[tile(D_MODEL), tile(ATT_WIDTH), tile(MLSTM_WIDTH), tile(D_MODEL), tile(D_MODEL),
                  _layer_spec((ATT_WIDTH, D_MODEL), layer), _layer_spec((MLSTM_WIDTH, D_MODEL), layer),
                  _layer_spec((D_MODEL, D_MODEL), layer), _layer_spec((1, D_MODEL), layer),
                  _layer_spec((D_MODEL, D_FF), layer), _layer_spec((D_FF, D_MODEL), layer),
                  _const_spec((1, D_MODEL))],
        out_specs=tile(D_MODEL),
        out_shape=jax.ShapeDtypeStruct((T, D_MODEL), F32),
        compiler_params=pltpu.CompilerParams(dimension_semantics=("parallel",),
                                             vmem_limit_bytes=VMEM_LIMIT),
        name="post",
    )(x2d, ya, yb, ga, gb, wpa, wpm, wo, norm_w, w1, w2, final_w)


PREP_STEPS = 8


PREP_IN_ROWS = 640


def _cast_in_proj_kernel(w_ref, o_ref):
    row = pl.program_id(0) * PREP_IN_ROWS + lax.broadcasted_iota(jnp.int32, w_ref.shape[1:], 0)
    for layer in range(w_ref.shape[0]):
        w = w_ref[layer]
        w = jnp.where(row < COL_AK, w * (ATT_HEAD_DIM ** -0.5), w)
        o_ref[layer] = jnp.where(row < D_IN, w, 0.0).T.astype(BF16)


def _cast_in_proj(w_t):
    layers, _, feat = w_t.shape
    assert W_PAD % PREP_IN_ROWS == 0 and PREP_IN_ROWS % LANES == 0
    return pl.pallas_call(
        _cast_in_proj_kernel,
        grid=(W_PAD // PREP_IN_ROWS,),
        in_specs=[pl.BlockSpec((layers, PREP_IN_ROWS, feat), lambda i: (0, i, 0))],
        out_specs=pl.BlockSpec((layers, feat, PREP_IN_ROWS), lambda i: (0, 0, i)),
        out_shape=jax.ShapeDtypeStruct((layers, feat, W_PAD), BF16),
        compiler_params=pltpu.CompilerParams(dimension_semantics=("parallel",), vmem_limit_bytes=VMEM_LIMIT),
        name="cast_in_proj",
    )(w_t)


def _cast_weights_kernel(*refs):
    n = len(refs) // 2
    for src, dst in zip(refs[:n], refs[n:]):
        dst[...] = src[...].astype(BF16)


def _cast_weights(weights):
    def slab(shape):
        assert shape[1] % (PREP_STEPS * 16) == 0
        return pl.BlockSpec((shape[0], shape[1] // PREP_STEPS, shape[2]), lambda i: (0, i, 0))

    return pl.pallas_call(
        _cast_weights_kernel,
        grid=(PREP_STEPS,),
        in_specs=[slab(a.shape) for a in weights],
        out_specs=[slab(a.shape) for a in weights],
        out_shape=[jax.ShapeDtypeStruct(a.shape, BF16) for a in weights],
        compiler_params=pltpu.CompilerParams(dimension_semantics=("parallel",), vmem_limit_bytes=VMEM_LIMIT),
        name="cast_weights",
    )(*weights)


def kernel(x, positions, norm_mix_w, w_in, b_in, conv_w, conv_b, mlstm_norm_w, w_proj_att, w_proj_mlstm,
           w_out, norm_mlp_w, w_ff1, w_ff2, final_norm_w):
    B, S, D = x.shape
    T = B * S
    depth = w_in.shape[0]
    assert D == D_MODEL and S % ATT_BLK == 0 and T % TM_IN == 0 and T % TM_POST == 0
    assert math.isclose(ATT_HEAD_DIM ** -0.5, 0.125)
    rope_c, rope_s = _rope_tables(positions)
    w_all = _cast_in_proj(jnp.swapaxes(w_in, 1, 2))
    wpa, wpm, wo, w1, w2 = _cast_weights((w_proj_att, w_proj_mlstm, w_out, w_ff1, w_ff2))
    q_scale = jnp.where(jnp.arange(W_PAD) < COL_AK, ATT_HEAD_DIM ** -0.5, 1.0).astype(F32)
    b_all = (jnp.pad(b_in, ((0, 0), (0, W_PAD - D_IN))) * q_scale).reshape(depth, 1, W_PAD)
    x2d = x.reshape(T, D)
    final_w = final_norm_w.reshape(1, D).astype(F32)
    for l in range(depth):
        aq, ak, av, mq, mk, mv, mo, ga, gb, gates, gates_t = _inproj(
            x2d, norm_mix_w.reshape(depth, 1, D), w_all, b_all, rope_c, rope_s,
            conv_w, conv_b.reshape(depth, 1, -1), S, l)
        ya = _attention(aq.reshape(B, S, ATT_WIDTH), ak.reshape(B, S, ATT_WIDTH), av.reshape(B, S, ATT_WIDTH))
        yb = _mlstm(mq, mk, mv, mo, gates, gates_t, mlstm_norm_w[l].reshape(1, -1), B)
        x2d = _post(x2d, ya.reshape(T, ATT_WIDTH), yb, ga, gb, wpa, wpm, wo,
                    norm_mlp_w.reshape(depth, 1, D), w1, w2, final_w, l, final_norm=(l == depth - 1))
    return x2d.reshape(B, S, D)
```

```python
import functools
import math

import jax
import jax.numpy as jnp
from jax import lax
from jax.experimental import pallas as pl
from jax.experimental.pallas import tpu as pltpu

F32 = jnp.float32
BF16 = jnp.bfloat16

D_MODEL = 1024
ATT_HEADS = 8
ATT_HEAD_DIM = 64
ATT_WIDTH = ATT_HEADS * ATT_HEAD_DIM
ATT_SPAN = 128
DILATIONS = (1, 4, 16)
ROPE_THETA = 500000.0
ROPE_DIM = ATT_HEAD_DIM // 4
MLSTM_HEADS = 4
MLSTM_HEAD_DIM = 128
MLSTM_WIDTH = MLSTM_HEADS * MLSTM_HEAD_DIM
CONV_WIDTH = 4
D_FF = 4 * D_MODEL
NORM_EPS = 1e-6

COL_AQ = 0
COL_AK = COL_AQ + ATT_WIDTH
COL_AV = COL_AK + ATT_WIDTH
COL_MQ = COL_AV + ATT_WIDTH
COL_MK = COL_MQ + MLSTM_WIDTH
COL_MV = COL_MK + MLSTM_WIDTH
COL_MO = COL_MV + MLSTM_WIDTH
COL_MI = COL_MO + MLSTM_WIDTH
COL_MF = COL_MI + MLSTM_HEADS
COL_GA = COL_MF + MLSTM_HEADS
COL_GB = COL_GA + D_MODEL
D_IN = COL_GB + D_MODEL

LANES = 128
assert COL_MI % LANES == 0
W_MAIN = D_IN // LANES * LANES
W_PAD = W_MAIN + LANES
GATE_SHIFT = 2 * MLSTM_HEADS

TM_IN = 512
TM_POST = 512
ATT_BLK = 2048
MCHUNK = 128
TM_MLSTM = 1024
VMEM_LIMIT = 56 * 1024 * 1024
NEG = -1e30


def _const_spec(shape):
    nd = len(shape)
    return pl.BlockSpec(shape, lambda *_: (0,) * nd, pipeline_mode=pl.Buffered(1))


ROPE_HALF = ROPE_DIM // 2
ROPE_PACK = LANES // ROPE_HALF
ROPE_ROWS = 128


def _rope_table_kernel(pos_ref, invf_ref, c_ref, s_ref):
    ang = pos_ref[...] * invf_ref[...]
    cosx = jnp.cos(ang)
    sinx = jnp.sin(ang)
    dst = lax.broadcasted_iota(jnp.int32, (ROPE_ROWS, LANES), 1)
    in_head = dst % ATT_HEAD_DIM
    rotary = in_head < ROPE_DIM
    sign = jnp.where(in_head < ROPE_HALF, -1.0, 1.0)
    for r in range(ROPE_PACK):
        src = r * ROPE_HALF + dst % ROPE_HALF
        c_ref[pl.ds(r, ROPE_ROWS, stride=ROPE_PACK), :] = jnp.where(
            rotary, jnp.take_along_axis(cosx, src, axis=1), 1.0)
        s_ref[pl.ds(r, ROPE_ROWS, stride=ROPE_PACK), :] = jnp.where(
            rotary, jnp.take_along_axis(sinx, src, axis=1) * sign, 0.0)


def _rope_tables(positions):
    T = positions.size
    assert T % (ROPE_PACK * ROPE_ROWS) == 0
    pos = jnp.repeat(positions.astype(F32).reshape(T // ROPE_PACK, ROPE_PACK), ROPE_HALF, axis=1)
    inv_freq = ROPE_THETA ** (-jnp.arange(0, ROPE_DIM, 2, dtype=F32) / ROPE_DIM)
    invf = jnp.tile(inv_freq, ROPE_PACK).reshape(1, LANES)
    table = jax.ShapeDtypeStruct((T, LANES), F32)
    out_spec = pl.BlockSpec((ROPE_PACK * ROPE_ROWS, LANES), lambda i: (i, 0))
    return pl.pallas_call(
        _rope_table_kernel,
        grid=(T // (ROPE_PACK * ROPE_ROWS),),
        in_specs=[pl.BlockSpec((ROPE_ROWS, LANES), lambda i: (i, 0)), pl.BlockSpec((1, LANES), lambda i: (0, 0))],
        out_specs=(out_spec, out_spec),
        out_shape=(table, table),
        compiler_params=pltpu.CompilerParams(dimension_semantics=("parallel",)),
        name="rope_tables",
    )(pos, invf)


CONV_TAIL = 8
GATE_ROWS = 8


def _log_sigmoid(x):
    return jnp.minimum(x, 0.0) - jnp.log(1.0 + jnp.exp(-jnp.abs(x)))


def _inproj_kernel(x_ref, nw_ref, w_ref, b_ref, c_ref, s_ref, cw_ref, cb_ref,
                   aq_ref, ak_ref, av_ref, mq_ref, mk_ref, mv_ref, mo_ref, ga_ref, gb_ref, gt_ref, gtt_ref,
                   u_s, res_s, *, tiles_per_seq):
    @pl.when(pl.program_id(0) % tiles_per_seq == 0)
    def _():
        u_s[:, TM_IN:TM_IN + CONV_TAIL, :] = jnp.zeros((u_s.shape[0], CONV_TAIL, LANES), F32)

    x = x_ref[...]
    var = jnp.mean(x * x, axis=-1, keepdims=True)
    h = (x * lax.rsqrt(var + NORM_EPS) * nw_ref[...]).astype(BF16)

    def proj(lo, width):
        return jnp.dot(h, w_ref[:, lo:lo + width], preferred_element_type=F32) + b_ref[:, lo:lo + width]

    cos = c_ref[...]
    sin = s_ref[...]
    lane = lax.broadcasted_iota(jnp.int32, cos.shape, 1)
    first_half = (lane % ATT_HEAD_DIM) < (ROPE_DIM // 2)

    def rope_store(dst_ref, lo):
        zz = proj(lo, ATT_WIDTH)
        for j in range(ATT_WIDTH // LANES):
            z = zz[:, j * LANES:(j + 1) * LANES]
            partner = jnp.where(first_half,
                                pltpu.roll(z, LANES - ROPE_DIM // 2, axis=1),
                                pltpu.roll(z, ROPE_DIM // 2, axis=1))
            dst_ref[:, j * LANES:(j + 1) * LANES] = z * cos + partner * sin

    def conv_silu_store(dst_ref, slab0, lo, col0, scale):
        z = proj(lo, MLSTM_WIDTH)
        half = TM_IN // 2
        for j in range(MLSTM_WIDTH // LANES):
            sl = slab0 + j
            ws = slice(col0 + j * LANES, col0 + (j + 1) * LANES)
            u_s[sl, 0:CONV_TAIL, :] = u_s[sl, TM_IN:TM_IN + CONV_TAIL, :]
            u_s[sl, CONV_TAIL:CONV_TAIL + TM_IN, :] = z[:, j * LANES:(j + 1) * LANES]
            for parity in range(2):
                out = cb_ref[:, ws]
                for t in range(CONV_WIDTH):
                    r0 = CONV_TAIL - (CONV_WIDTH - 1) + t + parity
                    out = out + cw_ref[t:t + 1, ws] * u_s[sl, pl.ds(r0, half, stride=2), :]
                out = out * jax.nn.sigmoid(out)
                res_s[sl, pl.ds(parity, half, stride=2), :] = out if scale is None else out * scale
            dst_ref[:, j * LANES:(j + 1) * LANES] = res_s[sl].astype(dst_ref.dtype)

    rope_store(aq_ref, COL_AQ)
    rope_store(ak_ref, COL_AK)
    av_ref[...] = proj(COL_AV, ATT_WIDTH)
    conv_silu_store(mq_ref, 0, COL_MQ, 0, MLSTM_HEAD_DIM ** -0.5)
    conv_silu_store(mk_ref, MLSTM_WIDTH // LANES, COL_MK, MLSTM_WIDTH, None)
    z_mv = proj(COL_MV, MLSTM_WIDTH)
    for cc in range(TM_IN // MCHUNK):
        for hd in range(MLSTM_HEADS):
            blk = z_mv[cc * MCHUNK:(cc + 1) * MCHUNK, hd * MLSTM_HEAD_DIM:(hd + 1) * MLSTM_HEAD_DIM]
            mv_ref[cc, hd * MLSTM_HEAD_DIM:(hd + 1) * MLSTM_HEAD_DIM, :] = blk.T.astype(mv_ref.dtype)
    mo_ref[...] = proj(COL_MO, MLSTM_WIDTH).astype(mo_ref.dtype)

    z_tail = proj(COL_MI, W_PAD - COL_MI)
    tiles = [z_tail[:, j * LANES:(j + 1) * LANES] for j in range((W_PAD - COL_MI) // LANES)]
    rolled = [pltpu.roll(t, LANES - GATE_SHIFT, axis=1) for t in tiles]
    low_lanes = lane < LANES - GATE_SHIFT
    for j in range(D_MODEL // LANES):
        ga_ref[:, j * LANES:(j + 1) * LANES] = jnp.where(low_lanes, rolled[j], rolled[j + 1]).astype(ga_ref.dtype)
        k = j + D_MODEL // LANES
        gb_ref[:, j * LANES:(j + 1) * LANES] = jnp.where(low_lanes, rolled[k], rolled[k + 1]).astype(gb_ref.dtype)

    zg = jnp.where(lane < GATE_SHIFT, tiles[0], 0.0)
    logf = _log_sigmoid(zg)
    ri = lax.broadcasted_iota(jnp.int32, (MCHUNK, MCHUNK), 0)
    ci = lax.broadcasted_iota(jnp.int32, (MCHUNK, MCHUNK), 1)
    tri = (ci <= ri).astype(F32)
    is_input_gate = lax.broadcasted_iota(jnp.int32, (MCHUNK, LANES), 1) < MLSTM_HEADS
    for cc in range(TM_IN // MCHUNK):
        rows = slice(cc * MCHUNK, (cc + 1) * MCHUNK)
        bcum = jnp.dot(tri, logf[rows], precision=lax.Precision.HIGHEST, preferred_element_type=F32)
        gc = jnp.where(is_input_gate, zg[rows], bcum)
        gt_ref[rows, :] = gc
        gtt_ref[cc] = gc.T[0:GATE_ROWS, :]


def _layer_spec(shape, layer):
    nd = len(shape)
    return pl.BlockSpec((None,) + tuple(shape), lambda *_: (layer,) + (0,) * nd, pipeline_mode=pl.Buffered(1))


def _inproj(x2d, norm_w, w_all, b_all, rope_c, rope_s, conv_w, conv_b, seq_len, layer):
    T = x2d.shape[0]
    tile = lambda w: pl.BlockSpec((TM_IN, w), lambda i: (i, 0))
    out_shapes = (
        jax.ShapeDtypeStruct((T, ATT_WIDTH), F32),
        jax.ShapeDtypeStruct((T, ATT_WIDTH), F32),
        jax.ShapeDtypeStruct((T, ATT_WIDTH), F32),
        jax.ShapeDtypeStruct((T, MLSTM_WIDTH), BF16),
        jax.ShapeDtypeStruct((T, MLSTM_WIDTH), BF16),
        jax.ShapeDtypeStruct((T // MCHUNK, MLSTM_WIDTH, MCHUNK), BF16),
        jax.ShapeDtypeStruct((T, MLSTM_WIDTH), BF16),
        jax.ShapeDtypeStruct((T, D_MODEL), BF16),
        jax.ShapeDtypeStruct((T, D_MODEL), BF16),
        jax.ShapeDtypeStruct((T, LANES), F32),
        jax.ShapeDtypeStruct((T // MCHUNK, GATE_ROWS, MCHUNK), F32),
    )
    per_chunk = lambda rows: pl.BlockSpec((TM_IN // MCHUNK, rows, MCHUNK), lambda i: (i, 0, 0))
    out_specs = tuple(tile(s.shape[1]) if len(s.shape) == 2 else per_chunk(s.shape[1]) for s in out_shapes)
    assert seq_len % TM_IN == 0
    return pl.pallas_call(
        functools.partial(_inproj_kernel, tiles_per_seq=seq_len // TM_IN),
        grid=(T // TM_IN,),
        in_specs=[tile(D_MODEL), _layer_spec((1, D_MODEL), layer), _layer_spec((D_MODEL, W_PAD), layer),
                  _layer_spec((1, W_PAD), layer), tile(LANES), tile(LANES),
                  _layer_spec((CONV_WIDTH, 2 * MLSTM_WIDTH), layer), _layer_spec((1, 2 * MLSTM_WIDTH), layer)],
        out_specs=out_specs,
        out_shape=out_shapes,
        scratch_shapes=[pltpu.VMEM((2 * MLSTM_WIDTH // LANES, TM_IN + CONV_TAIL, LANES), F32),
                        pltpu.VMEM((2 * MLSTM_WIDTH // LANES, TM_IN, LANES), F32)],
        compiler_params=pltpu.CompilerParams(dimension_semantics=("arbitrary",),
                                             vmem_limit_bytes=VMEM_LIMIT),
        name="inproj",
    )(x2d, norm_w, w_all, b_all, rope_c, rope_s, conv_w, conv_b)


ATT_UNROLL = 16
PITCH16 = ATT_SPAN + 8


def _attn_kernel(q_ref, k_ref, v_ref, o_ref,
                 kd1, vd1, qd4, kd4, vd4, qd16, kd16, vd16, tmp, st4_s, st16_s, bias_s):
    j = pl.program_id(2)
    blk = ATT_SPAN
    nsub = {d: ATT_BLK // d // blk for d in DILATIONS}
    kv_bufs = ((1, kd1, vd1), (4, kd4, vd4), (16, kd16, vd16))

    row = lax.broadcasted_iota(jnp.int32, (2 * blk, 2 * blk), 0) % blk
    col = lax.broadcasted_iota(jnp.int32, (2 * blk, 2 * blk), 1)
    band = (col >= row) & (col <= row + ATT_SPAN)
    bias_s[0] = jnp.where(band, 0.0, NEG)
    bias_s[1] = jnp.where(band & (col >= blk), 0.0, NEG)

    @pl.when(j == 0)
    def _():
        for d, kd, vd in kv_bufs:
            n = ATT_BLK // d
            for r in range(d):
                base = r * (n + blk)
                kd[base:base + blk] = jnp.zeros((blk, LANES), BF16)
                vd[base:base + blk] = jnp.zeros((blk, LANES), BF16)

    @pl.when(j != 0)
    def _():
        for d, kd, vd in kv_bufs:
            n = ATT_BLK // d
            for r in range(d):
                base = r * (n + blk)
                kd[base:base + blk] = kd[base + n:base + n + blk]
                vd[base:base + blk] = vd[base + n:base + n + blk]

    def deinterleave(src_ref, dst1, dst4, dst16, is_kv):
        pad = blk if is_kv else 0
        if dst1 is not None:
            dst1[blk:blk + ATT_BLK] = src_ref[0].astype(BF16)
        n4 = ATT_BLK // 4
        for r4 in range(4):
            t4 = src_ref[0, pl.ds(r4, n4, stride=4), :]
            tmp[r4] = t4
            o4 = r4 * (n4 + pad) + pad
            dst4[o4:o4 + n4] = t4.astype(BF16)
        n16 = ATT_BLK // 16
        for r4 in range(4):
            for rr in range(4):
                o16 = (4 * rr + r4) * (n16 + pad) + pad
                dst16[o16:o16 + n16] = tmp[r4, pl.ds(rr, n16, stride=4), :].astype(BF16)

    deinterleave(q_ref, None, qd4, qd16, False)
    deinterleave(k_ref, kd1, kd4, kd16, True)
    deinterleave(v_ref, vd1, vd4, vd16, True)

    head_a = lax.broadcasted_iota(jnp.int32, (blk, LANES), 1) < ATT_HEAD_DIM

    def unit(q2, k2, v2, bias):
        zero = jnp.zeros_like(q2)
        qs = jnp.concatenate([jnp.where(head_a, q2, zero), jnp.where(head_a, zero, q2)], axis=0)
        s = lax.dot_general(qs, k2, (((1,), (1,)), ((), ())), preferred_element_type=F32) + bias
        m = jnp.max(s, axis=-1, keepdims=True)
        p = jnp.exp(s - m).astype(BF16)
        v_aug = jnp.concatenate([v2, jnp.ones_like(v2)], axis=1)
        pv = jnp.dot(p, v_aug, preferred_element_type=F32)
        acc = jnp.where(head_a, pv[:blk, :LANES], pv[blk:, :LANES])
        ll = jnp.where(head_a, pv[:blk, LANES:], pv[blk:, LANES:])
        mm = jnp.where(head_a, m[:blk], m[blk:])
        return acc, mm, ll

    first_blk = jnp.where(j == 0, 1, 0)

    def body16(u, carry):
        k0 = pl.multiple_of(u * (2 * blk), blk)
        bias = bias_s[first_blk]
        res = unit(qd16[pl.ds(pl.multiple_of(u * blk, blk), blk), :], kd16[pl.ds(k0, 2 * blk), :],
                   vd16[pl.ds(k0, 2 * blk), :], bias)
        row0 = pl.multiple_of(u * PITCH16, 8)
        for a, val in enumerate(res):
            st16_s[a, pl.ds(row0, blk), :] = val
        return carry

    def body4(u, carry):
        r = u // nsub[4]
        sb = u % nsub[4]
        k0 = pl.multiple_of(r * (ATT_BLK // 4 + blk) + sb * blk, blk)
        bias = bias_s[jnp.where(sb == 0, first_blk, 0)]
        res = unit(qd4[pl.ds(pl.multiple_of(u * blk, blk), blk), :], kd4[pl.ds(k0, 2 * blk), :],
                   vd4[pl.ds(k0, 2 * blk), :], bias)
        t0 = sb * (blk * 4) + r
        for a, val in enumerate(res):
            st4_s[a, pl.ds(t0, blk, stride=4), :] = val
        return carry

    lax.fori_loop(0, ATT_BLK // blk, body16, 0, unroll=ATT_UNROLL)
    lax.fori_loop(0, ATT_BLK // blk, body4, 0, unroll=ATT_UNROLL)

    def body1(u, carry):
        r0 = pl.multiple_of(u * blk, blk)
        sl = pl.ds(r0, blk)
        q2 = q_ref[0, sl, :].astype(BF16)
        bias = bias_s[jnp.where(u == 0, first_blk, 0)]
        acc2, m2, l2 = unit(q2, kd1[pl.ds(r0, 2 * blk), :], vd1[pl.ds(r0, 2 * blk), :], bias)
        per_res = blk // 16
        acc1, m1, l1 = (jnp.concatenate([st16_s[a, pl.ds(per_res * u + k, 16, stride=PITCH16), :]
                                         for k in range(per_res)], axis=0) for a in range(3))
        acc0, m0, l0 = st4_s[0, sl, :], st4_s[1, sl, :], st4_s[2, sl, :]
        mx = jnp.maximum(jnp.maximum(m0, m1), m2)
        w0, w1, w2 = jnp.exp(m0 - mx), jnp.exp(m1 - mx), jnp.exp(m2 - mx)
        num = w0 * acc0 + w1 * acc1 + w2 * acc2
        den = w0 * l0 + w1 * l1 + w2 * l2
        o_ref[0, sl, :] = (num / den).astype(o_ref.dtype)
        return carry

    lax.fori_loop(0, nsub[1], body1, 0, unroll=ATT_UNROLL)


def _attention(aq, ak, av):
    B, S, _ = aq.shape
    blk = ATT_SPAN
    cur = pl.BlockSpec((1, ATT_BLK, LANES), lambda b, hp, j: (b, j, hp))
    kv_rows = {d: d * (ATT_BLK // d + blk) for d in DILATIONS}
    scratch = [
        pltpu.VMEM((kv_rows[1], LANES), BF16), pltpu.VMEM((kv_rows[1], LANES), BF16),
        pltpu.VMEM((ATT_BLK, LANES), BF16),
        pltpu.VMEM((kv_rows[4], LANES), BF16), pltpu.VMEM((kv_rows[4], LANES), BF16),
        pltpu.VMEM((ATT_BLK, LANES), BF16),
        pltpu.VMEM((kv_rows[16], LANES), BF16), pltpu.VMEM((kv_rows[16], LANES), BF16),
        pltpu.VMEM((4, ATT_BLK // 4, LANES), F32),
        pltpu.VMEM((3, ATT_BLK, LANES), F32),
        pltpu.VMEM((3, 16 * PITCH16, LANES), F32),
        pltpu.VMEM((2, 2 * blk, 2 * blk), F32),
    ]
    return pl.pallas_call(
        _attn_kernel,
        grid=(B, ATT_WIDTH // LANES, S // ATT_BLK),
        in_specs=[cur, cur, cur],
        out_specs=cur,
        out_shape=jax.ShapeDtypeStruct((B, S, ATT_WIDTH), BF16),
        scratch_shapes=scratch,
        compiler_params=pltpu.CompilerParams(dimension_semantics=("parallel", "parallel", "arbitrary"),
                                             vmem_limit_bytes=VMEM_LIMIT),
        name="dilated_attention",
    )(aq, ak, av)


NORM_ROWS = 16


def _mlstm_kernel(q_ref, k_ref, vt_ref, mo_ref, g_ref, gt_ref, nw_ref, y_ref, c_state, m_state):
    L = MCHUNK
    D = MLSTM_HEAD_DIM
    nt_dims = (((1,), (1,)), ((), ()))

    @pl.when(pl.program_id(1) == 0)
    def _():
        c_state[...] = jnp.zeros_like(c_state)
        m_state[...] = jnp.zeros_like(m_state)

    key = lax.broadcasted_iota(jnp.int32, (L, L), 0)
    qry = lax.broadcasted_iota(jnp.int32, (L, L), 1)
    visible = key <= qry
    ones_rows = jnp.ones((NORM_ROWS, L), BF16)

    units = [(cc, h) for cc in range(TM_MLSTM // L) for h in range(MLSTM_HEADS)]
    rows = lambda cc: slice(cc * L, (cc + 1) * L)
    cols = lambda h: slice(h * D, (h + 1) * D)

    s_t = {u: lax.dot_general(k_ref[rows(u[0]), cols(u[1])], q_ref[rows(u[0]), cols(u[1])], nt_dims,
                              preferred_element_type=F32) for u in units}
    b_row, g_tot, pm, vt_aug, intra, m_loc, kv = {}, {}, {}, {}, {}, {}, {}
    for u in units:
        cc, h = u
        fh = MLSTM_HEADS + h
        r_col = g_ref[rows(cc), h:h + 1] - g_ref[rows(cc), fh:fh + 1]
        b_row[u] = gt_ref[cc, fh:fh + 1, :]
        g_tot[u] = b_row[u][:, L - 1:L]
        r_vis = jnp.where(visible, r_col, NEG)
        pm[u] = jnp.max(r_vis, axis=0, keepdims=True)
        p_t = (jnp.exp(r_vis - pm[u]) * s_t[u]).astype(BF16)
        vt_aug[u] = jnp.concatenate([vt_ref[cc, cols(h), :], ones_rows], axis=0)
        intra[u] = jnp.dot(vt_aug[u], p_t, preferred_element_type=F32)
    for u in units:
        cc, h = u
        r_row = gt_ref[cc, h:h + 1, :] - b_row[u]
        m_loc[u] = jnp.max(g_tot[u] + r_row, axis=-1, keepdims=True)
        vw = (vt_aug[u].astype(F32) * jnp.exp(g_tot[u] + r_row - m_loc[u])).astype(BF16)
        kv[u] = jnp.dot(vw, k_ref[rows(cc), cols(h)], preferred_element_type=F32)

    c_aug = [c_state[h] for h in range(MLSTM_HEADS)]
    m_prev = [m_state[h:h + 1, 0:1] for h in range(MLSTM_HEADS)]
    for u in units:
        cc, h = u
        inter = lax.dot_general(c_aug[h].astype(BF16), q_ref[rows(cc), cols(h)], nt_dims,
                                preferred_element_type=F32)
        mm = jnp.maximum(pm[u], m_prev[h])
        nd = jnp.exp(pm[u] - mm) * intra[u] + jnp.exp(m_prev[h] - mm) * inter
        inv = 1.0 / jnp.maximum(jnp.abs(nd[D:D + 1, :]), jnp.exp(-(b_row[u] + mm)))
        hh = nd[0:D, :] * inv
        hn = (hh * lax.rsqrt(jnp.mean(hh * hh, axis=0, keepdims=True) + NORM_EPS)).T
        o_gate = jax.nn.sigmoid(mo_ref[rows(cc), cols(h)].astype(F32))
        y_ref[rows(cc), cols(h)] = (o_gate * hn * nw_ref[:, cols(h)]).astype(y_ref.dtype)

        m_new = jnp.maximum(g_tot[u] + m_prev[h], m_loc[u])
        c_aug[h] = jnp.exp(g_tot[u] + m_prev[h] - m_new) * c_aug[h] + jnp.exp(m_loc[u] - m_new) * kv[u]
        m_prev[h] = m_new
    for h in range(MLSTM_HEADS):
        c_state[h] = c_aug[h]
        m_state[h:h + 1, :] = jnp.broadcast_to(m_prev[h], (1, LANES))


def _mlstm(mq, mk, mvt, mo, gates, gates_t, norm_w, batch):
    T = mq.shape[0]
    nt = T // batch // TM_MLSTM
    nch = TM_MLSTM // MCHUNK
    tile = lambda w: pl.BlockSpec((TM_MLSTM, w), lambda b, c: (b * nt + c, 0))
    per_chunk = lambda r: pl.BlockSpec((nch, r, MCHUNK), lambda b, c: (b * nt + c, 0, 0))
    return pl.pallas_call(
        _mlstm_kernel,
        grid=(batch, nt),
        in_specs=[tile(MLSTM_WIDTH), tile(MLSTM_WIDTH), per_chunk(MLSTM_WIDTH), tile(MLSTM_WIDTH), tile(LANES),
                  per_chunk(GATE_ROWS), pl.BlockSpec((1, MLSTM_WIDTH), lambda b, c: (0, 0))],
        out_specs=tile(MLSTM_WIDTH),
        out_shape=jax.ShapeDtypeStruct((T, MLSTM_WIDTH), BF16),
        scratch_shapes=[
            pltpu.VMEM((MLSTM_HEADS, MLSTM_HEAD_DIM + NORM_ROWS, MLSTM_HEAD_DIM), F32),
            pltpu.VMEM((8, LANES), F32),
        ],
        compiler_params=pltpu.CompilerParams(dimension_semantics=("parallel", "arbitrary"),
                                             vmem_limit_bytes=VMEM_LIMIT),
        name="mlstm",
    )(mq, mk, mvt, mo, gates, gates_t, norm_w)


def _post_kernel(x_ref, ya_ref, yb_ref, ga_ref, gb_ref, wpa_ref, wpm_ref, wo_ref, nw_ref, w1_ref, w2_ref,
                 fw_ref, o_ref, *, final_norm):
    pa = jnp.dot(ya_ref[...], wpa_ref[...], preferred_element_type=F32)
    pb = jnp.dot(yb_ref[...], wpm_ref[...], preferred_element_type=F32)
    mixed = (jax.nn.sigmoid(ga_ref[...].astype(F32)) * pa
             + jax.nn.sigmoid(gb_ref[...].astype(F32)) * pb).astype(BF16)
    x1 = x_ref[...] + jnp.dot(mixed, wo_ref[...], preferred_element_type=F32)
    var = jnp.mean(x1 * x1, axis=-1, keepdims=True)
    h2 = (x1 * lax.rsqrt(var + NORM_EPS) * nw_ref[...]).astype(BF16)
    acc = x1
    ff_chunk = D_MODEL
    for c in range(D_FF // ff_chunk):
        cs = slice(c * ff_chunk, (c + 1) * ff_chunk)
        u = jnp.maximum(jnp.dot(h2, w1_ref[:, cs], preferred_element_type=F32), 0.0)
        acc = acc + jnp.dot((u * u).astype(BF16), w2_ref[cs, :], preferred_element_type=F32)
    if final_norm:
        var = jnp.mean(acc * acc, axis=-1, keepdims=True)
        acc = acc * lax.rsqrt(var + NORM_EPS) * fw_ref[...]
    o_ref[...] = acc


def _post(x2d, ya, yb, ga, gb, wpa, wpm, wo, norm_w, w1, w2, final_w, layer, final_norm):
    T = x2d.shape[0]
    tile = lambda w: pl.BlockSpec((TM_POST, w), lambda i: (i, 0))
    return pl.pallas_call(
        functools.partial(_post_kernel, final_norm=final_norm),
        grid=(T // TM_POST,),
        in_specs=[tile(D_MODEL), tile(ATT_WIDTH), tile(MLSTM_WIDTH), tile(D_MODEL), tile(D_MODEL),
                  _layer_spec((ATT_WIDTH, D_MODEL), layer), _layer_spec((MLSTM_WIDTH, D_MODEL), layer),
                  _layer_spec((D_MODEL, D_MODEL), layer), _layer_spec((1, D_MODEL), layer),
                  _layer_spec((D_MODEL, D_FF), layer), _layer_spec((D_FF, D_MODEL), layer),
                  _const_spec((1, D_MODEL))],
        out_specs=tile(D_MODEL),
        out_shape=jax.ShapeDtypeStruct((T, D_MODEL), F32),
        compiler_params=pltpu.CompilerParams(dimension_semantics=("parallel",),
                                             vmem_limit_bytes=VMEM_LIMIT),
        name="post",
    )(x2d, ya, yb, ga, gb, wpa, wpm, wo, norm_w, w1, w2, final_w)


PREP_STEPS = 8


PREP_IN_ROWS = 640


def _cast_in_proj_kernel(w_ref, o_ref):
    row = pl.program_id(0) * PREP_IN_ROWS + lax.broadcasted_iota(jnp.int32, w_ref.shape[1:], 0)
    for layer in range(w_ref.shape[0]):
        w = w_ref[layer]
        w = jnp.where(row < COL_AK, w * (ATT_HEAD_DIM ** -0.5), w)
        o_ref[layer] = jnp.where(row < D_IN, w, 0.0).T.astype(BF16)


def _cast_in_proj(w_t):
    layers, _, feat = w_t.shape
    assert W_PAD % PREP_IN_ROWS == 0 and PREP_IN_ROWS % LANES == 0
    return pl.pallas_call(
        _cast_in_proj_kernel,
        grid=(W_PAD // PREP_IN_ROWS,),
        in_specs=[pl.BlockSpec((layers, PREP_IN_ROWS, feat), lambda i: (0, i, 0))],
        out_specs=pl.BlockSpec((layers, feat, PREP_IN_ROWS), lambda i: (0, 0, i)),
        out_shape=jax.ShapeDtypeStruct((layers, feat, W_PAD), BF16),
        compiler_params=pltpu.CompilerParams(dimension_semantics=("parallel",), vmem_limit_bytes=VMEM_LIMIT),
        name="cast_in_proj",
    )(w_t)


def _cast_weights_kernel(*refs):
    n = len(refs) // 2
    for src, dst in zip(refs[:n], refs[n:]):
        dst[...] = src[...].astype(BF16)


def _cast_weights(weights):
    def slab(shape):
        assert shape[1] % (PREP_STEPS * 16) == 0
        return pl.BlockSpec((shape[0], shape[1] // PREP_STEPS, shape[2]), lambda i: (0, i, 0))

    return pl.pallas_call(
        _cast_weights_kernel,
        grid=(PREP_STEPS,),
        in_specs=[slab(a.shape) for a in weights],
        out_specs=[slab(a.shape) for a in weights],
        out_shape=[jax.ShapeDtypeStruct(a.shape, BF16) for a in weights],
        compiler_params=pltpu.CompilerParams(dimension_semantics=("parallel",), vmem_limit_bytes=VMEM_LIMIT),
        name="cast_weights",
    )(*weights)


def kernel(x, positions, norm_mix_w, w_in, b_in, conv_w, conv_b, mlstm_norm_w, w_proj_att, w_proj_mlstm,
           w_out, norm_mlp_w, w_ff1, w_ff2, final_norm_w):
    B, S, D = x.shape
    T = B * S
    depth = w_in.shape[0]
    assert D == D_MODEL and S % ATT_BLK == 0 and T % TM_IN == 0 and T % TM_POST == 0
    assert math.isclose(ATT_HEAD_DIM ** -0.5, 0.125)
    rope_c, rope_s = _rope_tables(positions)
    w_all = _cast_in_proj(jnp.swapaxes(w_in, 1, 2))
    wpa, wpm, wo, w1, w2 = _cast_weights((w_proj_att, w_proj_mlstm, w_out, w_ff1, w_ff2))
    q_scale = jnp.where(jnp.arange(W_PAD) < COL_AK, ATT_HEAD_DIM ** -0.5, 1.0).astype(F32)
    b_all = (jnp.pad(b_in, ((0, 0), (0, W_PAD - D_IN))) * q_scale).reshape(depth, 1, W_PAD)
    x2d = x.reshape(T, D)
    final_w = final_norm_w.reshape(1, D).astype(F32)
    for l in range(depth):
        aq, ak, av, mq, mk, mv, mo, ga, gb, gates, gates_t = _inproj(
            x2d, norm_mix_w.reshape(depth, 1, D), w_all, b_all, rope_c, rope_s,
            conv_w, conv_b.reshape(depth, 1, -1), S, l)
        ya = _attention(aq.reshape(B, S, ATT_WIDTH), ak.reshape(B, S, ATT_WIDTH), av.reshape(B, S, ATT_WIDTH))
        yb = _mlstm(mq, mk, mv, mo, gates, gates_t, mlstm_norm_w[l].reshape(1, -1), B)
        x2d = _post(x2d, ya.reshape(T, ATT_WIDTH), yb, ga, gb, wpa, wpm, wo,
                    norm_mlp_w.reshape(depth, 1, D), w1, w2, final_w, l, final_norm=(l == depth - 1))
    return x2d.reshape(B, S, D)
```

```python
import functools
import math

import jax
import jax.numpy as jnp
from jax import lax
from jax.experimental import pallas as pl
from jax.experimental.pallas import tpu as pltpu

F32 = jnp.float32
BF16 = jnp.bfloat16

D_MODEL = 1024
ATT_HEADS = 8
ATT_HEAD_DIM = 64
ATT_WIDTH = ATT_HEADS * ATT_HEAD_DIM
ATT_SPAN = 128
DILATIONS = (1, 4, 16)
ROPE_THETA = 500000.0
ROPE_DIM = ATT_HEAD_DIM // 4
MLSTM_HEADS = 4
MLSTM_HEAD_DIM = 128
MLSTM_WIDTH = MLSTM_HEADS * MLSTM_HEAD_DIM
CONV_WIDTH = 4
D_FF = 4 * D_MODEL
NORM_EPS = 1e-6

COL_AQ = 0
COL_AK = COL_AQ + ATT_WIDTH
COL_AV = COL_AK + ATT_WIDTH
COL_MQ = COL_AV + ATT_WIDTH
COL_MK = COL_MQ + MLSTM_WIDTH
COL_MV = COL_MK + MLSTM_WIDTH
COL_MO = COL_MV + MLSTM_WIDTH
COL_MI = COL_MO + MLSTM_WIDTH
COL_MF = COL_MI + MLSTM_HEADS
COL_GA = COL_MF + MLSTM_HEADS
COL_GB = COL_GA + D_MODEL
D_IN = COL_GB + D_MODEL

LANES = 128
assert COL_MI % LANES == 0
W_MAIN = D_IN // LANES * LANES
W_PAD = W_MAIN + LANES
GATE_SHIFT = 2 * MLSTM_HEADS

TM_IN = 512
TM_POST = 512
ATT_BLK = 2048
MCHUNK = 128
TM_MLSTM = 1024
VMEM_LIMIT = 56 * 1024 * 1024
NEG = -1e30


def _const_spec(shape):
    nd = len(shape)
    return pl.BlockSpec(shape, lambda *_: (0,) * nd, pipeline_mode=pl.Buffered(1))


ROPE_HALF = ROPE_DIM // 2
ROPE_PACK = LANES // ROPE_HALF
ROPE_ROWS = 128


def _rope_table_kernel(pos_ref, invf_ref, c_ref, s_ref):
    ang = pos_ref[...] * invf_ref[...]
    cosx = jnp.cos(ang)
    sinx = jnp.sin(ang)
    dst = lax.broadcasted_iota(jnp.int32, (ROPE_ROWS, LANES), 1)
    in_head = dst % ATT_HEAD_DIM
    rotary = in_head < ROPE_DIM
    sign = jnp.where(in_head < ROPE_HALF, -1.0, 1.0)
    for r in range(ROPE_PACK):
        src = r * ROPE_HALF + dst % ROPE_HALF
        c_ref[pl.ds(r, ROPE_ROWS, stride=ROPE_PACK), :] = jnp.where(
            rotary, jnp.take_along_axis(cosx, src, axis=1), 1.0)
        s_ref[pl.ds(r, ROPE_ROWS, stride=ROPE_PACK), :] = jnp.where(
            rotary, jnp.take_along_axis(sinx, src, axis=1) * sign, 0.0)


def _rope_tables(positions):
    T = positions.size
    assert T % (ROPE_PACK * ROPE_ROWS) == 0
    pos = jnp.repeat(positions.astype(F32).reshape(T // ROPE_PACK, ROPE_PACK), ROPE_HALF, axis=1)
    inv_freq = ROPE_THETA ** (-jnp.arange(0, ROPE_DIM, 2, dtype=F32) / ROPE_DIM)
    invf = jnp.tile(inv_freq, ROPE_PACK).reshape(1, LANES)
    table = jax.ShapeDtypeStruct((T, LANES), F32)
    out_spec = pl.BlockSpec((ROPE_PACK * ROPE_ROWS, LANES), lambda i: (i, 0))
    return pl.pallas_call(
        _rope_table_kernel,
        grid=(T // (ROPE_PACK * ROPE_ROWS),),
        in_specs=[pl.BlockSpec((ROPE_ROWS, LANES), lambda i: (i, 0)), pl.BlockSpec((1, LANES), lambda i: (0, 0))],
        out_specs=(out_spec, out_spec),
        out_shape=(table, table),
        compiler_params=pltpu.CompilerParams(dimension_semantics=("parallel",)),
        name="rope_tables",
    )(pos, invf)


CONV_TAIL = 8
GATE_ROWS = 8


def _log_sigmoid(x):
    return jnp.minimum(x, 0.0) - jnp.log(1.0 + jnp.exp(-jnp.abs(x)))


def _inproj_kernel(x_ref, nw_ref, w_ref, b_ref, c_ref, s_ref, cw_ref, cb_ref,
                   qkv_ref, mqko_ref, mv_ref, gab_ref, gt_ref, gtt_ref,
                   u_s, res_s, *, tiles_per_seq):
    aq_ref, ak_ref, av_ref = (qkv_ref.at[:, j * ATT_WIDTH:(j + 1) * ATT_WIDTH] for j in range(3))
    mq_ref, mk_ref, mo_ref = (mqko_ref.at[:, j * MLSTM_WIDTH:(j + 1) * MLSTM_WIDTH] for j in range(3))
    ga_ref, gb_ref = (gab_ref.at[:, j * D_MODEL:(j + 1) * D_MODEL] for j in range(2))
    @pl.when(pl.program_id(0) % tiles_per_seq == 0)
    def _():
        u_s[:, TM_IN:TM_IN + CONV_TAIL, :] = jnp.zeros((u_s.shape[0], CONV_TAIL, LANES), F32)

    x = x_ref[...]
    var = jnp.mean(x * x, axis=-1, keepdims=True)
    h = (x * lax.rsqrt(var + NORM_EPS) * nw_ref[...]).astype(BF16)

    def proj(lo, width):
        return jnp.dot(h, w_ref[:, lo:lo + width], preferred_element_type=F32) + b_ref[:, lo:lo + width]

    cos = c_ref[...]
    sin = s_ref[...]
    lane = lax.broadcasted_iota(jnp.int32, cos.shape, 1)
    first_half = (lane % ATT_HEAD_DIM) < (ROPE_DIM // 2)

    def rope_store(dst_ref, lo):
        zz = proj(lo, ATT_WIDTH)
        for j in range(ATT_WIDTH // LANES):
            z = zz[:, j * LANES:(j + 1) * LANES]
            partner = jnp.where(first_half,
                                pltpu.roll(z, LANES - ROPE_DIM // 2, axis=1),
                                pltpu.roll(z, ROPE_DIM // 2, axis=1))
            dst_ref[:, j * LANES:(j + 1) * LANES] = z * cos + partner * sin

    def conv_silu_store(dst_ref, slab0, lo, col0, scale):
        z = proj(lo, MLSTM_WIDTH)
        half = TM_IN // 2
        for j in range(MLSTM_WIDTH // LANES):
            sl = slab0 + j
            ws = slice(col0 + j * LANES, col0 + (j + 1) * LANES)
            u_s[sl, 0:CONV_TAIL, :] = u_s[sl, TM_IN:TM_IN + CONV_TAIL, :]
            u_s[sl, CONV_TAIL:CONV_TAIL + TM_IN, :] = z[:, j * LANES:(j + 1) * LANES]
            for parity in range(2):
                out = cb_ref[:, ws]
                for t in range(CONV_WIDTH):
                    r0 = CONV_TAIL - (CONV_WIDTH - 1) + t + parity
                    out = out + cw_ref[t:t + 1, ws] * u_s[sl, pl.ds(r0, half, stride=2), :]
                out = out * jax.nn.sigmoid(out)
                res_s[sl, pl.ds(parity, half, stride=2), :] = out if scale is None else out * scale
            dst_ref[:, j * LANES:(j + 1) * LANES] = res_s[sl].astype(dst_ref.dtype)

    rope_store(aq_ref, COL_AQ)
    rope_store(ak_ref, COL_AK)
    av_ref[...] = proj(COL_AV, ATT_WIDTH)
    conv_silu_store(mq_ref, 0, COL_MQ, 0, MLSTM_HEAD_DIM ** -0.5)
    conv_silu_store(mk_ref, MLSTM_WIDTH // LANES, COL_MK, MLSTM_WIDTH, None)
    z_mv = proj(COL_MV, MLSTM_WIDTH)
    for cc in range(TM_IN // MCHUNK):
        for hd in range(MLSTM_HEADS):
            blk = z_mv[cc * MCHUNK:(cc + 1) * MCHUNK, hd * MLSTM_HEAD_DIM:(hd + 1) * MLSTM_HEAD_DIM]
            mv_ref[cc, hd * MLSTM_HEAD_DIM:(hd + 1) * MLSTM_HEAD_DIM, :] = blk.T.astype(mv_ref.dtype)
    mo_ref[...] = proj(COL_MO, MLSTM_WIDTH).astype(mo_ref.dtype)

    z_tail = proj(COL_MI, W_PAD - COL_MI)
    tiles = [z_tail[:, j * LANES:(j + 1) * LANES] for j in range((W_PAD - COL_MI) // LANES)]
    rolled = [pltpu.roll(t, LANES - GATE_SHIFT, axis=1) for t in tiles]
    low_lanes = lane < LANES - GATE_SHIFT
    for j in range(D_MODEL // LANES):
        ga_ref[:, j * LANES:(j + 1) * LANES] = jnp.where(low_lanes, rolled[j], rolled[j + 1]).astype(ga_ref.dtype)
        k = j + D_MODEL // LANES
        gb_ref[:, j * LANES:(j + 1) * LANES] = jnp.where(low_lanes, rolled[k], rolled[k + 1]).astype(gb_ref.dtype)

    zg = jnp.where(lane < GATE_SHIFT, tiles[0], 0.0)
    logf = _log_sigmoid(zg)
    ri = lax.broadcasted_iota(jnp.int32, (MCHUNK, MCHUNK), 0)
    ci = lax.broadcasted_iota(jnp.int32, (MCHUNK, MCHUNK), 1)
    tri = (ci <= ri).astype(F32)
    is_input_gate = lax.broadcasted_iota(jnp.int32, (MCHUNK, LANES), 1) < MLSTM_HEADS
    for cc in range(TM_IN // MCHUNK):
        rows = slice(cc * MCHUNK, (cc + 1) * MCHUNK)
        bcum = jnp.dot(tri, logf[rows], precision=lax.Precision.HIGHEST, preferred_element_type=F32)
        gc = jnp.where(is_input_gate, zg[rows], bcum)
        gt_ref[rows, :] = gc
        gtt_ref[cc] = gc.T[0:GATE_ROWS, :]


def _layer_spec(shape, layer):
    nd = len(shape)
    return pl.BlockSpec((None,) + tuple(shape), lambda *_: (layer,) + (0,) * nd, pipeline_mode=pl.Buffered(1))


def _inproj(x2d, norm_w, w_all, b_all, rope_c, rope_s, conv_w, conv_b, seq_len, layer):
    T = x2d.shape[0]
    tile = lambda w: pl.BlockSpec((TM_IN, w), lambda i: (i, 0))
    out_shapes = (
        jax.ShapeDtypeStruct((T, 3 * ATT_WIDTH), F32),
        jax.ShapeDtypeStruct((T, 3 * MLSTM_WIDTH), BF16),
        jax.ShapeDtypeStruct((T // MCHUNK, MLSTM_WIDTH, MCHUNK), BF16),
        jax.ShapeDtypeStruct((T, 2 * D_MODEL), BF16),
        jax.ShapeDtypeStruct((T, LANES), F32),
        jax.ShapeDtypeStruct((T // MCHUNK, GATE_ROWS, MCHUNK), F32),
    )
    per_chunk = lambda rows: pl.BlockSpec((TM_IN // MCHUNK, rows, MCHUNK), lambda i: (i, 0, 0))
    out_specs = tuple(tile(s.shape[1]) if len(s.shape) == 2 else per_chunk(s.shape[1]) for s in out_shapes)
    assert seq_len % TM_IN == 0
    return pl.pallas_call(
        functools.partial(_inproj_kernel, tiles_per_seq=seq_len // TM_IN),
        grid=(T // TM_IN,),
        in_specs=[tile(D_MODEL), _layer_spec((1, D_MODEL), layer), _layer_spec((D_MODEL, W_PAD), layer),
                  _layer_spec((1, W_PAD), layer), tile(LANES), tile(LANES),
                  _layer_spec((CONV_WIDTH, 2 * MLSTM_WIDTH), layer), _layer_spec((1, 2 * MLSTM_WIDTH), layer)],
        out_specs=out_specs,
        out_shape=out_shapes,
        scratch_shapes=[pltpu.VMEM((2 * MLSTM_WIDTH // LANES, TM_IN + CONV_TAIL, LANES), F32),
                        pltpu.VMEM((2 * MLSTM_WIDTH // LANES, TM_IN, LANES), F32)],
        compiler_params=pltpu.CompilerParams(dimension_semantics=("arbitrary",),
                                             vmem_limit_bytes=VMEM_LIMIT),
        name="inproj",
    )(x2d, norm_w, w_all, b_all, rope_c, rope_s, conv_w, conv_b)


ATT_UNROLL = 16
PITCH16 = ATT_SPAN + 8


def _attn_kernel(q_ref, k_ref, v_ref, o_ref,
                 kd1, vd1, qd4, kd4, vd4, qd16, kd16, vd16, tmp, st4_s, st16_s, bias_s):
    j = pl.program_id(2)
    blk = ATT_SPAN
    nsub = {d: ATT_BLK // d // blk for d in DILATIONS}
    kv_bufs = ((1, kd1, vd1), (4, kd4, vd4), (16, kd16, vd16))

    row = lax.broadcasted_iota(jnp.int32, (2 * blk, 2 * blk), 0) % blk
    col = lax.broadcasted_iota(jnp.int32, (2 * blk, 2 * blk), 1)
    band = (col >= row) & (col <= row + ATT_SPAN)
    bias_s[0] = jnp.where(band, 0.0, NEG)
    bias_s[1] = jnp.where(band & (col >= blk), 0.0, NEG)

    @pl.when(j == 0)
    def _():
        for d, kd, vd in kv_bufs:
            n = ATT_BLK // d
            for r in range(d):
                base = r * (n + blk)
                kd[base:base + blk] = jnp.zeros((blk, LANES), BF16)
                vd[base:base + blk] = jnp.zeros((blk, LANES), BF16)

    @pl.when(j != 0)
    def _():
        for d, kd, vd in kv_bufs:
            n = ATT_BLK // d
            for r in range(d):
                base = r * (n + blk)
                kd[base:base + blk] = kd[base + n:base + n + blk]
                vd[base:base + blk] = vd[base + n:base + n + blk]

    def deinterleave(src_ref, dst1, dst4, dst16, is_kv):
        pad = blk if is_kv else 0
        if dst1 is not None:
            dst1[blk:blk + ATT_BLK] = src_ref[0].astype(BF16)
        n4 = ATT_BLK // 4
        for r4 in range(4):
            t4 = src_ref[0, pl.ds(r4, n4, stride=4), :]
            tmp[r4] = t4
            o4 = r4 * (n4 + pad) + pad
            dst4[o4:o4 + n4] = t4.astype(BF16)
        n16 = ATT_BLK // 16
        for r4 in range(4):
            for rr in range(4):
                o16 = (4 * rr + r4) * (n16 + pad) + pad
                dst16[o16:o16 + n16] = tmp[r4, pl.ds(rr, n16, stride=4), :].astype(BF16)

    deinterleave(q_ref, None, qd4, qd16, False)
    deinterleave(k_ref, kd1, kd4, kd16, True)
    deinterleave(v_ref, vd1, vd4, vd16, True)

    head_a = lax.broadcasted_iota(jnp.int32, (blk, LANES), 1) < ATT_HEAD_DIM

    def unit(q2, k2, v2, bias):
        zero = jnp.zeros_like(q2)
        qs = jnp.concatenate([jnp.where(head_a, q2, zero), jnp.where(head_a, zero, q2)], axis=0)
        s = lax.dot_general(qs, k2, (((1,), (1,)), ((), ())), preferred_element_type=F32) + bias
        m = jnp.max(s, axis=-1, keepdims=True)
        p = jnp.exp(s - m).astype(BF16)
        v_aug = jnp.concatenate([v2, jnp.ones_like(v2)], axis=1)
        pv = jnp.dot(p, v_aug, preferred_element_type=F32)
        acc = jnp.where(head_a, pv[:blk, :LANES], pv[blk:, :LANES])
        ll = jnp.where(head_a, pv[:blk, LANES:], pv[blk:, LANES:])
        mm = jnp.where(head_a, m[:blk], m[blk:])
        return acc, mm, ll

    first_blk = jnp.where(j == 0, 1, 0)

    def body16(u, carry):
        k0 = pl.multiple_of(u * (2 * blk), blk)
        bias = bias_s[first_blk]
        res = unit(qd16[pl.ds(pl.multiple_of(u * blk, blk), blk), :], kd16[pl.ds(k0, 2 * blk), :],
                   vd16[pl.ds(k0, 2 * blk), :], bias)
        row0 = pl.multiple_of(u * PITCH16, 8)
        for a, val in enumerate(res):
            st16_s[a, pl.ds(row0, blk), :] = val
        return carry

    def body4(u, carry):
        r = u // nsub[4]
        sb = u % nsub[4]
        k0 = pl.multiple_of(r * (ATT_BLK // 4 + blk) + sb * blk, blk)
        bias = bias_s[jnp.where(sb == 0, first_blk, 0)]
        res = unit(qd4[pl.ds(pl.multiple_of(u * blk, blk), blk), :], kd4[pl.ds(k0, 2 * blk), :],
                   vd4[pl.ds(k0, 2 * blk), :], bias)
        t0 = sb * (blk * 4) + r
        for a, val in enumerate(res):
            st4_s[a, pl.ds(t0, blk, stride=4), :] = val
        return carry

    lax.fori_loop(0, ATT_BLK // blk, body16, 0, unroll=ATT_UNROLL)
    lax.fori_loop(0, ATT_BLK // blk, body4, 0, unroll=ATT_UNROLL)

    def body1(u, carry):
        r0 = pl.multiple_of(u * blk, blk)
        sl = pl.ds(r0, blk)
        q2 = q_ref[0, sl, :].astype(BF16)
        bias = bias_s[jnp.where(u == 0, first_blk, 0)]
        acc2, m2, l2 = unit(q2, kd1[pl.ds(r0, 2 * blk), :], vd1[pl.ds(r0, 2 * blk), :], bias)
        per_res = blk // 16
        acc1, m1, l1 = (jnp.concatenate([st16_s[a, pl.ds(per_res * u + k, 16, stride=PITCH16), :]
                                         for k in range(per_res)], axis=0) for a in range(3))
        acc0, m0, l0 = st4_s[0, sl, :], st4_s[1, sl, :], st4_s[2, sl, :]
        mx = jnp.maximum(jnp.maximum(m0, m1), m2)
        w0, w1, w2 = jnp.exp(m0 - mx), jnp.exp(m1 - mx), jnp.exp(m2 - mx)
        num = w0 * acc0 + w1 * acc1 + w2 * acc2
        den = w0 * l0 + w1 * l1 + w2 * l2
        o_ref[0, sl, :] = (num / den).astype(o_ref.dtype)
        return carry

    lax.fori_loop(0, nsub[1], body1, 0, unroll=ATT_UNROLL)


def _attention(qkv):
    B, S, _ = qkv.shape
    blk = ATT_SPAN
    pairs = ATT_WIDTH // LANES
    cur = pl.BlockSpec((1, ATT_BLK, LANES), lambda b, hp, j: (b, j, hp))
    part = lambda p: pl.BlockSpec((1, ATT_BLK, LANES), lambda b, hp, j: (b, j, p * pairs + hp))
    kv_rows = {d: d * (ATT_BLK // d + blk) for d in DILATIONS}
    scratch = [
        pltpu.VMEM((kv_rows[1], LANES), BF16), pltpu.VMEM((kv_rows[1], LANES), BF16),
        pltpu.VMEM((ATT_BLK, LANES), BF16),
        pltpu.VMEM((kv_rows[4], LANES), BF16), pltpu.VMEM((kv_rows[4], LANES), BF16),
        pltpu.VMEM((ATT_BLK, LANES), BF16),
        pltpu.VMEM((kv_rows[16], LANES), BF16), pltpu.VMEM((kv_rows[16], LANES), BF16),
        pltpu.VMEM((4, ATT_BLK // 4, LANES), F32),
        pltpu.VMEM((3, ATT_BLK, LANES), F32),
        pltpu.VMEM((3, 16 * PITCH16, LANES), F32),
        pltpu.VMEM((2, 2 * blk, 2 * blk), F32),
    ]
    return pl.pallas_call(
        _attn_kernel,
        grid=(B, pairs, S // ATT_BLK),
        in_specs=[part(0), part(1), part(2)],
        out_specs=cur,
        out_shape=jax.ShapeDtypeStruct((B, S, ATT_WIDTH), BF16),
        scratch_shapes=scratch,
        compiler_params=pltpu.CompilerParams(dimension_semantics=("parallel", "parallel", "arbitrary"),
                                             vmem_limit_bytes=VMEM_LIMIT),
        name="dilated_attention",
    )(qkv, qkv, qkv)


NORM_ROWS = 16


def _mlstm_kernel(qko_ref, vt_ref, g_ref, gt_ref, nw_ref, y_ref, c_state, m_state):
    L = MCHUNK
    D = MLSTM_HEAD_DIM
    nt_dims = (((1,), (1,)), ((), ()))
    q_ref, k_ref, mo_ref = (qko_ref.at[:, j * MLSTM_WIDTH:(j + 1) * MLSTM_WIDTH] for j in range(3))

    @pl.when(pl.program_id(1) == 0)
    def _():
        c_state[...] = jnp.zeros_like(c_state)
        m_state[...] = jnp.zeros_like(m_state)

    key = lax.broadcasted_iota(jnp.int32, (L, L), 0)
    qry = lax.broadcasted_iota(jnp.int32, (L, L), 1)
    visible = key <= qry
    ones_rows = jnp.ones((NORM_ROWS, L), BF16)

    units = [(cc, h) for cc in range(TM_MLSTM // L) for h in range(MLSTM_HEADS)]
    rows = lambda cc: slice(cc * L, (cc + 1) * L)
    cols = lambda h: slice(h * D, (h + 1) * D)

    s_t = {u: lax.dot_general(k_ref[rows(u[0]), cols(u[1])], q_ref[rows(u[0]), cols(u[1])], nt_dims,
                              preferred_element_type=F32) for u in units}
    b_row, g_tot, pm, vt_aug, intra, m_loc, kv = {}, {}, {}, {}, {}, {}, {}
    for u in units:
        cc, h = u
        fh = MLSTM_HEADS + h
        r_col = g_ref[rows(cc), h:h + 1] - g_ref[rows(cc), fh:fh + 1]
        b_row[u] = gt_ref[cc, fh:fh + 1, :]
        g_tot[u] = b_row[u][:, L - 1:L]
        r_vis = jnp.where(visible, r_col, NEG)
        pm[u] = jnp.max(r_vis, axis=0, keepdims=True)
        p_t = (jnp.exp(r_vis - pm[u]) * s_t[u]).astype(BF16)
        vt_aug[u] = jnp.concatenate([vt_ref[cc, cols(h), :], ones_rows], axis=0)
        intra[u] = jnp.dot(vt_aug[u], p_t, preferred_element_type=F32)
    for u in units:
        cc, h = u
        r_row = gt_ref[cc, h:h + 1, :] - b_row[u]
        m_loc[u] = jnp.max(g_tot[u] + r_row, axis=-1, keepdims=True)
        vw = (vt_aug[u].astype(F32) * jnp.exp(g_tot[u] + r_row - m_loc[u])).astype(BF16)
        kv[u] = jnp.dot(vw, k_ref[rows(cc), cols(h)], preferred_element_type=F32)

    c_aug = [c_state[h] for h in range(MLSTM_HEADS)]
    m_prev = [m_state[h:h + 1, 0:1] for h in range(MLSTM_HEADS)]
    for u in units:
        cc, h = u
        inter = lax.dot_general(c_aug[h].astype(BF16), q_ref[rows(cc), cols(h)], nt_dims,
                                preferred_element_type=F32)
        mm = jnp.maximum(pm[u], m_prev[h])
        nd = jnp.exp(pm[u] - mm) * intra[u] + jnp.exp(m_prev[h] - mm) * inter
        inv = 1.0 / jnp.maximum(jnp.abs(nd[D:D + 1, :]), jnp.exp(-(b_row[u] + mm)))
        hh = nd[0:D, :] * inv
        hn = (hh * lax.rsqrt(jnp.mean(hh * hh, axis=0, keepdims=True) + NORM_EPS)).T
        o_gate = jax.nn.sigmoid(mo_ref[rows(cc), cols(h)].astype(F32))
        y_ref[rows(cc), cols(h)] = (o_gate * hn * nw_ref[:, cols(h)]).astype(y_ref.dtype)

        m_new = jnp.maximum(g_tot[u] + m_prev[h], m_loc[u])
        c_aug[h] = jnp.exp(g_tot[u] + m_prev[h] - m_new) * c_aug[h] + jnp.exp(m_loc[u] - m_new) * kv[u]
        m_prev[h] = m_new
    for h in range(MLSTM_HEADS):
        c_state[h] = c_aug[h]
        m_state[h:h + 1, :] = jnp.broadcast_to(m_prev[h], (1, LANES))


def _mlstm(mqko, mvt, gates, gates_t, norm_w, batch):
    T = mqko.shape[0]
    nt = T // batch // TM_MLSTM
    nch = TM_MLSTM // MCHUNK
    tile = lambda w: pl.BlockSpec((TM_MLSTM, w), lambda b, c: (b * nt + c, 0))
    per_chunk = lambda r: pl.BlockSpec((nch, r, MCHUNK), lambda b, c: (b * nt + c, 0, 0))
    return pl.pallas_call(
        _mlstm_kernel,
        grid=(batch, nt),
        in_specs=[tile(3 * MLSTM_WIDTH), per_chunk(MLSTM_WIDTH), tile(LANES),
                  per_chunk(GATE_ROWS), pl.BlockSpec((1, MLSTM_WIDTH), lambda b, c: (0, 0))],
        out_specs=tile(MLSTM_WIDTH),
        out_shape=jax.ShapeDtypeStruct((T, MLSTM_WIDTH), BF16),
        scratch_shapes=[
            pltpu.VMEM((MLSTM_HEADS, MLSTM_HEAD_DIM + NORM_ROWS, MLSTM_HEAD_DIM), F32),
            pltpu.VMEM((8, LANES), F32),
        ],
        compiler_params=pltpu.CompilerParams(dimension_semantics=("parallel", "arbitrary"),
                                             vmem_limit_bytes=VMEM_LIMIT),
        name="mlstm",
    )(mqko, mvt, gates, gates_t, norm_w)


def _post_kernel(x_ref, ya_ref, yb_ref, gab_ref, wpa_ref, wpm_ref, wo_ref, nw_ref, w1_ref, w2_ref,
                 fw_ref, o_ref, *, final_norm):
    ga_ref, gb_ref = (gab_ref.at[:, j * D_MODEL:(j + 1) * D_MODEL] for j in range(2))
    pa = jnp.dot(ya_ref[...], wpa_ref[...], preferred_element_type=F32)
    pb = jnp.dot(yb_ref[...], wpm_ref[...], preferred_element_type=F32)
    mixed = (jax.nn.sigmoid(ga_ref[...].astype(F32)) * pa
             + jax.nn.sigmoid(gb_ref[...].astype(F32)) * pb).astype(BF16)
    x1 = x_ref[...] + jnp.dot(mixed, wo_ref[...], preferred_element_type=F32)
    var = jnp.mean(x1 * x1, axis=-1, keepdims=True)
    h2 = (x1 * lax.rsqrt(var + NORM_EPS) * nw_ref[...]).astype(BF16)
    acc = x1
    ff_chunk = D_MODEL
    for c in range(D_FF // ff_chunk):
        cs = slice(c * ff_chunk, (c + 1) * ff_chunk)
        u = jnp.maximum(jnp.dot(h2, w1_ref[:, cs], preferred_element_type=F32), 0.0)
        acc = acc + jnp.dot((u * u).astype(BF16), w2_ref[cs, :], preferred_element_type=F32)
    if final_norm:
        var = jnp.mean(acc * acc, axis=-1, keepdims=True)
        acc = acc * lax.rsqrt(var + NORM_EPS) * fw_ref[...]
    o_ref[...] = acc


def _post(x2d, ya, yb, gab, wpa, wpm, wo, norm_w, w1, w2, final_w, layer, final_norm):
    T = x2d.shape[0]
    tile = lambda w: pl.BlockSpec((TM_POST, w), lambda i: (i, 0))
    return pl.pallas_call(
        functools.partial(_post_kernel, final_norm=final_norm),
        grid=(T // TM_POST,),
        in_specs=[tile(D_MODEL), tile(ATT_WIDTH), tile(MLSTM_WIDTH), tile(2 * D_MODEL),
                  _layer_spec((ATT_WIDTH, D_MODEL), layer), _layer_spec((MLSTM_WIDTH, D_MODEL), layer),
                  _layer_spec((D_MODEL, D_MODEL), layer), _layer_spec((1, D_MODEL), layer),
                  _layer_spec((D_MODEL, D_FF), layer), _layer_spec((D_FF, D_MODEL), layer),
                  _const_spec((1, D_MODEL))],
        out_specs=tile(D_MODEL),
        out_shape=jax.ShapeDtypeStruct((T, D_MODEL), F32),
        compiler_params=pltpu.CompilerParams(dimension_semantics=("parallel",),
                                             vmem_limit_bytes=VMEM_LIMIT),
        name="post",
    )(x2d, ya, yb, gab, wpa, wpm, wo, norm_w, w1, w2, final_w)


PREP_STEPS = 8


PREP_IN_ROWS = 640


def _cast_in_proj_kernel(w_ref, o_ref):
    row = pl.program_id(0) * PREP_IN_ROWS + lax.broadcasted_iota(jnp.int32, w_ref.shape[1:], 0)
    for layer in range(w_ref.shape[0]):
        w = w_ref[layer]
        w = jnp.where(row < COL_AK, w * (ATT_HEAD_DIM ** -0.5), w)
        o_ref[layer] = jnp.where(row < D_IN, w, 0.0).T.astype(BF16)


def _cast_in_proj(w_t):
    layers, _, feat = w_t.shape
    assert W_PAD % PREP_IN_ROWS == 0 and PREP_IN_ROWS % LANES == 0
    return pl.pallas_call(
        _cast_in_proj_kernel,
        grid=(W_PAD // PREP_IN_ROWS,),
        in_specs=[pl.BlockSpec((layers, PREP_IN_ROWS, feat), lambda i: (0, i, 0))],
        out_specs=pl.BlockSpec((layers, feat, PREP_IN_ROWS), lambda i: (0, 0, i)),
        out_shape=jax.ShapeDtypeStruct((layers, feat, W_PAD), BF16),
        compiler_params=pltpu.CompilerParams(dimension_semantics=("parallel",), vmem_limit_bytes=VMEM_LIMIT),
        name="cast_in_proj",
    )(w_t)


def _cast_weights_kernel(*refs):
    n = len(refs) // 2
    for src, dst in zip(refs[:n], refs[n:]):
        dst[...] = src[...].astype(BF16)


def _cast_weights(weights):
    def slab(shape):
        assert shape[1] % (PREP_STEPS * 16) == 0
        return pl.BlockSpec((shape[0], shape[1] // PREP_STEPS, shape[2]), lambda i: (0, i, 0))

    return pl.pallas_call(
        _cast_weights_kernel,
        grid=(PREP_STEPS,),
        in_specs=[slab(a.shape) for a in weights],
        out_specs=[slab(a.shape) for a in weights],
        out_shape=[jax.ShapeDtypeStruct(a.shape, BF16) for a in weights],
        compiler_params=pltpu.CompilerParams(dimension_semantics=("parallel",), vmem_limit_bytes=VMEM_LIMIT),
        name="cast_weights",
    )(*weights)


def kernel(x, positions, norm_mix_w, w_in, b_in, conv_w, conv_b, mlstm_norm_w, w_proj_att, w_proj_mlstm,
           w_out, norm_mlp_w, w_ff1, w_ff2, final_norm_w):
    B, S, D = x.shape
    T = B * S
    depth = w_in.shape[0]
    assert D == D_MODEL and S % ATT_BLK == 0 and T % TM_IN == 0 and T % TM_POST == 0
    assert math.isclose(ATT_HEAD_DIM ** -0.5, 0.125)
    rope_c, rope_s = _rope_tables(positions)
    w_all = _cast_in_proj(jnp.swapaxes(w_in, 1, 2))
    wpa, wpm, wo, w1, w2 = _cast_weights((w_proj_att, w_proj_mlstm, w_out, w_ff1, w_ff2))
    q_scale = jnp.where(jnp.arange(W_PAD) < COL_AK, ATT_HEAD_DIM ** -0.5, 1.0).astype(F32)
    b_all = (jnp.pad(b_in, ((0, 0), (0, W_PAD - D_IN))) * q_scale).reshape(depth, 1, W_PAD)
    x2d = x.reshape(T, D)
    final_w = final_norm_w.reshape(1, D).astype(F32)
    for l in range(depth):
        qkv, mqko, mvt, gab, gates, gates_t = _inproj(
            x2d, norm_mix_w.reshape(depth, 1, D), w_all, b_all, rope_c, rope_s,
            conv_w, conv_b.reshape(depth, 1, -1), S, l)
        ya = _attention(qkv.reshape(B, S, 3 * ATT_WIDTH))
        yb = _mlstm(mqko, mvt, gates, gates_t, mlstm_norm_w[l].reshape(1, -1), B)
        x2d = _post(x2d, ya.reshape(T, ATT_WIDTH), yb, gab, wpa, wpm, wo,
                    norm_mlp_w.reshape(depth, 1, D), w1, w2, final_w, l, final_norm=(l == depth - 1))
    return x2d.reshape(B, S, D)
```

```python
import functools
import math

import jax
import jax.numpy as jnp
from jax import lax
from jax.experimental import pallas as pl
from jax.experimental.pallas import tpu as pltpu

F32 = jnp.float32
BF16 = jnp.bfloat16

D_MODEL = 1024
ATT_HEADS = 8
ATT_HEAD_DIM = 64
ATT_WIDTH = ATT_HEADS * ATT_HEAD_DIM
ATT_SPAN = 128
DILATIONS = (1, 4, 16)
ROPE_THETA = 500000.0
ROPE_DIM = ATT_HEAD_DIM // 4
MLSTM_HEADS = 4
MLSTM_HEAD_DIM = 128
MLSTM_WIDTH = MLSTM_HEADS * MLSTM_HEAD_DIM
CONV_WIDTH = 4
D_FF = 4 * D_MODEL
NORM_EPS = 1e-6

COL_AQ = 0
COL_AK = COL_AQ + ATT_WIDTH
COL_AV = COL_AK + ATT_WIDTH
COL_MQ = COL_AV + ATT_WIDTH
COL_MK = COL_MQ + MLSTM_WIDTH
COL_MV = COL_MK + MLSTM_WIDTH
COL_MO = COL_MV + MLSTM_WIDTH
COL_MI = COL_MO + MLSTM_WIDTH
COL_MF = COL_MI + MLSTM_HEADS
COL_GA = COL_MF + MLSTM_HEADS
COL_GB = COL_GA + D_MODEL
D_IN = COL_GB + D_MODEL

LANES = 128
assert COL_MI % LANES == 0
W_MAIN = D_IN // LANES * LANES
W_PAD = W_MAIN + LANES
GATE_SHIFT = 2 * MLSTM_HEADS

TM_IN = 512
TM_POST = 512
ATT_BLK = 2048
MCHUNK = 128
TM_MLSTM = 1024
VMEM_LIMIT = 56 * 1024 * 1024
NEG = -1e30
LOG2E = math.log2(math.e)


def _const_spec(shape):
    nd = len(shape)
    return pl.BlockSpec(shape, lambda *_: (0,) * nd, pipeline_mode=pl.Buffered(1))


ROPE_HALF = ROPE_DIM // 2
ROPE_PACK = LANES // ROPE_HALF
ROPE_ROWS = 128


def _rope_table_kernel(pos_ref, invf_ref, c_ref, s_ref):
    ang = pos_ref[...] * invf_ref[...]
    cosx = jnp.cos(ang)
    sinx = jnp.sin(ang)
    dst = lax.broadcasted_iota(jnp.int32, (ROPE_ROWS, LANES), 1)
    in_head = dst % ATT_HEAD_DIM
    rotary = in_head < ROPE_DIM
    sign = jnp.where(in_head < ROPE_HALF, -1.0, 1.0)
    for r in range(ROPE_PACK):
        src = r * ROPE_HALF + dst % ROPE_HALF
        c_ref[pl.ds(r, ROPE_ROWS, stride=ROPE_PACK), :] = jnp.where(
            rotary, jnp.take_along_axis(cosx, src, axis=1), 1.0)
        s_ref[pl.ds(r, ROPE_ROWS, stride=ROPE_PACK), :] = jnp.where(
            rotary, jnp.take_along_axis(sinx, src, axis=1) * sign, 0.0)


def _rope_tables(positions):
    T = positions.size
    assert T % (ROPE_PACK * ROPE_ROWS) == 0
    pos = jnp.repeat(positions.astype(F32).reshape(T // ROPE_PACK, ROPE_PACK), ROPE_HALF, axis=1)
    inv_freq = ROPE_THETA ** (-jnp.arange(0, ROPE_DIM, 2, dtype=F32) / ROPE_DIM)
    invf = jnp.tile(inv_freq, ROPE_PACK).reshape(1, LANES)
    table = jax.ShapeDtypeStruct((T, LANES), F32)
    out_spec = pl.BlockSpec((ROPE_PACK * ROPE_ROWS, LANES), lambda i: (i, 0))
    return pl.pallas_call(
        _rope_table_kernel,
        grid=(T // (ROPE_PACK * ROPE_ROWS),),
        in_specs=[pl.BlockSpec((ROPE_ROWS, LANES), lambda i: (i, 0)), pl.BlockSpec((1, LANES), lambda i: (0, 0))],
        out_specs=(out_spec, out_spec),
        out_shape=(table, table),
        compiler_params=pltpu.CompilerParams(dimension_semantics=("parallel",)),
        name="rope_tables",
    )(pos, invf)


CONV_TAIL = 8
GATE_ROWS = 8


def _log_sigmoid(x):
    return jnp.minimum(x, 0.0) - jnp.log(1.0 + jnp.exp(-jnp.abs(x)))


def _inproj_kernel(x_ref, nw_ref, w_ref, b_ref, c_ref, s_ref, cw_ref, cb_ref,
                   qkv_ref, mqko_ref, mv_ref, gab_ref, gt_ref, gtt_ref,
                   u_s, res_s, *, tiles_per_seq):
    aq_ref, ak_ref, av_ref = (qkv_ref.at[:, j * ATT_WIDTH:(j + 1) * ATT_WIDTH] for j in range(3))
    mq_ref, mk_ref, mo_ref = (mqko_ref.at[:, j * MLSTM_WIDTH:(j + 1) * MLSTM_WIDTH] for j in range(3))
    ga_ref, gb_ref = (gab_ref.at[:, j * D_MODEL:(j + 1) * D_MODEL] for j in range(2))
    @pl.when(pl.program_id(0) % tiles_per_seq == 0)
    def _():
        u_s[:, TM_IN:TM_IN + CONV_TAIL, :] = jnp.zeros((u_s.shape[0], CONV_TAIL, LANES), F32)

    x = x_ref[...]
    var = jnp.mean(x * x, axis=-1, keepdims=True)
    h = (x * lax.rsqrt(var + NORM_EPS) * nw_ref[...]).astype(BF16)

    def proj(lo, width):
        return jnp.dot(h, w_ref[:, lo:lo + width], preferred_element_type=F32) + b_ref[:, lo:lo + width]

    cos = c_ref[...]
    sin = s_ref[...]
    lane = lax.broadcasted_iota(jnp.int32, cos.shape, 1)
    first_half = (lane % ATT_HEAD_DIM) < (ROPE_DIM // 2)

    def rope_store(dst_ref, lo, scale=None):
        zz = proj(lo, ATT_WIDTH)
        for j in range(ATT_WIDTH // LANES):
            z = zz[:, j * LANES:(j + 1) * LANES]
            partner = jnp.where(first_half,
                                pltpu.roll(z, LANES - ROPE_DIM // 2, axis=1),
                                pltpu.roll(z, ROPE_DIM // 2, axis=1))
            out = z * cos + partner * sin
            dst_ref[:, j * LANES:(j + 1) * LANES] = out if scale is None else out * scale

    def conv_silu_store(dst_ref, slab0, lo, col0, scale):
        z = proj(lo, MLSTM_WIDTH)
        half = TM_IN // 2
        for j in range(MLSTM_WIDTH // LANES):
            sl = slab0 + j
            ws = slice(col0 + j * LANES, col0 + (j + 1) * LANES)
            u_s[sl, 0:CONV_TAIL, :] = u_s[sl, TM_IN:TM_IN + CONV_TAIL, :]
            u_s[sl, CONV_TAIL:CONV_TAIL + TM_IN, :] = z[:, j * LANES:(j + 1) * LANES]
            for parity in range(2):
                out = cb_ref[:, ws]
                for t in range(CONV_WIDTH):
                    r0 = CONV_TAIL - (CONV_WIDTH - 1) + t + parity
                    out = out + cw_ref[t:t + 1, ws] * u_s[sl, pl.ds(r0, half, stride=2), :]
                out = out * jax.nn.sigmoid(out)
                res_s[sl, pl.ds(parity, half, stride=2), :] = out if scale is None else out * scale
            dst_ref[:, j * LANES:(j + 1) * LANES] = res_s[sl].astype(dst_ref.dtype)

    rope_store(aq_ref, COL_AQ, LOG2E)
    rope_store(ak_ref, COL_AK)
    av_ref[...] = proj(COL_AV, ATT_WIDTH)
    conv_silu_store(mq_ref, 0, COL_MQ, 0, MLSTM_HEAD_DIM ** -0.5)
    conv_silu_store(mk_ref, MLSTM_WIDTH // LANES, COL_MK, MLSTM_WIDTH, None)
    z_mv = proj(COL_MV, MLSTM_WIDTH)
    for cc in range(TM_IN // MCHUNK):
        for hd in range(MLSTM_HEADS):
            blk = z_mv[cc * MCHUNK:(cc + 1) * MCHUNK, hd * MLSTM_HEAD_DIM:(hd + 1) * MLSTM_HEAD_DIM]
            mv_ref[cc, hd * MLSTM_HEAD_DIM:(hd + 1) * MLSTM_HEAD_DIM, :] = blk.T.astype(mv_ref.dtype)
    mo_ref[...] = proj(COL_MO, MLSTM_WIDTH).astype(mo_ref.dtype)

    z_tail = proj(COL_MI, W_PAD - COL_MI)
    tiles = [z_tail[:, j * LANES:(j + 1) * LANES] for j in range((W_PAD - COL_MI) // LANES)]
    rolled = [pltpu.roll(t, LANES - GATE_SHIFT, axis=1) for t in tiles]
    low_lanes = lane < LANES - GATE_SHIFT
    for j in range(D_MODEL // LANES):
        ga_ref[:, j * LANES:(j + 1) * LANES] = jnp.where(low_lanes, rolled[j], rolled[j + 1]).astype(ga_ref.dtype)
        k = j + D_MODEL // LANES
        gb_ref[:, j * LANES:(j + 1) * LANES] = jnp.where(low_lanes, rolled[k], rolled[k + 1]).astype(gb_ref.dtype)

    zg = jnp.where(lane < GATE_SHIFT, tiles[0], 0.0)
    logf = _log_sigmoid(zg)
    ri = lax.broadcasted_iota(jnp.int32, (MCHUNK, MCHUNK), 0)
    ci = lax.broadcasted_iota(jnp.int32, (MCHUNK, MCHUNK), 1)
    tri = (ci <= ri).astype(F32)
    is_input_gate = lax.broadcasted_iota(jnp.int32, (MCHUNK, LANES), 1) < MLSTM_HEADS
    for cc in range(TM_IN // MCHUNK):
        rows = slice(cc * MCHUNK, (cc + 1) * MCHUNK)
        bcum = jnp.dot(tri, logf[rows], precision=lax.Precision.HIGHEST, preferred_element_type=F32)
        gc = jnp.where(is_input_gate, zg[rows], bcum)
        gt_ref[rows, :] = gc
        gtt_ref[cc] = gc.T[0:GATE_ROWS, :]


def _layer_spec(shape, layer):
    nd = len(shape)
    return pl.BlockSpec((None,) + tuple(shape), lambda *_: (layer,) + (0,) * nd, pipeline_mode=pl.Buffered(1))


def _inproj(x2d, norm_w, w_all, b_all, rope_c, rope_s, conv_w, conv_b, seq_len, layer):
    T = x2d.shape[0]
    tile = lambda w: pl.BlockSpec((TM_IN, w), lambda i: (i, 0))
    out_shapes = (
        jax.ShapeDtypeStruct((T, 3 * ATT_WIDTH), F32),
        jax.ShapeDtypeStruct((T, 3 * MLSTM_WIDTH), BF16),
        jax.ShapeDtypeStruct((T // MCHUNK, MLSTM_WIDTH, MCHUNK), BF16),
        jax.ShapeDtypeStruct((T, 2 * D_MODEL), BF16),
        jax.ShapeDtypeStruct((T, LANES), F32),
        jax.ShapeDtypeStruct((T // MCHUNK, GATE_ROWS, MCHUNK), F32),
    )
    per_chunk = lambda rows: pl.BlockSpec((TM_IN // MCHUNK, rows, MCHUNK), lambda i: (i, 0, 0))
    out_specs = tuple(tile(s.shape[1]) if len(s.shape) == 2 else per_chunk(s.shape[1]) for s in out_shapes)
    assert seq_len % TM_IN == 0
    return pl.pallas_call(
        functools.partial(_inproj_kernel, tiles_per_seq=seq_len // TM_IN),
        grid=(T // TM_IN,),
        in_specs=[tile(D_MODEL), _layer_spec((1, D_MODEL), layer), _layer_spec((D_MODEL, W_PAD), layer),
                  _layer_spec((1, W_PAD), layer), tile(LANES), tile(LANES),
                  _layer_spec((CONV_WIDTH, 2 * MLSTM_WIDTH), layer), _layer_spec((1, 2 * MLSTM_WIDTH), layer)],
        out_specs=out_specs,
        out_shape=out_shapes,
        scratch_shapes=[pltpu.VMEM((2 * MLSTM_WIDTH // LANES, TM_IN + CONV_TAIL, LANES), F32),
                        pltpu.VMEM((2 * MLSTM_WIDTH // LANES, TM_IN, LANES), F32)],
        compiler_params=pltpu.CompilerParams(dimension_semantics=("arbitrary",),
                                             vmem_limit_bytes=VMEM_LIMIT),
        name="inproj",
    )(x2d, norm_w, w_all, b_all, rope_c, rope_s, conv_w, conv_b)


ATT_UNROLL = 16
PITCH16 = ATT_SPAN + 8


def _attn_kernel(q_ref, k_ref, v_ref, o_ref,
                 kd1, vd1, qd4, kd4, vd4, qd16, kd16, vd16, tmp, st4_s, st16_s, bias_s):
    j = pl.program_id(2)
    blk = ATT_SPAN
    nsub = {d: ATT_BLK // d // blk for d in DILATIONS}
    kv_bufs = ((1, kd1, vd1), (4, kd4, vd4), (16, kd16, vd16))

    row = lax.broadcasted_iota(jnp.int32, (2 * blk, 2 * blk), 0) % blk
    col = lax.broadcasted_iota(jnp.int32, (2 * blk, 2 * blk), 1)
    band = (col >= row) & (col <= row + ATT_SPAN)
    bias_s[0] = jnp.where(band, 0.0, NEG)
    bias_s[1] = jnp.where(band & (col >= blk), 0.0, NEG)

    @pl.when(j == 0)
    def _():
        for d, kd, vd in kv_bufs:
            n = ATT_BLK // d
            for r in range(d):
                base = r * (n + blk)
                kd[base:base + blk] = jnp.zeros((blk, LANES), BF16)
                vd[base:base + blk] = jnp.zeros((blk, LANES), BF16)

    @pl.when(j != 0)
    def _():
        for d, kd, vd in kv_bufs:
            n = ATT_BLK // d
            for r in range(d):
                base = r * (n + blk)
                kd[base:base + blk] = kd[base + n:base + n + blk]
                vd[base:base + blk] = vd[base + n:base + n + blk]

    def deinterleave(src_ref, dst1, dst4, dst16, is_kv):
        pad = blk if is_kv else 0
        if dst1 is not None:
            dst1[blk:blk + ATT_BLK] = src_ref[0].astype(BF16)
        n4 = ATT_BLK // 4
        for r4 in range(4):
            t4 = src_ref[0, pl.ds(r4, n4, stride=4), :]
            tmp[r4] = t4
            o4 = r4 * (n4 + pad) + pad
            dst4[o4:o4 + n4] = t4.astype(BF16)
        n16 = ATT_BLK // 16
        for r4 in range(4):
            for rr in range(4):
                o16 = (4 * rr + r4) * (n16 + pad) + pad
                dst16[o16:o16 + n16] = tmp[r4, pl.ds(rr, n16, stride=4), :].astype(BF16)

    deinterleave(q_ref, None, qd4, qd16, False)
    deinterleave(k_ref, kd1, kd4, kd16, True)
    deinterleave(v_ref, vd1, vd4, vd16, True)

    head_a = lax.broadcasted_iota(jnp.int32, (blk, LANES), 1) < ATT_HEAD_DIM

    def unit(q2, k2, v2, bias):
        zero = jnp.zeros_like(q2)
        qs = jnp.concatenate([jnp.where(head_a, q2, zero), jnp.where(head_a, zero, q2)], axis=0)
        s = lax.dot_general(qs, k2, (((1,), (1,)), ((), ())), preferred_element_type=F32) + bias
        m = jnp.max(s, axis=-1, keepdims=True)
        p = jnp.exp2(s - m).astype(BF16)
        v_aug = jnp.concatenate([v2, jnp.ones_like(v2)], axis=1)
        pv = jnp.dot(p, v_aug, preferred_element_type=F32)
        acc = jnp.where(head_a, pv[:blk, :LANES], pv[blk:, :LANES])
        ll = jnp.where(head_a, pv[:blk, LANES:], pv[blk:, LANES:])
        mm = jnp.where(head_a, m[:blk], m[blk:])
        return acc, mm, ll

    first_blk = jnp.where(j == 0, 1, 0)

    def body16(u, carry):
        k0 = pl.multiple_of(u * (2 * blk), blk)
        bias = bias_s[first_blk]
        res = unit(qd16[pl.ds(pl.multiple_of(u * blk, blk), blk), :], kd16[pl.ds(k0, 2 * blk), :],
                   vd16[pl.ds(k0, 2 * blk), :], bias)
        row0 = pl.multiple_of(u * PITCH16, 8)
        for a, val in enumerate(res):
            st16_s[a, pl.ds(row0, blk), :] = val
        return carry

    def body4(u, carry):
        r = u // nsub[4]
        sb = u % nsub[4]
        k0 = pl.multiple_of(r * (ATT_BLK // 4 + blk) + sb * blk, blk)
        bias = bias_s[jnp.where(sb == 0, first_blk, 0)]
        res = unit(qd4[pl.ds(pl.multiple_of(u * blk, blk), blk), :], kd4[pl.ds(k0, 2 * blk), :],
                   vd4[pl.ds(k0, 2 * blk), :], bias)
        t0 = sb * (blk * 4) + r
        for a, val in enumerate(res):
            st4_s[a, pl.ds(t0, blk, stride=4), :] = val
        return carry

    lax.fori_loop(0, ATT_BLK // blk, body16, 0, unroll=ATT_UNROLL)
    lax.fori_loop(0, ATT_BLK // blk, body4, 0, unroll=ATT_UNROLL)

    def body1(u, carry):
        r0 = pl.multiple_of(u * blk, blk)
        sl = pl.ds(r0, blk)
        q2 = q_ref[0, sl, :].astype(BF16)
        bias = bias_s[jnp.where(u == 0, first_blk, 0)]
        acc2, m2, l2 = unit(q2, kd1[pl.ds(r0, 2 * blk), :], vd1[pl.ds(r0, 2 * blk), :], bias)
        per_res = blk // 16
        acc1, m1, l1 = (jnp.concatenate([st16_s[a, pl.ds(per_res * u + k, 16, stride=PITCH16), :]
                                         for k in range(per_res)], axis=0) for a in range(3))
        acc0, m0, l0 = st4_s[0, sl, :], st4_s[1, sl, :], st4_s[2, sl, :]
        mx = jnp.maximum(jnp.maximum(m0, m1), m2)
        w0, w1, w2 = jnp.exp2(m0 - mx), jnp.exp2(m1 - mx), jnp.exp2(m2 - mx)
        num = w0 * acc0 + w1 * acc1 + w2 * acc2
        den = w0 * l0 + w1 * l1 + w2 * l2
        o_ref[0, sl, :] = (num / den).astype(o_ref.dtype)
        return carry

    lax.fori_loop(0, nsub[1], body1, 0, unroll=ATT_UNROLL)


def _attention(qkv):
    B, S, _ = qkv.shape
    blk = ATT_SPAN
    pairs = ATT_WIDTH // LANES
    cur = pl.BlockSpec((1, ATT_BLK, LANES), lambda b, hp, j: (b, j, hp))
    part = lambda p: pl.BlockSpec((1, ATT_BLK, LANES), lambda b, hp, j: (b, j, p * pairs + hp))
    kv_rows = {d: d * (ATT_BLK // d + blk) for d in DILATIONS}
    scratch = [
        pltpu.VMEM((kv_rows[1], LANES), BF16), pltpu.VMEM((kv_rows[1], LANES), BF16),
        pltpu.VMEM((ATT_BLK, LANES), BF16),
        pltpu.VMEM((kv_rows[4], LANES), BF16), pltpu.VMEM((kv_rows[4], LANES), BF16),
        pltpu.VMEM((ATT_BLK, LANES), BF16),
        pltpu.VMEM((kv_rows[16], LANES), BF16), pltpu.VMEM((kv_rows[16], LANES), BF16),
        pltpu.VMEM((4, ATT_BLK // 4, LANES), F32),
        pltpu.VMEM((3, ATT_BLK, LANES), F32),
        pltpu.VMEM((3, 16 * PITCH16, LANES), F32),
        pltpu.VMEM((2, 2 * blk, 2 * blk), F32),
    ]
    return pl.pallas_call(
        _attn_kernel,
        grid=(B, pairs, S // ATT_BLK),
        in_specs=[part(0), part(1), part(2)],
        out_specs=cur,
        out_shape=jax.ShapeDtypeStruct((B, S, ATT_WIDTH), BF16),
        scratch_shapes=scratch,
        compiler_params=pltpu.CompilerParams(dimension_semantics=("parallel", "parallel", "arbitrary"),
                                             vmem_limit_bytes=VMEM_LIMIT),
        name="dilated_attention",
    )(qkv, qkv, qkv)


NORM_ROWS = 16


def _mlstm_kernel(qko_ref, vt_ref, g_ref, gt_ref, nw_ref, y_ref, c_state, m_state):
    L = MCHUNK
    D = MLSTM_HEAD_DIM
    nt_dims = (((1,), (1,)), ((), ()))
    q_ref, k_ref, mo_ref = (qko_ref.at[:, j * MLSTM_WIDTH:(j + 1) * MLSTM_WIDTH] for j in range(3))

    @pl.when(pl.program_id(1) == 0)
    def _():
        c_state[...] = jnp.zeros_like(c_state)
        m_state[...] = jnp.zeros_like(m_state)

    key = lax.broadcasted_iota(jnp.int32, (L, L), 0)
    qry = lax.broadcasted_iota(jnp.int32, (L, L), 1)
    visible = key <= qry
    ones_rows = jnp.ones((NORM_ROWS, L), BF16)

    units = [(cc, h) for cc in range(TM_MLSTM // L) for h in range(MLSTM_HEADS)]
    rows = lambda cc: slice(cc * L, (cc + 1) * L)
    cols = lambda h: slice(h * D, (h + 1) * D)

    s_t = {u: lax.dot_general(k_ref[rows(u[0]), cols(u[1])], q_ref[rows(u[0]), cols(u[1])], nt_dims,
                              preferred_element_type=F32) for u in units}
    b_row, g_tot, pm, vt_aug, intra, m_loc, kv = {}, {}, {}, {}, {}, {}, {}
    for u in units:
        cc, h = u
        fh = MLSTM_HEADS + h
        r_col = g_ref[rows(cc), h:h + 1] - g_ref[rows(cc), fh:fh + 1]
        b_row[u] = gt_ref[cc, fh:fh + 1, :]
        g_tot[u] = b_row[u][:, L - 1:L]
        r_vis = jnp.where(visible, r_col, NEG)
        pm[u] = jnp.max(r_vis, axis=0, keepdims=True)
        p_t = (jnp.exp(r_vis - pm[u]) * s_t[u]).astype(BF16)
        vt_aug[u] = jnp.concatenate([vt_ref[cc, cols(h), :], ones_rows], axis=0)
        intra[u] = jnp.dot(vt_aug[u], p_t, preferred_element_type=F32)
    for u in units:
        cc, h = u
        r_row = gt_ref[cc, h:h + 1, :] - b_row[u]
        m_loc[u] = jnp.max(g_tot[u] + r_row, axis=-1, keepdims=True)
        vw = (vt_aug[u].astype(F32) * jnp.exp(g_tot[u] + r_row - m_loc[u])).astype(BF16)
        kv[u] = jnp.dot(vw, k_ref[rows(cc), cols(h)], preferred_element_type=F32)

    c_aug = [c_state[h] for h in range(MLSTM_HEADS)]
    m_prev = [m_state[h:h + 1, 0:1] for h in range(MLSTM_HEADS)]
    for u in units:
        cc, h = u
        inter = lax.dot_general(c_aug[h].astype(BF16), q_ref[rows(cc), cols(h)], nt_dims,
                                preferred_element_type=F32)
        mm = jnp.maximum(pm[u], m_prev[h])
        nd = jnp.exp(pm[u] - mm) * intra[u] + jnp.exp(m_prev[h] - mm) * inter
        inv = 1.0 / jnp.maximum(jnp.abs(nd[D:D + 1, :]), jnp.exp(-(b_row[u] + mm)))
        hh = nd[0:D, :] * inv
        hn = (hh * lax.rsqrt(jnp.mean(hh * hh, axis=0, keepdims=True) + NORM_EPS)).T
        o_gate = jax.nn.sigmoid(mo_ref[rows(cc), cols(h)].astype(F32))
        y_ref[rows(cc), cols(h)] = (o_gate * hn * nw_ref[:, cols(h)]).astype(y_ref.dtype)

        m_new = jnp.maximum(g_tot[u] + m_prev[h], m_loc[u])
        c_aug[h] = jnp.exp(g_tot[u] + m_prev[h] - m_new) * c_aug[h] + jnp.exp(m_loc[u] - m_new) * kv[u]
        m_prev[h] = m_new
    for h in range(MLSTM_HEADS):
        c_state[h] = c_aug[h]
        m_state[h:h + 1, :] = jnp.broadcast_to(m_prev[h], (1, LANES))


def _mlstm(mqko, mvt, gates, gates_t, norm_w, batch):
    T = mqko.shape[0]
    nt = T // batch // TM_MLSTM
    nch = TM_MLSTM // MCHUNK
    tile = lambda w: pl.BlockSpec((TM_MLSTM, w), lambda b, c: (b * nt + c, 0))
    per_chunk = lambda r: pl.BlockSpec((nch, r, MCHUNK), lambda b, c: (b * nt + c, 0, 0))
    return pl.pallas_call(
        _mlstm_kernel,
        grid=(batch, nt),
        in_specs=[tile(3 * MLSTM_WIDTH), per_chunk(MLSTM_WIDTH), tile(LANES),
                  per_chunk(GATE_ROWS), pl.BlockSpec((1, MLSTM_WIDTH), lambda b, c: (0, 0))],
        out_specs=tile(MLSTM_WIDTH),
        out_shape=jax.ShapeDtypeStruct((T, MLSTM_WIDTH), BF16),
        scratch_shapes=[
            pltpu.VMEM((MLSTM_HEADS, MLSTM_HEAD_DIM + NORM_ROWS, MLSTM_HEAD_DIM), F32),
            pltpu.VMEM((8, LANES), F32),
        ],
        compiler_params=pltpu.CompilerParams(dimension_semantics=("parallel", "arbitrary"),
                                             vmem_limit_bytes=VMEM_LIMIT),
        name="mlstm",
    )(mqko, mvt, gates, gates_t, norm_w)


def _post_kernel(x_ref, ya_ref, yb_ref, gab_ref, wpa_ref, wpm_ref, wo_ref, nw_ref, w1_ref, w2_ref,
                 fw_ref, o_ref, *, final_norm):
    ga_ref, gb_ref = (gab_ref.at[:, j * D_MODEL:(j + 1) * D_MODEL] for j in range(2))
    pa = jnp.dot(ya_ref[...], wpa_ref[...], preferred_element_type=F32)
    pb = jnp.dot(yb_ref[...], wpm_ref[...], preferred_element_type=F32)
    mixed = (jax.nn.sigmoid(ga_ref[...].astype(F32)) * pa
             + jax.nn.sigmoid(gb_ref[...].astype(F32)) * pb).astype(BF16)
    x1 = x_ref[...] + jnp.dot(mixed, wo_ref[...], preferred_element_type=F32)
    var = jnp.mean(x1 * x1, axis=-1, keepdims=True)
    h2 = (x1 * lax.rsqrt(var + NORM_EPS) * nw_ref[...]).astype(BF16)
    acc = x1
    ff_chunk = D_MODEL
    for c in range(D_FF // ff_chunk):
        cs = slice(c * ff_chunk, (c + 1) * ff_chunk)
        u = jnp.maximum(jnp.dot(h2, w1_ref[:, cs], preferred_element_type=F32), 0.0)
        acc = acc + jnp.dot((u * u).astype(BF16), w2_ref[cs, :], preferred_element_type=F32)
    if final_norm:
        var = jnp.mean(acc * acc, axis=-1, keepdims=True)
        acc = acc * lax.rsqrt(var + NORM_EPS) * fw_ref[...]
    o_ref[...] = acc


def _post(x2d, ya, yb, gab, wpa, wpm, wo, norm_w, w1, w2, final_w, layer, final_norm):
    T = x2d.shape[0]
    tile = lambda w: pl.BlockSpec((TM_POST, w), lambda i: (i, 0))
    return pl.pallas_call(
        functools.partial(_post_kernel, final_norm=final_norm),
        grid=(T // TM_POST,),
        in_specs=[tile(D_MODEL), tile(ATT_WIDTH), tile(MLSTM_WIDTH), tile(2 * D_MODEL),
                  _layer_spec((ATT_WIDTH, D_MODEL), layer), _layer_spec((MLSTM_WIDTH, D_MODEL), layer),
                  _layer_spec((D_MODEL, D_MODEL), layer), _layer_spec((1, D_MODEL), layer),
                  _layer_spec((D_MODEL, D_FF), layer), _layer_spec((D_FF, D_MODEL), layer),
                  _const_spec((1, D_MODEL))],
        out_specs=tile(D_MODEL),
        out_shape=jax.ShapeDtypeStruct((T, D_MODEL), F32),
        compiler_params=pltpu.CompilerParams(dimension_semantics=("parallel",),
                                             vmem_limit_bytes=VMEM_LIMIT),
        name="post",
    )(x2d, ya, yb, gab, wpa, wpm, wo, norm_w, w1, w2, final_w)


PREP_STEPS = 8


PREP_IN_ROWS = 640


def _cast_in_proj_kernel(w_ref, o_ref):
    row = pl.program_id(0) * PREP_IN_ROWS + lax.broadcasted_iota(jnp.int32, w_ref.shape[1:], 0)
    for layer in range(w_ref.shape[0]):
        w = w_ref[layer]
        w = jnp.where(row < COL_AK, w * (ATT_HEAD_DIM ** -0.5), w)
        o_ref[layer] = jnp.where(row < D_IN, w, 0.0).T.astype(BF16)


def _cast_in_proj(w_t):
    layers, _, feat = w_t.shape
    assert W_PAD % PREP_IN_ROWS == 0 and PREP_IN_ROWS % LANES == 0
    return pl.pallas_call(
        _cast_in_proj_kernel,
        grid=(W_PAD // PREP_IN_ROWS,),
        in_specs=[pl.BlockSpec((layers, PREP_IN_ROWS, feat), lambda i: (0, i, 0))],
        out_specs=pl.BlockSpec((layers, feat, PREP_IN_ROWS), lambda i: (0, 0, i)),
        out_shape=jax.ShapeDtypeStruct((layers, feat, W_PAD), BF16),
        compiler_params=pltpu.CompilerParams(dimension_semantics=("parallel",), vmem_limit_bytes=VMEM_LIMIT),
        name="cast_in_proj",
    )(w_t)


def _cast_weights_kernel(*refs):
    n = len(refs) // 2
    for src, dst in zip(refs[:n], refs[n:]):
        dst[...] = src[...].astype(BF16)


def _cast_weights(weights):
    def slab(shape):
        assert shape[1] % (PREP_STEPS * 16) == 0
        return pl.BlockSpec((shape[0], shape[1] // PREP_STEPS, shape[2]), lambda i: (0, i, 0))

    return pl.pallas_call(
        _cast_weights_kernel,
        grid=(PREP_STEPS,),
        in_specs=[slab(a.shape) for a in weights],
        out_specs=[slab(a.shape) for a in weights],
        out_shape=[jax.ShapeDtypeStruct(a.shape, BF16) for a in weights],
        compiler_params=pltpu.CompilerParams(dimension_semantics=("parallel",), vmem_limit_bytes=VMEM_LIMIT),
        name="cast_weights",
    )(*weights)


def kernel(x, positions, norm_mix_w, w_in, b_in, conv_w, conv_b, mlstm_norm_w, w_proj_att, w_proj_mlstm,
           w_out, norm_mlp_w, w_ff1, w_ff2, final_norm_w):
    B, S, D = x.shape
    T = B * S
    depth = w_in.shape[0]
    assert D == D_MODEL and S % ATT_BLK == 0 and T % TM_IN == 0 and T % TM_POST == 0
    assert math.isclose(ATT_HEAD_DIM ** -0.5, 0.125)
    rope_c, rope_s = _rope_tables(positions)
    w_all = _cast_in_proj(jnp.swapaxes(w_in, 1, 2))
    wpa, wpm, wo, w1, w2 = _cast_weights((w_proj_att, w_proj_mlstm, w_out, w_ff1, w_ff2))
    q_scale = jnp.where(jnp.arange(W_PAD) < COL_AK, ATT_HEAD_DIM ** -0.5, 1.0).astype(F32)
    b_all = (jnp.pad(b_in, ((0, 0), (0, W_PAD - D_IN))) * q_scale).reshape(depth, 1, W_PAD)
    x2d = x.reshape(T, D)
    final_w = final_norm_w.reshape(1, D).astype(F32)
    for l in range(depth):
        qkv, mqko, mvt, gab, gates, gates_t = _inproj(
            x2d, norm_mix_w.reshape(depth, 1, D), w_all, b_all, rope_c, rope_s,
            conv_w, conv_b.reshape(depth, 1, -1), S, l)
        ya = _attention(qkv.reshape(B, S, 3 * ATT_WIDTH))
        yb = _mlstm(mqko, mvt, gates, gates_t, mlstm_norm_w[l].reshape(1, -1), B)
        x2d = _post(x2d, ya.reshape(T, ATT_WIDTH), yb, gab, wpa, wpm, wo,
                    norm_mlp_w.reshape(depth, 1, D), w1, w2, final_w, l, final_norm=(l == depth - 1))
    return x2d.reshape(B, S, D)
```

```python
import functools
import math

import jax
import jax.numpy as jnp
from jax import lax
from jax.experimental import pallas as pl
from jax.experimental.pallas import tpu as pltpu

F32 = jnp.float32
BF16 = jnp.bfloat16

D_MODEL = 1024
ATT_HEADS = 8
ATT_HEAD_DIM = 64
ATT_WIDTH = ATT_HEADS * ATT_HEAD_DIM
ATT_SPAN = 128
DILATIONS = (1, 4, 16)
ROPE_THETA = 500000.0
ROPE_DIM = ATT_HEAD_DIM // 4
MLSTM_HEADS = 4
MLSTM_HEAD_DIM = 128
MLSTM_WIDTH = MLSTM_HEADS * MLSTM_HEAD_DIM
CONV_WIDTH = 4
D_FF = 4 * D_MODEL
NORM_EPS = 1e-6

COL_AQ = 0
COL_AK = COL_AQ + ATT_WIDTH
COL_AV = COL_AK + ATT_WIDTH
COL_MQ = COL_AV + ATT_WIDTH
COL_MK = COL_MQ + MLSTM_WIDTH
COL_MV = COL_MK + MLSTM_WIDTH
COL_MO = COL_MV + MLSTM_WIDTH
COL_MI = COL_MO + MLSTM_WIDTH
COL_MF = COL_MI + MLSTM_HEADS
COL_GA = COL_MF + MLSTM_HEADS
COL_GB = COL_GA + D_MODEL
D_IN = COL_GB + D_MODEL

LANES = 128
assert COL_MI % LANES == 0
W_MAIN = D_IN // LANES * LANES
W_PAD = W_MAIN + LANES
GATE_SHIFT = 2 * MLSTM_HEADS

TM_IN = 512
TM_POST = 512
ATT_BLK = 2048
MCHUNK = 128
TM_MLSTM = 1024
VMEM_LIMIT = 56 * 1024 * 1024
NEG = -1e30
LOG2E = math.log2(math.e)


def _const_spec(shape):
    nd = len(shape)
    return pl.BlockSpec(shape, lambda *_: (0,) * nd, pipeline_mode=pl.Buffered(1))


ROPE_HALF = ROPE_DIM // 2
ROPE_PACK = LANES // ROPE_HALF
ROPE_ROWS = 128


def _rope_table_kernel(pos_ref, invf_ref, c_ref, s_ref):
    ang = pos_ref[...] * invf_ref[...]
    cosx = jnp.cos(ang)
    sinx = jnp.sin(ang)
    dst = lax.broadcasted_iota(jnp.int32, (ROPE_ROWS, LANES), 1)
    in_head = dst % ATT_HEAD_DIM
    rotary = in_head < ROPE_DIM
    sign = jnp.where(in_head < ROPE_HALF, -1.0, 1.0)
    for r in range(ROPE_PACK):
        src = r * ROPE_HALF + dst % ROPE_HALF
        c_ref[pl.ds(r, ROPE_ROWS, stride=ROPE_PACK), :] = jnp.where(
            rotary, jnp.take_along_axis(cosx, src, axis=1), 1.0)
        s_ref[pl.ds(r, ROPE_ROWS, stride=ROPE_PACK), :] = jnp.where(
            rotary, jnp.take_along_axis(sinx, src, axis=1) * sign, 0.0)


def _rope_tables(positions):
    T = positions.size
    assert T % (ROPE_PACK * ROPE_ROWS) == 0
    pos = jnp.repeat(positions.astype(F32).reshape(T // ROPE_PACK, ROPE_PACK), ROPE_HALF, axis=1)
    inv_freq = ROPE_THETA ** (-jnp.arange(0, ROPE_DIM, 2, dtype=F32) / ROPE_DIM)
    invf = jnp.tile(inv_freq, ROPE_PACK).reshape(1, LANES)
    table = jax.ShapeDtypeStruct((T, LANES), F32)
    out_spec = pl.BlockSpec((ROPE_PACK * ROPE_ROWS, LANES), lambda i: (i, 0))
    return pl.pallas_call(
        _rope_table_kernel,
        grid=(T // (ROPE_PACK * ROPE_ROWS),),
        in_specs=[pl.BlockSpec((ROPE_ROWS, LANES), lambda i: (i, 0)), pl.BlockSpec((1, LANES), lambda i: (0, 0))],
        out_specs=(out_spec, out_spec),
        out_shape=(table, table),
        compiler_params=pltpu.CompilerParams(dimension_semantics=("parallel",)),
        name="rope_tables",
    )(pos, invf)


CONV_TAIL = 8
GATE_ROWS = 8


def _log_sigmoid(x):
    return jnp.minimum(x, 0.0) - jnp.log(1.0 + jnp.exp(-jnp.abs(x)))


def _inproj_kernel(x_ref, nw_ref, w_ref, b_ref, c_ref, s_ref, cw_ref, cb_ref,
                   qkv_ref, mqko_ref, mv_ref, gab_ref, gt_ref, gtt_ref,
                   u_s, res_s, *, tiles_per_seq):
    aq_ref, ak_ref, av_ref = (qkv_ref.at[:, j * ATT_WIDTH:(j + 1) * ATT_WIDTH] for j in range(3))
    mq_ref, mk_ref, mo_ref = (mqko_ref.at[:, j * MLSTM_WIDTH:(j + 1) * MLSTM_WIDTH] for j in range(3))
    ga_ref, gb_ref = (gab_ref.at[:, j * D_MODEL:(j + 1) * D_MODEL] for j in range(2))
    @pl.when(pl.program_id(0) % tiles_per_seq == 0)
    def _():
        u_s[:, TM_IN:TM_IN + CONV_TAIL, :] = jnp.zeros((u_s.shape[0], CONV_TAIL, LANES), F32)

    x = x_ref[...]
    var = jnp.mean(x * x, axis=-1, keepdims=True)
    h = (x * lax.rsqrt(var + NORM_EPS) * nw_ref[...]).astype(BF16)

    def proj(lo, width):
        return jnp.dot(h, w_ref[:, lo:lo + width], preferred_element_type=F32) + b_ref[:, lo:lo + width]

    cos = c_ref[...]
    sin = s_ref[...]
    lane = lax.broadcasted_iota(jnp.int32, cos.shape, 1)
    first_half = (lane % ATT_HEAD_DIM) < (ROPE_DIM // 2)

    def rope_store(dst_ref, lo, scale=None):
        zz = proj(lo, ATT_WIDTH)
        for j in range(ATT_WIDTH // LANES):
            z = zz[:, j * LANES:(j + 1) * LANES]
            partner = jnp.where(first_half,
                                pltpu.roll(z, LANES - ROPE_DIM // 2, axis=1),
                                pltpu.roll(z, ROPE_DIM // 2, axis=1))
            out = z * cos + partner * sin
            dst_ref[:, j * LANES:(j + 1) * LANES] = out if scale is None else out * scale

    def conv_silu_store(dst_ref, slab0, lo, col0, scale):
        z = proj(lo, MLSTM_WIDTH)
        half = TM_IN // 2
        for j in range(MLSTM_WIDTH // LANES):
            sl = slab0 + j
            ws = slice(col0 + j * LANES, col0 + (j + 1) * LANES)
            u_s[sl, 0:CONV_TAIL, :] = u_s[sl, TM_IN:TM_IN + CONV_TAIL, :]
            u_s[sl, CONV_TAIL:CONV_TAIL + TM_IN, :] = z[:, j * LANES:(j + 1) * LANES]
            for parity in range(2):
                out = cb_ref[:, ws]
                for t in range(CONV_WIDTH):
                    r0 = CONV_TAIL - (CONV_WIDTH - 1) + t + parity
                    out = out + cw_ref[t:t + 1, ws] * u_s[sl, pl.ds(r0, half, stride=2), :]
                out = out * jax.nn.sigmoid(out)
                res_s[sl, pl.ds(parity, half, stride=2), :] = out if scale is None else out * scale
            dst_ref[:, j * LANES:(j + 1) * LANES] = res_s[sl].astype(dst_ref.dtype)

    rope_store(aq_ref, COL_AQ, LOG2E)
    rope_store(ak_ref, COL_AK)
    av_ref[...] = proj(COL_AV, ATT_WIDTH)
    conv_silu_store(mq_ref, 0, COL_MQ, 0, MLSTM_HEAD_DIM ** -0.5)
    conv_silu_store(mk_ref, MLSTM_WIDTH // LANES, COL_MK, MLSTM_WIDTH, None)
    z_mv = proj(COL_MV, MLSTM_WIDTH)
    for cc in range(TM_IN // MCHUNK):
        for hd in range(MLSTM_HEADS):
            blk = z_mv[cc * MCHUNK:(cc + 1) * MCHUNK, hd * MLSTM_HEAD_DIM:(hd + 1) * MLSTM_HEAD_DIM]
            mv_ref[cc, hd * MLSTM_HEAD_DIM:(hd + 1) * MLSTM_HEAD_DIM, :] = blk.T.astype(mv_ref.dtype)
    mo_ref[...] = proj(COL_MO, MLSTM_WIDTH).astype(mo_ref.dtype)

    z_tail = proj(COL_MI, W_PAD - COL_MI)
    tiles = [z_tail[:, j * LANES:(j + 1) * LANES] for j in range((W_PAD - COL_MI) // LANES)]
    rolled = [pltpu.roll(t, LANES - GATE_SHIFT, axis=1) for t in tiles]
    low_lanes = lane < LANES - GATE_SHIFT
    for j in range(D_MODEL // LANES):
        ga_ref[:, j * LANES:(j + 1) * LANES] = jnp.where(low_lanes, rolled[j], rolled[j + 1]).astype(ga_ref.dtype)
        k = j + D_MODEL // LANES
        gb_ref[:, j * LANES:(j + 1) * LANES] = jnp.where(low_lanes, rolled[k], rolled[k + 1]).astype(gb_ref.dtype)

    zg = jnp.where(lane < GATE_SHIFT, tiles[0], 0.0)
    logf = _log_sigmoid(zg)
    ri = lax.broadcasted_iota(jnp.int32, (MCHUNK, MCHUNK), 0)
    ci = lax.broadcasted_iota(jnp.int32, (MCHUNK, MCHUNK), 1)
    tri = (ci <= ri).astype(F32)
    is_input_gate = lax.broadcasted_iota(jnp.int32, (MCHUNK, LANES), 1) < MLSTM_HEADS
    for cc in range(TM_IN // MCHUNK):
        rows = slice(cc * MCHUNK, (cc + 1) * MCHUNK)
        bcum = jnp.dot(tri, logf[rows], precision=lax.Precision.HIGHEST, preferred_element_type=F32)
        gc = jnp.where(is_input_gate, zg[rows], bcum)
        gt_ref[rows, :] = gc
        gtt_ref[cc] = gc.T[0:GATE_ROWS, :]


def _layer_spec(shape, layer):
    nd = len(shape)
    return pl.BlockSpec((None,) + tuple(shape), lambda *_: (layer,) + (0,) * nd, pipeline_mode=pl.Buffered(1))


def _inproj(x2d, norm_w, w_all, b_all, rope_c, rope_s, conv_w, conv_b, seq_len, layer):
    T = x2d.shape[0]
    tile = lambda w: pl.BlockSpec((TM_IN, w), lambda i: (i, 0))
    out_shapes = (
        jax.ShapeDtypeStruct((T, 3 * ATT_WIDTH), F32),
        jax.ShapeDtypeStruct((T, 3 * MLSTM_WIDTH), BF16),
        jax.ShapeDtypeStruct((T // MCHUNK, MLSTM_WIDTH, MCHUNK), BF16),
        jax.ShapeDtypeStruct((T, 2 * D_MODEL), BF16),
        jax.ShapeDtypeStruct((T, LANES), F32),
        jax.ShapeDtypeStruct((T // MCHUNK, GATE_ROWS, MCHUNK), F32),
    )
    per_chunk = lambda rows: pl.BlockSpec((TM_IN // MCHUNK, rows, MCHUNK), lambda i: (i, 0, 0))
    out_specs = tuple(tile(s.shape[1]) if len(s.shape) == 2 else per_chunk(s.shape[1]) for s in out_shapes)
    assert seq_len % TM_IN == 0
    return pl.pallas_call(
        functools.partial(_inproj_kernel, tiles_per_seq=seq_len // TM_IN),
        grid=(T // TM_IN,),
        in_specs=[tile(D_MODEL), _layer_spec((1, D_MODEL), layer), _layer_spec((D_MODEL, W_PAD), layer),
                  _layer_spec((1, W_PAD), layer), tile(LANES), tile(LANES),
                  _layer_spec((CONV_WIDTH, 2 * MLSTM_WIDTH), layer), _layer_spec((1, 2 * MLSTM_WIDTH), layer)],
        out_specs=out_specs,
        out_shape=out_shapes,
        scratch_shapes=[pltpu.VMEM((2 * MLSTM_WIDTH // LANES, TM_IN + CONV_TAIL, LANES), F32),
                        pltpu.VMEM((2 * MLSTM_WIDTH // LANES, TM_IN, LANES), F32)],
        compiler_params=pltpu.CompilerParams(dimension_semantics=("arbitrary",),
                                             vmem_limit_bytes=VMEM_LIMIT),
        name="inproj",
    )(x2d, norm_w, w_all, b_all, rope_c, rope_s, conv_w, conv_b)


ATT_UNROLL = 16
PITCH16 = ATT_SPAN + 8


def _attn_kernel(q_ref, k_ref, v_ref, o_ref,
                 kd1, vd1, qd4, kd4, vd4, qd16, kd16, vd16, tmp, st4_s, st16_s, bias_s):
    j = pl.program_id(2)
    blk = ATT_SPAN
    nsub = {d: ATT_BLK // d // blk for d in DILATIONS}
    kv_bufs = ((1, kd1, vd1), (4, kd4, vd4), (16, kd16, vd16))

    row = lax.broadcasted_iota(jnp.int32, (2 * blk, 2 * blk), 0) % blk
    col = lax.broadcasted_iota(jnp.int32, (2 * blk, 2 * blk), 1)
    band = (col >= row) & (col <= row + ATT_SPAN)
    bias_s[0] = jnp.where(band, 0.0, NEG)
    bias_s[1] = jnp.where(band & (col >= blk), 0.0, NEG)

    @pl.when(j == 0)
    def _():
        for d, kd, vd in kv_bufs:
            n = ATT_BLK // d
            for r in range(d):
                base = r * (n + blk)
                kd[base:base + blk] = jnp.zeros((blk, LANES), BF16)
                vd[base:base + blk] = jnp.zeros((blk, LANES), BF16)

    @pl.when(j != 0)
    def _():
        for d, kd, vd in kv_bufs:
            n = ATT_BLK // d
            for r in range(d):
                base = r * (n + blk)
                kd[base:base + blk] = kd[base + n:base + n + blk]
                vd[base:base + blk] = vd[base + n:base + n + blk]

    def deinterleave(src_ref, dst1, dst4, dst16, is_kv):
        pad = blk if is_kv else 0
        if dst1 is not None:
            dst1[blk:blk + ATT_BLK] = src_ref[0].astype(BF16)
        n4 = ATT_BLK // 4
        for r4 in range(4):
            t4 = src_ref[0, pl.ds(r4, n4, stride=4), :]
            tmp[r4] = t4
            o4 = r4 * (n4 + pad) + pad
            dst4[o4:o4 + n4] = t4.astype(BF16)
        n16 = ATT_BLK // 16
        for r4 in range(4):
            for rr in range(4):
                o16 = (4 * rr + r4) * (n16 + pad) + pad
                dst16[o16:o16 + n16] = tmp[r4, pl.ds(rr, n16, stride=4), :].astype(BF16)

    deinterleave(q_ref, None, qd4, qd16, False)
    deinterleave(k_ref, kd1, kd4, kd16, True)
    deinterleave(v_ref, vd1, vd4, vd16, True)

    head_a = lax.broadcasted_iota(jnp.int32, (blk, LANES), 1) < ATT_HEAD_DIM

    def unit(q2, k2, v2, bias):
        zero = jnp.zeros_like(q2)
        qs = jnp.concatenate([jnp.where(head_a, q2, zero), jnp.where(head_a, zero, q2)], axis=0)
        s = lax.dot_general(qs, k2, (((1,), (1,)), ((), ())), preferred_element_type=F32) + bias
        m = jnp.max(s, axis=-1, keepdims=True)
        p = jnp.exp2(s - m).astype(BF16)
        v_aug = jnp.concatenate([v2, jnp.ones_like(v2)], axis=1)
        pv = jnp.dot(p, v_aug, preferred_element_type=F32)
        acc = jnp.where(head_a, pv[:blk, :LANES], pv[blk:, :LANES])
        ll = jnp.where(head_a, pv[:blk, LANES:], pv[blk:, LANES:])
        mm = jnp.where(head_a, m[:blk], m[blk:])
        return acc, mm, ll

    first_blk = jnp.where(j == 0, 1, 0)

    def body16(u, carry):
        k0 = pl.multiple_of(u * (2 * blk), blk)
        bias = bias_s[first_blk]
        res = unit(qd16[pl.ds(pl.multiple_of(u * blk, blk), blk), :], kd16[pl.ds(k0, 2 * blk), :],
                   vd16[pl.ds(k0, 2 * blk), :], bias)
        row0 = pl.multiple_of(u * PITCH16, 8)
        for a, val in enumerate(res):
            st16_s[a, pl.ds(row0, blk), :] = val
        return carry

    def body4(u, carry):
        r = u // nsub[4]
        sb = u % nsub[4]
        k0 = pl.multiple_of(r * (ATT_BLK // 4 + blk) + sb * blk, blk)
        bias = bias_s[jnp.where(sb == 0, first_blk, 0)]
        res = unit(qd4[pl.ds(pl.multiple_of(u * blk, blk), blk), :], kd4[pl.ds(k0, 2 * blk), :],
                   vd4[pl.ds(k0, 2 * blk), :], bias)
        t0 = sb * (blk * 4) + r
        for a, val in enumerate(res):
            st4_s[a, pl.ds(t0, blk, stride=4), :] = val
        return carry

    lax.fori_loop(0, ATT_BLK // blk, body16, 0, unroll=ATT_UNROLL)
    lax.fori_loop(0, ATT_BLK // blk, body4, 0, unroll=ATT_UNROLL)

    def body1(u, carry):
        r0 = pl.multiple_of(u * blk, blk)
        sl = pl.ds(r0, blk)
        q2 = q_ref[0, sl, :].astype(BF16)
        bias = bias_s[jnp.where(u == 0, first_blk, 0)]
        acc2, m2, l2 = unit(q2, kd1[pl.ds(r0, 2 * blk), :], vd1[pl.ds(r0, 2 * blk), :], bias)
        per_res = blk // 16
        acc1, m1, l1 = (jnp.concatenate([st16_s[a, pl.ds(per_res * u + k, 16, stride=PITCH16), :]
                                         for k in range(per_res)], axis=0) for a in range(3))
        acc0, m0, l0 = st4_s[0, sl, :], st4_s[1, sl, :], st4_s[2, sl, :]
        mx = jnp.maximum(jnp.maximum(m0, m1), m2)
        w0, w1, w2 = jnp.exp2(m0 - mx), jnp.exp2(m1 - mx), jnp.exp2(m2 - mx)
        num = w0 * acc0 + w1 * acc1 + w2 * acc2
        den = w0 * l0 + w1 * l1 + w2 * l2
        o_ref[0, sl, :] = (num / den).astype(o_ref.dtype)
        return carry

    lax.fori_loop(0, nsub[1], body1, 0, unroll=ATT_UNROLL)


def _attention(qkv):
    B, S, _ = qkv.shape
    blk = ATT_SPAN
    pairs = ATT_WIDTH // LANES
    cur = pl.BlockSpec((1, ATT_BLK, LANES), lambda b, hp, j: (b, j, hp))
    part = lambda p: pl.BlockSpec((1, ATT_BLK, LANES), lambda b, hp, j: (b, j, p * pairs + hp))
    kv_rows = {d: d * (ATT_BLK // d + blk) for d in DILATIONS}
    scratch = [
        pltpu.VMEM((kv_rows[1], LANES), BF16), pltpu.VMEM((kv_rows[1], LANES), BF16),
        pltpu.VMEM((ATT_BLK, LANES), BF16),
        pltpu.VMEM((kv_rows[4], LANES), BF16), pltpu.VMEM((kv_rows[4], LANES), BF16),
        pltpu.VMEM((ATT_BLK, LANES), BF16),
        pltpu.VMEM((kv_rows[16], LANES), BF16), pltpu.VMEM((kv_rows[16], LANES), BF16),
        pltpu.VMEM((4, ATT_BLK // 4, LANES), F32),
        pltpu.VMEM((3, ATT_BLK, LANES), F32),
        pltpu.VMEM((3, 16 * PITCH16, LANES), F32),
        pltpu.VMEM((2, 2 * blk, 2 * blk), F32),
    ]
    return pl.pallas_call(
        _attn_kernel,
        grid=(B, pairs, S // ATT_BLK),
        in_specs=[part(0), part(1), part(2)],
        out_specs=cur,
        out_shape=jax.ShapeDtypeStruct((B, S, ATT_WIDTH), BF16),
        scratch_shapes=scratch,
        compiler_params=pltpu.CompilerParams(dimension_semantics=("parallel", "parallel", "arbitrary"),
                                             vmem_limit_bytes=VMEM_LIMIT),
        name="dilated_attention",
    )(qkv, qkv, qkv)


NORM_ROWS = 16


def _mlstm_kernel(qko_ref, vt_ref, g_ref, gt_ref, nw_ref, y_ref, c_state, m_state):
    L = MCHUNK
    D = MLSTM_HEAD_DIM
    nt_dims = (((1,), (1,)), ((), ()))
    q_ref, k_ref, mo_ref = (qko_ref.at[:, j * MLSTM_WIDTH:(j + 1) * MLSTM_WIDTH] for j in range(3))

    @pl.when(pl.program_id(1) == 0)
    def _():
        c_state[...] = jnp.zeros_like(c_state)
        m_state[...] = jnp.zeros_like(m_state)

    key = lax.broadcasted_iota(jnp.int32, (L, L), 0)
    qry = lax.broadcasted_iota(jnp.int32, (L, L), 1)
    visible = key <= qry
    ones_rows = jnp.ones((NORM_ROWS, L), BF16)

    units = [(cc, h) for cc in range(TM_MLSTM // L) for h in range(MLSTM_HEADS)]
    rows = lambda cc: slice(cc * L, (cc + 1) * L)
    cols = lambda h: slice(h * D, (h + 1) * D)

    s_t = {u: lax.dot_general(k_ref[rows(u[0]), cols(u[1])], q_ref[rows(u[0]), cols(u[1])], nt_dims,
                              preferred_element_type=F32) for u in units}
    b_row, g_tot, pm, vt_aug, intra, m_loc, kv = {}, {}, {}, {}, {}, {}, {}
    for u in units:
        cc, h = u
        fh = MLSTM_HEADS + h
        r_col = g_ref[rows(cc), h:h + 1] - g_ref[rows(cc), fh:fh + 1]
        b_row[u] = gt_ref[cc, fh:fh + 1, :]
        g_tot[u] = b_row[u][:, L - 1:L]
        r_vis = jnp.where(visible, r_col, NEG)
        pm[u] = jnp.max(r_vis, axis=0, keepdims=True)
        p_t = (jnp.exp(r_vis - pm[u]) * s_t[u]).astype(BF16)
        vt_aug[u] = jnp.concatenate([vt_ref[cc, cols(h), :], ones_rows], axis=0)
        intra[u] = jnp.dot(vt_aug[u], p_t, preferred_element_type=F32)
    for u in units:
        cc, h = u
        r_row = gt_ref[cc, h:h + 1, :] - b_row[u]
        m_loc[u] = jnp.max(g_tot[u] + r_row, axis=-1, keepdims=True)
        vw = (vt_aug[u].astype(F32) * jnp.exp(g_tot[u] + r_row - m_loc[u])).astype(BF16)
        kv[u] = jnp.dot(vw, k_ref[rows(cc), cols(h)], preferred_element_type=F32)

    c_aug = [c_state[h] for h in range(MLSTM_HEADS)]
    m_prev = [m_state[h:h + 1, 0:1] for h in range(MLSTM_HEADS)]
    for u in units:
        cc, h = u
        inter = lax.dot_general(c_aug[h].astype(BF16), q_ref[rows(cc), cols(h)], nt_dims,
                                preferred_element_type=F32)
        mm = jnp.maximum(pm[u], m_prev[h])
        nd = jnp.exp(pm[u] - mm) * intra[u] + jnp.exp(m_prev[h] - mm) * inter
        inv = 1.0 / jnp.maximum(jnp.abs(nd[D:D + 1, :]), jnp.exp(-(b_row[u] + mm)))
        hh = nd[0:D, :] * inv
        hn = (hh * lax.rsqrt(jnp.mean(hh * hh, axis=0, keepdims=True) + NORM_EPS)).T
        o_gate = jax.nn.sigmoid(mo_ref[rows(cc), cols(h)].astype(F32))
        y_ref[rows(cc), cols(h)] = (o_gate * hn * nw_ref[:, cols(h)]).astype(y_ref.dtype)

        m_new = jnp.maximum(g_tot[u] + m_prev[h], m_loc[u])
        c_aug[h] = jnp.exp(g_tot[u] + m_prev[h] - m_new) * c_aug[h] + jnp.exp(m_loc[u] - m_new) * kv[u]
        m_prev[h] = m_new
    for h in range(MLSTM_HEADS):
        c_state[h] = c_aug[h]
        m_state[h:h + 1, :] = jnp.broadcast_to(m_prev[h], (1, LANES))


def _mlstm(mqko, mvt, gates, gates_t, norm_w, batch):
    T = mqko.shape[0]
    nt = T // batch // TM_MLSTM
    nch = TM_MLSTM // MCHUNK
    tile = lambda w: pl.BlockSpec((TM_MLSTM, w), lambda b, c: (b * nt + c, 0))
    per_chunk = lambda r: pl.BlockSpec((nch, r, MCHUNK), lambda b, c: (b * nt + c, 0, 0))
    return pl.pallas_call(
        _mlstm_kernel,
        grid=(batch, nt),
        in_specs=[tile(3 * MLSTM_WIDTH), per_chunk(MLSTM_WIDTH), tile(LANES),
                  per_chunk(GATE_ROWS), pl.BlockSpec((1, MLSTM_WIDTH), lambda b, c: (0, 0))],
        out_specs=tile(MLSTM_WIDTH),
        out_shape=jax.ShapeDtypeStruct((T, MLSTM_WIDTH), BF16),
        scratch_shapes=[
            pltpu.VMEM((MLSTM_HEADS, MLSTM_HEAD_DIM + NORM_ROWS, MLSTM_HEAD_DIM), F32),
            pltpu.VMEM((8, LANES), F32),
        ],
        compiler_params=pltpu.CompilerParams(dimension_semantics=("parallel", "arbitrary"),
                                             vmem_limit_bytes=VMEM_LIMIT),
        name="mlstm",
    )(mqko, mvt, gates, gates_t, norm_w)


def _post_kernel(x_ref, ya_ref, yb_ref, gab_ref, wpa_ref, wpm_ref, wo_ref, nw_ref, w1_ref, w2_ref,
                 fw_ref, *rest, final_norm):
    n_cast = len(rest) // 2
    o_ref = rest[n_cast]
    for src, dst in zip(rest[:n_cast], rest[n_cast + 1:]):
        dst[...] = src[...].astype(BF16)
    ga_ref, gb_ref = (gab_ref.at[:, j * D_MODEL:(j + 1) * D_MODEL] for j in range(2))
    pa = jnp.dot(ya_ref[...], wpa_ref[...], preferred_element_type=F32)
    pb = jnp.dot(yb_ref[...], wpm_ref[...], preferred_element_type=F32)
    mixed = (jax.nn.sigmoid(ga_ref[...].astype(F32)) * pa
             + jax.nn.sigmoid(gb_ref[...].astype(F32)) * pb).astype(BF16)
    x1 = x_ref[...] + jnp.dot(mixed, wo_ref[...], preferred_element_type=F32)
    var = jnp.mean(x1 * x1, axis=-1, keepdims=True)
    h2 = (x1 * lax.rsqrt(var + NORM_EPS) * nw_ref[...]).astype(BF16)
    acc = x1
    ff_chunk = D_MODEL
    for c in range(D_FF // ff_chunk):
        cs = slice(c * ff_chunk, (c + 1) * ff_chunk)
        u = jnp.maximum(jnp.dot(h2, w1_ref[:, cs], preferred_element_type=F32), 0.0)
        acc = acc + jnp.dot((u * u).astype(BF16), w2_ref[cs, :], preferred_element_type=F32)
    if final_norm:
        var = jnp.mean(acc * acc, axis=-1, keepdims=True)
        acc = acc * lax.rsqrt(var + NORM_EPS) * fw_ref[...]
    o_ref[...] = acc


def _post(x2d, ya, yb, gab, weights, norm_w, final_w, layer, final_norm, next_weights):
    T = x2d.shape[0]
    steps = T // TM_POST
    tile = lambda w: pl.BlockSpec((TM_POST, w), lambda i: (i, 0))
    wpa, wpm, wo, w1, w2 = weights
    cast_in, cast_out, cast_shape = _cast_specs(next_weights, layer + 1, steps)
    out = pl.pallas_call(
        functools.partial(_post_kernel, final_norm=final_norm),
        grid=(steps,),
        in_specs=[tile(D_MODEL), tile(ATT_WIDTH), tile(MLSTM_WIDTH), tile(2 * D_MODEL),
                  _const_spec(wpa.shape), _const_spec(wpm.shape), _const_spec(wo.shape),
                  _layer_spec((1, D_MODEL), layer), _const_spec(w1.shape), _const_spec(w2.shape),
                  _const_spec((1, D_MODEL))] + cast_in,
        out_specs=[tile(D_MODEL)] + cast_out,
        out_shape=[jax.ShapeDtypeStruct((T, D_MODEL), F32)] + cast_shape,
        compiler_params=pltpu.CompilerParams(dimension_semantics=("arbitrary",),
                                             vmem_limit_bytes=VMEM_LIMIT),
        name="post",
    )(x2d, ya, yb, gab, wpa, wpm, wo, norm_w, w1, w2, final_w, *next_weights)
    return out[0], tuple(out[1:])


PREP_STEPS = 8


PREP_IN_ROWS = 640


def _cast_in_proj_kernel(w_ref, o_ref):
    row = pl.program_id(0) * PREP_IN_ROWS + lax.broadcasted_iota(jnp.int32, w_ref.shape[1:], 0)
    for layer in range(w_ref.shape[0]):
        w = w_ref[layer]
        w = jnp.where(row < COL_AK, w * (ATT_HEAD_DIM ** -0.5), w)
        o_ref[layer] = jnp.where(row < D_IN, w, 0.0).T.astype(BF16)


def _cast_in_proj(w_t):
    layers, _, feat = w_t.shape
    assert W_PAD % PREP_IN_ROWS == 0 and PREP_IN_ROWS % LANES == 0
    return pl.pallas_call(
        _cast_in_proj_kernel,
        grid=(W_PAD // PREP_IN_ROWS,),
        in_specs=[pl.BlockSpec((layers, PREP_IN_ROWS, feat), lambda i: (0, i, 0))],
        out_specs=pl.BlockSpec((layers, feat, PREP_IN_ROWS), lambda i: (0, 0, i)),
        out_shape=jax.ShapeDtypeStruct((layers, feat, W_PAD), BF16),
        compiler_params=pltpu.CompilerParams(dimension_semantics=("parallel",), vmem_limit_bytes=VMEM_LIMIT),
        name="cast_in_proj",
    )(w_t)


def _cast_weights_kernel(*refs):
    n = len(refs) // 2
    for src, dst in zip(refs[:n], refs[n:]):
        dst[...] = src[...].astype(BF16)


def _cast_specs(weights, layer, steps):
    for a in weights:
        assert a.shape[1] % (steps * 16) == 0
    in_specs = [pl.BlockSpec((None, a.shape[1] // steps, a.shape[2]), lambda i: (layer, i, 0)) for a in weights]
    out_specs = [pl.BlockSpec((a.shape[1] // steps, a.shape[2]), lambda i: (i, 0)) for a in weights]
    out_shape = [jax.ShapeDtypeStruct(a.shape[1:], BF16) for a in weights]
    return in_specs, out_specs, out_shape


def _cast_weights(weights, layer):
    in_specs, out_specs, out_shape = _cast_specs(weights, layer, PREP_STEPS)
    return pl.pallas_call(
        _cast_weights_kernel,
        grid=(PREP_STEPS,),
        in_specs=in_specs,
        out_specs=out_specs,
        out_shape=out_shape,
        compiler_params=pltpu.CompilerParams(dimension_semantics=("parallel",), vmem_limit_bytes=VMEM_LIMIT),
        name="cast_weights",
    )(*weights)


def kernel(x, positions, norm_mix_w, w_in, b_in, conv_w, conv_b, mlstm_norm_w, w_proj_att, w_proj_mlstm,
           w_out, norm_mlp_w, w_ff1, w_ff2, final_norm_w):
    B, S, D = x.shape
    T = B * S
    depth = w_in.shape[0]
    assert D == D_MODEL and S % ATT_BLK == 0 and T % TM_IN == 0 and T % TM_POST == 0
    assert math.isclose(ATT_HEAD_DIM ** -0.5, 0.125)
    rope_c, rope_s = _rope_tables(positions)
    w_all = _cast_in_proj(jnp.swapaxes(w_in, 1, 2))
    stacked = (w_proj_att, w_proj_mlstm, w_out, w_ff1, w_ff2)
    weights = _cast_weights(stacked, 0)
    q_scale = jnp.where(jnp.arange(W_PAD) < COL_AK, ATT_HEAD_DIM ** -0.5, 1.0).astype(F32)
    b_all = (jnp.pad(b_in, ((0, 0), (0, W_PAD - D_IN))) * q_scale).reshape(depth, 1, W_PAD)
    x2d = x.reshape(T, D)
    final_w = final_norm_w.reshape(1, D).astype(F32)
    for l in range(depth):
        qkv, mqko, mvt, gab, gates, gates_t = _inproj(
            x2d, norm_mix_w.reshape(depth, 1, D), w_all, b_all, rope_c, rope_s,
            conv_w, conv_b.reshape(depth, 1, -1), S, l)
        ya = _attention(qkv.reshape(B, S, 3 * ATT_WIDTH))
        yb = _mlstm(mqko, mvt, gates, gates_t, mlstm_norm_w[l].reshape(1, -1), B)
        last = l == depth - 1
        x2d, weights = _post(x2d, ya.reshape(T, ATT_WIDTH), yb, gab, weights, norm_mlp_w.reshape(depth, 1, D),
                             final_w, l, final_norm=last, next_weights=() if last else stacked)
    return x2d.reshape(B, S, D)
```

```python
import functools
import math

import jax
import jax.numpy as jnp
from jax import lax
from jax.experimental import pallas as pl
from jax.experimental.pallas import tpu as pltpu

F32 = jnp.float32
BF16 = jnp.bfloat16

D_MODEL = 1024
ATT_HEADS = 8
ATT_HEAD_DIM = 64
ATT_WIDTH = ATT_HEADS * ATT_HEAD_DIM
ATT_SPAN = 128
DILATIONS = (1, 4, 16)
ROPE_THETA = 500000.0
ROPE_DIM = ATT_HEAD_DIM // 4
MLSTM_HEADS = 4
MLSTM_HEAD_DIM = 128
MLSTM_WIDTH = MLSTM_HEADS * MLSTM_HEAD_DIM
CONV_WIDTH = 4
D_FF = 4 * D_MODEL
NORM_EPS = 1e-6

COL_AQ = 0
COL_AK = COL_AQ + ATT_WIDTH
COL_AV = COL_AK + ATT_WIDTH
COL_MQ = COL_AV + ATT_WIDTH
COL_MK = COL_MQ + MLSTM_WIDTH
COL_MV = COL_MK + MLSTM_WIDTH
COL_MO = COL_MV + MLSTM_WIDTH
COL_MI = COL_MO + MLSTM_WIDTH
COL_MF = COL_MI + MLSTM_HEADS
COL_GA = COL_MF + MLSTM_HEADS
COL_GB = COL_GA + D_MODEL
D_IN = COL_GB + D_MODEL

LANES = 128
assert COL_MI % LANES == 0
MXU_COLS = 256
W_PAD = -(-D_IN // MXU_COLS) * MXU_COLS
GATE_SHIFT = 2 * MLSTM_HEADS

TM_IN = 512
TM_POST = 512
ATT_BLK = 2048
MCHUNK = 128
TM_MLSTM = 1024
VMEM_LIMIT = 56 * 1024 * 1024
NEG = -1e30
LOG2E = math.log2(math.e)


def _const_spec(shape):
    nd = len(shape)
    return pl.BlockSpec(shape, lambda *_: (0,) * nd, pipeline_mode=pl.Buffered(1))


ROPE_HALF = ROPE_DIM // 2
ROPE_PACK = LANES // ROPE_HALF
ROPE_ROWS = 128


def _rope_table_kernel(pos_ref, invf_ref, c_ref, s_ref):
    ang = pos_ref[...] * invf_ref[...]
    cosx = jnp.cos(ang)
    sinx = jnp.sin(ang)
    dst = lax.broadcasted_iota(jnp.int32, (ROPE_ROWS, LANES), 1)
    in_head = dst % ATT_HEAD_DIM
    rotary = in_head < ROPE_DIM
    sign = jnp.where(in_head < ROPE_HALF, -1.0, 1.0)
    for r in range(ROPE_PACK):
        src = r * ROPE_HALF + dst % ROPE_HALF
        c_ref[pl.ds(r, ROPE_ROWS, stride=ROPE_PACK), :] = jnp.where(
            rotary, jnp.take_along_axis(cosx, src, axis=1), 1.0)
        s_ref[pl.ds(r, ROPE_ROWS, stride=ROPE_PACK), :] = jnp.where(
            rotary, jnp.take_along_axis(sinx, src, axis=1) * sign, 0.0)


def _rope_tables(positions):
    T = positions.size
    assert T % (ROPE_PACK * ROPE_ROWS) == 0
    pos = jnp.repeat(positions.astype(F32).reshape(T // ROPE_PACK, ROPE_PACK), ROPE_HALF, axis=1)
    inv_freq = ROPE_THETA ** (-jnp.arange(0, ROPE_DIM, 2, dtype=F32) / ROPE_DIM)
    invf = jnp.tile(inv_freq, ROPE_PACK).reshape(1, LANES)
    table = jax.ShapeDtypeStruct((T, LANES), F32)
    out_spec = pl.BlockSpec((ROPE_PACK * ROPE_ROWS, LANES), lambda i: (i, 0))
    return pl.pallas_call(
        _rope_table_kernel,
        grid=(T // (ROPE_PACK * ROPE_ROWS),),
        in_specs=[pl.BlockSpec((ROPE_ROWS, LANES), lambda i: (i, 0)), pl.BlockSpec((1, LANES), lambda i: (0, 0))],
        out_specs=(out_spec, out_spec),
        out_shape=(table, table),
        compiler_params=pltpu.CompilerParams(dimension_semantics=("parallel",)),
        name="rope_tables",
    )(pos, invf)


CONV_TAIL = 8
GATE_ROWS = 8


def _log_sigmoid(x):
    return jnp.minimum(x, 0.0) - jnp.log(1.0 + jnp.exp(-jnp.abs(x)))


def _inproj_kernel(x_ref, nw_ref, w_ref, b_ref, c_ref, s_ref, cw_ref, cb_ref,
                   qkv_ref, mqko_ref, mv_ref, gab_ref, gt_ref, gtt_ref,
                   u_s, res_s, *, tiles_per_seq):
    aq_ref, ak_ref, av_ref = (qkv_ref.at[:, j * ATT_WIDTH:(j + 1) * ATT_WIDTH] for j in range(3))
    mq_ref, mk_ref, mo_ref = (mqko_ref.at[:, j * MLSTM_WIDTH:(j + 1) * MLSTM_WIDTH] for j in range(3))
    ga_ref, gb_ref = (gab_ref.at[:, j * D_MODEL:(j + 1) * D_MODEL] for j in range(2))
    @pl.when(pl.program_id(0) % tiles_per_seq == 0)
    def _():
        u_s[:, TM_IN:TM_IN + CONV_TAIL, :] = jnp.zeros((u_s.shape[0], CONV_TAIL, LANES), F32)

    x = x_ref[...]
    var = jnp.mean(x * x, axis=-1, keepdims=True)
    h = (x * lax.rsqrt(var + NORM_EPS) * nw_ref[...]).astype(BF16)

    def proj(lo, width):
        return jnp.dot(h, w_ref[:, lo:lo + width], preferred_element_type=F32) + b_ref[:, lo:lo + width]

    cos = c_ref[...]
    sin = s_ref[...]
    lane = lax.broadcasted_iota(jnp.int32, cos.shape, 1)
    first_half = (lane % ATT_HEAD_DIM) < (ROPE_DIM // 2)

    def rope_store(dst_ref, lo, scale=None):
        zz = proj(lo, ATT_WIDTH)
        for j in range(ATT_WIDTH // LANES):
            z = zz[:, j * LANES:(j + 1) * LANES]
            partner = jnp.where(first_half,
                                pltpu.roll(z, LANES - ROPE_DIM // 2, axis=1),
                                pltpu.roll(z, ROPE_DIM // 2, axis=1))
            out = z * cos + partner * sin
            dst_ref[:, j * LANES:(j + 1) * LANES] = out if scale is None else out * scale

    def conv_silu_store(dst_ref, slab0, lo, col0, scale):
        z = proj(lo, MLSTM_WIDTH)
        half = TM_IN // 2
        for j in range(MLSTM_WIDTH // LANES):
            sl = slab0 + j
            ws = slice(col0 + j * LANES, col0 + (j + 1) * LANES)
            u_s[sl, 0:CONV_TAIL, :] = u_s[sl, TM_IN:TM_IN + CONV_TAIL, :]
            u_s[sl, CONV_TAIL:CONV_TAIL + TM_IN, :] = z[:, j * LANES:(j + 1) * LANES]
            for parity in range(2):
                out = cb_ref[:, ws]
                for t in range(CONV_WIDTH):
                    r0 = CONV_TAIL - (CONV_WIDTH - 1) + t + parity
                    out = out + cw_ref[t:t + 1, ws] * u_s[sl, pl.ds(r0, half, stride=2), :]
                out = out * jax.nn.sigmoid(out)
                res_s[sl, pl.ds(parity, half, stride=2), :] = out if scale is None else out * scale
            dst_ref[:, j * LANES:(j + 1) * LANES] = res_s[sl].astype(dst_ref.dtype)

    rope_store(aq_ref, COL_AQ, LOG2E)
    rope_store(ak_ref, COL_AK)
    av_ref[...] = proj(COL_AV, ATT_WIDTH)
    conv_silu_store(mq_ref, 0, COL_MQ, 0, MLSTM_HEAD_DIM ** -0.5)
    conv_silu_store(mk_ref, MLSTM_WIDTH // LANES, COL_MK, MLSTM_WIDTH, None)
    z_mv = proj(COL_MV, MLSTM_WIDTH)
    for cc in range(TM_IN // MCHUNK):
        for hd in range(MLSTM_HEADS):
            blk = z_mv[cc * MCHUNK:(cc + 1) * MCHUNK, hd * MLSTM_HEAD_DIM:(hd + 1) * MLSTM_HEAD_DIM]
            mv_ref[cc, hd * MLSTM_HEAD_DIM:(hd + 1) * MLSTM_HEAD_DIM, :] = blk.T.astype(mv_ref.dtype)
    mo_ref[...] = proj(COL_MO, MLSTM_WIDTH).astype(mo_ref.dtype)

    z_tail = proj(COL_MI, W_PAD - COL_MI)
    tiles = [z_tail[:, j * LANES:(j + 1) * LANES] for j in range((W_PAD - COL_MI) // LANES)]
    rolled = [pltpu.roll(t, LANES - GATE_SHIFT, axis=1) for t in tiles]
    low_lanes = lane < LANES - GATE_SHIFT
    for j in range(D_MODEL // LANES):
        ga_ref[:, j * LANES:(j + 1) * LANES] = jnp.where(low_lanes, rolled[j], rolled[j + 1]).astype(ga_ref.dtype)
        k = j + D_MODEL // LANES
        gb_ref[:, j * LANES:(j + 1) * LANES] = jnp.where(low_lanes, rolled[k], rolled[k + 1]).astype(gb_ref.dtype)

    zg = jnp.where(lane < GATE_SHIFT, tiles[0], 0.0)
    logf = _log_sigmoid(zg)
    ri = lax.broadcasted_iota(jnp.int32, (MCHUNK, MCHUNK), 0)
    ci = lax.broadcasted_iota(jnp.int32, (MCHUNK, MCHUNK), 1)
    tri = (ci <= ri).astype(F32)
    is_input_gate = lax.broadcasted_iota(jnp.int32, (MCHUNK, LANES), 1) < MLSTM_HEADS
    for cc in range(TM_IN // MCHUNK):
        rows = slice(cc * MCHUNK, (cc + 1) * MCHUNK)
        bcum = jnp.dot(tri, logf[rows], precision=lax.Precision.HIGHEST, preferred_element_type=F32)
        gc = jnp.where(is_input_gate, zg[rows], bcum)
        gt_ref[rows, :] = gc
        gtt_ref[cc] = gc.T[0:GATE_ROWS, :]


def _layer_spec(shape, layer):
    nd = len(shape)
    return pl.BlockSpec((None,) + tuple(shape), lambda *_: (layer,) + (0,) * nd, pipeline_mode=pl.Buffered(1))


def _inproj(x2d, norm_w, w_all, b_all, rope_c, rope_s, conv_w, conv_b, seq_len, layer):
    T = x2d.shape[0]
    tile = lambda w: pl.BlockSpec((TM_IN, w), lambda i: (i, 0))
    out_shapes = (
        jax.ShapeDtypeStruct((T, 3 * ATT_WIDTH), F32),
        jax.ShapeDtypeStruct((T, 3 * MLSTM_WIDTH), BF16),
        jax.ShapeDtypeStruct((T // MCHUNK, MLSTM_WIDTH, MCHUNK), BF16),
        jax.ShapeDtypeStruct((T, 2 * D_MODEL), BF16),
        jax.ShapeDtypeStruct((T, LANES), F32),
        jax.ShapeDtypeStruct((T // MCHUNK, GATE_ROWS, MCHUNK), F32),
    )
    per_chunk = lambda rows: pl.BlockSpec((TM_IN // MCHUNK, rows, MCHUNK), lambda i: (i, 0, 0))
    out_specs = tuple(tile(s.shape[1]) if len(s.shape) == 2 else per_chunk(s.shape[1]) for s in out_shapes)
    assert seq_len % TM_IN == 0
    return pl.pallas_call(
        functools.partial(_inproj_kernel, tiles_per_seq=seq_len // TM_IN),
        grid=(T // TM_IN,),
        in_specs=[tile(D_MODEL), _layer_spec((1, D_MODEL), layer), _const_spec((D_MODEL, W_PAD)),
                  _layer_spec((1, W_PAD), layer), tile(LANES), tile(LANES),
                  _layer_spec((CONV_WIDTH, 2 * MLSTM_WIDTH), layer), _layer_spec((1, 2 * MLSTM_WIDTH), layer)],
        out_specs=out_specs,
        out_shape=out_shapes,
        scratch_shapes=[pltpu.VMEM((2 * MLSTM_WIDTH // LANES, TM_IN + CONV_TAIL, LANES), F32),
                        pltpu.VMEM((2 * MLSTM_WIDTH // LANES, TM_IN, LANES), F32)],
        compiler_params=pltpu.CompilerParams(dimension_semantics=("arbitrary",),
                                             vmem_limit_bytes=VMEM_LIMIT),
        name="inproj",
    )(x2d, norm_w, w_all, b_all, rope_c, rope_s, conv_w, conv_b)


ATT_UNROLL = 16
PITCH16 = ATT_SPAN + 8


def _attn_kernel(q_ref, k_ref, v_ref, o_ref,
                 kd1, vd1, qd4, kd4, vd4, qd16, kd16, vd16, tmp, st4_s, st16_s, bias_s):
    j = pl.program_id(2)
    blk = ATT_SPAN
    nsub = {d: ATT_BLK // d // blk for d in DILATIONS}
    kv_bufs = ((1, kd1, vd1), (4, kd4, vd4), (16, kd16, vd16))

    row = lax.broadcasted_iota(jnp.int32, (2 * blk, 2 * blk), 0) % blk
    col = lax.broadcasted_iota(jnp.int32, (2 * blk, 2 * blk), 1)
    band = (col >= row) & (col <= row + ATT_SPAN)
    bias_s[0] = jnp.where(band, 0.0, NEG)
    bias_s[1] = jnp.where(band & (col >= blk), 0.0, NEG)

    @pl.when(j == 0)
    def _():
        for d, kd, vd in kv_bufs:
            n = ATT_BLK // d
            for r in range(d):
                base = r * (n + blk)
                kd[base:base + blk] = jnp.zeros((blk, LANES), BF16)
                vd[base:base + blk] = jnp.zeros((blk, LANES), BF16)

    @pl.when(j != 0)
    def _():
        for d, kd, vd in kv_bufs:
            n = ATT_BLK // d
            for r in range(d):
                base = r * (n + blk)
                kd[base:base + blk] = kd[base + n:base + n + blk]
                vd[base:base + blk] = vd[base + n:base + n + blk]

    def deinterleave(src_ref, dst1, dst4, dst16, is_kv):
        pad = blk if is_kv else 0
        if dst1 is not None:
            dst1[blk:blk + ATT_BLK] = src_ref[0].astype(BF16)
        n4 = ATT_BLK // 4
        for r4 in range(4):
            t4 = src_ref[0, pl.ds(r4, n4, stride=4), :]
            tmp[r4] = t4
            o4 = r4 * (n4 + pad) + pad
            dst4[o4:o4 + n4] = t4.astype(BF16)
        n16 = ATT_BLK // 16
        for r4 in range(4):
            for rr in range(4):
                o16 = (4 * rr + r4) * (n16 + pad) + pad
                dst16[o16:o16 + n16] = tmp[r4, pl.ds(rr, n16, stride=4), :].astype(BF16)

    deinterleave(q_ref, None, qd4, qd16, False)
    deinterleave(k_ref, kd1, kd4, kd16, True)
    deinterleave(v_ref, vd1, vd4, vd16, True)

    head_a = lax.broadcasted_iota(jnp.int32, (blk, LANES), 1) < ATT_HEAD_DIM

    def unit(q2, k2, v2, bias):
        zero = jnp.zeros_like(q2)
        qs = jnp.concatenate([jnp.where(head_a, q2, zero), jnp.where(head_a, zero, q2)], axis=0)
        s = lax.dot_general(qs, k2, (((1,), (1,)), ((), ())), preferred_element_type=F32) + bias
        m = jnp.max(s, axis=-1, keepdims=True)
        p = jnp.exp2(s - m).astype(BF16)
        v_aug = jnp.concatenate([v2, jnp.ones_like(v2)], axis=1)
        pv = jnp.dot(p, v_aug, preferred_element_type=F32)
        acc = jnp.where(head_a, pv[:blk, :LANES], pv[blk:, :LANES])
        ll = jnp.where(head_a, pv[:blk, LANES:], pv[blk:, LANES:])
        mm = jnp.where(head_a, m[:blk], m[blk:])
        return acc, mm, ll

    first_blk = jnp.where(j == 0, 1, 0)

    def body16(u, carry):
        k0 = pl.multiple_of(u * (2 * blk), blk)
        bias = bias_s[first_blk]
        res = unit(qd16[pl.ds(pl.multiple_of(u * blk, blk), blk), :], kd16[pl.ds(k0, 2 * blk), :],
                   vd16[pl.ds(k0, 2 * blk), :], bias)
        row0 = pl.multiple_of(u * PITCH16, 8)
        for a, val in enumerate(res):
            st16_s[a, pl.ds(row0, blk), :] = val
        return carry

    def body4(u, carry):
        r = u // nsub[4]
        sb = u % nsub[4]
        k0 = pl.multiple_of(r * (ATT_BLK // 4 + blk) + sb * blk, blk)
        bias = bias_s[jnp.where(sb == 0, first_blk, 0)]
        res = unit(qd4[pl.ds(pl.multiple_of(u * blk, blk), blk), :], kd4[pl.ds(k0, 2 * blk), :],
                   vd4[pl.ds(k0, 2 * blk), :], bias)
        t0 = sb * (blk * 4) + r
        for a, val in enumerate(res):
            st4_s[a, pl.ds(t0, blk, stride=4), :] = val
        return carry

    lax.fori_loop(0, ATT_BLK // blk, body16, 0, unroll=ATT_UNROLL)
    lax.fori_loop(0, ATT_BLK // blk, body4, 0, unroll=ATT_UNROLL)

    def body1(u, carry):
        r0 = pl.multiple_of(u * blk, blk)
        sl = pl.ds(r0, blk)
        q2 = q_ref[0, sl, :].astype(BF16)
        bias = bias_s[jnp.where(u == 0, first_blk, 0)]
        acc2, m2, l2 = unit(q2, kd1[pl.ds(r0, 2 * blk), :], vd1[pl.ds(r0, 2 * blk), :], bias)
        per_res = blk // 16
        acc1, m1, l1 = (jnp.concatenate([st16_s[a, pl.ds(per_res * u + k, 16, stride=PITCH16), :]
                                         for k in range(per_res)], axis=0) for a in range(3))
        acc0, m0, l0 = st4_s[0, sl, :], st4_s[1, sl, :], st4_s[2, sl, :]
        mx = jnp.maximum(jnp.maximum(m0, m1), m2)
        w0, w1, w2 = jnp.exp2(m0 - mx), jnp.exp2(m1 - mx), jnp.exp2(m2 - mx)
        num = w0 * acc0 + w1 * acc1 + w2 * acc2
        den = w0 * l0 + w1 * l1 + w2 * l2
        o_ref[0, sl, :] = (num / den).astype(o_ref.dtype)
        return carry

    lax.fori_loop(0, nsub[1], body1, 0, unroll=ATT_UNROLL)


def _attention(qkv):
    B, S, _ = qkv.shape
    blk = ATT_SPAN
    pairs = ATT_WIDTH // LANES
    cur = pl.BlockSpec((1, ATT_BLK, LANES), lambda b, hp, j: (b, j, hp))
    part = lambda p: pl.BlockSpec((1, ATT_BLK, LANES), lambda b, hp, j: (b, j, p * pairs + hp))
    kv_rows = {d: d * (ATT_BLK // d + blk) for d in DILATIONS}
    scratch = [
        pltpu.VMEM((kv_rows[1], LANES), BF16), pltpu.VMEM((kv_rows[1], LANES), BF16),
        pltpu.VMEM((ATT_BLK, LANES), BF16),
        pltpu.VMEM((kv_rows[4], LANES), BF16), pltpu.VMEM((kv_rows[4], LANES), BF16),
        pltpu.VMEM((ATT_BLK, LANES), BF16),
        pltpu.VMEM((kv_rows[16], LANES), BF16), pltpu.VMEM((kv_rows[16], LANES), BF16),
        pltpu.VMEM((4, ATT_BLK // 4, LANES), F32),
        pltpu.VMEM((3, ATT_BLK, LANES), F32),
        pltpu.VMEM((3, 16 * PITCH16, LANES), F32),
        pltpu.VMEM((2, 2 * blk, 2 * blk), F32),
    ]
    return pl.pallas_call(
        _attn_kernel,
        grid=(B, pairs, S // ATT_BLK),
        in_specs=[part(0), part(1), part(2)],
        out_specs=cur,
        out_shape=jax.ShapeDtypeStruct((B, S, ATT_WIDTH), BF16),
        scratch_shapes=scratch,
        compiler_params=pltpu.CompilerParams(dimension_semantics=("parallel", "parallel", "arbitrary"),
                                             vmem_limit_bytes=VMEM_LIMIT),
        name="dilated_attention",
    )(qkv, qkv, qkv)


NORM_ROWS = 16


def _mlstm_kernel(qko_ref, vt_ref, g_ref, gt_ref, nw_ref, y_ref, c_state, m_state):
    L = MCHUNK
    D = MLSTM_HEAD_DIM
    nt_dims = (((1,), (1,)), ((), ()))
    q_ref, k_ref, mo_ref = (qko_ref.at[:, j * MLSTM_WIDTH:(j + 1) * MLSTM_WIDTH] for j in range(3))

    @pl.when(pl.program_id(1) == 0)
    def _():
        c_state[...] = jnp.zeros_like(c_state)
        m_state[...] = jnp.zeros_like(m_state)

    key = lax.broadcasted_iota(jnp.int32, (L, L), 0)
    qry = lax.broadcasted_iota(jnp.int32, (L, L), 1)
    visible = key <= qry
    ones_rows = jnp.ones((NORM_ROWS, L), BF16)

    units = [(cc, h) for cc in range(TM_MLSTM // L) for h in range(MLSTM_HEADS)]
    rows = lambda cc: slice(cc * L, (cc + 1) * L)
    cols = lambda h: slice(h * D, (h + 1) * D)

    s_t = {u: lax.dot_general(k_ref[rows(u[0]), cols(u[1])], q_ref[rows(u[0]), cols(u[1])], nt_dims,
                              preferred_element_type=F32) for u in units}
    b_row, g_tot, pm, vt_aug, intra, m_loc, kv = {}, {}, {}, {}, {}, {}, {}
    for u in units:
        cc, h = u
        fh = MLSTM_HEADS + h
        r_col = g_ref[rows(cc), h:h + 1] - g_ref[rows(cc), fh:fh + 1]
        b_row[u] = gt_ref[cc, fh:fh + 1, :]
        g_tot[u] = b_row[u][:, L - 1:L]
        r_vis = jnp.where(visible, r_col, NEG)
        pm[u] = jnp.max(r_vis, axis=0, keepdims=True)
        p_t = (jnp.exp(r_vis - pm[u]) * s_t[u]).astype(BF16)
        vt_aug[u] = jnp.concatenate([vt_ref[cc, cols(h), :], ones_rows], axis=0)
        intra[u] = jnp.dot(vt_aug[u], p_t, preferred_element_type=F32)
    for u in units:
        cc, h = u
        r_row = gt_ref[cc, h:h + 1, :] - b_row[u]
        m_loc[u] = jnp.max(g_tot[u] + r_row, axis=-1, keepdims=True)
        vw = (vt_aug[u].astype(F32) * jnp.exp(g_tot[u] + r_row - m_loc[u])).astype(BF16)
        kv[u] = jnp.dot(vw, k_ref[rows(cc), cols(h)], preferred_element_type=F32)

    c_aug = [c_state[h] for h in range(MLSTM_HEADS)]
    m_prev = [m_state[h:h + 1, 0:1] for h in range(MLSTM_HEADS)]
    for u in units:
        cc, h = u
        inter = lax.dot_general(c_aug[h].astype(BF16), q_ref[rows(cc), cols(h)], nt_dims,
                                preferred_element_type=F32)
        mm = jnp.maximum(pm[u], m_prev[h])
        nd = jnp.exp(pm[u] - mm) * intra[u] + jnp.exp(m_prev[h] - mm) * inter
        inv = 1.0 / jnp.maximum(jnp.abs(nd[D:D + 1, :]), jnp.exp(-(b_row[u] + mm)))
        hh = nd[0:D, :] * inv
        hn = (hh * lax.rsqrt(jnp.mean(hh * hh, axis=0, keepdims=True) + NORM_EPS)).T
        o_gate = jax.nn.sigmoid(mo_ref[rows(cc), cols(h)].astype(F32))
        y_ref[rows(cc), cols(h)] = (o_gate * hn * nw_ref[:, cols(h)]).astype(y_ref.dtype)

        m_new = jnp.maximum(g_tot[u] + m_prev[h], m_loc[u])
        c_aug[h] = jnp.exp(g_tot[u] + m_prev[h] - m_new) * c_aug[h] + jnp.exp(m_loc[u] - m_new) * kv[u]
        m_prev[h] = m_new
    for h in range(MLSTM_HEADS):
        c_state[h] = c_aug[h]
        m_state[h:h + 1, :] = jnp.broadcast_to(m_prev[h], (1, LANES))


def _mlstm(mqko, mvt, gates, gates_t, norm_w, batch):
    T = mqko.shape[0]
    nt = T // batch // TM_MLSTM
    nch = TM_MLSTM // MCHUNK
    tile = lambda w: pl.BlockSpec((TM_MLSTM, w), lambda b, c: (b * nt + c, 0))
    per_chunk = lambda r: pl.BlockSpec((nch, r, MCHUNK), lambda b, c: (b * nt + c, 0, 0))
    return pl.pallas_call(
        _mlstm_kernel,
        grid=(batch, nt),
        in_specs=[tile(3 * MLSTM_WIDTH), per_chunk(MLSTM_WIDTH), tile(LANES),
                  per_chunk(GATE_ROWS), pl.BlockSpec((1, MLSTM_WIDTH), lambda b, c: (0, 0))],
        out_specs=tile(MLSTM_WIDTH),
        out_shape=jax.ShapeDtypeStruct((T, MLSTM_WIDTH), BF16),
        scratch_shapes=[
            pltpu.VMEM((MLSTM_HEADS, MLSTM_HEAD_DIM + NORM_ROWS, MLSTM_HEAD_DIM), F32),
            pltpu.VMEM((8, LANES), F32),
        ],
        compiler_params=pltpu.CompilerParams(dimension_semantics=("parallel", "arbitrary"),
                                             vmem_limit_bytes=VMEM_LIMIT),
        name="mlstm",
    )(mqko, mvt, gates, gates_t, norm_w)


def _post_kernel(x_ref, ya_ref, yb_ref, gab_ref, wpa_ref, wpm_ref, wo_ref, nw_ref, w1_ref, w2_ref,
                 fw_ref, *rest, final_norm):
    n_cast = len(rest) // 2
    o_ref = rest[n_cast]
    for src, dst in zip(rest[:n_cast - 1], rest[n_cast + 1:-1]):
        dst[...] = src[...].astype(BF16)
    if n_cast:
        _cast_in_proj_slab(rest[n_cast - 1], rest[-1], jnp.minimum(pl.program_id(0), IN_SLABS - 1))
    ga_ref, gb_ref = (gab_ref.at[:, j * D_MODEL:(j + 1) * D_MODEL] for j in range(2))
    pa = jnp.dot(ya_ref[...], wpa_ref[...], preferred_element_type=F32)
    pb = jnp.dot(yb_ref[...], wpm_ref[...], preferred_element_type=F32)
    mixed = (jax.nn.sigmoid(ga_ref[...].astype(F32)) * pa
             + jax.nn.sigmoid(gb_ref[...].astype(F32)) * pb).astype(BF16)
    x1 = x_ref[...] + jnp.dot(mixed, wo_ref[...], preferred_element_type=F32)
    var = jnp.mean(x1 * x1, axis=-1, keepdims=True)
    h2 = (x1 * lax.rsqrt(var + NORM_EPS) * nw_ref[...]).astype(BF16)
    acc = x1
    ff_chunk = D_MODEL
    for c in range(D_FF // ff_chunk):
        cs = slice(c * ff_chunk, (c + 1) * ff_chunk)
        u = jnp.maximum(jnp.dot(h2, w1_ref[:, cs], preferred_element_type=F32), 0.0)
        acc = acc + jnp.dot((u * u).astype(BF16), w2_ref[cs, :], preferred_element_type=F32)
    if final_norm:
        var = jnp.mean(acc * acc, axis=-1, keepdims=True)
        acc = acc * lax.rsqrt(var + NORM_EPS) * fw_ref[...]
    o_ref[...] = acc


def _post(x2d, ya, yb, gab, weights, norm_w, final_w, layer, final_norm, next_weights):
    T = x2d.shape[0]
    steps = T // TM_POST
    tile = lambda w: pl.BlockSpec((TM_POST, w), lambda i: (i, 0))
    wpa, wpm, wo, w1, w2 = weights
    cast_in, cast_out, cast_shape = _cast_specs(next_weights[:-1], layer + 1, steps)
    if next_weights:
        assert steps >= IN_SLABS
        in_spec, out_spec, out_shape = _cast_in_proj_specs(layer + 1)
        cast_in, cast_out, cast_shape = cast_in + [in_spec], cast_out + [out_spec], cast_shape + [out_shape]
    out = pl.pallas_call(
        functools.partial(_post_kernel, final_norm=final_norm),
        grid=(steps,),
        in_specs=[tile(D_MODEL), tile(ATT_WIDTH), tile(MLSTM_WIDTH), tile(2 * D_MODEL),
                  _const_spec(wpa.shape), _const_spec(wpm.shape), _const_spec(wo.shape),
                  _layer_spec((1, D_MODEL), layer), _const_spec(w1.shape), _const_spec(w2.shape),
                  _const_spec((1, D_MODEL))] + cast_in,
        out_specs=[tile(D_MODEL)] + cast_out,
        out_shape=[jax.ShapeDtypeStruct((T, D_MODEL), F32)] + cast_shape,
        compiler_params=pltpu.CompilerParams(dimension_semantics=("arbitrary",),
                                             vmem_limit_bytes=VMEM_LIMIT),
        name="post",
    )(x2d, ya, yb, gab, wpa, wpm, wo, norm_w, w1, w2, final_w, *next_weights)
    return out[0], tuple(out[1:])


PREP_STEPS = 8


PREP_IN_ROWS = 256
IN_SLABS = W_PAD // PREP_IN_ROWS


def _cast_in_proj_slab(w_ref, o_ref, slab):
    row = slab * PREP_IN_ROWS + lax.broadcasted_iota(jnp.int32, w_ref.shape, 0)
    w = w_ref[...]
    w = jnp.where(row < COL_AK, w * (ATT_HEAD_DIM ** -0.5), w)
    o_ref[...] = jnp.where(row < D_IN, w, 0.0).T.astype(BF16)


def _cast_in_proj_specs(layer):
    slab = lambda i: jnp.minimum(i, IN_SLABS - 1)
    in_spec = pl.BlockSpec((None, PREP_IN_ROWS, D_MODEL), lambda i: (layer, slab(i), 0))
    out_spec = pl.BlockSpec((D_MODEL, PREP_IN_ROWS), lambda i: (0, slab(i)))
    return in_spec, out_spec, jax.ShapeDtypeStruct((D_MODEL, W_PAD), BF16)


def _cast_in_proj_kernel(w_ref, o_ref):
    _cast_in_proj_slab(w_ref, o_ref, pl.program_id(0))


def _cast_in_proj(w_t, layer):
    in_spec, out_spec, out_shape = _cast_in_proj_specs(layer)
    return pl.pallas_call(
        _cast_in_proj_kernel,
        grid=(IN_SLABS,),
        in_specs=[in_spec],
        out_specs=out_spec,
        out_shape=out_shape,
        compiler_params=pltpu.CompilerParams(dimension_semantics=("parallel",), vmem_limit_bytes=VMEM_LIMIT),
        name="cast_in_proj",
    )(w_t)


def _cast_weights_kernel(*refs):
    n = len(refs) // 2
    for src, dst in zip(refs[:n], refs[n:]):
        dst[...] = src[...].astype(BF16)


def _cast_specs(weights, layer, steps):
    for a in weights:
        assert a.shape[1] % (steps * 16) == 0
    in_specs = [pl.BlockSpec((None, a.shape[1] // steps, a.shape[2]), lambda i: (layer, i, 0)) for a in weights]
    out_specs = [pl.BlockSpec((a.shape[1] // steps, a.shape[2]), lambda i: (i, 0)) for a in weights]
    out_shape = [jax.ShapeDtypeStruct(a.shape[1:], BF16) for a in weights]
    return in_specs, out_specs, out_shape


def _cast_weights(weights, layer):
    in_specs, out_specs, out_shape = _cast_specs(weights, layer, PREP_STEPS)
    return pl.pallas_call(
        _cast_weights_kernel,
        grid=(PREP_STEPS,),
        in_specs=in_specs,
        out_specs=out_specs,
        out_shape=out_shape,
        compiler_params=pltpu.CompilerParams(dimension_semantics=("parallel",), vmem_limit_bytes=VMEM_LIMIT),
        name="cast_weights",
    )(*weights)


def kernel(x, positions, norm_mix_w, w_in, b_in, conv_w, conv_b, mlstm_norm_w, w_proj_att, w_proj_mlstm,
           w_out, norm_mlp_w, w_ff1, w_ff2, final_norm_w):
    B, S, D = x.shape
    T = B * S
    depth = w_in.shape[0]
    assert D == D_MODEL and S % ATT_BLK == 0 and T % TM_IN == 0 and T % TM_POST == 0
    assert math.isclose(ATT_HEAD_DIM ** -0.5, 0.125)
    rope_c, rope_s = _rope_tables(positions)
    w_in_t = jnp.swapaxes(w_in, 1, 2)
    stacked = (w_proj_att, w_proj_mlstm, w_out, w_ff1, w_ff2)
    w_all = _cast_in_proj(w_in_t, 0)
    weights = _cast_weights(stacked, 0)
    q_scale = jnp.where(jnp.arange(W_PAD) < COL_AK, ATT_HEAD_DIM ** -0.5, 1.0).astype(F32)
    b_all = (jnp.pad(b_in, ((0, 0), (0, W_PAD - D_IN))) * q_scale).reshape(depth, 1, W_PAD)
    x2d = x.reshape(T, D)
    final_w = final_norm_w.reshape(1, D).astype(F32)
    for l in range(depth):
        qkv, mqko, mvt, gab, gates, gates_t = _inproj(
            x2d, norm_mix_w.reshape(depth, 1, D), w_all, b_all, rope_c, rope_s,
            conv_w, conv_b.reshape(depth, 1, -1), S, l)
        ya = _attention(qkv.reshape(B, S, 3 * ATT_WIDTH))
        yb = _mlstm(mqko, mvt, gates, gates_t, mlstm_norm_w[l].reshape(1, -1), B)
        last = l == depth - 1
        x2d, cast = _post(x2d, ya.reshape(T, ATT_WIDTH), yb, gab, weights, norm_mlp_w.reshape(depth, 1, D),
                          final_w, l, final_norm=last, next_weights=() if last else stacked + (w_in_t,))
        if not last:
            weights, w_all = cast[:-1], cast[-1]
    return x2d.reshape(B, S, D)
```

```python
import functools
import math

import jax
import jax.numpy as jnp
from jax import lax
from jax.experimental import pallas as pl
from jax.experimental.pallas import tpu as pltpu

F32 = jnp.float32
BF16 = jnp.bfloat16

D_MODEL = 1024
ATT_HEADS = 8
ATT_HEAD_DIM = 64
ATT_WIDTH = ATT_HEADS * ATT_HEAD_DIM
ATT_SPAN = 128
DILATIONS = (1, 4, 16)
ROPE_THETA = 500000.0
ROPE_DIM = ATT_HEAD_DIM // 4
MLSTM_HEADS = 4
MLSTM_HEAD_DIM = 128
MLSTM_WIDTH = MLSTM_HEADS * MLSTM_HEAD_DIM
CONV_WIDTH = 4
D_FF = 4 * D_MODEL
NORM_EPS = 1e-6

COL_AQ = 0
COL_AK = COL_AQ + ATT_WIDTH
COL_AV = COL_AK + ATT_WIDTH
COL_MQ = COL_AV + ATT_WIDTH
COL_MK = COL_MQ + MLSTM_WIDTH
COL_MV = COL_MK + MLSTM_WIDTH
COL_MO = COL_MV + MLSTM_WIDTH
COL_MI = COL_MO + MLSTM_WIDTH
COL_MF = COL_MI + MLSTM_HEADS
COL_GA = COL_MF + MLSTM_HEADS
COL_GB = COL_GA + D_MODEL
D_IN = COL_GB + D_MODEL

LANES = 128
assert COL_MI % LANES == 0
MXU_COLS = 256
W_PAD = -(-D_IN // MXU_COLS) * MXU_COLS
GATE_SHIFT = 2 * MLSTM_HEADS

TM_IN = 512
TM_POST = 512
ATT_BLK = 2048
MCHUNK = 128
TM_MLSTM = 1024
VMEM_LIMIT = 56 * 1024 * 1024
NEG = -1e30
LOG2E = math.log2(math.e)


def _const_spec(shape):
    nd = len(shape)
    return pl.BlockSpec(shape, lambda *_: (0,) * nd, pipeline_mode=pl.Buffered(1))


ROPE_HALF = ROPE_DIM // 2
ROPE_PACK = LANES // ROPE_HALF
ROPE_ROWS = 128


def _rope_table_kernel(pos_ref, invf_ref, c_ref, s_ref):
    ang = pos_ref[...] * invf_ref[...]
    cosx = jnp.cos(ang)
    sinx = jnp.sin(ang)
    dst = lax.broadcasted_iota(jnp.int32, (ROPE_ROWS, LANES), 1)
    in_head = dst % ATT_HEAD_DIM
    rotary = in_head < ROPE_DIM
    sign = jnp.where(in_head < ROPE_HALF, -1.0, 1.0)
    for r in range(ROPE_PACK):
        src = r * ROPE_HALF + dst % ROPE_HALF
        c_ref[pl.ds(r, ROPE_ROWS, stride=ROPE_PACK), :] = jnp.where(
            rotary, jnp.take_along_axis(cosx, src, axis=1), 1.0)
        s_ref[pl.ds(r, ROPE_ROWS, stride=ROPE_PACK), :] = jnp.where(
            rotary, jnp.take_along_axis(sinx, src, axis=1) * sign, 0.0)


def _rope_tables(positions):
    T = positions.size
    assert T % (ROPE_PACK * ROPE_ROWS) == 0
    pos = jnp.repeat(positions.astype(F32).reshape(T // ROPE_PACK, ROPE_PACK), ROPE_HALF, axis=1)
    inv_freq = ROPE_THETA ** (-jnp.arange(0, ROPE_DIM, 2, dtype=F32) / ROPE_DIM)
    invf = jnp.tile(inv_freq, ROPE_PACK).reshape(1, LANES)
    table = jax.ShapeDtypeStruct((T, LANES), F32)
    out_spec = pl.BlockSpec((ROPE_PACK * ROPE_ROWS, LANES), lambda i: (i, 0))
    return pl.pallas_call(
        _rope_table_kernel,
        grid=(T // (ROPE_PACK * ROPE_ROWS),),
        in_specs=[pl.BlockSpec((ROPE_ROWS, LANES), lambda i: (i, 0)), pl.BlockSpec((1, LANES), lambda i: (0, 0))],
        out_specs=(out_spec, out_spec),
        out_shape=(table, table),
        compiler_params=pltpu.CompilerParams(dimension_semantics=("parallel",)),
        name="rope_tables",
    )(pos, invf)


CONV_TAIL = 8
GATE_ROWS = 8


def _log_sigmoid(x):
    return jnp.minimum(x, 0.0) - jnp.log(1.0 + jnp.exp(-jnp.abs(x)))


def _inproj_kernel(x_ref, nw_ref, w_ref, b_ref, c_ref, s_ref, cw_ref, cb_ref,
                   qkv_ref, mqko_ref, mv_ref, gab_ref, gt_ref, gtt_ref,
                   u_s, res_s, *, tiles_per_seq):
    aq_ref, ak_ref, av_ref = (qkv_ref.at[:, j * ATT_WIDTH:(j + 1) * ATT_WIDTH] for j in range(3))
    mq_ref, mk_ref, mo_ref = (mqko_ref.at[:, j * MLSTM_WIDTH:(j + 1) * MLSTM_WIDTH] for j in range(3))
    ga_ref, gb_ref = (gab_ref.at[:, j * D_MODEL:(j + 1) * D_MODEL] for j in range(2))
    @pl.when(pl.program_id(0) % tiles_per_seq == 0)
    def _():
        u_s[:, TM_IN:TM_IN + CONV_TAIL, :] = jnp.zeros((u_s.shape[0], CONV_TAIL, LANES), F32)

    x = x_ref[...]
    var = jnp.mean(x * x, axis=-1, keepdims=True)
    h = (x * lax.rsqrt(var + NORM_EPS) * nw_ref[...]).astype(BF16)

    def proj(lo, width):
        return jnp.dot(h, w_ref[:, lo:lo + width], preferred_element_type=F32) + b_ref[:, lo:lo + width]

    cos = c_ref[...]
    sin = s_ref[...]
    lane = lax.broadcasted_iota(jnp.int32, cos.shape, 1)
    first_half = (lane % ATT_HEAD_DIM) < (ROPE_DIM // 2)

    def rope_store(dst_ref, lo, scale=None):
        zz = proj(lo, ATT_WIDTH)
        for j in range(ATT_WIDTH // LANES):
            z = zz[:, j * LANES:(j + 1) * LANES]
            partner = jnp.where(first_half,
                                pltpu.roll(z, LANES - ROPE_DIM // 2, axis=1),
                                pltpu.roll(z, ROPE_DIM // 2, axis=1))
            out = z * cos + partner * sin
            dst_ref[:, j * LANES:(j + 1) * LANES] = out if scale is None else out * scale

    def conv_silu_store(dst_ref, slab0, lo, col0, scale):
        z = proj(lo, MLSTM_WIDTH)
        half = TM_IN // 2
        for j in range(MLSTM_WIDTH // LANES):
            sl = slab0 + j
            ws = slice(col0 + j * LANES, col0 + (j + 1) * LANES)
            u_s[sl, 0:CONV_TAIL, :] = u_s[sl, TM_IN:TM_IN + CONV_TAIL, :]
            u_s[sl, CONV_TAIL:CONV_TAIL + TM_IN, :] = z[:, j * LANES:(j + 1) * LANES]
            for parity in range(2):
                out = cb_ref[:, ws]
                for t in range(CONV_WIDTH):
                    r0 = CONV_TAIL - (CONV_WIDTH - 1) + t + parity
                    out = out + cw_ref[t:t + 1, ws] * u_s[sl, pl.ds(r0, half, stride=2), :]
                out = out * jax.nn.sigmoid(out)
                res_s[sl, pl.ds(parity, half, stride=2), :] = out if scale is None else out * scale
            dst_ref[:, j * LANES:(j + 1) * LANES] = res_s[sl].astype(dst_ref.dtype)

    rope_store(aq_ref, COL_AQ, LOG2E)
    rope_store(ak_ref, COL_AK)
    av_ref[...] = proj(COL_AV, ATT_WIDTH)
    conv_silu_store(mq_ref, 0, COL_MQ, 0, MLSTM_HEAD_DIM ** -0.5)
    conv_silu_store(mk_ref, MLSTM_WIDTH // LANES, COL_MK, MLSTM_WIDTH, None)
    z_mv = proj(COL_MV, MLSTM_WIDTH)
    for cc in range(TM_IN // MCHUNK):
        for hd in range(MLSTM_HEADS):
            blk = z_mv[cc * MCHUNK:(cc + 1) * MCHUNK, hd * MLSTM_HEAD_DIM:(hd + 1) * MLSTM_HEAD_DIM]
            mv_ref[cc, hd * MLSTM_HEAD_DIM:(hd + 1) * MLSTM_HEAD_DIM, :] = blk.T.astype(mv_ref.dtype)
    mo_ref[...] = proj(COL_MO, MLSTM_WIDTH).astype(mo_ref.dtype)

    z_tail = proj(COL_MI, W_PAD - COL_MI)
    tiles = [z_tail[:, j * LANES:(j + 1) * LANES] for j in range((W_PAD - COL_MI) // LANES)]
    rolled = [pltpu.roll(t, LANES - GATE_SHIFT, axis=1) for t in tiles]
    low_lanes = lane < LANES - GATE_SHIFT
    for j in range(D_MODEL // LANES):
        ga_ref[:, j * LANES:(j + 1) * LANES] = jnp.where(low_lanes, rolled[j], rolled[j + 1]).astype(ga_ref.dtype)
        k = j + D_MODEL // LANES
        gb_ref[:, j * LANES:(j + 1) * LANES] = jnp.where(low_lanes, rolled[k], rolled[k + 1]).astype(gb_ref.dtype)

    zg = jnp.where(lane < GATE_SHIFT, tiles[0], 0.0)
    logf = _log_sigmoid(zg)
    ri = lax.broadcasted_iota(jnp.int32, (MCHUNK, MCHUNK), 0)
    ci = lax.broadcasted_iota(jnp.int32, (MCHUNK, MCHUNK), 1)
    tri = (ci <= ri).astype(F32)
    is_input_gate = lax.broadcasted_iota(jnp.int32, (MCHUNK, LANES), 1) < MLSTM_HEADS
    for cc in range(TM_IN // MCHUNK):
        rows = slice(cc * MCHUNK, (cc + 1) * MCHUNK)
        bcum = jnp.dot(tri, logf[rows], precision=lax.Precision.HIGHEST, preferred_element_type=F32)
        gc = jnp.where(is_input_gate, zg[rows], bcum)
        gt_ref[rows, :] = gc
        gtt_ref[cc] = gc.T[0:GATE_ROWS, :]


def _layer_spec(shape, layer):
    nd = len(shape)
    return pl.BlockSpec((None,) + tuple(shape), lambda *_: (layer,) + (0,) * nd, pipeline_mode=pl.Buffered(1))


def _inproj(x2d, norm_w, w_all, b_all, rope_c, rope_s, conv_w, conv_b, seq_len, layer):
    T = x2d.shape[0]
    tile = lambda w: pl.BlockSpec((TM_IN, w), lambda i: (i, 0))
    out_shapes = (
        jax.ShapeDtypeStruct((T, 3 * ATT_WIDTH), F32),
        jax.ShapeDtypeStruct((T, 3 * MLSTM_WIDTH), BF16),
        jax.ShapeDtypeStruct((T // MCHUNK, MLSTM_WIDTH, MCHUNK), BF16),
        jax.ShapeDtypeStruct((T, 2 * D_MODEL), BF16),
        jax.ShapeDtypeStruct((T, LANES), F32),
        jax.ShapeDtypeStruct((T // MCHUNK, GATE_ROWS, MCHUNK), F32),
    )
    per_chunk = lambda rows: pl.BlockSpec((TM_IN // MCHUNK, rows, MCHUNK), lambda i: (i, 0, 0))
    out_specs = tuple(tile(s.shape[1]) if len(s.shape) == 2 else per_chunk(s.shape[1]) for s in out_shapes)
    assert seq_len % TM_IN == 0
    return pl.pallas_call(
        functools.partial(_inproj_kernel, tiles_per_seq=seq_len // TM_IN),
        grid=(T // TM_IN,),
        in_specs=[tile(D_MODEL), _layer_spec((1, D_MODEL), layer), _const_spec((D_MODEL, W_PAD)),
                  _layer_spec((1, W_PAD), layer), tile(LANES), tile(LANES),
                  _layer_spec((CONV_WIDTH, 2 * MLSTM_WIDTH), layer), _layer_spec((1, 2 * MLSTM_WIDTH), layer)],
        out_specs=out_specs,
        out_shape=out_shapes,
        scratch_shapes=[pltpu.VMEM((2 * MLSTM_WIDTH // LANES, TM_IN + CONV_TAIL, LANES), F32),
                        pltpu.VMEM((2 * MLSTM_WIDTH // LANES, TM_IN, LANES), F32)],
        compiler_params=pltpu.CompilerParams(dimension_semantics=("arbitrary",),
                                             vmem_limit_bytes=VMEM_LIMIT),
        name="inproj",
    )(x2d, norm_w, w_all, b_all, rope_c, rope_s, conv_w, conv_b)


ATT_UNROLL = 16
PITCH16 = ATT_SPAN + 8


def _attn_kernel(q_ref, k_ref, v_ref, o_ref,
                 kd1, vd1, qd4, kd4, vd4, qd16, kd16, vd16, tmp, st4_s, st16_s, bias_s):
    j = pl.program_id(2)
    blk = ATT_SPAN
    nsub = {d: ATT_BLK // d // blk for d in DILATIONS}
    kv_bufs = ((1, kd1, vd1), (4, kd4, vd4), (16, kd16, vd16))

    row = lax.broadcasted_iota(jnp.int32, (2 * blk, 2 * blk), 0) % blk
    col = lax.broadcasted_iota(jnp.int32, (2 * blk, 2 * blk), 1)
    band = (col >= row) & (col <= row + ATT_SPAN)
    bias_s[0] = jnp.where(band, 0.0, NEG)
    bias_s[1] = jnp.where(band & (col >= blk), 0.0, NEG)

    @pl.when(j == 0)
    def _():
        for d, kd, vd in kv_bufs:
            n = ATT_BLK // d
            for r in range(d):
                base = r * (n + blk)
                kd[base:base + blk] = jnp.zeros((blk, LANES), BF16)
                vd[base:base + blk] = jnp.zeros((blk, LANES), BF16)

    @pl.when(j != 0)
    def _():
        for d, kd, vd in kv_bufs:
            n = ATT_BLK // d
            for r in range(d):
                base = r * (n + blk)
                kd[base:base + blk] = kd[base + n:base + n + blk]
                vd[base:base + blk] = vd[base + n:base + n + blk]

    def deinterleave(src_ref, dst1, dst4, dst16, is_kv):
        pad = blk if is_kv else 0
        if dst1 is not None:
            dst1[blk:blk + ATT_BLK] = src_ref[0].astype(BF16)
        n4 = ATT_BLK // 4
        for r4 in range(4):
            t4 = src_ref[0, pl.ds(r4, n4, stride=4), :]
            tmp[r4] = t4
            o4 = r4 * (n4 + pad) + pad
            dst4[o4:o4 + n4] = t4.astype(BF16)
        n16 = ATT_BLK // 16
        for r4 in range(4):
            for rr in range(4):
                o16 = (4 * rr + r4) * (n16 + pad) + pad
                dst16[o16:o16 + n16] = tmp[r4, pl.ds(rr, n16, stride=4), :].astype(BF16)

    deinterleave(q_ref, None, qd4, qd16, False)
    deinterleave(k_ref, kd1, kd4, kd16, True)
    deinterleave(v_ref, vd1, vd4, vd16, True)

    head_a = lax.broadcasted_iota(jnp.int32, (blk, LANES), 1) < ATT_HEAD_DIM

    def unit(q2, k2, v2, bias):
        zero = jnp.zeros_like(q2)
        qs = jnp.concatenate([jnp.where(head_a, q2, zero), jnp.where(head_a, zero, q2)], axis=0)
        s = lax.dot_general(qs, k2, (((1,), (1,)), ((), ())), preferred_element_type=F32) + bias
        m = jnp.max(s, axis=-1, keepdims=True)
        p = jnp.exp2(s - m).astype(BF16)
        v_aug = jnp.concatenate([v2, jnp.ones_like(v2)], axis=1)
        pv = jnp.dot(p, v_aug, preferred_element_type=F32)
        acc = jnp.where(head_a, pv[:blk, :LANES], pv[blk:, :LANES])
        ll = jnp.where(head_a, pv[:blk, LANES:], pv[blk:, LANES:])
        mm = jnp.where(head_a, m[:blk], m[blk:])
        return acc, mm, ll

    first_blk = jnp.where(j == 0, 1, 0)

    def body16(u, carry):
        k0 = pl.multiple_of(u * (2 * blk), blk)
        bias = bias_s[first_blk]
        res = unit(qd16[pl.ds(pl.multiple_of(u * blk, blk), blk), :], kd16[pl.ds(k0, 2 * blk), :],
                   vd16[pl.ds(k0, 2 * blk), :], bias)
        row0 = pl.multiple_of(u * PITCH16, 8)
        for a, val in enumerate(res):
            st16_s[a, pl.ds(row0, blk), :] = val
        return carry

    def body4(u, carry):
        r = u // nsub[4]
        sb = u % nsub[4]
        k0 = pl.multiple_of(r * (ATT_BLK // 4 + blk) + sb * blk, blk)
        bias = bias_s[jnp.where(sb == 0, first_blk, 0)]
        res = unit(qd4[pl.ds(pl.multiple_of(u * blk, blk), blk), :], kd4[pl.ds(k0, 2 * blk), :],
                   vd4[pl.ds(k0, 2 * blk), :], bias)
        t0 = sb * (blk * 4) + r
        for a, val in enumerate(res):
            st4_s[a, pl.ds(t0, blk, stride=4), :] = val
        return carry

    lax.fori_loop(0, ATT_BLK // blk, body16, 0, unroll=ATT_UNROLL)
    lax.fori_loop(0, ATT_BLK // blk, body4, 0, unroll=ATT_UNROLL)

    def body1(u, carry):
        r0 = pl.multiple_of(u * blk, blk)
        sl = pl.ds(r0, blk)
        q2 = q_ref[0, sl, :].astype(BF16)
        bias = bias_s[jnp.where(u == 0, first_blk, 0)]
        acc2, m2, l2 = unit(q2, kd1[pl.ds(r0, 2 * blk), :], vd1[pl.ds(r0, 2 * blk), :], bias)
        per_res = blk // 16
        acc1, m1, l1 = (jnp.concatenate([st16_s[a, pl.ds(per_res * u + k, 16, stride=PITCH16), :]
                                         for k in range(per_res)], axis=0) for a in range(3))
        acc0, m0, l0 = st4_s[0, sl, :], st4_s[1, sl, :], st4_s[2, sl, :]
        mx = jnp.maximum(jnp.maximum(m0, m1), m2)
        w0, w1, w2 = jnp.exp2(m0 - mx), jnp.exp2(m1 - mx), jnp.exp2(m2 - mx)
        num = w0 * acc0 + w1 * acc1 + w2 * acc2
        den = w0 * l0 + w1 * l1 + w2 * l2
        o_ref[0, sl, :] = (num / den).astype(o_ref.dtype)
        return carry

    lax.fori_loop(0, nsub[1], body1, 0, unroll=ATT_UNROLL)


def _attention(qkv):
    B, S, _ = qkv.shape
    blk = ATT_SPAN
    pairs = ATT_WIDTH // LANES
    cur = pl.BlockSpec((1, ATT_BLK, LANES), lambda b, hp, j: (b, j, hp))
    part = lambda p: pl.BlockSpec((1, ATT_BLK, LANES), lambda b, hp, j: (b, j, p * pairs + hp))
    kv_rows = {d: d * (ATT_BLK // d + blk) for d in DILATIONS}
    scratch = [
        pltpu.VMEM((kv_rows[1], LANES), BF16), pltpu.VMEM((kv_rows[1], LANES), BF16),
        pltpu.VMEM((ATT_BLK, LANES), BF16),
        pltpu.VMEM((kv_rows[4], LANES), BF16), pltpu.VMEM((kv_rows[4], LANES), BF16),
        pltpu.VMEM((ATT_BLK, LANES), BF16),
        pltpu.VMEM((kv_rows[16], LANES), BF16), pltpu.VMEM((kv_rows[16], LANES), BF16),
        pltpu.VMEM((4, ATT_BLK // 4, LANES), F32),
        pltpu.VMEM((3, ATT_BLK, LANES), F32),
        pltpu.VMEM((3, 16 * PITCH16, LANES), F32),
        pltpu.VMEM((2, 2 * blk, 2 * blk), F32),
    ]
    return pl.pallas_call(
        _attn_kernel,
        grid=(B, pairs, S // ATT_BLK),
        in_specs=[part(0), part(1), part(2)],
        out_specs=cur,
        out_shape=jax.ShapeDtypeStruct((B, S, ATT_WIDTH), BF16),
        scratch_shapes=scratch,
        compiler_params=pltpu.CompilerParams(dimension_semantics=("parallel", "parallel", "arbitrary"),
                                             vmem_limit_bytes=VMEM_LIMIT),
        name="dilated_attention",
    )(qkv, qkv, qkv)


NORM_ROWS = 16


def _mlstm_kernel(qko_ref, vt_ref, g_ref, gt_ref, nw_ref, y_ref, c_state, m_state):
    L = MCHUNK
    D = MLSTM_HEAD_DIM
    nt_dims = (((1,), (1,)), ((), ()))
    q_ref, k_ref, mo_ref = (qko_ref.at[:, j * MLSTM_WIDTH:(j + 1) * MLSTM_WIDTH] for j in range(3))

    @pl.when(pl.program_id(1) == 0)
    def _():
        c_state[...] = jnp.zeros_like(c_state)
        m_state[...] = jnp.zeros_like(m_state)

    key = lax.broadcasted_iota(jnp.int32, (L, L), 0)
    qry = lax.broadcasted_iota(jnp.int32, (L, L), 1)
    visible = key <= qry
    ones_rows = jnp.ones((NORM_ROWS, L), BF16)

    units = [(cc, h) for cc in range(TM_MLSTM // L) for h in range(MLSTM_HEADS)]
    rows = lambda cc: slice(cc * L, (cc + 1) * L)
    cols = lambda h: slice(h * D, (h + 1) * D)

    s_t = {u: lax.dot_general(k_ref[rows(u[0]), cols(u[1])], q_ref[rows(u[0]), cols(u[1])], nt_dims,
                              preferred_element_type=F32) for u in units}
    b_row, g_tot, pm, vt_aug, intra, m_loc, kv = {}, {}, {}, {}, {}, {}, {}
    for u in units:
        cc, h = u
        fh = MLSTM_HEADS + h
        r_col = g_ref[rows(cc), h:h + 1] - g_ref[rows(cc), fh:fh + 1]
        b_row[u] = gt_ref[cc, fh:fh + 1, :]
        g_tot[u] = b_row[u][:, L - 1:L]
        r_vis = jnp.where(visible, r_col, NEG)
        pm[u] = jnp.max(r_vis, axis=0, keepdims=True)
        p_t = (jnp.exp(r_vis - pm[u]) * s_t[u]).astype(BF16)
        vt_aug[u] = jnp.concatenate([vt_ref[cc, cols(h), :], ones_rows], axis=0)
        intra[u] = jnp.dot(vt_aug[u], p_t, preferred_element_type=F32)
    for u in units:
        cc, h = u
        r_row = gt_ref[cc, h:h + 1, :] - b_row[u]
        m_loc[u] = jnp.max(g_tot[u] + r_row, axis=-1, keepdims=True)
        vw = (vt_aug[u].astype(F32) * jnp.exp(g_tot[u] + r_row - m_loc[u])).astype(BF16)
        kv[u] = jnp.dot(vw, k_ref[rows(cc), cols(h)], preferred_element_type=F32)

    c_aug = [c_state[h] for h in range(MLSTM_HEADS)]
    m_prev = [m_state[h:h + 1, 0:1] for h in range(MLSTM_HEADS)]
    for u in units:
        cc, h = u
        inter = lax.dot_general(c_aug[h].astype(BF16), q_ref[rows(cc), cols(h)], nt_dims,
                                preferred_element_type=F32)
        mm = jnp.maximum(pm[u], m_prev[h])
        nd = jnp.exp(pm[u] - mm) * intra[u] + jnp.exp(m_prev[h] - mm) * inter
        inv = 1.0 / jnp.maximum(jnp.abs(nd[D:D + 1, :]), jnp.exp(-(b_row[u] + mm)))
        hh = nd[0:D, :] * inv
        hn = (hh * lax.rsqrt(jnp.mean(hh * hh, axis=0, keepdims=True) + NORM_EPS)).T
        o_gate = jax.nn.sigmoid(mo_ref[rows(cc), cols(h)].astype(F32))
        y_ref[rows(cc), cols(h)] = (o_gate * hn * nw_ref[:, cols(h)]).astype(y_ref.dtype)

        m_new = jnp.maximum(g_tot[u] + m_prev[h], m_loc[u])
        c_aug[h] = jnp.exp(g_tot[u] + m_prev[h] - m_new) * c_aug[h] + jnp.exp(m_loc[u] - m_new) * kv[u]
        m_prev[h] = m_new
    for h in range(MLSTM_HEADS):
        c_state[h] = c_aug[h]
        m_state[h:h + 1, :] = jnp.broadcast_to(m_prev[h], (1, LANES))


def _mlstm(mqko, mvt, gates, gates_t, norm_w, batch):
    T = mqko.shape[0]
    nt = T // batch // TM_MLSTM
    nch = TM_MLSTM // MCHUNK
    tile = lambda w: pl.BlockSpec((TM_MLSTM, w), lambda b, c: (b * nt + c, 0))
    per_chunk = lambda r: pl.BlockSpec((nch, r, MCHUNK), lambda b, c: (b * nt + c, 0, 0))
    return pl.pallas_call(
        _mlstm_kernel,
        grid=(batch, nt),
        in_specs=[tile(3 * MLSTM_WIDTH), per_chunk(MLSTM_WIDTH), tile(LANES),
                  per_chunk(GATE_ROWS), pl.BlockSpec((1, MLSTM_WIDTH), lambda b, c: (0, 0))],
        out_specs=tile(MLSTM_WIDTH),
        out_shape=jax.ShapeDtypeStruct((T, MLSTM_WIDTH), BF16),
        scratch_shapes=[
            pltpu.VMEM((MLSTM_HEADS, MLSTM_HEAD_DIM + NORM_ROWS, MLSTM_HEAD_DIM), F32),
            pltpu.VMEM((8, LANES), F32),
        ],
        compiler_params=pltpu.CompilerParams(dimension_semantics=("parallel", "arbitrary"),
                                             vmem_limit_bytes=VMEM_LIMIT),
        name="mlstm",
    )(mqko, mvt, gates, gates_t, norm_w)


def _post_kernel(x_ref, ya_ref, yb_ref, gab_ref, wpa_ref, wpm_ref, wo_ref, nw_ref, w1_ref, w2_ref,
                 fw_ref, *rest, final_norm):
    n_cast = len(rest) // 2
    o_ref = rest[n_cast]
    for src, dst in zip(rest[:n_cast - 1], rest[n_cast + 1:-1]):
        dst[...] = src[...].astype(BF16)
    if n_cast:
        _cast_in_proj_slab(rest[n_cast - 1], rest[-1], jnp.minimum(pl.program_id(0), IN_SLABS - 1))
    ga_ref, gb_ref = (gab_ref.at[:, j * D_MODEL:(j + 1) * D_MODEL] for j in range(2))
    pa = jnp.dot(ya_ref[...], wpa_ref[...], preferred_element_type=F32)
    pb = jnp.dot(yb_ref[...], wpm_ref[...], preferred_element_type=F32)
    mixed = (jax.nn.sigmoid(ga_ref[...].astype(F32)) * pa
             + jax.nn.sigmoid(gb_ref[...].astype(F32)) * pb).astype(BF16)
    x1 = x_ref[...] + jnp.dot(mixed, wo_ref[...], preferred_element_type=F32)
    var = jnp.mean(x1 * x1, axis=-1, keepdims=True)
    h2 = (x1 * lax.rsqrt(var + NORM_EPS) * nw_ref[...]).astype(BF16)
    acc = x1
    ff_chunk = D_MODEL
    for c in range(D_FF // ff_chunk):
        cs = slice(c * ff_chunk, (c + 1) * ff_chunk)
        u = jnp.maximum(jnp.dot(h2, w1_ref[:, cs], preferred_element_type=F32), 0.0)
        acc = acc + jnp.dot((u * u).astype(BF16), w2_ref[cs, :], preferred_element_type=F32)
    if final_norm:
        var = jnp.mean(acc * acc, axis=-1, keepdims=True)
        acc = acc * lax.rsqrt(var + NORM_EPS) * fw_ref[...]
    o_ref[...] = acc


def _post(x2d, ya, yb, gab, weights, norm_w, final_w, layer, final_norm, next_weights):
    T = x2d.shape[0]
    steps = T // TM_POST
    tile = lambda w: pl.BlockSpec((TM_POST, w), lambda i: (i, 0))
    wpa, wpm, wo, w1, w2 = weights
    cast_in, cast_out, cast_shape = _cast_specs(next_weights[:-1], layer + 1, steps)
    if next_weights:
        assert steps >= IN_SLABS
        in_spec, out_spec, out_shape = _cast_in_proj_specs(layer + 1, PREP_IN_ROWS)
        cast_in, cast_out, cast_shape = cast_in + [in_spec], cast_out + [out_spec], cast_shape + [out_shape]
    out = pl.pallas_call(
        functools.partial(_post_kernel, final_norm=final_norm),
        grid=(steps,),
        in_specs=[tile(D_MODEL), tile(ATT_WIDTH), tile(MLSTM_WIDTH), tile(2 * D_MODEL),
                  _const_spec(wpa.shape), _const_spec(wpm.shape), _const_spec(wo.shape),
                  _layer_spec((1, D_MODEL), layer), _const_spec(w1.shape), _const_spec(w2.shape),
                  _const_spec((1, D_MODEL))] + cast_in,
        out_specs=[tile(D_MODEL)] + cast_out,
        out_shape=[jax.ShapeDtypeStruct((T, D_MODEL), F32)] + cast_shape,
        compiler_params=pltpu.CompilerParams(dimension_semantics=("arbitrary",),
                                             vmem_limit_bytes=VMEM_LIMIT),
        name="post",
    )(x2d, ya, yb, gab, wpa, wpm, wo, norm_w, w1, w2, final_w, *next_weights)
    return out[0], tuple(out[1:])


PREP_STEPS = 8


PREP_IN_ROWS = MXU_COLS
IN_SLABS = W_PAD // PREP_IN_ROWS
PREP_IN_ROWS_ALONE = W_PAD // 2


def _cast_in_proj_slab(w_ref, o_ref, slab):
    row = slab * w_ref.shape[0] + lax.broadcasted_iota(jnp.int32, w_ref.shape, 0)
    w = w_ref[...]
    w = jnp.where(row < COL_AK, w * (ATT_HEAD_DIM ** -0.5), w)
    o_ref[...] = jnp.where(row < D_IN, w, 0.0).T.astype(BF16)


def _cast_in_proj_specs(layer, rows):
    assert W_PAD % rows == 0 and rows % LANES == 0
    slab = lambda i: jnp.minimum(i, W_PAD // rows - 1)
    in_spec = pl.BlockSpec((None, rows, D_MODEL), lambda i: (layer, slab(i), 0))
    out_spec = pl.BlockSpec((D_MODEL, rows), lambda i: (0, slab(i)))
    return in_spec, out_spec, jax.ShapeDtypeStruct((D_MODEL, W_PAD), BF16)


def _cast_in_proj_kernel(w_ref, o_ref):
    _cast_in_proj_slab(w_ref, o_ref, pl.program_id(0))


def _cast_in_proj(w_t, layer):
    in_spec, out_spec, out_shape = _cast_in_proj_specs(layer, PREP_IN_ROWS_ALONE)
    return pl.pallas_call(
        _cast_in_proj_kernel,
        grid=(W_PAD // PREP_IN_ROWS_ALONE,),
        in_specs=[in_spec],
        out_specs=out_spec,
        out_shape=out_shape,
        compiler_params=pltpu.CompilerParams(dimension_semantics=("parallel",), vmem_limit_bytes=VMEM_LIMIT),
        name="cast_in_proj",
    )(w_t)


def _cast_weights_kernel(*refs):
    n = len(refs) // 2
    for src, dst in zip(refs[:n], refs[n:]):
        dst[...] = src[...].astype(BF16)


def _cast_specs(weights, layer, steps):
    for a in weights:
        assert a.shape[1] % (steps * 16) == 0
    in_specs = [pl.BlockSpec((None, a.shape[1] // steps, a.shape[2]), lambda i: (layer, i, 0)) for a in weights]
    out_specs = [pl.BlockSpec((a.shape[1] // steps, a.shape[2]), lambda i: (i, 0)) for a in weights]
    out_shape = [jax.ShapeDtypeStruct(a.shape[1:], BF16) for a in weights]
    return in_specs, out_specs, out_shape


def _cast_weights(weights, layer):
    in_specs, out_specs, out_shape = _cast_specs(weights, layer, PREP_STEPS)
    return pl.pallas_call(
        _cast_weights_kernel,
        grid=(PREP_STEPS,),
        in_specs=in_specs,
        out_specs=out_specs,
        out_shape=out_shape,
        compiler_params=pltpu.CompilerParams(dimension_semantics=("parallel",), vmem_limit_bytes=VMEM_LIMIT),
        name="cast_weights",
    )(*weights)


def kernel(x, positions, norm_mix_w, w_in, b_in, conv_w, conv_b, mlstm_norm_w, w_proj_att, w_proj_mlstm,
           w_out, norm_mlp_w, w_ff1, w_ff2, final_norm_w):
    B, S, D = x.shape
    T = B * S
    depth = w_in.shape[0]
    assert D == D_MODEL and S % ATT_BLK == 0 and T % TM_IN == 0 and T % TM_POST == 0
    assert math.isclose(ATT_HEAD_DIM ** -0.5, 0.125)
    rope_c, rope_s = _rope_tables(positions)
    w_in_t = jnp.swapaxes(w_in, 1, 2)
    stacked = (w_proj_att, w_proj_mlstm, w_out, w_ff1, w_ff2)
    w_all = _cast_in_proj(w_in_t, 0)
    weights = _cast_weights(stacked, 0)
    q_scale = jnp.where(jnp.arange(W_PAD) < COL_AK, ATT_HEAD_DIM ** -0.5, 1.0).astype(F32)
    b_all = (jnp.pad(b_in, ((0, 0), (0, W_PAD - D_IN))) * q_scale).reshape(depth, 1, W_PAD)
    x2d = x.reshape(T, D)
    final_w = final_norm_w.reshape(1, D).astype(F32)
    for l in range(depth):
        qkv, mqko, mvt, gab, gates, gates_t = _inproj(
            x2d, norm_mix_w.reshape(depth, 1, D), w_all, b_all, rope_c, rope_s,
            conv_w, conv_b.reshape(depth, 1, -1), S, l)
        ya = _attention(qkv.reshape(B, S, 3 * ATT_WIDTH))
        yb = _mlstm(mqko, mvt, gates, gates_t, mlstm_norm_w[l].reshape(1, -1), B)
        last = l == depth - 1
        x2d, cast = _post(x2d, ya.reshape(T, ATT_WIDTH), yb, gab, weights, norm_mlp_w.reshape(depth, 1, D),
                          final_w, l, final_norm=last, next_weights=() if last else stacked + (w_in_t,))
        if not last:
            weights, w_all = cast[:-1], cast[-1]
    return x2d.reshape(B, S, D)
```

```python
import functools
import math

import jax
import jax.numpy as jnp
from jax import lax
from jax.experimental import pallas as pl
from jax.experimental.pallas import tpu as pltpu

F32 = jnp.float32
BF16 = jnp.bfloat16

D_MODEL = 1024
ATT_HEADS = 8
ATT_HEAD_DIM = 64
ATT_WIDTH = ATT_HEADS * ATT_HEAD_DIM
ATT_SPAN = 128
DILATIONS = (1, 4, 16)
ROPE_THETA = 500000.0
ROPE_DIM = ATT_HEAD_DIM // 4
MLSTM_HEADS = 4
MLSTM_HEAD_DIM = 128
MLSTM_WIDTH = MLSTM_HEADS * MLSTM_HEAD_DIM
CONV_WIDTH = 4
D_FF = 4 * D_MODEL
NORM_EPS = 1e-6

COL_AQ = 0
COL_AK = COL_AQ + ATT_WIDTH
COL_AV = COL_AK + ATT_WIDTH
COL_MQ = COL_AV + ATT_WIDTH
COL_MK = COL_MQ + MLSTM_WIDTH
COL_MV = COL_MK + MLSTM_WIDTH
COL_MO = COL_MV + MLSTM_WIDTH
COL_MI = COL_MO + MLSTM_WIDTH
COL_MF = COL_MI + MLSTM_HEADS
COL_GA = COL_MF + MLSTM_HEADS
COL_GB = COL_GA + D_MODEL
D_IN = COL_GB + D_MODEL

LANES = 128
assert COL_MI % LANES == 0
MXU_COLS = 256
W_PAD = -(-D_IN // MXU_COLS) * MXU_COLS
GATE_SHIFT = 2 * MLSTM_HEADS

TM_IN = 512
TM_POST = 512
ATT_BLK = 2048
MCHUNK = 128
TM_MLSTM = 1024
VMEM_LIMIT = 56 * 1024 * 1024
NEG = -1e30
LOG2E = math.log2(math.e)


def _const_spec(shape):
    nd = len(shape)
    return pl.BlockSpec(shape, lambda *_: (0,) * nd, pipeline_mode=pl.Buffered(1))


ROPE_HALF = ROPE_DIM // 2
ROPE_PACK = LANES // ROPE_HALF
ROPE_ROWS = 128


def _rope_table_kernel(pos_ref, invf_ref, c_ref, s_ref):
    ang = pos_ref[...] * invf_ref[...]
    cosx = jnp.cos(ang)
    sinx = jnp.sin(ang)
    dst = lax.broadcasted_iota(jnp.int32, (ROPE_ROWS, LANES), 1)
    in_head = dst % ATT_HEAD_DIM
    rotary = in_head < ROPE_DIM
    sign = jnp.where(in_head < ROPE_HALF, -1.0, 1.0)
    for r in range(ROPE_PACK):
        src = r * ROPE_HALF + dst % ROPE_HALF
        c_ref[pl.ds(r, ROPE_ROWS, stride=ROPE_PACK), :] = jnp.where(
            rotary, jnp.take_along_axis(cosx, src, axis=1), 1.0)
        s_ref[pl.ds(r, ROPE_ROWS, stride=ROPE_PACK), :] = jnp.where(
            rotary, jnp.take_along_axis(sinx, src, axis=1) * sign, 0.0)


def _rope_tables(positions):
    T = positions.size
    assert T % (ROPE_PACK * ROPE_ROWS) == 0
    pos = jnp.repeat(positions.astype(F32).reshape(T // ROPE_PACK, ROPE_PACK), ROPE_HALF, axis=1)
    inv_freq = ROPE_THETA ** (-jnp.arange(0, ROPE_DIM, 2, dtype=F32) / ROPE_DIM)
    invf = jnp.tile(inv_freq, ROPE_PACK).reshape(1, LANES)
    table = jax.ShapeDtypeStruct((T, LANES), F32)
    out_spec = pl.BlockSpec((ROPE_PACK * ROPE_ROWS, LANES), lambda i: (i, 0))
    return pl.pallas_call(
        _rope_table_kernel,
        grid=(T // (ROPE_PACK * ROPE_ROWS),),
        in_specs=[pl.BlockSpec((ROPE_ROWS, LANES), lambda i: (i, 0)), pl.BlockSpec((1, LANES), lambda i: (0, 0))],
        out_specs=(out_spec, out_spec),
        out_shape=(table, table),
        compiler_params=pltpu.CompilerParams(dimension_semantics=("parallel",)),
        name="rope_tables",
    )(pos, invf)


CONV_TAIL = 8
GATE_ROWS = 8


def _log_sigmoid(x):
    return jnp.minimum(x, 0.0) - jnp.log(1.0 + jnp.exp(-jnp.abs(x)))


def _inproj_kernel(x_ref, nw_ref, w_ref, b_ref, c_ref, s_ref, cw_ref, cb_ref,
                   qkv_ref, mqko_ref, mv_ref, gab_ref, gt_ref, gtt_ref,
                   u_s, res_s, *, tiles_per_seq):
    aq_ref, ak_ref, av_ref = (qkv_ref.at[:, j * ATT_WIDTH:(j + 1) * ATT_WIDTH] for j in range(3))
    mq_ref, mk_ref, mo_ref = (mqko_ref.at[:, j * MLSTM_WIDTH:(j + 1) * MLSTM_WIDTH] for j in range(3))
    ga_ref, gb_ref = (gab_ref.at[:, j * D_MODEL:(j + 1) * D_MODEL] for j in range(2))
    @pl.when(pl.program_id(0) % tiles_per_seq == 0)
    def _():
        u_s[:, TM_IN:TM_IN + CONV_TAIL, :] = jnp.zeros((u_s.shape[0], CONV_TAIL, LANES), F32)

    x = x_ref[...]
    var = jnp.mean(x * x, axis=-1, keepdims=True)
    h = (x * lax.rsqrt(var + NORM_EPS) * nw_ref[...]).astype(BF16)

    def proj(lo, width):
        return jnp.dot(h, w_ref[:, lo:lo + width], preferred_element_type=F32) + b_ref[:, lo:lo + width]

    cos = c_ref[...]
    sin = s_ref[...]
    lane = lax.broadcasted_iota(jnp.int32, cos.shape, 1)
    first_half = (lane % ATT_HEAD_DIM) < (ROPE_DIM // 2)

    def rope_store(dst_ref, lo, scale=None):
        zz = proj(lo, ATT_WIDTH)
        for j in range(ATT_WIDTH // LANES):
            z = zz[:, j * LANES:(j + 1) * LANES]
            partner = jnp.where(first_half,
                                pltpu.roll(z, LANES - ROPE_DIM // 2, axis=1),
                                pltpu.roll(z, ROPE_DIM // 2, axis=1))
            out = z * cos + partner * sin
            dst_ref[:, j * LANES:(j + 1) * LANES] = out if scale is None else out * scale

    def conv_silu_store(dst_ref, slab0, lo, col0, scale):
        z = proj(lo, MLSTM_WIDTH)
        half = TM_IN // 2
        for j in range(MLSTM_WIDTH // LANES):
            sl = slab0 + j
            ws = slice(col0 + j * LANES, col0 + (j + 1) * LANES)
            u_s[sl, 0:CONV_TAIL, :] = u_s[sl, TM_IN:TM_IN + CONV_TAIL, :]
            u_s[sl, CONV_TAIL:CONV_TAIL + TM_IN, :] = z[:, j * LANES:(j + 1) * LANES]
            for parity in range(2):
                out = cb_ref[:, ws]
                for t in range(CONV_WIDTH):
                    r0 = CONV_TAIL - (CONV_WIDTH - 1) + t + parity
                    out = out + cw_ref[t:t + 1, ws] * u_s[sl, pl.ds(r0, half, stride=2), :]
                out = out * jax.nn.sigmoid(out)
                res_s[sl, pl.ds(parity, half, stride=2), :] = out if scale is None else out * scale
            dst_ref[:, j * LANES:(j + 1) * LANES] = res_s[sl].astype(dst_ref.dtype)

    rope_store(aq_ref, COL_AQ, LOG2E)
    rope_store(ak_ref, COL_AK)
    av_ref[...] = proj(COL_AV, ATT_WIDTH)
    conv_silu_store(mq_ref, 0, COL_MQ, 0, MLSTM_HEAD_DIM ** -0.5)
    conv_silu_store(mk_ref, MLSTM_WIDTH // LANES, COL_MK, MLSTM_WIDTH, None)
    z_mv = proj(COL_MV, MLSTM_WIDTH)
    for cc in range(TM_IN // MCHUNK):
        for hd in range(MLSTM_HEADS):
            blk = z_mv[cc * MCHUNK:(cc + 1) * MCHUNK, hd * MLSTM_HEAD_DIM:(hd + 1) * MLSTM_HEAD_DIM]
            mv_ref[cc, hd * MLSTM_HEAD_DIM:(hd + 1) * MLSTM_HEAD_DIM, :] = blk.T.astype(mv_ref.dtype)
    mo_ref[...] = proj(COL_MO, MLSTM_WIDTH).astype(mo_ref.dtype)

    z_tail = proj(COL_MI, W_PAD - COL_MI)
    tiles = [z_tail[:, j * LANES:(j + 1) * LANES] for j in range((W_PAD - COL_MI) // LANES)]
    rolled = [pltpu.roll(t, LANES - GATE_SHIFT, axis=1) for t in tiles]
    low_lanes = lane < LANES - GATE_SHIFT
    for j in range(D_MODEL // LANES):
        ga_ref[:, j * LANES:(j + 1) * LANES] = jnp.where(low_lanes, rolled[j], rolled[j + 1]).astype(ga_ref.dtype)
        k = j + D_MODEL // LANES
        gb_ref[:, j * LANES:(j + 1) * LANES] = jnp.where(low_lanes, rolled[k], rolled[k + 1]).astype(gb_ref.dtype)

    zg = jnp.where(lane < GATE_SHIFT, tiles[0], 0.0)
    logf = _log_sigmoid(zg)
    ri = lax.broadcasted_iota(jnp.int32, (MCHUNK, MCHUNK), 0)
    ci = lax.broadcasted_iota(jnp.int32, (MCHUNK, MCHUNK), 1)
    tri = (ci <= ri).astype(F32)
    is_input_gate = lax.broadcasted_iota(jnp.int32, (MCHUNK, LANES), 1) < MLSTM_HEADS
    for cc in range(TM_IN // MCHUNK):
        rows = slice(cc * MCHUNK, (cc + 1) * MCHUNK)
        bcum = jnp.dot(tri, logf[rows], precision=lax.Precision.HIGHEST, preferred_element_type=F32)
        gc = jnp.where(is_input_gate, zg[rows], bcum)
        gt_ref[rows, :] = gc
        gtt_ref[cc] = gc.T[0:GATE_ROWS, :]


def _layer_spec(shape, layer):
    nd = len(shape)
    return pl.BlockSpec((None,) + tuple(shape), lambda *_: (layer,) + (0,) * nd, pipeline_mode=pl.Buffered(1))


def _inproj(x2d, norm_w, w_all, b_all, rope_c, rope_s, conv_w, conv_b, seq_len, layer):
    T = x2d.shape[0]
    tile = lambda w: pl.BlockSpec((TM_IN, w), lambda i: (i, 0))
    out_shapes = (
        jax.ShapeDtypeStruct((T, 3 * ATT_WIDTH), F32),
        jax.ShapeDtypeStruct((T, 3 * MLSTM_WIDTH), BF16),
        jax.ShapeDtypeStruct((T // MCHUNK, MLSTM_WIDTH, MCHUNK), BF16),
        jax.ShapeDtypeStruct((T, 2 * D_MODEL), BF16),
        jax.ShapeDtypeStruct((T, LANES), F32),
        jax.ShapeDtypeStruct((T // MCHUNK, GATE_ROWS, MCHUNK), F32),
    )
    per_chunk = lambda rows: pl.BlockSpec((TM_IN // MCHUNK, rows, MCHUNK), lambda i: (i, 0, 0))
    out_specs = tuple(tile(s.shape[1]) if len(s.shape) == 2 else per_chunk(s.shape[1]) for s in out_shapes)
    assert seq_len % TM_IN == 0
    return pl.pallas_call(
        functools.partial(_inproj_kernel, tiles_per_seq=seq_len // TM_IN),
        grid=(T // TM_IN,),
        in_specs=[tile(D_MODEL), _layer_spec((1, D_MODEL), layer), _const_spec((D_MODEL, W_PAD)),
                  _layer_spec((1, W_PAD), layer), tile(LANES), tile(LANES),
                  _layer_spec((CONV_WIDTH, 2 * MLSTM_WIDTH), layer), _layer_spec((1, 2 * MLSTM_WIDTH), layer)],
        out_specs=out_specs,
        out_shape=out_shapes,
        scratch_shapes=[pltpu.VMEM((2 * MLSTM_WIDTH // LANES, TM_IN + CONV_TAIL, LANES), F32),
                        pltpu.VMEM((2 * MLSTM_WIDTH // LANES, TM_IN, LANES), F32)],
        compiler_params=pltpu.CompilerParams(dimension_semantics=("arbitrary",),
                                             vmem_limit_bytes=VMEM_LIMIT),
        name="inproj",
    )(x2d, norm_w, w_all, b_all, rope_c, rope_s, conv_w, conv_b)


ATT_UNROLL = 16
PITCH16 = ATT_SPAN + 8


def _attn_kernel(q_ref, k_ref, v_ref, o_ref,
                 kd1, vd1, qd4, kd4, vd4, qd16, kd16, vd16, tmp, st4_s, st16_s, bias_s):
    j = pl.program_id(2)
    blk = ATT_SPAN
    nsub = {d: ATT_BLK // d // blk for d in DILATIONS}
    kv_bufs = ((1, kd1, vd1), (4, kd4, vd4), (16, kd16, vd16))

    row = lax.broadcasted_iota(jnp.int32, (2 * blk, 2 * blk), 0) % blk
    col = lax.broadcasted_iota(jnp.int32, (2 * blk, 2 * blk), 1)
    band = (col >= row) & (col <= row + ATT_SPAN)
    bias_s[0] = jnp.where(band, 0.0, NEG)
    bias_s[1] = jnp.where(band & (col >= blk), 0.0, NEG)

    @pl.when(j == 0)
    def _():
        for d, kd, vd in kv_bufs:
            n = ATT_BLK // d
            for r in range(d):
                base = r * (n + blk)
                kd[base:base + blk] = jnp.zeros((blk, LANES), BF16)
                vd[base:base + blk] = jnp.zeros((blk, LANES), BF16)

    @pl.when(j != 0)
    def _():
        for d, kd, vd in kv_bufs:
            n = ATT_BLK // d
            for r in range(d):
                base = r * (n + blk)
                kd[base:base + blk] = kd[base + n:base + n + blk]
                vd[base:base + blk] = vd[base + n:base + n + blk]

    def deinterleave(src_ref, dst1, dst4, dst16, is_kv):
        pad = blk if is_kv else 0
        if dst1 is not None:
            dst1[blk:blk + ATT_BLK] = src_ref[0].astype(BF16)
        n4 = ATT_BLK // 4
        for r4 in range(4):
            t4 = src_ref[0, pl.ds(r4, n4, stride=4), :]
            tmp[r4] = t4
            o4 = r4 * (n4 + pad) + pad
            dst4[o4:o4 + n4] = t4.astype(BF16)
        n16 = ATT_BLK // 16
        for r4 in range(4):
            for rr in range(4):
                o16 = (4 * rr + r4) * (n16 + pad) + pad
                dst16[o16:o16 + n16] = tmp[r4, pl.ds(rr, n16, stride=4), :].astype(BF16)

    deinterleave(q_ref, None, qd4, qd16, False)
    deinterleave(k_ref, kd1, kd4, kd16, True)
    deinterleave(v_ref, vd1, vd4, vd16, True)

    head_a = lax.broadcasted_iota(jnp.int32, (blk, LANES), 1) < ATT_HEAD_DIM

    def unit(q2, k2, v2, bias):
        zero = jnp.zeros_like(q2)
        qs = jnp.concatenate([jnp.where(head_a, q2, zero), jnp.where(head_a, zero, q2)], axis=0)
        s = lax.dot_general(qs, k2, (((1,), (1,)), ((), ())), preferred_element_type=F32) + bias
        m = jnp.max(s, axis=-1, keepdims=True)
        p = jnp.exp2(s - m).astype(BF16)
        v_aug = jnp.concatenate([v2, jnp.ones_like(v2)], axis=1)
        pv = jnp.dot(p, v_aug, preferred_element_type=F32)
        acc = jnp.where(head_a, pv[:blk, :LANES], pv[blk:, :LANES])
        ll = jnp.where(head_a, pv[:blk, LANES:], pv[blk:, LANES:])
        mm = jnp.where(head_a, m[:blk], m[blk:])
        return acc, mm, ll

    first_blk = jnp.where(j == 0, 1, 0)

    def body16(u, carry):
        k0 = pl.multiple_of(u * (2 * blk), blk)
        bias = bias_s[first_blk]
        res = unit(qd16[pl.ds(pl.multiple_of(u * blk, blk), blk), :], kd16[pl.ds(k0, 2 * blk), :],
                   vd16[pl.ds(k0, 2 * blk), :], bias)
        row0 = pl.multiple_of(u * PITCH16, 8)
        for a, val in enumerate(res):
            st16_s[a, pl.ds(row0, blk), :] = val
        return carry

    def body4(u, carry):
        r = u // nsub[4]
        sb = u % nsub[4]
        k0 = pl.multiple_of(r * (ATT_BLK // 4 + blk) + sb * blk, blk)
        bias = bias_s[jnp.where(sb == 0, first_blk, 0)]
        res = unit(qd4[pl.ds(pl.multiple_of(u * blk, blk), blk), :], kd4[pl.ds(k0, 2 * blk), :],
                   vd4[pl.ds(k0, 2 * blk), :], bias)
        t0 = sb * (blk * 4) + r
        for a, val in enumerate(res):
            st4_s[a, pl.ds(t0, blk, stride=4), :] = val
        return carry

    lax.fori_loop(0, ATT_BLK // blk, body16, 0, unroll=ATT_UNROLL)
    lax.fori_loop(0, ATT_BLK // blk, body4, 0, unroll=ATT_UNROLL)

    def body1(u, carry):
        r0 = pl.multiple_of(u * blk, blk)
        sl = pl.ds(r0, blk)
        q2 = q_ref[0, sl, :].astype(BF16)
        bias = bias_s[jnp.where(u == 0, first_blk, 0)]
        acc2, m2, l2 = unit(q2, kd1[pl.ds(r0, 2 * blk), :], vd1[pl.ds(r0, 2 * blk), :], bias)
        per_res = blk // 16
        acc1, m1, l1 = (jnp.concatenate([st16_s[a, pl.ds(per_res * u + k, 16, stride=PITCH16), :]
                                         for k in range(per_res)], axis=0) for a in range(3))
        acc0, m0, l0 = st4_s[0, sl, :], st4_s[1, sl, :], st4_s[2, sl, :]
        mx = jnp.maximum(jnp.maximum(m0, m1), m2)
        w0, w1, w2 = jnp.exp2(m0 - mx), jnp.exp2(m1 - mx), jnp.exp2(m2 - mx)
        num = w0 * acc0 + w1 * acc1 + w2 * acc2
        den = w0 * l0 + w1 * l1 + w2 * l2
        o_ref[0, sl, :] = (num / den).astype(o_ref.dtype)
        return carry

    lax.fori_loop(0, nsub[1], body1, 0, unroll=ATT_UNROLL)


def _attention(qkv):
    B, S, _ = qkv.shape
    blk = ATT_SPAN
    pairs = ATT_WIDTH // LANES
    cur = pl.BlockSpec((1, ATT_BLK, LANES), lambda b, hp, j: (b, j, hp))
    part = lambda p: pl.BlockSpec((1, ATT_BLK, LANES), lambda b, hp, j: (b, j, p * pairs + hp))
    kv_rows = {d: d * (ATT_BLK // d + blk) for d in DILATIONS}
    scratch = [
        pltpu.VMEM((kv_rows[1], LANES), BF16), pltpu.VMEM((kv_rows[1], LANES), BF16),
        pltpu.VMEM((ATT_BLK, LANES), BF16),
        pltpu.VMEM((kv_rows[4], LANES), BF16), pltpu.VMEM((kv_rows[4], LANES), BF16),
        pltpu.VMEM((ATT_BLK, LANES), BF16),
        pltpu.VMEM((kv_rows[16], LANES), BF16), pltpu.VMEM((kv_rows[16], LANES), BF16),
        pltpu.VMEM((4, ATT_BLK // 4, LANES), F32),
        pltpu.VMEM((3, ATT_BLK, LANES), F32),
        pltpu.VMEM((3, 16 * PITCH16, LANES), F32),
        pltpu.VMEM((2, 2 * blk, 2 * blk), F32),
    ]
    return pl.pallas_call(
        _attn_kernel,
        grid=(B, pairs, S // ATT_BLK),
        in_specs=[part(0), part(1), part(2)],
        out_specs=cur,
        out_shape=jax.ShapeDtypeStruct((B, S, ATT_WIDTH), BF16),
        scratch_shapes=scratch,
        compiler_params=pltpu.CompilerParams(dimension_semantics=("parallel", "parallel", "arbitrary"),
                                             vmem_limit_bytes=VMEM_LIMIT),
        name="dilated_attention",
    )(qkv, qkv, qkv)


NORM_ROWS = 16


def _mlstm_kernel(qko_ref, vt_ref, g_ref, gt_ref, nw_ref, *rest, n_cast):
    L = MCHUNK
    D = MLSTM_HEAD_DIM
    nt_dims = (((1,), (1,)), ((), ()))
    q_ref, k_ref, mo_ref = (qko_ref.at[:, j * MLSTM_WIDTH:(j + 1) * MLSTM_WIDTH] for j in range(3))
    y_ref, c_state, m_state = rest[n_cast], rest[-2], rest[-1]
    for src, dst in zip(rest[:n_cast], rest[n_cast + 1:-2]):
        dst[...] = src[...].astype(BF16)

    @pl.when(pl.program_id(1) == 0)
    def _():
        c_state[...] = jnp.zeros_like(c_state)
        m_state[...] = jnp.zeros_like(m_state)

    key = lax.broadcasted_iota(jnp.int32, (L, L), 0)
    qry = lax.broadcasted_iota(jnp.int32, (L, L), 1)
    visible = key <= qry
    ones_rows = jnp.ones((NORM_ROWS, L), BF16)

    units = [(cc, h) for cc in range(TM_MLSTM // L) for h in range(MLSTM_HEADS)]
    rows = lambda cc: slice(cc * L, (cc + 1) * L)
    cols = lambda h: slice(h * D, (h + 1) * D)

    s_t = {u: lax.dot_general(k_ref[rows(u[0]), cols(u[1])], q_ref[rows(u[0]), cols(u[1])], nt_dims,
                              preferred_element_type=F32) for u in units}
    b_row, g_tot, pm, vt_aug, intra, m_loc, kv = {}, {}, {}, {}, {}, {}, {}
    for u in units:
        cc, h = u
        fh = MLSTM_HEADS + h
        r_col = g_ref[rows(cc), h:h + 1] - g_ref[rows(cc), fh:fh + 1]
        b_row[u] = gt_ref[cc, fh:fh + 1, :]
        g_tot[u] = b_row[u][:, L - 1:L]
        r_vis = jnp.where(visible, r_col, NEG)
        pm[u] = jnp.max(r_vis, axis=0, keepdims=True)
        p_t = (jnp.exp(r_vis - pm[u]) * s_t[u]).astype(BF16)
        vt_aug[u] = jnp.concatenate([vt_ref[cc, cols(h), :], ones_rows], axis=0)
        intra[u] = jnp.dot(vt_aug[u], p_t, preferred_element_type=F32)
    for u in units:
        cc, h = u
        r_row = gt_ref[cc, h:h + 1, :] - b_row[u]
        m_loc[u] = jnp.max(g_tot[u] + r_row, axis=-1, keepdims=True)
        vw = (vt_aug[u].astype(F32) * jnp.exp(g_tot[u] + r_row - m_loc[u])).astype(BF16)
        kv[u] = jnp.dot(vw, k_ref[rows(cc), cols(h)], preferred_element_type=F32)

    c_aug = [c_state[h] for h in range(MLSTM_HEADS)]
    m_prev = [m_state[h:h + 1, 0:1] for h in range(MLSTM_HEADS)]
    for u in units:
        cc, h = u
        inter = lax.dot_general(c_aug[h].astype(BF16), q_ref[rows(cc), cols(h)], nt_dims,
                                preferred_element_type=F32)
        mm = jnp.maximum(pm[u], m_prev[h])
        nd = jnp.exp(pm[u] - mm) * intra[u] + jnp.exp(m_prev[h] - mm) * inter
        inv = 1.0 / jnp.maximum(jnp.abs(nd[D:D + 1, :]), jnp.exp(-(b_row[u] + mm)))
        hh = nd[0:D, :] * inv
        hn = (hh * lax.rsqrt(jnp.mean(hh * hh, axis=0, keepdims=True) + NORM_EPS)).T
        o_gate = jax.nn.sigmoid(mo_ref[rows(cc), cols(h)].astype(F32))
        y_ref[rows(cc), cols(h)] = (o_gate * hn * nw_ref[:, cols(h)]).astype(y_ref.dtype)

        m_new = jnp.maximum(g_tot[u] + m_prev[h], m_loc[u])
        c_aug[h] = jnp.exp(g_tot[u] + m_prev[h] - m_new) * c_aug[h] + jnp.exp(m_loc[u] - m_new) * kv[u]
        m_prev[h] = m_new
    for h in range(MLSTM_HEADS):
        c_state[h] = c_aug[h]
        m_state[h:h + 1, :] = jnp.broadcast_to(m_prev[h], (1, LANES))


def _mlstm(mqko, mvt, gates, gates_t, norm_w, batch, cast_weights, layer):
    T = mqko.shape[0]
    nt = T // batch // TM_MLSTM
    nch = TM_MLSTM // MCHUNK
    tile = lambda w: pl.BlockSpec((TM_MLSTM, w), lambda b, c: (b * nt + c, 0))
    per_chunk = lambda r: pl.BlockSpec((nch, r, MCHUNK), lambda b, c: (b * nt + c, 0, 0))
    cast_in, cast_out, cast_shape = _cast_specs(cast_weights, layer, batch * nt, lambda b, c: b * nt + c)
    out = pl.pallas_call(
        functools.partial(_mlstm_kernel, n_cast=len(cast_weights)),
        grid=(batch, nt),
        in_specs=[tile(3 * MLSTM_WIDTH), per_chunk(MLSTM_WIDTH), tile(LANES),
                  per_chunk(GATE_ROWS), pl.BlockSpec((1, MLSTM_WIDTH), lambda b, c: (0, 0))] + cast_in,
        out_specs=[tile(MLSTM_WIDTH)] + cast_out,
        out_shape=[jax.ShapeDtypeStruct((T, MLSTM_WIDTH), BF16)] + cast_shape,
        scratch_shapes=[
            pltpu.VMEM((MLSTM_HEADS, MLSTM_HEAD_DIM + NORM_ROWS, MLSTM_HEAD_DIM), F32),
            pltpu.VMEM((8, LANES), F32),
        ],
        compiler_params=pltpu.CompilerParams(dimension_semantics=("parallel", "arbitrary"),
                                             vmem_limit_bytes=VMEM_LIMIT),
        name="mlstm",
    )(mqko, mvt, gates, gates_t, norm_w, *cast_weights)
    return out[0], tuple(out[1:])


def _post_kernel(x_ref, ya_ref, yb_ref, gab_ref, wpa_ref, wpm_ref, wo_ref, nw_ref, w1_ref, w2_ref,
                 fw_ref, *rest, final_norm):
    n_cast = len(rest) // 2
    o_ref = rest[n_cast]
    for src, dst in zip(rest[:n_cast - 1], rest[n_cast + 1:-1]):
        dst[...] = src[...].astype(BF16)
    if n_cast:
        _cast_in_proj_slab(rest[n_cast - 1], rest[-1], jnp.minimum(pl.program_id(0), IN_SLABS - 1))
    ga_ref, gb_ref = (gab_ref.at[:, j * D_MODEL:(j + 1) * D_MODEL] for j in range(2))
    pa = jnp.dot(ya_ref[...], wpa_ref[...], preferred_element_type=F32)
    pb = jnp.dot(yb_ref[...], wpm_ref[...], preferred_element_type=F32)
    mixed = (jax.nn.sigmoid(ga_ref[...].astype(F32)) * pa
             + jax.nn.sigmoid(gb_ref[...].astype(F32)) * pb).astype(BF16)
    x1 = x_ref[...] + jnp.dot(mixed, wo_ref[...], preferred_element_type=F32)
    var = jnp.mean(x1 * x1, axis=-1, keepdims=True)
    h2 = (x1 * lax.rsqrt(var + NORM_EPS) * nw_ref[...]).astype(BF16)
    acc = x1
    ff_chunk = D_MODEL
    for c in range(D_FF // ff_chunk):
        cs = slice(c * ff_chunk, (c + 1) * ff_chunk)
        u = jnp.maximum(jnp.dot(h2, w1_ref[:, cs], preferred_element_type=F32), 0.0)
        acc = acc + jnp.dot((u * u).astype(BF16), w2_ref[cs, :], preferred_element_type=F32)
    if final_norm:
        var = jnp.mean(acc * acc, axis=-1, keepdims=True)
        acc = acc * lax.rsqrt(var + NORM_EPS) * fw_ref[...]
    o_ref[...] = acc


def _post(x2d, ya, yb, gab, weights, norm_w, final_w, layer, final_norm, next_weights):
    T = x2d.shape[0]
    steps = T // TM_POST
    tile = lambda w: pl.BlockSpec((TM_POST, w), lambda i: (i, 0))
    wpa, wpm, wo, w1, w2 = weights
    cast_in, cast_out, cast_shape = _cast_specs(next_weights[:-1], layer + 1, steps)
    if next_weights:
        assert steps >= IN_SLABS
        in_spec, out_spec, out_shape = _cast_in_proj_specs(layer + 1, PREP_IN_ROWS)
        cast_in, cast_out, cast_shape = cast_in + [in_spec], cast_out + [out_spec], cast_shape + [out_shape]
    out = pl.pallas_call(
        functools.partial(_post_kernel, final_norm=final_norm),
        grid=(steps,),
        in_specs=[tile(D_MODEL), tile(ATT_WIDTH), tile(MLSTM_WIDTH), tile(2 * D_MODEL),
                  _const_spec(wpa.shape), _const_spec(wpm.shape), _const_spec(wo.shape),
                  _layer_spec((1, D_MODEL), layer), _const_spec(w1.shape), _const_spec(w2.shape),
                  _const_spec((1, D_MODEL))] + cast_in,
        out_specs=[tile(D_MODEL)] + cast_out,
        out_shape=[jax.ShapeDtypeStruct((T, D_MODEL), F32)] + cast_shape,
        compiler_params=pltpu.CompilerParams(dimension_semantics=("arbitrary",),
                                             vmem_limit_bytes=VMEM_LIMIT),
        name="post",
    )(x2d, ya, yb, gab, wpa, wpm, wo, norm_w, w1, w2, final_w, *next_weights)
    return out[0], tuple(out[1:])


PREP_IN_ROWS = MXU_COLS
IN_SLABS = W_PAD // PREP_IN_ROWS
PREP_IN_ROWS_ALONE = W_PAD // 2


def _cast_in_proj_slab(w_ref, o_ref, slab):
    row = slab * w_ref.shape[0] + lax.broadcasted_iota(jnp.int32, w_ref.shape, 0)
    w = w_ref[...]
    w = jnp.where(row < COL_AK, w * (ATT_HEAD_DIM ** -0.5), w)
    o_ref[...] = jnp.where(row < D_IN, w, 0.0).T.astype(BF16)


def _cast_in_proj_specs(layer, rows):
    assert W_PAD % rows == 0 and rows % LANES == 0
    slab = lambda i: jnp.minimum(i, W_PAD // rows - 1)
    in_spec = pl.BlockSpec((None, rows, D_MODEL), lambda i: (layer, slab(i), 0))
    out_spec = pl.BlockSpec((D_MODEL, rows), lambda i: (0, slab(i)))
    return in_spec, out_spec, jax.ShapeDtypeStruct((D_MODEL, W_PAD), BF16)


def _cast_in_proj_kernel(w_ref, o_ref):
    _cast_in_proj_slab(w_ref, o_ref, pl.program_id(0))


def _cast_in_proj(w_t, layer):
    in_spec, out_spec, out_shape = _cast_in_proj_specs(layer, PREP_IN_ROWS_ALONE)
    return pl.pallas_call(
        _cast_in_proj_kernel,
        grid=(W_PAD // PREP_IN_ROWS_ALONE,),
        in_specs=[in_spec],
        out_specs=out_spec,
        out_shape=out_shape,
        compiler_params=pltpu.CompilerParams(dimension_semantics=("parallel",), vmem_limit_bytes=VMEM_LIMIT),
        name="cast_in_proj",
    )(w_t)


def _cast_specs(weights, layer, steps, step_of=lambda i: i):
    for a in weights:
        assert a.shape[1] % (steps * 16) == 0
    in_specs = [pl.BlockSpec((None, a.shape[1] // steps, a.shape[2]), lambda *g: (layer, step_of(*g), 0))
                for a in weights]
    out_specs = [pl.BlockSpec((a.shape[1] // steps, a.shape[2]), lambda *g: (step_of(*g), 0)) for a in weights]
    out_shape = [jax.ShapeDtypeStruct(a.shape[1:], BF16) for a in weights]
    return in_specs, out_specs, out_shape


def kernel(x, positions, norm_mix_w, w_in, b_in, conv_w, conv_b, mlstm_norm_w, w_proj_att, w_proj_mlstm,
           w_out, norm_mlp_w, w_ff1, w_ff2, final_norm_w):
    B, S, D = x.shape
    T = B * S
    depth = w_in.shape[0]
    assert D == D_MODEL and S % ATT_BLK == 0 and T % TM_IN == 0 and T % TM_POST == 0
    assert math.isclose(ATT_HEAD_DIM ** -0.5, 0.125)
    rope_c, rope_s = _rope_tables(positions)
    w_in_t = jnp.swapaxes(w_in, 1, 2)
    stacked = (w_proj_att, w_proj_mlstm, w_out, w_ff1, w_ff2)
    w_all = _cast_in_proj(w_in_t, 0)
    weights = None
    q_scale = jnp.where(jnp.arange(W_PAD) < COL_AK, ATT_HEAD_DIM ** -0.5, 1.0).astype(F32)
    b_all = (jnp.pad(b_in, ((0, 0), (0, W_PAD - D_IN))) * q_scale).reshape(depth, 1, W_PAD)
    x2d = x.reshape(T, D)
    final_w = final_norm_w.reshape(1, D).astype(F32)
    for l in range(depth):
        qkv, mqko, mvt, gab, gates, gates_t = _inproj(
            x2d, norm_mix_w.reshape(depth, 1, D), w_all, b_all, rope_c, rope_s,
            conv_w, conv_b.reshape(depth, 1, -1), S, l)
        ya = _attention(qkv.reshape(B, S, 3 * ATT_WIDTH))
        yb, cast = _mlstm(mqko, mvt, gates, gates_t, mlstm_norm_w[l].reshape(1, -1), B,
                          stacked if weights is None else (), l)
        weights = cast if weights is None else weights
        last = l == depth - 1
        x2d, cast = _post(x2d, ya.reshape(T, ATT_WIDTH), yb, gab, weights, norm_mlp_w.reshape(depth, 1, D),
                          final_w, l, final_norm=last, next_weights=() if last else stacked + (w_in_t,))
        if not last:
            weights, w_all = cast[:-1], cast[-1]
    return x2d.reshape(B, S, D)
```

```python
import functools
import math

import jax
import jax.numpy as jnp
from jax import lax
from jax.experimental import pallas as pl
from jax.experimental.pallas import tpu as pltpu

F32 = jnp.float32
BF16 = jnp.bfloat16

D_MODEL = 1024
ATT_HEADS = 8
ATT_HEAD_DIM = 64
ATT_WIDTH = ATT_HEADS * ATT_HEAD_DIM
ATT_SPAN = 128
DILATIONS = (1, 4, 16)
ROPE_THETA = 500000.0
ROPE_DIM = ATT_HEAD_DIM // 4
MLSTM_HEADS = 4
MLSTM_HEAD_DIM = 128
MLSTM_WIDTH = MLSTM_HEADS * MLSTM_HEAD_DIM
CONV_WIDTH = 4
D_FF = 4 * D_MODEL
NORM_EPS = 1e-6

COL_AQ = 0
COL_AK = COL_AQ + ATT_WIDTH
COL_AV = COL_AK + ATT_WIDTH
COL_MQ = COL_AV + ATT_WIDTH
COL_MK = COL_MQ + MLSTM_WIDTH
COL_MV = COL_MK + MLSTM_WIDTH
COL_MO = COL_MV + MLSTM_WIDTH
COL_MI = COL_MO + MLSTM_WIDTH
COL_MF = COL_MI + MLSTM_HEADS
COL_GA = COL_MF + MLSTM_HEADS
COL_GB = COL_GA + D_MODEL
D_IN = COL_GB + D_MODEL

LANES = 128
assert COL_MI % LANES == 0
MXU_COLS = 256
W_PAD = -(-D_IN // MXU_COLS) * MXU_COLS
GATE_SHIFT = 2 * MLSTM_HEADS

TM_IN = 512
TM_POST = 512
ATT_BLK = 2048
MCHUNK = 128
TM_MLSTM = 2048
VMEM_LIMIT = 56 * 1024 * 1024
NEG = -1e30
LOG2E = math.log2(math.e)


def _const_spec(shape):
    nd = len(shape)
    return pl.BlockSpec(shape, lambda *_: (0,) * nd, pipeline_mode=pl.Buffered(1))


ROPE_HALF = ROPE_DIM // 2
ROPE_PACK = LANES // ROPE_HALF
ROPE_ROWS = 128


def _rope_table_kernel(pos_ref, invf_ref, c_ref, s_ref):
    ang = pos_ref[...] * invf_ref[...]
    cosx = jnp.cos(ang)
    sinx = jnp.sin(ang)
    dst = lax.broadcasted_iota(jnp.int32, (ROPE_ROWS, LANES), 1)
    in_head = dst % ATT_HEAD_DIM
    rotary = in_head < ROPE_DIM
    sign = jnp.where(in_head < ROPE_HALF, -1.0, 1.0)
    for r in range(ROPE_PACK):
        src = r * ROPE_HALF + dst % ROPE_HALF
        c_ref[pl.ds(r, ROPE_ROWS, stride=ROPE_PACK), :] = jnp.where(
            rotary, jnp.take_along_axis(cosx, src, axis=1), 1.0)
        s_ref[pl.ds(r, ROPE_ROWS, stride=ROPE_PACK), :] = jnp.where(
            rotary, jnp.take_along_axis(sinx, src, axis=1) * sign, 0.0)


def _rope_tables(positions):
    T = positions.size
    assert T % (ROPE_PACK * ROPE_ROWS) == 0
    pos = jnp.repeat(positions.astype(F32).reshape(T // ROPE_PACK, ROPE_PACK), ROPE_HALF, axis=1)
    inv_freq = ROPE_THETA ** (-jnp.arange(0, ROPE_DIM, 2, dtype=F32) / ROPE_DIM)
    invf = jnp.tile(inv_freq, ROPE_PACK).reshape(1, LANES)
    table = jax.ShapeDtypeStruct((T, LANES), F32)
    out_spec = pl.BlockSpec((ROPE_PACK * ROPE_ROWS, LANES), lambda i: (i, 0))
    return pl.pallas_call(
        _rope_table_kernel,
        grid=(T // (ROPE_PACK * ROPE_ROWS),),
        in_specs=[pl.BlockSpec((ROPE_ROWS, LANES), lambda i: (i, 0)), pl.BlockSpec((1, LANES), lambda i: (0, 0))],
        out_specs=(out_spec, out_spec),
        out_shape=(table, table),
        compiler_params=pltpu.CompilerParams(dimension_semantics=("parallel",)),
        name="rope_tables",
    )(pos, invf)


CONV_TAIL = 8
CONV_SPARE = 8
GATE_ROWS = 8


def _log_sigmoid(x):
    return jnp.minimum(x, 0.0) - jnp.log(1.0 + jnp.exp(-jnp.abs(x)))


def _inproj_kernel(x_ref, nw_ref, w_ref, b_ref, c_ref, s_ref, cw_ref, cb_ref,
                   qkv_ref, mqko_ref, mv_ref, gab_ref, gt_ref, gtt_ref,
                   u_s, res_s, *, tiles_per_seq):
    aq_ref, ak_ref, av_ref = (qkv_ref.at[:, j * ATT_WIDTH:(j + 1) * ATT_WIDTH] for j in range(3))
    mq_ref, mk_ref, mo_ref = (mqko_ref.at[:, j * MLSTM_WIDTH:(j + 1) * MLSTM_WIDTH] for j in range(3))
    ga_ref, gb_ref = (gab_ref.at[:, j * D_MODEL:(j + 1) * D_MODEL] for j in range(2))
    @pl.when(pl.program_id(0) % tiles_per_seq == 0)
    def _():
        u_s[:, TM_IN:TM_IN + CONV_TAIL, :] = jnp.zeros((u_s.shape[0], CONV_TAIL, LANES), F32)

    x = x_ref[...]
    var = jnp.mean(x * x, axis=-1, keepdims=True)
    h = (x * lax.rsqrt(var + NORM_EPS) * nw_ref[...]).astype(BF16)

    def proj(lo, width):
        return jnp.dot(h, w_ref[:, lo:lo + width], preferred_element_type=F32) + b_ref[:, lo:lo + width]

    cos = c_ref[...]
    sin = s_ref[...]
    lane = lax.broadcasted_iota(jnp.int32, cos.shape, 1)
    first_half = (lane % ATT_HEAD_DIM) < (ROPE_DIM // 2)

    def rope_store(dst_ref, lo, scale=None):
        zz = proj(lo, ATT_WIDTH)
        for j in range(ATT_WIDTH // LANES):
            z = zz[:, j * LANES:(j + 1) * LANES]
            partner = jnp.where(first_half,
                                pltpu.roll(z, LANES - ROPE_DIM // 2, axis=1),
                                pltpu.roll(z, ROPE_DIM // 2, axis=1))
            out = z * cos + partner * sin
            dst_ref[:, j * LANES:(j + 1) * LANES] = out if scale is None else out * scale
        return zz

    def stage_conv_input(slab0, lo):
        z = proj(lo, MLSTM_WIDTH)
        for j in range(MLSTM_WIDTH // LANES):
            sl = slab0 + j
            u_s[sl, 0:CONV_TAIL, :] = u_s[sl, TM_IN:TM_IN + CONV_TAIL, :]
            u_s[sl, CONV_TAIL:CONV_TAIL + TM_IN, :] = z[:, j * LANES:(j + 1) * LANES]

    def conv_silu_store(dst_ref, slab0, col0, j, scale, after):
        sl = slab0 + j
        half = TM_IN // 2
        ws = slice(col0 + j * LANES, col0 + (j + 1) * LANES)
        spare0 = TM_IN + CONV_TAIL
        u_s[sl, spare0:spare0 + CONV_SPARE, :] = after[TM_IN - CONV_SPARE:TM_IN, 0:LANES]
        traced_zero = jnp.minimum(pl.program_id(0), 0)
        for parity in range(2):
            out = cb_ref[:, ws]
            for t in range(CONV_WIDTH):
                r0 = CONV_TAIL - (CONV_WIDTH - 1) + t + parity + traced_zero
                out = out + cw_ref[t:t + 1, ws] * u_s[sl, pl.ds(r0, half, stride=2), :]
            out = out * jax.nn.sigmoid(out)
            res_s[sl, pl.ds(parity, half, stride=2), :] = out if scale is None else out * scale
        dst_ref[:, j * LANES:(j + 1) * LANES] = res_s[sl].astype(dst_ref.dtype)

    q_scale = MLSTM_HEAD_DIM ** -0.5
    k_slab0 = MLSTM_WIDTH // LANES
    stage_conv_input(0, COL_MQ)
    stage_conv_input(k_slab0, COL_MK)
    z_aq = rope_store(aq_ref, COL_AQ, LOG2E)
    conv_silu_store(mq_ref, 0, 0, 0, q_scale, z_aq)
    z_ak = rope_store(ak_ref, COL_AK)
    conv_silu_store(mq_ref, 0, 0, 1, q_scale, z_ak)
    z_av = proj(COL_AV, ATT_WIDTH)
    av_ref[...] = z_av
    conv_silu_store(mq_ref, 0, 0, 2, q_scale, z_av)
    z_mv = proj(COL_MV, MLSTM_WIDTH)
    for cc in range(TM_IN // MCHUNK):
        for hd in range(MLSTM_HEADS):
            blk = z_mv[cc * MCHUNK:(cc + 1) * MCHUNK, hd * MLSTM_HEAD_DIM:(hd + 1) * MLSTM_HEAD_DIM]
            mv_ref[cc, hd * MLSTM_HEAD_DIM:(hd + 1) * MLSTM_HEAD_DIM, :] = blk.T.astype(mv_ref.dtype)
    conv_silu_store(mq_ref, 0, 0, 3, q_scale, z_mv)
    z_mo = proj(COL_MO, MLSTM_WIDTH)
    mo_ref[...] = z_mo.astype(mo_ref.dtype)
    conv_silu_store(mk_ref, k_slab0, MLSTM_WIDTH, 0, None, z_mo)

    z_tail = proj(COL_MI, W_PAD - COL_MI)
    tiles = [z_tail[:, j * LANES:(j + 1) * LANES] for j in range((W_PAD - COL_MI) // LANES)]
    conv_silu_store(mk_ref, k_slab0, MLSTM_WIDTH, 1, None, tiles[4])
    conv_silu_store(mk_ref, k_slab0, MLSTM_WIDTH, 2, None, tiles[9])
    conv_silu_store(mk_ref, k_slab0, MLSTM_WIDTH, 3, None, tiles[14])
    rolled = [pltpu.roll(t, LANES - GATE_SHIFT, axis=1) for t in tiles]
    low_lanes = lane < LANES - GATE_SHIFT
    for j in range(D_MODEL // LANES):
        ga_ref[:, j * LANES:(j + 1) * LANES] = jnp.where(low_lanes, rolled[j], rolled[j + 1]).astype(ga_ref.dtype)
        k = j + D_MODEL // LANES
        gb_ref[:, j * LANES:(j + 1) * LANES] = jnp.where(low_lanes, rolled[k], rolled[k + 1]).astype(gb_ref.dtype)

    zg = jnp.where(lane < GATE_SHIFT, tiles[0], 0.0)
    logf = _log_sigmoid(zg)
    ri = lax.broadcasted_iota(jnp.int32, (MCHUNK, MCHUNK), 0)
    ci = lax.broadcasted_iota(jnp.int32, (MCHUNK, MCHUNK), 1)
    tri = (ci <= ri).astype(F32)
    is_input_gate = lax.broadcasted_iota(jnp.int32, (MCHUNK, LANES), 1) < MLSTM_HEADS
    for cc in range(TM_IN // MCHUNK):
        rows = slice(cc * MCHUNK, (cc + 1) * MCHUNK)
        bcum = jnp.dot(tri, logf[rows], precision=lax.Precision.HIGHEST, preferred_element_type=F32)
        gc = jnp.where(is_input_gate, zg[rows], bcum) * LOG2E
        gt_ref[rows, :] = gc
        gtt_ref[cc] = gc.T[0:GATE_ROWS, :]


def _layer_spec(shape, layer):
    nd = len(shape)
    return pl.BlockSpec((None,) + tuple(shape), lambda *_: (layer,) + (0,) * nd, pipeline_mode=pl.Buffered(1))


def _inproj(x2d, norm_w, w_all, b_all, rope_c, rope_s, conv_w, conv_b, seq_len, layer):
    T = x2d.shape[0]
    tile = lambda w: pl.BlockSpec((TM_IN, w), lambda i: (i, 0))
    out_shapes = (
        jax.ShapeDtypeStruct((T, 3 * ATT_WIDTH), F32),
        jax.ShapeDtypeStruct((T, 3 * MLSTM_WIDTH), BF16),
        jax.ShapeDtypeStruct((T // MCHUNK, MLSTM_WIDTH, MCHUNK), BF16),
        jax.ShapeDtypeStruct((T, 2 * D_MODEL), BF16),
        jax.ShapeDtypeStruct((T, LANES), F32),
        jax.ShapeDtypeStruct((T // MCHUNK, GATE_ROWS, MCHUNK), F32),
    )
    per_chunk = lambda rows: pl.BlockSpec((TM_IN // MCHUNK, rows, MCHUNK), lambda i: (i, 0, 0))
    out_specs = tuple(tile(s.shape[1]) if len(s.shape) == 2 else per_chunk(s.shape[1]) for s in out_shapes)
    assert seq_len % TM_IN == 0
    return pl.pallas_call(
        functools.partial(_inproj_kernel, tiles_per_seq=seq_len // TM_IN),
        grid=(T // TM_IN,),
        in_specs=[tile(D_MODEL), _layer_spec((1, D_MODEL), layer), _const_spec((D_MODEL, W_PAD)),
                  _layer_spec((1, W_PAD), layer), tile(LANES), tile(LANES),
                  _layer_spec((CONV_WIDTH, 2 * MLSTM_WIDTH), layer), _layer_spec((1, 2 * MLSTM_WIDTH), layer)],
        out_specs=out_specs,
        out_shape=out_shapes,
        scratch_shapes=[pltpu.VMEM((2 * MLSTM_WIDTH // LANES, TM_IN + CONV_TAIL + CONV_SPARE, LANES), F32),
                        pltpu.VMEM((2 * MLSTM_WIDTH // LANES, TM_IN, LANES), F32)],
        compiler_params=pltpu.CompilerParams(dimension_semantics=("arbitrary",),
                                             vmem_limit_bytes=VMEM_LIMIT),
        name="inproj",
    )(x2d, norm_w, w_all, b_all, rope_c, rope_s, conv_w, conv_b)


ATT_UNROLL = 16
PITCH16 = ATT_SPAN + 8


def _attn_kernel(q_ref, k_ref, v_ref, o_ref,
                 kd1, vd1, qd4, kd4, vd4, qd16, kd16, vd16, tmp, st4_s, st16_s, bias_s):
    j = pl.program_id(2)
    blk = ATT_SPAN
    nsub = {d: ATT_BLK // d // blk for d in DILATIONS}
    kv_bufs = ((1, kd1, vd1), (4, kd4, vd4), (16, kd16, vd16))

    row = lax.broadcasted_iota(jnp.int32, (2 * blk, 2 * blk), 0) % blk
    col = lax.broadcasted_iota(jnp.int32, (2 * blk, 2 * blk), 1)
    band = (col >= row) & (col <= row + ATT_SPAN)
    bias_s[0] = jnp.where(band, 0.0, NEG)
    bias_s[1] = jnp.where(band & (col >= blk), 0.0, NEG)

    @pl.when(j == 0)
    def _():
        for d, kd, vd in kv_bufs:
            n = ATT_BLK // d
            for r in range(d):
                base = r * (n + blk)
                kd[base:base + blk] = jnp.zeros((blk, LANES), BF16)
                vd[base:base + blk] = jnp.zeros((blk, LANES), BF16)

    @pl.when(j != 0)
    def _():
        for d, kd, vd in kv_bufs:
            n = ATT_BLK // d
            for r in range(d):
                base = r * (n + blk)
                kd[base:base + blk] = kd[base + n:base + n + blk]
                vd[base:base + blk] = vd[base + n:base + n + blk]

    def deinterleave(src_ref, dst1, dst4, dst16, is_kv):
        pad = blk if is_kv else 0
        if dst1 is not None:
            dst1[blk:blk + ATT_BLK] = src_ref[0].astype(BF16)
        n4 = ATT_BLK // 4
        for r4 in range(4):
            t4 = src_ref[0, pl.ds(r4, n4, stride=4), :]
            tmp[r4] = t4
            o4 = r4 * (n4 + pad) + pad
            dst4[o4:o4 + n4] = t4.astype(BF16)
        n16 = ATT_BLK // 16
        for r4 in range(4):
            for rr in range(4):
                o16 = (4 * rr + r4) * (n16 + pad) + pad
                dst16[o16:o16 + n16] = tmp[r4, pl.ds(rr, n16, stride=4), :].astype(BF16)

    deinterleave(q_ref, None, qd4, qd16, False)
    deinterleave(k_ref, kd1, kd4, kd16, True)
    deinterleave(v_ref, vd1, vd4, vd16, True)

    head_a = lax.broadcasted_iota(jnp.int32, (blk, LANES), 1) < ATT_HEAD_DIM

    def unit(q2, k2, v2, bias):
        zero = jnp.zeros_like(q2)
        qs = jnp.concatenate([jnp.where(head_a, q2, zero), jnp.where(head_a, zero, q2)], axis=0)
        s = lax.dot_general(qs, k2, (((1,), (1,)), ((), ())), preferred_element_type=F32) + bias
        m = jnp.max(s, axis=-1, keepdims=True)
        p = jnp.exp2(s - m).astype(BF16)
        v_aug = jnp.concatenate([v2, jnp.ones_like(v2)], axis=1)
        pv = jnp.dot(p, v_aug, preferred_element_type=F32)
        acc = jnp.where(head_a, pv[:blk, :LANES], pv[blk:, :LANES])
        ll = jnp.where(head_a, pv[:blk, LANES:], pv[blk:, LANES:])
        mm = jnp.where(head_a, m[:blk], m[blk:])
        return acc, mm, ll

    first_blk = jnp.where(j == 0, 1, 0)

    def body16(u, carry):
        k0 = pl.multiple_of(u * (2 * blk), blk)
        bias = bias_s[first_blk]
        res = unit(qd16[pl.ds(pl.multiple_of(u * blk, blk), blk), :], kd16[pl.ds(k0, 2 * blk), :],
                   vd16[pl.ds(k0, 2 * blk), :], bias)
        row0 = pl.multiple_of(u * PITCH16, 8)
        for a, val in enumerate(res):
            st16_s[a, pl.ds(row0, blk), :] = val
        return carry

    def body4(u, carry):
        r = u // nsub[4]
        sb = u % nsub[4]
        k0 = pl.multiple_of(r * (ATT_BLK // 4 + blk) + sb * blk, blk)
        bias = bias_s[jnp.where(sb == 0, first_blk, 0)]
        res = unit(qd4[pl.ds(pl.multiple_of(u * blk, blk), blk), :], kd4[pl.ds(k0, 2 * blk), :],
                   vd4[pl.ds(k0, 2 * blk), :], bias)
        t0 = sb * (blk * 4) + r
        for a, val in enumerate(res):
            st4_s[a, pl.ds(t0, blk, stride=4), :] = val
        return carry

    lax.fori_loop(0, ATT_BLK // blk, body16, 0, unroll=ATT_UNROLL)
    lax.fori_loop(0, ATT_BLK // blk, body4, 0, unroll=ATT_UNROLL)

    def body1(u, carry):
        r0 = pl.multiple_of(u * blk, blk)
        sl = pl.ds(r0, blk)
        q2 = q_ref[0, sl, :].astype(BF16)
        bias = bias_s[jnp.where(u == 0, first_blk, 0)]
        acc2, m2, l2 = unit(q2, kd1[pl.ds(r0, 2 * blk), :], vd1[pl.ds(r0, 2 * blk), :], bias)
        per_res = blk // 16
        acc1, m1, l1 = (jnp.concatenate([st16_s[a, pl.ds(per_res * u + k, 16, stride=PITCH16), :]
                                         for k in range(per_res)], axis=0) for a in range(3))
        acc0, m0, l0 = st4_s[0, sl, :], st4_s[1, sl, :], st4_s[2, sl, :]
        mx = jnp.maximum(jnp.maximum(m0, m1), m2)
        w0, w1, w2 = jnp.exp2(m0 - mx), jnp.exp2(m1 - mx), jnp.exp2(m2 - mx)
        num = w0 * acc0 + w1 * acc1 + w2 * acc2
        den = w0 * l0 + w1 * l1 + w2 * l2
        o_ref[0, sl, :] = (num / den).astype(o_ref.dtype)
        return carry

    lax.fori_loop(0, nsub[1], body1, 0, unroll=ATT_UNROLL)


def _attention(qkv):
    B, S, _ = qkv.shape
    blk = ATT_SPAN
    pairs = ATT_WIDTH // LANES
    cur = pl.BlockSpec((1, ATT_BLK, LANES), lambda b, hp, j: (b, j, hp))
    part = lambda p: pl.BlockSpec((1, ATT_BLK, LANES), lambda b, hp, j: (b, j, p * pairs + hp))
    kv_rows = {d: d * (ATT_BLK // d + blk) for d in DILATIONS}
    scratch = [
        pltpu.VMEM((kv_rows[1], LANES), BF16), pltpu.VMEM((kv_rows[1], LANES), BF16),
        pltpu.VMEM((ATT_BLK, LANES), BF16),
        pltpu.VMEM((kv_rows[4], LANES), BF16), pltpu.VMEM((kv_rows[4], LANES), BF16),
        pltpu.VMEM((ATT_BLK, LANES), BF16),
        pltpu.VMEM((kv_rows[16], LANES), BF16), pltpu.VMEM((kv_rows[16], LANES), BF16),
        pltpu.VMEM((4, ATT_BLK // 4, LANES), F32),
        pltpu.VMEM((3, ATT_BLK, LANES), F32),
        pltpu.VMEM((3, 16 * PITCH16, LANES), F32),
        pltpu.VMEM((2, 2 * blk, 2 * blk), F32),
    ]
    return pl.pallas_call(
        _attn_kernel,
        grid=(B, pairs, S // ATT_BLK),
        in_specs=[part(0), part(1), part(2)],
        out_specs=cur,
        out_shape=jax.ShapeDtypeStruct((B, S, ATT_WIDTH), BF16),
        scratch_shapes=scratch,
        compiler_params=pltpu.CompilerParams(dimension_semantics=("parallel", "parallel", "arbitrary"),
                                             vmem_limit_bytes=VMEM_LIMIT),
        name="dilated_attention",
    )(qkv, qkv, qkv)


NORM_ROWS = 16


def _mlstm_kernel(qko_ref, vt_ref, g_ref, gt_ref, nw_ref, *rest, n_cast):
    L = MCHUNK
    D = MLSTM_HEAD_DIM
    nt_dims = (((1,), (1,)), ((), ()))
    q_ref, k_ref, mo_ref = (qko_ref.at[:, j * MLSTM_WIDTH:(j + 1) * MLSTM_WIDTH] for j in range(3))
    y_ref, c_state, m_state = rest[n_cast], rest[-2], rest[-1]
    for src, dst in zip(rest[:n_cast], rest[n_cast + 1:-2]):
        dst[...] = src[...].astype(BF16)

    @pl.when(pl.program_id(1) == 0)
    def _():
        c_state[...] = jnp.zeros_like(c_state)
        m_state[...] = jnp.zeros_like(m_state)

    key = lax.broadcasted_iota(jnp.int32, (L, L), 0)
    qry = lax.broadcasted_iota(jnp.int32, (L, L), 1)
    visible = key <= qry
    ones_rows = jnp.ones((NORM_ROWS, L), BF16)

    units = [(cc, h) for cc in range(TM_MLSTM // L) for h in range(MLSTM_HEADS)]
    rows = lambda cc: slice(cc * L, (cc + 1) * L)
    cols = lambda h: slice(h * D, (h + 1) * D)

    s_t = {u: lax.dot_general(k_ref[rows(u[0]), cols(u[1])], q_ref[rows(u[0]), cols(u[1])], nt_dims,
                              preferred_element_type=F32) for u in units}
    b_row, g_tot, pm, vt_aug, intra, m_loc, kv = {}, {}, {}, {}, {}, {}, {}
    for u in units:
        cc, h = u
        fh = MLSTM_HEADS + h
        r_col = g_ref[rows(cc), h:h + 1] - g_ref[rows(cc), fh:fh + 1]
        b_row[u] = gt_ref[cc, fh:fh + 1, :]
        g_tot[u] = b_row[u][:, L - 1:L]
        r_vis = jnp.where(visible, r_col, NEG)
        pm[u] = jnp.max(r_vis, axis=0, keepdims=True)
        p_t = (jnp.exp2(r_vis - pm[u]) * s_t[u]).astype(BF16)
        vt_aug[u] = jnp.concatenate([vt_ref[cc, cols(h), :], ones_rows], axis=0)
        intra[u] = jnp.dot(vt_aug[u], p_t, preferred_element_type=F32)
    for u in units:
        cc, h = u
        r_row = gt_ref[cc, h:h + 1, :] - b_row[u]
        m_loc[u] = jnp.max(g_tot[u] + r_row, axis=-1, keepdims=True)
        vw = (vt_aug[u].astype(F32) * jnp.exp2(g_tot[u] + r_row - m_loc[u])).astype(BF16)
        kv[u] = jnp.dot(vw, k_ref[rows(cc), cols(h)], preferred_element_type=F32)

    c_aug = [c_state[h] for h in range(MLSTM_HEADS)]
    m_prev = [m_state[h:h + 1, 0:1] for h in range(MLSTM_HEADS)]
    for u in units:
        cc, h = u
        inter = lax.dot_general(c_aug[h].astype(BF16), q_ref[rows(cc), cols(h)], nt_dims,
                                preferred_element_type=F32)
        mm = jnp.maximum(pm[u], m_prev[h])
        nd = jnp.exp2(pm[u] - mm) * intra[u] + jnp.exp2(m_prev[h] - mm) * inter
        inv = 1.0 / jnp.maximum(jnp.abs(nd[D:D + 1, :]), jnp.exp2(-(b_row[u] + mm)))
        hh = nd[0:D, :] * inv
        hn = (hh * lax.rsqrt(jnp.mean(hh * hh, axis=0, keepdims=True) + NORM_EPS)).T
        o_gate = jax.nn.sigmoid(mo_ref[rows(cc), cols(h)].astype(F32))
        y_ref[rows(cc), cols(h)] = (o_gate * hn * nw_ref[:, cols(h)]).astype(y_ref.dtype)

        m_new = jnp.maximum(g_tot[u] + m_prev[h], m_loc[u])
        c_aug[h] = jnp.exp2(g_tot[u] + m_prev[h] - m_new) * c_aug[h] + jnp.exp2(m_loc[u] - m_new) * kv[u]
        m_prev[h] = m_new
    for h in range(MLSTM_HEADS):
        c_state[h] = c_aug[h]
        m_state[h:h + 1, :] = jnp.broadcast_to(m_prev[h], (1, LANES))


def _mlstm(mqko, mvt, gates, gates_t, norm_w, batch, cast_weights, layer):
    T = mqko.shape[0]
    nt = T // batch // TM_MLSTM
    nch = TM_MLSTM // MCHUNK
    tile = lambda w: pl.BlockSpec((TM_MLSTM, w), lambda b, c: (b * nt + c, 0))
    per_chunk = lambda r: pl.BlockSpec((nch, r, MCHUNK), lambda b, c: (b * nt + c, 0, 0))
    cast_in, cast_out, cast_shape = _cast_specs(cast_weights, layer, batch * nt, lambda b, c: b * nt + c)
    out = pl.pallas_call(
        functools.partial(_mlstm_kernel, n_cast=len(cast_weights)),
        grid=(batch, nt),
        in_specs=[tile(3 * MLSTM_WIDTH), per_chunk(MLSTM_WIDTH), tile(LANES),
                  per_chunk(GATE_ROWS), pl.BlockSpec((1, MLSTM_WIDTH), lambda b, c: (0, 0))] + cast_in,
        out_specs=[tile(MLSTM_WIDTH)] + cast_out,
        out_shape=[jax.ShapeDtypeStruct((T, MLSTM_WIDTH), BF16)] + cast_shape,
        scratch_shapes=[
            pltpu.VMEM((MLSTM_HEADS, MLSTM_HEAD_DIM + NORM_ROWS, MLSTM_HEAD_DIM), F32),
            pltpu.VMEM((8, LANES), F32),
        ],
        compiler_params=pltpu.CompilerParams(dimension_semantics=("parallel", "arbitrary"),
                                             vmem_limit_bytes=VMEM_LIMIT),
        name="mlstm",
    )(mqko, mvt, gates, gates_t, norm_w, *cast_weights)
    return out[0], tuple(out[1:])


def _post_kernel(x_ref, ya_ref, yb_ref, gab_ref, wpa_ref, wpm_ref, wo_ref, nw_ref, w1_ref, w2_ref,
                 fw_ref, *rest, final_norm):
    n_cast = len(rest) // 2
    o_ref = rest[n_cast]
    for src, dst in zip(rest[:n_cast - 1], rest[n_cast + 1:-1]):
        dst[...] = src[...].astype(BF16)
    if n_cast:
        _cast_in_proj_slab(rest[n_cast - 1], rest[-1], jnp.minimum(pl.program_id(0), IN_SLABS - 1))
    ga_ref, gb_ref = (gab_ref.at[:, j * D_MODEL:(j + 1) * D_MODEL] for j in range(2))
    pa = jnp.dot(ya_ref[...], wpa_ref[...], preferred_element_type=F32)
    pb = jnp.dot(yb_ref[...], wpm_ref[...], preferred_element_type=F32)
    mixed = (jax.nn.sigmoid(ga_ref[...].astype(F32)) * pa
             + jax.nn.sigmoid(gb_ref[...].astype(F32)) * pb).astype(BF16)
    x1 = x_ref[...] + jnp.dot(mixed, wo_ref[...], preferred_element_type=F32)
    var = jnp.mean(x1 * x1, axis=-1, keepdims=True)
    h2 = (x1 * lax.rsqrt(var + NORM_EPS) * nw_ref[...]).astype(BF16)
    acc = x1
    ff_chunk = D_MODEL
    for c in range(D_FF // ff_chunk):
        cs = slice(c * ff_chunk, (c + 1) * ff_chunk)
        u = jnp.maximum(jnp.dot(h2, w1_ref[:, cs], preferred_element_type=F32), 0.0)
        acc = acc + jnp.dot((u * u).astype(BF16), w2_ref[cs, :], preferred_element_type=F32)
    if final_norm:
        var = jnp.mean(acc * acc, axis=-1, keepdims=True)
        acc = acc * lax.rsqrt(var + NORM_EPS) * fw_ref[...]
    o_ref[...] = acc


def _post(x2d, ya, yb, gab, weights, norm_w, final_w, layer, final_norm, next_weights):
    T = x2d.shape[0]
    steps = T // TM_POST
    tile = lambda w: pl.BlockSpec((TM_POST, w), lambda i: (i, 0))
    wpa, wpm, wo, w1, w2 = weights
    cast_in, cast_out, cast_shape = _cast_specs(next_weights[:-1], layer + 1, steps)
    if next_weights:
        assert steps >= IN_SLABS
        in_spec, out_spec, out_shape = _cast_in_proj_specs(layer + 1, PREP_IN_ROWS)
        cast_in, cast_out, cast_shape = cast_in + [in_spec], cast_out + [out_spec], cast_shape + [out_shape]
    out = pl.pallas_call(
        functools.partial(_post_kernel, final_norm=final_norm),
        grid=(steps,),
        in_specs=[tile(D_MODEL), tile(ATT_WIDTH), tile(MLSTM_WIDTH), tile(2 * D_MODEL),
                  _const_spec(wpa.shape), _const_spec(wpm.shape), _const_spec(wo.shape),
                  _layer_spec((1, D_MODEL), layer), _const_spec(w1.shape), _const_spec(w2.shape),
                  _const_spec((1, D_MODEL))] + cast_in,
        out_specs=[tile(D_MODEL)] + cast_out,
        out_shape=[jax.ShapeDtypeStruct((T, D_MODEL), F32)] + cast_shape,
        compiler_params=pltpu.CompilerParams(dimension_semantics=("arbitrary",),
                                             vmem_limit_bytes=VMEM_LIMIT),
        name="post",
    )(x2d, ya, yb, gab, wpa, wpm, wo, norm_w, w1, w2, final_w, *next_weights)
    return out[0], tuple(out[1:])


PREP_IN_ROWS = MXU_COLS
IN_SLABS = W_PAD // PREP_IN_ROWS
PREP_IN_ROWS_ALONE = W_PAD // 2


def _cast_in_proj_slab(w_ref, o_ref, slab):
    row = slab * w_ref.shape[0] + lax.broadcasted_iota(jnp.int32, w_ref.shape, 0)
    w = w_ref[...]
    w = jnp.where(row < COL_AK, w * (ATT_HEAD_DIM ** -0.5), w)
    o_ref[...] = jnp.where(row < D_IN, w, 0.0).T.astype(BF16)


def _cast_in_proj_specs(layer, rows):
    assert W_PAD % rows == 0 and rows % LANES == 0
    slab = lambda i: jnp.minimum(i, W_PAD // rows - 1)
    in_spec = pl.BlockSpec((None, rows, D_MODEL), lambda i: (layer, slab(i), 0))
    out_spec = pl.BlockSpec((D_MODEL, rows), lambda i: (0, slab(i)))
    return in_spec, out_spec, jax.ShapeDtypeStruct((D_MODEL, W_PAD), BF16)


def _cast_in_proj_kernel(w_ref, o_ref):
    _cast_in_proj_slab(w_ref, o_ref, pl.program_id(0))


def _cast_in_proj(w_t, layer):
    in_spec, out_spec, out_shape = _cast_in_proj_specs(layer, PREP_IN_ROWS_ALONE)
    return pl.pallas_call(
        _cast_in_proj_kernel,
        grid=(W_PAD // PREP_IN_ROWS_ALONE,),
        in_specs=[in_spec],
        out_specs=out_spec,
        out_shape=out_shape,
        compiler_params=pltpu.CompilerParams(dimension_semantics=("parallel",), vmem_limit_bytes=VMEM_LIMIT),
        name="cast_in_proj",
    )(w_t)


def _cast_specs(weights, layer, steps, step_of=lambda i: i):
    for a in weights:
        assert a.shape[1] % (steps * 16) == 0
    in_specs = [pl.BlockSpec((None, a.shape[1] // steps, a.shape[2]), lambda *g: (layer, step_of(*g), 0))
                for a in weights]
    out_specs = [pl.BlockSpec((a.shape[1] // steps, a.shape[2]), lambda *g: (step_of(*g), 0)) for a in weights]
    out_shape = [jax.ShapeDtypeStruct(a.shape[1:], BF16) for a in weights]
    return in_specs, out_specs, out_shape


def kernel(x, positions, norm_mix_w, w_in, b_in, conv_w, conv_b, mlstm_norm_w, w_proj_att, w_proj_mlstm,
           w_out, norm_mlp_w, w_ff1, w_ff2, final_norm_w):
    B, S, D = x.shape
    T = B * S
    depth = w_in.shape[0]
    assert D == D_MODEL and S % ATT_BLK == 0 and T % TM_IN == 0 and T % TM_POST == 0
    assert math.isclose(ATT_HEAD_DIM ** -0.5, 0.125)
    rope_c, rope_s = _rope_tables(positions)
    w_in_t = jnp.swapaxes(w_in, 1, 2)
    stacked = (w_proj_att, w_proj_mlstm, w_out, w_ff1, w_ff2)
    w_all = _cast_in_proj(w_in_t, 0)
    weights = None
    q_scale = jnp.where(jnp.arange(W_PAD) < COL_AK, ATT_HEAD_DIM ** -0.5, 1.0).astype(F32)
    b_all = (jnp.pad(b_in, ((0, 0), (0, W_PAD - D_IN))) * q_scale).reshape(depth, 1, W_PAD)
    x2d = x.reshape(T, D)
    final_w = final_norm_w.reshape(1, D).astype(F32)
    for l in range(depth):
        qkv, mqko, mvt, gab, gates, gates_t = _inproj(
            x2d, norm_mix_w.reshape(depth, 1, D), w_all, b_all, rope_c, rope_s,
            conv_w, conv_b.reshape(depth, 1, -1), S, l)
        ya = _attention(qkv.reshape(B, S, 3 * ATT_WIDTH))
        yb, cast = _mlstm(mqko, mvt, gates, gates_t, mlstm_norm_w[l].reshape(1, -1), B,
                          stacked if weights is None else (), l)
        weights = cast if weights is None else weights
        last = l == depth - 1
        x2d, cast = _post(x2d, ya.reshape(T, ATT_WIDTH), yb, gab, weights, norm_mlp_w.reshape(depth, 1, D),
                          final_w, l, final_norm=last, next_weights=() if last else stacked + (w_in_t,))
        if not last:
            weights, w_all = cast[:-1], cast[-1]
    return x2d.reshape(B, S, D)
```

```python
import functools
import math

import jax
import jax.numpy as jnp
from jax import lax
from jax.experimental import pallas as pl
from jax.experimental.pallas import tpu as pltpu

F32 = jnp.float32
BF16 = jnp.bfloat16

D_MODEL = 1024
ATT_HEADS = 8
ATT_HEAD_DIM = 64
ATT_WIDTH = ATT_HEADS * ATT_HEAD_DIM
ATT_SPAN = 128
DILATIONS = (1, 4, 16)
ROPE_THETA = 500000.0
ROPE_DIM = ATT_HEAD_DIM // 4
MLSTM_HEADS = 4
MLSTM_HEAD_DIM = 128
MLSTM_WIDTH = MLSTM_HEADS * MLSTM_HEAD_DIM
CONV_WIDTH = 4
D_FF = 4 * D_MODEL
NORM_EPS = 1e-6

COL_AQ = 0
COL_AK = COL_AQ + ATT_WIDTH
COL_AV = COL_AK + ATT_WIDTH
COL_MQ = COL_AV + ATT_WIDTH
COL_MK = COL_MQ + MLSTM_WIDTH
COL_MV = COL_MK + MLSTM_WIDTH
COL_MO = COL_MV + MLSTM_WIDTH
COL_MI = COL_MO + MLSTM_WIDTH
COL_MF = COL_MI + MLSTM_HEADS
COL_GA = COL_MF + MLSTM_HEADS
COL_GB = COL_GA + D_MODEL
D_IN = COL_GB + D_MODEL

LANES = 128
assert COL_MI % LANES == 0
MXU_COLS = 256
W_PAD = -(-D_IN // MXU_COLS) * MXU_COLS
GATE_SHIFT = 2 * MLSTM_HEADS

TM_IN = 512
TM_POST = 512
ATT_BLK = 2048
MCHUNK = 128
TM_MLSTM = 2048
VMEM_LIMIT = 56 * 1024 * 1024
NEG = -1e30
LOG2E = math.log2(math.e)


def _const_spec(shape):
    nd = len(shape)
    return pl.BlockSpec(shape, lambda *_: (0,) * nd, pipeline_mode=pl.Buffered(1))


ROPE_HALF = ROPE_DIM // 2
ROPE_PACK = LANES // ROPE_HALF
ROPE_ROWS = 128


def _rope_table_kernel(pos_ref, invf_ref, c_ref, s_ref):
    ang = pos_ref[...] * invf_ref[...]
    cosx = jnp.cos(ang)
    sinx = jnp.sin(ang)
    dst = lax.broadcasted_iota(jnp.int32, (ROPE_ROWS, LANES), 1)
    in_head = dst % ATT_HEAD_DIM
    rotary = in_head < ROPE_DIM
    sign = jnp.where(in_head < ROPE_HALF, -1.0, 1.0)
    for r in range(ROPE_PACK):
        src = r * ROPE_HALF + dst % ROPE_HALF
        c_ref[pl.ds(r, ROPE_ROWS, stride=ROPE_PACK), :] = jnp.where(
            rotary, jnp.take_along_axis(cosx, src, axis=1), 1.0)
        s_ref[pl.ds(r, ROPE_ROWS, stride=ROPE_PACK), :] = jnp.where(
            rotary, jnp.take_along_axis(sinx, src, axis=1) * sign, 0.0)


def _rope_tables(positions):
    T = positions.size
    assert T % (ROPE_PACK * ROPE_ROWS) == 0
    pos = jnp.repeat(positions.astype(F32).reshape(T // ROPE_PACK, ROPE_PACK), ROPE_HALF, axis=1)
    inv_freq = ROPE_THETA ** (-jnp.arange(0, ROPE_DIM, 2, dtype=F32) / ROPE_DIM)
    invf = jnp.tile(inv_freq, ROPE_PACK).reshape(1, LANES)
    table = jax.ShapeDtypeStruct((T, LANES), F32)
    out_spec = pl.BlockSpec((ROPE_PACK * ROPE_ROWS, LANES), lambda i: (i, 0))
    return pl.pallas_call(
        _rope_table_kernel,
        grid=(T // (ROPE_PACK * ROPE_ROWS),),
        in_specs=[pl.BlockSpec((ROPE_ROWS, LANES), lambda i: (i, 0)), pl.BlockSpec((1, LANES), lambda i: (0, 0))],
        out_specs=(out_spec, out_spec),
        out_shape=(table, table),
        compiler_params=pltpu.CompilerParams(dimension_semantics=("parallel",)),
        name="rope_tables",
    )(pos, invf)


CONV_TAIL = 8
CONV_SPARE = 8
GATE_ROWS = 8


def _log_sigmoid(x):
    return jnp.minimum(x, 0.0) - jnp.log(1.0 + jnp.exp(-jnp.abs(x)))


def _inproj_kernel(x_ref, nw_ref, w_ref, b_ref, c_ref, s_ref, cw_ref, cb_ref,
                   qkv_ref, mqko_ref, mv_ref, gab_ref, gt_ref, gtt_ref,
                   u_s, res_s, *, tiles_per_seq):
    aq_ref, ak_ref, av_ref = (qkv_ref.at[:, j * ATT_WIDTH:(j + 1) * ATT_WIDTH] for j in range(3))
    mq_ref, mk_ref, mo_ref = (mqko_ref.at[:, j * MLSTM_WIDTH:(j + 1) * MLSTM_WIDTH] for j in range(3))
    ga_ref, gb_ref = (gab_ref.at[:, j * D_MODEL:(j + 1) * D_MODEL] for j in range(2))
    @pl.when(pl.program_id(0) % tiles_per_seq == 0)
    def _():
        u_s[:, TM_IN:TM_IN + CONV_TAIL, :] = jnp.zeros((u_s.shape[0], CONV_TAIL, LANES), F32)

    x = x_ref[...]
    var = jnp.mean(x * x, axis=-1, keepdims=True)
    h = (x * lax.rsqrt(var + NORM_EPS) * nw_ref[...]).astype(BF16)

    def proj(lo, width):
        return jnp.dot(h, w_ref[:, lo:lo + width], preferred_element_type=F32) + b_ref[:, lo:lo + width]

    cos = c_ref[...]
    sin = s_ref[...]
    lane = lax.broadcasted_iota(jnp.int32, cos.shape, 1)
    first_half = (lane % ATT_HEAD_DIM) < (ROPE_DIM // 2)

    def rope_store(dst_ref, lo, scale=None):
        zz = proj(lo, ATT_WIDTH)
        for j in range(ATT_WIDTH // LANES):
            z = zz[:, j * LANES:(j + 1) * LANES]
            partner = jnp.where(first_half,
                                pltpu.roll(z, LANES - ROPE_DIM // 2, axis=1),
                                pltpu.roll(z, ROPE_DIM // 2, axis=1))
            out = z * cos + partner * sin
            dst_ref[:, j * LANES:(j + 1) * LANES] = out if scale is None else out * scale
        return zz

    def stage_conv_input(slab0, lo):
        z = proj(lo, MLSTM_WIDTH)
        for j in range(MLSTM_WIDTH // LANES):
            sl = slab0 + j
            u_s[sl, 0:CONV_TAIL, :] = u_s[sl, TM_IN:TM_IN + CONV_TAIL, :]
            u_s[sl, CONV_TAIL:CONV_TAIL + TM_IN, :] = z[:, j * LANES:(j + 1) * LANES]

    def conv_silu_store(dst_ref, slab0, col0, j, scale, after):
        sl = slab0 + j
        half = TM_IN // 2
        ws = slice(col0 + j * LANES, col0 + (j + 1) * LANES)
        spare0 = TM_IN + CONV_TAIL
        u_s[sl, spare0:spare0 + CONV_SPARE, :] = after[TM_IN - CONV_SPARE:TM_IN, 0:LANES]
        traced_zero = jnp.minimum(pl.program_id(0), 0)
        for parity in range(2):
            out = cb_ref[:, ws]
            for t in range(CONV_WIDTH):
                r0 = CONV_TAIL - (CONV_WIDTH - 1) + t + parity + traced_zero
                out = out + cw_ref[t:t + 1, ws] * u_s[sl, pl.ds(r0, half, stride=2), :]
            out = out * jax.nn.sigmoid(out)
            res_s[sl, pl.ds(parity, half, stride=2), :] = out if scale is None else out * scale
        dst_ref[:, j * LANES:(j + 1) * LANES] = res_s[sl].astype(dst_ref.dtype)

    q_scale = MLSTM_HEAD_DIM ** -0.5
    k_slab0 = MLSTM_WIDTH // LANES
    stage_conv_input(0, COL_MQ)
    stage_conv_input(k_slab0, COL_MK)
    z_aq = rope_store(aq_ref, COL_AQ, LOG2E)
    conv_silu_store(mq_ref, 0, 0, 0, q_scale, z_aq)
    z_ak = rope_store(ak_ref, COL_AK)
    conv_silu_store(mq_ref, 0, 0, 1, q_scale, z_ak)
    z_av = proj(COL_AV, ATT_WIDTH)
    av_ref[...] = z_av
    conv_silu_store(mq_ref, 0, 0, 2, q_scale, z_av)
    z_mv = proj(COL_MV, MLSTM_WIDTH)
    for cc in range(TM_IN // MCHUNK):
        for hd in range(MLSTM_HEADS):
            blk = z_mv[cc * MCHUNK:(cc + 1) * MCHUNK, hd * MLSTM_HEAD_DIM:(hd + 1) * MLSTM_HEAD_DIM]
            mv_ref[cc, hd * MLSTM_HEAD_DIM:(hd + 1) * MLSTM_HEAD_DIM, :] = blk.T.astype(mv_ref.dtype)
    conv_silu_store(mq_ref, 0, 0, 3, q_scale, z_mv)
    z_mo = proj(COL_MO, MLSTM_WIDTH)
    mo_ref[...] = z_mo.astype(mo_ref.dtype)
    conv_silu_store(mk_ref, k_slab0, MLSTM_WIDTH, 0, None, z_mo)

    z_tail = proj(COL_MI, W_PAD - COL_MI)
    tiles = [z_tail[:, j * LANES:(j + 1) * LANES] for j in range((W_PAD - COL_MI) // LANES)]
    conv_silu_store(mk_ref, k_slab0, MLSTM_WIDTH, 1, None, tiles[1])
    conv_silu_store(mk_ref, k_slab0, MLSTM_WIDTH, 2, None, tiles[5])
    conv_silu_store(mk_ref, k_slab0, MLSTM_WIDTH, 3, None, tiles[9])
    rolled = [pltpu.roll(t, LANES - GATE_SHIFT, axis=1) for t in tiles]
    low_lanes = lane < LANES - GATE_SHIFT
    for j in range(D_MODEL // LANES):
        ga_ref[:, j * LANES:(j + 1) * LANES] = jnp.where(low_lanes, rolled[j], rolled[j + 1]).astype(ga_ref.dtype)
        k = j + D_MODEL // LANES
        gb_ref[:, j * LANES:(j + 1) * LANES] = jnp.where(low_lanes, rolled[k], rolled[k + 1]).astype(gb_ref.dtype)

    zg = jnp.where(lane < GATE_SHIFT, tiles[0], 0.0)
    logf = _log_sigmoid(zg)
    ri = lax.broadcasted_iota(jnp.int32, (MCHUNK, MCHUNK), 0)
    ci = lax.broadcasted_iota(jnp.int32, (MCHUNK, MCHUNK), 1)
    tri = (ci <= ri).astype(F32)
    is_input_gate = lax.broadcasted_iota(jnp.int32, (MCHUNK, LANES), 1) < MLSTM_HEADS
    for cc in range(TM_IN // MCHUNK):
        rows = slice(cc * MCHUNK, (cc + 1) * MCHUNK)
        bcum = jnp.dot(tri, logf[rows], precision=lax.Precision.HIGHEST, preferred_element_type=F32)
        gc = jnp.where(is_input_gate, zg[rows], bcum) * LOG2E
        gt_ref[rows, :] = gc
        gtt_ref[cc] = gc.T[0:GATE_ROWS, :]


def _layer_spec(shape, layer):
    nd = len(shape)
    return pl.BlockSpec((None,) + tuple(shape), lambda *_: (layer,) + (0,) * nd, pipeline_mode=pl.Buffered(1))


def _inproj(x2d, norm_w, w_all, b_all, rope_c, rope_s, conv_w, conv_b, seq_len, layer):
    T = x2d.shape[0]
    tile = lambda w: pl.BlockSpec((TM_IN, w), lambda i: (i, 0))
    out_shapes = (
        jax.ShapeDtypeStruct((T, 3 * ATT_WIDTH), F32),
        jax.ShapeDtypeStruct((T, 3 * MLSTM_WIDTH), BF16),
        jax.ShapeDtypeStruct((T // MCHUNK, MLSTM_WIDTH, MCHUNK), BF16),
        jax.ShapeDtypeStruct((T, 2 * D_MODEL), BF16),
        jax.ShapeDtypeStruct((T, LANES), F32),
        jax.ShapeDtypeStruct((T // MCHUNK, GATE_ROWS, MCHUNK), F32),
    )
    per_chunk = lambda rows: pl.BlockSpec((TM_IN // MCHUNK, rows, MCHUNK), lambda i: (i, 0, 0))
    out_specs = tuple(tile(s.shape[1]) if len(s.shape) == 2 else per_chunk(s.shape[1]) for s in out_shapes)
    assert seq_len % TM_IN == 0
    return pl.pallas_call(
        functools.partial(_inproj_kernel, tiles_per_seq=seq_len // TM_IN),
        grid=(T // TM_IN,),
        in_specs=[tile(D_MODEL), _layer_spec((1, D_MODEL), layer), _const_spec((D_MODEL, W_PAD)),
                  _layer_spec((1, W_PAD), layer), tile(LANES), tile(LANES),
                  _layer_spec((CONV_WIDTH, 2 * MLSTM_WIDTH), layer), _layer_spec((1, 2 * MLSTM_WIDTH), layer)],
        out_specs=out_specs,
        out_shape=out_shapes,
        scratch_shapes=[pltpu.VMEM((2 * MLSTM_WIDTH // LANES, TM_IN + CONV_TAIL + CONV_SPARE, LANES), F32),
                        pltpu.VMEM((2 * MLSTM_WIDTH // LANES, TM_IN, LANES), F32)],
        compiler_params=pltpu.CompilerParams(dimension_semantics=("arbitrary",),
                                             vmem_limit_bytes=VMEM_LIMIT),
        name="inproj",
    )(x2d, norm_w, w_all, b_all, rope_c, rope_s, conv_w, conv_b)


ATT_UNROLL = 16
PITCH16 = ATT_SPAN + 8


def _attn_kernel(q_ref, k_ref, v_ref, o_ref,
                 kd1, vd1, qd4, kd4, vd4, qd16, kd16, vd16, tmp, st4_s, st16_s, bias_s):
    j = pl.program_id(2)
    blk = ATT_SPAN
    nsub = {d: ATT_BLK // d // blk for d in DILATIONS}
    kv_bufs = ((1, kd1, vd1), (4, kd4, vd4), (16, kd16, vd16))

    row = lax.broadcasted_iota(jnp.int32, (2 * blk, 2 * blk), 0) % blk
    col = lax.broadcasted_iota(jnp.int32, (2 * blk, 2 * blk), 1)
    band = (col >= row) & (col <= row + ATT_SPAN)
    bias_s[0] = jnp.where(band, 0.0, NEG)
    bias_s[1] = jnp.where(band & (col >= blk), 0.0, NEG)

    @pl.when(j == 0)
    def _():
        for d, kd, vd in kv_bufs:
            n = ATT_BLK // d
            for r in range(d):
                base = r * (n + blk)
                kd[base:base + blk] = jnp.zeros((blk, LANES), BF16)
                vd[base:base + blk] = jnp.zeros((blk, LANES), BF16)

    @pl.when(j != 0)
    def _():
        for d, kd, vd in kv_bufs:
            n = ATT_BLK // d
            for r in range(d):
                base = r * (n + blk)
                kd[base:base + blk] = kd[base + n:base + n + blk]
                vd[base:base + blk] = vd[base + n:base + n + blk]

    def deinterleave(src_ref, dst1, dst4, dst16, is_kv):
        pad = blk if is_kv else 0
        if dst1 is not None:
            dst1[blk:blk + ATT_BLK] = src_ref[0].astype(BF16)
        n4 = ATT_BLK // 4
        for r4 in range(4):
            t4 = src_ref[0, pl.ds(r4, n4, stride=4), :]
            tmp[r4] = t4
            o4 = r4 * (n4 + pad) + pad
            dst4[o4:o4 + n4] = t4.astype(BF16)
        n16 = ATT_BLK // 16
        for r4 in range(4):
            for rr in range(4):
                o16 = (4 * rr + r4) * (n16 + pad) + pad
                dst16[o16:o16 + n16] = tmp[r4, pl.ds(rr, n16, stride=4), :].astype(BF16)

    deinterleave(q_ref, None, qd4, qd16, False)
    deinterleave(k_ref, kd1, kd4, kd16, True)
    deinterleave(v_ref, vd1, vd4, vd16, True)

    head_a = lax.broadcasted_iota(jnp.int32, (blk, LANES), 1) < ATT_HEAD_DIM

    def unit(q2, k2, v2, bias):
        zero = jnp.zeros_like(q2)
        qs = jnp.concatenate([jnp.where(head_a, q2, zero), jnp.where(head_a, zero, q2)], axis=0)
        s = lax.dot_general(qs, k2, (((1,), (1,)), ((), ())), preferred_element_type=F32) + bias
        m = jnp.max(s, axis=-1, keepdims=True)
        p = jnp.exp2(s - m).astype(BF16)
        v_aug = jnp.concatenate([v2, jnp.ones_like(v2)], axis=1)
        pv = jnp.dot(p, v_aug, preferred_element_type=F32)
        acc = jnp.where(head_a, pv[:blk, :LANES], pv[blk:, :LANES])
        ll = jnp.where(head_a, pv[:blk, LANES:], pv[blk:, LANES:])
        mm = jnp.where(head_a, m[:blk], m[blk:])
        return acc, mm, ll

    first_blk = jnp.where(j == 0, 1, 0)

    def body16(u, carry):
        k0 = pl.multiple_of(u * (2 * blk), blk)
        bias = bias_s[first_blk]
        res = unit(qd16[pl.ds(pl.multiple_of(u * blk, blk), blk), :], kd16[pl.ds(k0, 2 * blk), :],
                   vd16[pl.ds(k0, 2 * blk), :], bias)
        row0 = pl.multiple_of(u * PITCH16, 8)
        for a, val in enumerate(res):
            st16_s[a, pl.ds(row0, blk), :] = val
        return carry

    def body4(u, carry):
        r = u // nsub[4]
        sb = u % nsub[4]
        k0 = pl.multiple_of(r * (ATT_BLK // 4 + blk) + sb * blk, blk)
        bias = bias_s[jnp.where(sb == 0, first_blk, 0)]
        res = unit(qd4[pl.ds(pl.multiple_of(u * blk, blk), blk), :], kd4[pl.ds(k0, 2 * blk), :],
                   vd4[pl.ds(k0, 2 * blk), :], bias)
        t0 = sb * (blk * 4) + r
        for a, val in enumerate(res):
            st4_s[a, pl.ds(t0, blk, stride=4), :] = val
        return carry

    lax.fori_loop(0, ATT_BLK // blk, body16, 0, unroll=ATT_UNROLL)
    lax.fori_loop(0, ATT_BLK // blk, body4, 0, unroll=ATT_UNROLL)

    def body1(u, carry):
        r0 = pl.multiple_of(u * blk, blk)
        sl = pl.ds(r0, blk)
        q2 = q_ref[0, sl, :].astype(BF16)
        bias = bias_s[jnp.where(u == 0, first_blk, 0)]
        acc2, m2, l2 = unit(q2, kd1[pl.ds(r0, 2 * blk), :], vd1[pl.ds(r0, 2 * blk), :], bias)
        per_res = blk // 16
        acc1, m1, l1 = (jnp.concatenate([st16_s[a, pl.ds(per_res * u + k, 16, stride=PITCH16), :]
                                         for k in range(per_res)], axis=0) for a in range(3))
        acc0, m0, l0 = st4_s[0, sl, :], st4_s[1, sl, :], st4_s[2, sl, :]
        mx = jnp.maximum(jnp.maximum(m0, m1), m2)
        w0, w1, w2 = jnp.exp2(m0 - mx), jnp.exp2(m1 - mx), jnp.exp2(m2 - mx)
        num = w0 * acc0 + w1 * acc1 + w2 * acc2
        den = w0 * l0 + w1 * l1 + w2 * l2
        o_ref[0, sl, :] = (num / den).astype(o_ref.dtype)
        return carry

    lax.fori_loop(0, nsub[1], body1, 0, unroll=ATT_UNROLL)


def _attention(qkv):
    B, S, _ = qkv.shape
    blk = ATT_SPAN
    pairs = ATT_WIDTH // LANES
    cur = pl.BlockSpec((1, ATT_BLK, LANES), lambda b, hp, j: (b, j, hp))
    part = lambda p: pl.BlockSpec((1, ATT_BLK, LANES), lambda b, hp, j: (b, j, p * pairs + hp))
    kv_rows = {d: d * (ATT_BLK // d + blk) for d in DILATIONS}
    scratch = [
        pltpu.VMEM((kv_rows[1], LANES), BF16), pltpu.VMEM((kv_rows[1], LANES), BF16),
        pltpu.VMEM((ATT_BLK, LANES), BF16),
        pltpu.VMEM((kv_rows[4], LANES), BF16), pltpu.VMEM((kv_rows[4], LANES), BF16),
        pltpu.VMEM((ATT_BLK, LANES), BF16),
        pltpu.VMEM((kv_rows[16], LANES), BF16), pltpu.VMEM((kv_rows[16], LANES), BF16),
        pltpu.VMEM((4, ATT_BLK // 4, LANES), F32),
        pltpu.VMEM((3, ATT_BLK, LANES), F32),
        pltpu.VMEM((3, 16 * PITCH16, LANES), F32),
        pltpu.VMEM((2, 2 * blk, 2 * blk), F32),
    ]
    return pl.pallas_call(
        _attn_kernel,
        grid=(B, pairs, S // ATT_BLK),
        in_specs=[part(0), part(1), part(2)],
        out_specs=cur,
        out_shape=jax.ShapeDtypeStruct((B, S, ATT_WIDTH), BF16),
        scratch_shapes=scratch,
        compiler_params=pltpu.CompilerParams(dimension_semantics=("parallel", "parallel", "arbitrary"),
                                             vmem_limit_bytes=VMEM_LIMIT),
        name="dilated_attention",
    )(qkv, qkv, qkv)


NORM_ROWS = 16


def _mlstm_kernel(qko_ref, vt_ref, g_ref, gt_ref, nw_ref, *rest, n_cast):
    L = MCHUNK
    D = MLSTM_HEAD_DIM
    nt_dims = (((1,), (1,)), ((), ()))
    q_ref, k_ref, mo_ref = (qko_ref.at[:, j * MLSTM_WIDTH:(j + 1) * MLSTM_WIDTH] for j in range(3))
    y_ref, c_state, m_state = rest[n_cast], rest[-2], rest[-1]
    for src, dst in zip(rest[:n_cast], rest[n_cast + 1:-2]):
        dst[...] = src[...].astype(BF16)

    @pl.when(pl.program_id(1) == 0)
    def _():
        c_state[...] = jnp.zeros_like(c_state)
        m_state[...] = jnp.zeros_like(m_state)

    key = lax.broadcasted_iota(jnp.int32, (L, L), 0)
    qry = lax.broadcasted_iota(jnp.int32, (L, L), 1)
    visible = key <= qry
    ones_rows = jnp.ones((NORM_ROWS, L), BF16)

    units = [(cc, h) for cc in range(TM_MLSTM // L) for h in range(MLSTM_HEADS)]
    rows = lambda cc: slice(cc * L, (cc + 1) * L)
    cols = lambda h: slice(h * D, (h + 1) * D)

    s_t = {u: lax.dot_general(k_ref[rows(u[0]), cols(u[1])], q_ref[rows(u[0]), cols(u[1])], nt_dims,
                              preferred_element_type=F32) for u in units}
    b_row, g_tot, pm, vt_aug, intra, m_loc, kv = {}, {}, {}, {}, {}, {}, {}
    for u in units:
        cc, h = u
        fh = MLSTM_HEADS + h
        r_col = g_ref[rows(cc), h:h + 1] - g_ref[rows(cc), fh:fh + 1]
        b_row[u] = gt_ref[cc, fh:fh + 1, :]
        g_tot[u] = b_row[u][:, L - 1:L]
        r_vis = jnp.where(visible, r_col, NEG)
        pm[u] = jnp.max(r_vis, axis=0, keepdims=True)
        p_t = (jnp.exp2(r_vis - pm[u]) * s_t[u]).astype(BF16)
        vt_aug[u] = jnp.concatenate([vt_ref[cc, cols(h), :], ones_rows], axis=0)
        intra[u] = jnp.dot(vt_aug[u], p_t, preferred_element_type=F32)
    for u in units:
        cc, h = u
        r_row = gt_ref[cc, h:h + 1, :] - b_row[u]
        m_loc[u] = jnp.max(g_tot[u] + r_row, axis=-1, keepdims=True)
        vw = (vt_aug[u].astype(F32) * jnp.exp2(g_tot[u] + r_row - m_loc[u])).astype(BF16)
        kv[u] = jnp.dot(vw, k_ref[rows(cc), cols(h)], preferred_element_type=F32)

    c_aug = [c_state[h] for h in range(MLSTM_HEADS)]
    m_prev = [m_state[h:h + 1, 0:1] for h in range(MLSTM_HEADS)]
    for u in units:
        cc, h = u
        inter = lax.dot_general(c_aug[h].astype(BF16), q_ref[rows(cc), cols(h)], nt_dims,
                                preferred_element_type=F32)
        mm = jnp.maximum(pm[u], m_prev[h])
        nd = jnp.exp2(pm[u] - mm) * intra[u] + jnp.exp2(m_prev[h] - mm) * inter
        inv = 1.0 / jnp.maximum(jnp.abs(nd[D:D + 1, :]), jnp.exp2(-(b_row[u] + mm)))
        hh = nd[0:D, :] * inv
        hn = (hh * lax.rsqrt(jnp.mean(hh * hh, axis=0, keepdims=True) + NORM_EPS)).T
        o_gate = jax.nn.sigmoid(mo_ref[rows(cc), cols(h)].astype(F32))
        y_ref[rows(cc), cols(h)] = (o_gate * hn * nw_ref[:, cols(h)]).astype(y_ref.dtype)

        m_new = jnp.maximum(g_tot[u] + m_prev[h], m_loc[u])
        c_aug[h] = jnp.exp2(g_tot[u] + m_prev[h] - m_new) * c_aug[h] + jnp.exp2(m_loc[u] - m_new) * kv[u]
        m_prev[h] = m_new
    for h in range(MLSTM_HEADS):
        c_state[h] = c_aug[h]
        m_state[h:h + 1, :] = jnp.broadcast_to(m_prev[h], (1, LANES))


def _mlstm(mqko, mvt, gates, gates_t, norm_w, batch, cast_weights, layer):
    T = mqko.shape[0]
    nt = T // batch // TM_MLSTM
    nch = TM_MLSTM // MCHUNK
    tile = lambda w: pl.BlockSpec((TM_MLSTM, w), lambda b, c: (b * nt + c, 0))
    per_chunk = lambda r: pl.BlockSpec((nch, r, MCHUNK), lambda b, c: (b * nt + c, 0, 0))
    cast_in, cast_out, cast_shape = _cast_specs(cast_weights, layer, batch * nt, lambda b, c: b * nt + c)
    out = pl.pallas_call(
        functools.partial(_mlstm_kernel, n_cast=len(cast_weights)),
        grid=(batch, nt),
        in_specs=[tile(3 * MLSTM_WIDTH), per_chunk(MLSTM_WIDTH), tile(LANES),
                  per_chunk(GATE_ROWS), pl.BlockSpec((1, MLSTM_WIDTH), lambda b, c: (0, 0))] + cast_in,
        out_specs=[tile(MLSTM_WIDTH)] + cast_out,
        out_shape=[jax.ShapeDtypeStruct((T, MLSTM_WIDTH), BF16)] + cast_shape,
        scratch_shapes=[
            pltpu.VMEM((MLSTM_HEADS, MLSTM_HEAD_DIM + NORM_ROWS, MLSTM_HEAD_DIM), F32),
            pltpu.VMEM((8, LANES), F32),
        ],
        compiler_params=pltpu.CompilerParams(dimension_semantics=("parallel", "arbitrary"),
                                             vmem_limit_bytes=VMEM_LIMIT),
        name="mlstm",
    )(mqko, mvt, gates, gates_t, norm_w, *cast_weights)
    return out[0], tuple(out[1:])


def _post_kernel(x_ref, ya_ref, yb_ref, gab_ref, wpa_ref, wpm_ref, wo_ref, nw_ref, w1_ref, w2_ref,
                 fw_ref, *rest, final_norm):
    n_cast = len(rest) // 2
    o_ref = rest[n_cast]
    for src, dst in zip(rest[:n_cast - 1], rest[n_cast + 1:-1]):
        dst[...] = src[...].astype(BF16)
    if n_cast:
        _cast_in_proj_slab(rest[n_cast - 1], rest[-1], jnp.minimum(pl.program_id(0), IN_SLABS - 1))
    ga_ref, gb_ref = (gab_ref.at[:, j * D_MODEL:(j + 1) * D_MODEL] for j in range(2))
    pa = jnp.dot(ya_ref[...], wpa_ref[...], preferred_element_type=F32)
    pb = jnp.dot(yb_ref[...], wpm_ref[...], preferred_element_type=F32)
    mixed = (jax.nn.sigmoid(ga_ref[...].astype(F32)) * pa
             + jax.nn.sigmoid(gb_ref[...].astype(F32)) * pb).astype(BF16)
    x1 = x_ref[...] + jnp.dot(mixed, wo_ref[...], preferred_element_type=F32)
    var = jnp.mean(x1 * x1, axis=-1, keepdims=True)
    h2 = (x1 * lax.rsqrt(var + NORM_EPS) * nw_ref[...]).astype(BF16)
    acc = x1
    ff_chunk = D_MODEL
    for c in range(D_FF // ff_chunk):
        cs = slice(c * ff_chunk, (c + 1) * ff_chunk)
        u = jnp.maximum(jnp.dot(h2, w1_ref[:, cs], preferred_element_type=F32), 0.0)
        acc = acc + jnp.dot((u * u).astype(BF16), w2_ref[cs, :], preferred_element_type=F32)
    if final_norm:
        var = jnp.mean(acc * acc, axis=-1, keepdims=True)
        acc = acc * lax.rsqrt(var + NORM_EPS) * fw_ref[...]
    o_ref[...] = acc


def _post(x2d, ya, yb, gab, weights, norm_w, final_w, layer, final_norm, next_weights):
    T = x2d.shape[0]
    steps = T // TM_POST
    tile = lambda w: pl.BlockSpec((TM_POST, w), lambda i: (i, 0))
    wpa, wpm, wo, w1, w2 = weights
    cast_in, cast_out, cast_shape = _cast_specs(next_weights[:-1], layer + 1, steps)
    if next_weights:
        assert steps >= IN_SLABS
        in_spec, out_spec, out_shape = _cast_in_proj_specs(layer + 1, PREP_IN_ROWS)
        cast_in, cast_out, cast_shape = cast_in + [in_spec], cast_out + [out_spec], cast_shape + [out_shape]
    out = pl.pallas_call(
        functools.partial(_post_kernel, final_norm=final_norm),
        grid=(steps,),
        in_specs=[tile(D_MODEL), tile(ATT_WIDTH), tile(MLSTM_WIDTH), tile(2 * D_MODEL),
                  _const_spec(wpa.shape), _const_spec(wpm.shape), _const_spec(wo.shape),
                  _layer_spec((1, D_MODEL), layer), _const_spec(w1.shape), _const_spec(w2.shape),
                  _const_spec((1, D_MODEL))] + cast_in,
        out_specs=[tile(D_MODEL)] + cast_out,
        out_shape=[jax.ShapeDtypeStruct((T, D_MODEL), F32)] + cast_shape,
        compiler_params=pltpu.CompilerParams(dimension_semantics=("arbitrary",),
                                             vmem_limit_bytes=VMEM_LIMIT),
        name="post",
    )(x2d, ya, yb, gab, wpa, wpm, wo, norm_w, w1, w2, final_w, *next_weights)
    return out[0], tuple(out[1:])


PREP_IN_ROWS = MXU_COLS
IN_SLABS = W_PAD // PREP_IN_ROWS
PREP_IN_ROWS_ALONE = W_PAD // 2


def _cast_in_proj_slab(w_ref, o_ref, slab):
    row = slab * w_ref.shape[0] + lax.broadcasted_iota(jnp.int32, w_ref.shape, 0)
    w = w_ref[...]
    w = jnp.where(row < COL_AK, w * (ATT_HEAD_DIM ** -0.5), w)
    o_ref[...] = jnp.where(row < D_IN, w, 0.0).T.astype(BF16)


def _cast_in_proj_specs(layer, rows):
    assert W_PAD % rows == 0 and rows % LANES == 0
    slab = lambda i: jnp.minimum(i, W_PAD // rows - 1)
    in_spec = pl.BlockSpec((None, rows, D_MODEL), lambda i: (layer, slab(i), 0))
    out_spec = pl.BlockSpec((D_MODEL, rows), lambda i: (0, slab(i)))
    return in_spec, out_spec, jax.ShapeDtypeStruct((D_MODEL, W_PAD), BF16)


def _cast_in_proj_kernel(w_ref, o_ref):
    _cast_in_proj_slab(w_ref, o_ref, pl.program_id(0))


def _cast_in_proj(w_t, layer):
    in_spec, out_spec, out_shape = _cast_in_proj_specs(layer, PREP_IN_ROWS_ALONE)
    return pl.pallas_call(
        _cast_in_proj_kernel,
        grid=(W_PAD // PREP_IN_ROWS_ALONE,),
        in_specs=[in_spec],
        out_specs=out_spec,
        out_shape=out_shape,
        compiler_params=pltpu.CompilerParams(dimension_semantics=("parallel",), vmem_limit_bytes=VMEM_LIMIT),
        name="cast_in_proj",
    )(w_t)


def _cast_specs(weights, layer, steps, step_of=lambda i: i):
    for a in weights:
        assert a.shape[1] % (steps * 16) == 0
    in_specs = [pl.BlockSpec((None, a.shape[1] // steps, a.shape[2]), lambda *g: (layer, step_of(*g), 0))
                for a in weights]
    out_specs = [pl.BlockSpec((a.shape[1] // steps, a.shape[2]), lambda *g: (step_of(*g), 0)) for a in weights]
    out_shape = [jax.ShapeDtypeStruct(a.shape[1:], BF16) for a in weights]
    return in_specs, out_specs, out_shape


def kernel(x, positions, norm_mix_w, w_in, b_in, conv_w, conv_b, mlstm_norm_w, w_proj_att, w_proj_mlstm,
           w_out, norm_mlp_w, w_ff1, w_ff2, final_norm_w):
    B, S, D = x.shape
    T = B * S
    depth = w_in.shape[0]
    assert D == D_MODEL and S % ATT_BLK == 0 and T % TM_IN == 0 and T % TM_POST == 0
    assert math.isclose(ATT_HEAD_DIM ** -0.5, 0.125)
    rope_c, rope_s = _rope_tables(positions)
    w_in_t = jnp.swapaxes(w_in, 1, 2)
    stacked = (w_proj_att, w_proj_mlstm, w_out, w_ff1, w_ff2)
    w_all = _cast_in_proj(w_in_t, 0)
    weights = None
    q_scale = jnp.where(jnp.arange(W_PAD) < COL_AK, ATT_HEAD_DIM ** -0.5, 1.0).astype(F32)
    b_all = (jnp.pad(b_in, ((0, 0), (0, W_PAD - D_IN))) * q_scale).reshape(depth, 1, W_PAD)
    x2d = x.reshape(T, D)
    final_w = final_norm_w.reshape(1, D).astype(F32)
    for l in range(depth):
        qkv, mqko, mvt, gab, gates, gates_t = _inproj(
            x2d, norm_mix_w.reshape(depth, 1, D), w_all, b_all, rope_c, rope_s,
            conv_w, conv_b.reshape(depth, 1, -1), S, l)
        ya = _attention(qkv.reshape(B, S, 3 * ATT_WIDTH))
        yb, cast = _mlstm(mqko, mvt, gates, gates_t, mlstm_norm_w[l].reshape(1, -1), B,
                          stacked if weights is None else (), l)
        weights = cast if weights is None else weights
        last = l == depth - 1
        x2d, cast = _post(x2d, ya.reshape(T, ATT_WIDTH), yb, gab, weights, norm_mlp_w.reshape(depth, 1, D),
                          final_w, l, final_norm=last, next_weights=() if last else stacked + (w_in_t,))
        if not last:
            weights, w_all = cast[:-1], cast[-1]
    return x2d.reshape(B, S, D)
```
